```python
import math
import jax, jax.numpy as jnp
from jax import lax
import numpy as np

D_MODEL = 2048
BATCH = 8
SEQ = 2048
DEPTH = 1

D_MIX = D_MODEL
A_HEADS = 8
A_HEAD_DIM = 128
A_WIDTH = A_HEADS * A_HEAD_DIM
A_KV_RANK = 256
IDX_HEADS = 16
IDX_DIM = 64
IDX_TOPK_MAX = 256
Q_BLOCK = 128
B_HEADS = 8
B_KEY_DIM = 128
B_VAL_DIM = 128
B_WIDTH = B_HEADS * B_VAL_DIM
B_CHUNK = 64
REL_BUCKETS = 32
REL_MAX_DIST = 128
N_EXPERTS = 64
TOP_K = 8
N_GROUPS = 8
TOPK_GROUPS = 4
D_EXPERT = 512
D_SHARED = 512
ROUTED_SCALE = 2.5
MOE_BLOCK = 128
EPS = 1e-6

IN_SIZES = (A_WIDTH,
            A_KV_RANK,
            IDX_HEADS * IDX_DIM,
            IDX_DIM,
            IDX_HEADS,
            B_HEADS * B_KEY_DIM,
            B_HEADS * B_KEY_DIM,
            B_WIDTH,
            B_WIDTH)
D_IN = sum(IN_SIZES)

kernel_name = 'hymba_dsa_hgrn2_moe_adaln'


def rmsnorm(x, g):
    xf = x.astype(jnp.float32)
    y = xf * lax.rsqrt(jnp.mean(xf * xf, axis=-1, keepdims=True) + EPS)
    return (y * g.astype(jnp.float32)).astype(x.dtype)


def split_cols(t, sizes):
    out, off = [], 0
    for s in sizes:
        out.append(t[..., off:off + s])
        off += s
    return out


def t5_bucket(rel):
    n = jnp.maximum(rel, 0)
    max_exact = REL_BUCKETS // 2
    n_large = jnp.maximum(n, max_exact).astype(jnp.float32)
    large = max_exact + (jnp.log(n_large / max_exact) / math.log(REL_MAX_DIST / max_exact)
                         * (REL_BUCKETS - max_exact)).astype(jnp.int32)
    large = jnp.minimum(large, REL_BUCKETS - 1)
    return jnp.where(n < max_exact, n, large)


def dsa_mixer(q, ckv, iq, ik, iw, w_uk, w_uv, rel_bias):
    bsz, L = q.shape[0], q.shape[1]
    topk = min(IDX_TOPK_MAX, L // 4)
    nqb = L // Q_BLOCK
    q_lat = jnp.einsum('blhd,hcd->blhc', q, w_uk) * (A_HEAD_DIM ** -0.5)
    kpos = jnp.arange(L, dtype=jnp.int32)

    def to_blocks(t):
        return jnp.moveaxis(t.reshape((bsz, nqb, Q_BLOCK) + t.shape[2:]), 1, 0)

    def block(args):
        qb, iqb, iwb, start = args
        qpos = start + jnp.arange(Q_BLOCK, dtype=jnp.int32)
        isc = jax.nn.relu(jnp.einsum('bqhd,bsd->bqhs', iqb, ik))
        isc = jnp.einsum('bqhs,bqh->bqs', isc, iwb).astype(jnp.float32)
        causal = kpos[None, :] <= qpos[:, None]
        isc = jnp.where(causal[None], isc, -jnp.inf)
        _, sel = lax.top_k(isc, topk)
        kv = jax.vmap(lambda cb, ib: cb[ib])(ckv, sel)
        logits = jnp.einsum('bqhc,bqkc->bqhk', qb, kv).astype(jnp.float32)
        rel = qpos[None, :, None] - sel
        bias = rel_bias[t5_bucket(rel)].astype(jnp.float32)
        logits = logits + jnp.moveaxis(bias, -1, 2)
        logits = jnp.where((rel >= 0)[:, :, None, :], logits, -jnp.inf)
        p = jax.nn.softmax(logits, axis=-1).astype(kv.dtype)
        return jnp.einsum('bqhk,bqkc->bqhc', p, kv)

    starts = jnp.arange(nqb, dtype=jnp.int32) * Q_BLOCK
    o_lat = lax.map(block, (to_blocks(q_lat), to_blocks(iq), to_blocks(iw), starts))
    o_lat = jnp.moveaxis(o_lat, 0, 1).reshape(bsz, L, A_HEADS, A_KV_RANK)
    return jnp.einsum('blhc,hcd->blhd', o_lat, w_uv)


def hgrn2_mixer(q, f_raw, i_in, lb):
    bsz, L = q.shape[0], q.shape[1]
    nc = L // B_CHUNK
    f = lb + (1.0 - lb) * jax.nn.sigmoid(f_raw.astype(jnp.float32))
    log_f = jnp.log(f)
    k = 1.0 - f

    def to_chunks(t):
        return t.astype(jnp.float32).reshape(bsz, nc, B_CHUNK, B_HEADS, -1).transpose(1, 0, 3, 2, 4)

    causal = jnp.tril(jnp.ones((B_CHUNK, B_CHUNK), dtype=bool))

    def step(S, inp):
        qc, kc, vc, lfc = inp
        bc = jnp.cumsum(lfc, axis=2)
        o_inter = jnp.einsum('bhtk,bhkv->bhtv', qc * jnp.exp(bc), S)
        diff = bc[:, :, :, None, :] - bc[:, :, None, :, :]
        decay = jnp.exp(jnp.where(causal[:, :, None], diff, -jnp.inf))
        scores = jnp.einsum('bhtk,bhsk,bhtsk->bhts', qc, kc, decay)
        o_intra = jnp.einsum('bhts,bhsv->bhtv', scores, vc)
        b_last = bc[:, :, -1:, :]
        S = (jnp.exp(b_last[:, :, 0, :])[..., None] * S
             + jnp.einsum('bhsk,bhsv->bhkv', kc * jnp.exp(b_last - bc), vc))
        return S, o_inter + o_intra

    S0 = jnp.zeros((bsz, B_HEADS, B_KEY_DIM, B_VAL_DIM), jnp.float32)
    _, o = lax.scan(step, S0, (to_chunks(q), to_chunks(k), to_chunks(i_in), to_chunks(log_f)))
    return o.transpose(1, 0, 3, 2, 4).reshape(bsz, L, B_HEADS, B_VAL_DIM)


def moe_ffn(h, w_router, router_bias, w_e_gate, w_e_up, w_e_down, w_s_gate, w_s_up, w_s_down):
    bsz, L, D = h.shape
    T = bsz * L
    ht = h.reshape(T, D)
    scores = jax.nn.sigmoid((ht @ w_router).astype(jnp.float32))
    choice = scores + router_bias.astype(jnp.float32)
    per_group = N_EXPERTS // N_GROUPS
    grp = lax.top_k(choice.reshape(T, N_GROUPS, per_group), 2)[0].sum(-1)
    _, gsel = lax.top_k(grp, TOPK_GROUPS)
    gmask = jax.nn.one_hot(gsel, N_GROUPS, dtype=jnp.float32).sum(1) > 0
    emask = jnp.repeat(gmask, per_group, axis=1)
    _, eidx = lax.top_k(jnp.where(emask, choice, -jnp.inf), TOP_K)
    ew = jnp.take_along_axis(scores, eidx, axis=1)
    ew = ew / jnp.sum(ew, axis=-1, keepdims=True) * ROUTED_SCALE

    A = T * TOP_K
    flat_e = eidx.reshape(A)
    flat_tok = jnp.arange(A, dtype=jnp.int32) // TOP_K
    order = jnp.argsort(flat_e)
    e_sorted = flat_e[order]
    counts = jnp.bincount(flat_e, length=N_EXPERTS)
    padded = (counts + MOE_BLOCK - 1) // MOE_BLOCK * MOE_BLOCK
    pad_end = jnp.cumsum(padded)
    pad_start = pad_end - padded
    start = jnp.cumsum(counts) - counts
    dest = pad_start[e_sorted] + jnp.arange(A, dtype=jnp.int32) - start[e_sorted]
    n_rows = (A + N_EXPERTS * (MOE_BLOCK - 1) + MOE_BLOCK - 1) // MOE_BLOCK * MOE_BLOCK
    nb = n_rows // MOE_BLOCK
    row_tok = jnp.full((n_rows,), T, jnp.int32).at[dest].set(flat_tok[order])
    row_w = jnp.zeros((n_rows,), jnp.float32).at[dest].set(ew.reshape(A)[order])
    block_e = jnp.minimum(jnp.searchsorted(pad_end, jnp.arange(nb, dtype=jnp.int32) * MOE_BLOCK,
                                           side='right'), N_EXPERTS - 1)
    h_pad = jnp.concatenate([ht, jnp.zeros((1, D), ht.dtype)], axis=0)

    def expert_block(args):
        toks, e = args
        xb = h_pad[toks]
        return (jax.nn.silu(xb @ w_e_gate[e]) * (xb @ w_e_up[e])) @ w_e_down[e]

    y_rows = lax.map(expert_block, (row_tok.reshape(nb, MOE_BLOCK), block_e)).reshape(n_rows, D)
    routed = jax.ops.segment_sum(y_rows * row_w[:, None].astype(y_rows.dtype), row_tok,
                                 num_segments=T + 1)[:T]
    shared = (jax.nn.silu(ht @ w_s_gate) * (ht @ w_s_up)) @ w_s_down
    return (routed + shared).reshape(bsz, L, D)


def setup_inputs(seed: int = 0) -> dict:
    key = jax.random.key(seed)
    ks = jax.random.split(key, 26)
    D = D_MODEL

    def nrm(k, shape, scale):
        return jax.random.normal(k, shape, jnp.float32) * scale

    return {
        'x': nrm(ks[0], (BATCH, SEQ, D), 1.0),
        'c': nrm(ks[1], (BATCH, D), 1.0),
        'w_ada': nrm(ks[2], (DEPTH, D, 6 * D), 0.5 * D ** -0.5),
        'b_ada': nrm(ks[3], (DEPTH, 6 * D), 0.02),
        'norm1_g': 1.0 + nrm(ks[4], (DEPTH, D), 0.02),
        'w_in': nrm(ks[5], (DEPTH, D, D_IN), D ** -0.5),
        'ckv_norm_g': 1.0 + nrm(ks[6], (DEPTH, A_KV_RANK), 0.02),
        'idx_k_norm_g': 1.0 + nrm(ks[7], (DEPTH, IDX_DIM), 0.02),
        'w_uk': nrm(ks[8], (DEPTH, A_HEADS, A_KV_RANK, A_HEAD_DIM), A_KV_RANK ** -0.5),
        'w_uv': nrm(ks[9], (DEPTH, A_HEADS, A_KV_RANK, A_HEAD_DIM), A_KV_RANK ** -0.5),
        'rel_bias': nrm(ks[10], (REL_BUCKETS, A_HEADS), 0.5),
        'lb_logits': nrm(ks[11], (DEPTH + 1, B_HEADS * B_KEY_DIM), 0.5),
        'attn_out_norm_g': 1.0 + nrm(ks[12], (DEPTH, A_WIDTH), 0.02),
        'hgrn_out_norm_g': 1.0 + nrm(ks[13], (DEPTH, B_WIDTH), 0.02),
        'w_out': nrm(ks[14], (DEPTH, D_MIX, D), D_MIX ** -0.5),
        'norm2_g': 1.0 + nrm(ks[15], (DEPTH, D), 0.02),
        'w_router': nrm(ks[16], (DEPTH, D, N_EXPERTS), D ** -0.5),
        'router_bias': nrm(ks[17], (DEPTH, N_EXPERTS), 0.01),
        'w_e_gate': nrm(ks[18], (DEPTH, N_EXPERTS, D, D_EXPERT), D ** -0.5),
        'w_e_up': nrm(ks[19], (DEPTH, N_EXPERTS, D, D_EXPERT), D ** -0.5),
        'w_e_down': nrm(ks[20], (DEPTH, N_EXPERTS, D_EXPERT, D), D_EXPERT ** -0.5),
        'w_s_gate': nrm(ks[21], (DEPTH, D, D_SHARED), D ** -0.5),
        'w_s_up': nrm(ks[22], (DEPTH, D, D_SHARED), D ** -0.5),
        'w_s_down': nrm(ks[23], (DEPTH, D_SHARED, D), D_SHARED ** -0.5),
        'final_norm_g': 1.0 + nrm(ks[24], (D,), 0.02),
    }


def reference(x, c, w_ada, b_ada, norm1_g, w_in, ckv_norm_g, idx_k_norm_g, w_uk, w_uv, rel_bias,
              lb_logits, attn_out_norm_g, hgrn_out_norm_g, w_out, norm2_g, w_router, router_bias,
              w_e_gate, w_e_up, w_e_down, w_s_gate, w_s_up, w_s_down, final_norm_g):
    bsz, L, D = x.shape
    lb_all = jnp.cumsum(jax.nn.softmax(lb_logits.astype(jnp.float32), axis=0), axis=0)
    c_act = jax.nn.silu(c)
    for layer in range(DEPTH):
        mod = c_act @ w_ada[layer] + b_ada[layer]
        sh1, sc1, g1, sh2, sc2, g2 = jnp.split(mod, 6, axis=-1)

        h = rmsnorm(x, norm1_g[layer]) * (1.0 + sc1[:, None]) + sh1[:, None]
        proj = h @ w_in[layer]
        q_a, ckv, iq, ik, iw, q_b, f_b, i_b, g_b = split_cols(proj, IN_SIZES)
        ckv = rmsnorm(ckv, ckv_norm_g[layer])
        ik = rmsnorm(ik, idx_k_norm_g[layer])
        iw = iw * (IDX_HEADS ** -0.5 * IDX_DIM ** -0.5)
        o_a = dsa_mixer(q_a.reshape(bsz, L, A_HEADS, A_HEAD_DIM), ckv,
                        iq.reshape(bsz, L, IDX_HEADS, IDX_DIM), ik, iw,
                        w_uk[layer], w_uv[layer], rel_bias)
        o_a = rmsnorm(o_a.reshape(bsz, L, A_WIDTH), attn_out_norm_g[layer])
        lb = lb_all[layer].reshape(B_HEADS, B_KEY_DIM)
        o_b = hgrn2_mixer(q_b.reshape(bsz, L, B_HEADS, B_KEY_DIM),
                          f_b.reshape(bsz, L, B_HEADS, B_KEY_DIM),
                          i_b.reshape(bsz, L, B_HEADS, B_VAL_DIM), lb)
        o_b = rmsnorm(o_b, hgrn_out_norm_g[layer].reshape(B_HEADS, B_VAL_DIM)).astype(x.dtype)
        o_b = o_b.reshape(bsz, L, B_WIDTH) * jax.nn.silu(g_b)
        mix = jnp.concatenate([o_a, o_b], axis=-1) @ w_out[layer]
        x = x + g1[:, None] * mix

        h = rmsnorm(x, norm2_g[layer]) * (1.0 + sc2[:, None]) + sh2[:, None]
        y = moe_ffn(h, w_router[layer], router_bias[layer], w_e_gate[layer], w_e_up[layer],
                    w_e_down[layer], w_s_gate[layer], w_s_up[layer], w_s_down[layer])
        x = x + g2[:, None] * y
    return rmsnorm(x, final_norm_g)
```

```python
import functools
import math

import numpy as np
import jax
import jax.numpy as jnp
from jax import lax
from jax.experimental import pallas as pl
from jax.experimental.pallas import tpu as pltpu

F32 = jnp.float32
BF16 = jnp.bfloat16
I32 = jnp.int32
U32 = jnp.uint32

EPS = 1e-6
A_HEADS = 8
A_HEAD_DIM = 128
A_KV_RANK = 256
IDX_HEADS = 16
IDX_DIM = 64
IDX_TOPK_MAX = 256
B_HEADS = 8
B_DIM = 128
REL_BUCKETS = 32
REL_MAX_DIST = 128
N_EXPERTS = 64
TOP_K = 8
N_GROUPS = 8
TOPK_GROUPS = 4
ROUTED_SCALE = 2.5

VMEM_LIMIT_BYTES = 56 * 1024 * 1024
LANES = 128

NT_DIMS = (((1,), (1,)), ((), ()))
TN_DIMS = (((0,), (0,)), ((), ()))

INT_MIN = -2 ** 31
KEY_NEG_INF = -2139095041


def _cparams(sem):
    return pltpu.CompilerParams(dimension_semantics=sem, vmem_limit_bytes=VMEM_LIMIT_BYTES)


def _silu(v):
    return v * jax.nn.sigmoid(v)


def _pack_halves(v):
    n = v.shape[1] // 2
    lo = lax.bitcast_convert_type(v[:, :n].astype(BF16).astype(F32), U32)
    hi = lax.bitcast_convert_type(v[:, n:].astype(BF16).astype(F32), U32)
    return lax.shift_right_logical(lo, jnp.uint32(16)) | (hi & jnp.uint32(0xFFFF0000))


def _unpack_halves(w):
    left = lax.bitcast_convert_type(lax.shift_left(w, jnp.uint32(16)), F32)
    right = lax.bitcast_convert_type(w & jnp.uint32(0xFFFF0000), F32)
    return left, right


def _ada_kernel(c_ref, w_ref, b_ref, o_ref):
    a = _silu(c_ref[...]).astype(BF16)
    o_ref[...] = jnp.dot(a, w_ref[...].astype(BF16), preferred_element_type=F32) + b_ref[...]


def _ada(c, w, b, tn=1024):
    bsz, d = c.shape
    n = w.shape[1]
    return pl.pallas_call(
        _ada_kernel,
        grid=(n // tn,),
        in_specs=[pl.BlockSpec((bsz, d), lambda j: (0, 0)),
                  pl.BlockSpec((d, tn), lambda j: (0, j)),
                  pl.BlockSpec((1, tn), lambda j: (0, j))],
        out_specs=pl.BlockSpec((bsz, tn), lambda j: (0, j)),
        out_shape=jax.ShapeDtypeStruct((bsz, n), F32),
        compiler_params=_cparams(("arbitrary",)),
    )(c, w, b.reshape(1, n))


def _norm1_kernel(x_ref, mod_ref, g_ref, o_ref):
    x = x_ref[0]
    y = x * lax.rsqrt(jnp.mean(x * x, axis=-1, keepdims=True) + EPS) * g_ref[...]
    sh = mod_ref[0, 0:1, :]
    sc = mod_ref[0, 1:2, :]
    o_ref[0] = (y * (1.0 + sc) + sh).astype(o_ref.dtype)


def _norm1(x, mod3, g, tm=512):
    bsz, L, d = x.shape
    return pl.pallas_call(
        _norm1_kernel,
        grid=(bsz, L // tm),
        in_specs=[pl.BlockSpec((1, tm, d), lambda b, i: (b, i, 0)),
                  pl.BlockSpec((1, 6, d), lambda b, i: (b, 0, 0)),
                  pl.BlockSpec((1, d), lambda b, i: (0, 0))],
        out_specs=pl.BlockSpec((1, tm, d), lambda b, i: (b, i, 0)),
        out_shape=jax.ShapeDtypeStruct((bsz, L, d), BF16),
        compiler_params=_cparams(("parallel", "parallel")),
    )(x, mod3, g.reshape(1, d))


def _mm_kernel(a_ref, w_ref, o_ref):
    o_ref[...] = jnp.dot(a_ref[...], w_ref[...], preferred_element_type=F32).astype(o_ref.dtype)


def _matmul(a, w, out_dtype, tm, tn):
    m, k = a.shape
    n = w.shape[1]
    return pl.pallas_call(
        _mm_kernel,
        grid=(m // tm, n // tn),
        in_specs=[pl.BlockSpec((tm, k), lambda i, j: (i, 0)),
                  pl.BlockSpec((k, tn), lambda i, j: (0, j))],
        out_specs=pl.BlockSpec((tm, tn), lambda i, j: (i, j)),
        out_shape=jax.ShapeDtypeStruct((m, n), out_dtype),
        compiler_params=_cparams(("parallel", "arbitrary")),
    )(a, w)


def _kvnorm_kernel(aux_ref, gc_ref, gk_ref, ckv_ref, iklo_ref, ikhi_ref):
    ckv = aux_ref[:, :A_KV_RANK]
    ckv_ref[...] = (ckv * lax.rsqrt(jnp.mean(ckv * ckv, axis=-1, keepdims=True) + EPS)
                    * gc_ref[...]).astype(BF16)
    v = aux_ref[:, A_KV_RANK:A_KV_RANK + LANES]
    lane = lax.broadcasted_iota(I32, v.shape, 1)
    ik = jnp.where(lane < IDX_DIM, v, 0.0)
    ms = jnp.sum(ik * ik, axis=-1, keepdims=True) * (1.0 / IDX_DIM)
    ikn = ik * lax.rsqrt(ms + EPS) * gk_ref[...]
    iklo_ref[...] = ikn.astype(BF16)
    ikhi_ref[...] = pltpu.roll(ikn, IDX_DIM, 1).astype(BF16)


def _kvnorm(aux, gc, gk, tm=1024):
    t = aux.shape[0]
    gk_pad = jnp.concatenate([gk, jnp.zeros((LANES - IDX_DIM,), F32)]).reshape(1, LANES)
    return pl.pallas_call(
        _kvnorm_kernel,
        grid=(t // tm,),
        in_specs=[pl.BlockSpec((tm, aux.shape[1]), lambda i: (i, 0)),
                  pl.BlockSpec((1, A_KV_RANK), lambda i: (0, 0)),
                  pl.BlockSpec((1, LANES), lambda i: (0, 0))],
        out_specs=[pl.BlockSpec((tm, A_KV_RANK), lambda i: (i, 0)),
                   pl.BlockSpec((tm, LANES), lambda i: (i, 0)),
                   pl.BlockSpec((tm, LANES), lambda i: (i, 0))],
        out_shape=[jax.ShapeDtypeStruct((t, A_KV_RANK), BF16),
                   jax.ShapeDtypeStruct((t, LANES), BF16),
                   jax.ShapeDtypeStruct((t, LANES), BF16)],
        compiler_params=_cparams(("parallel",)),
    )(aux, gc.reshape(1, A_KV_RANK), gk_pad)


def _t5_bucket(rel):
    n = jnp.maximum(rel, 0)
    max_exact = REL_BUCKETS // 2
    n_large = jnp.maximum(n, max_exact).astype(F32)
    large = max_exact + (jnp.log(n_large / max_exact) / math.log(REL_MAX_DIST / max_exact)
                         * (REL_BUCKETS - max_exact)).astype(I32)
    large = jnp.minimum(large, REL_BUCKETS - 1)
    return jnp.where(n < max_exact, n, large)


def _bias_tables(rel_bias, tq):
    assert tq + 1 >= REL_MAX_DIST
    i = jnp.arange(tq, dtype=I32)[:, None]
    j = jnp.arange(tq, dtype=I32)[None, :]
    near = rel_bias[_t5_bucket(i - j)]
    prev = rel_bias[_t5_bucket(tq + i - j)]
    far = jnp.broadcast_to(rel_bias[REL_BUCKETS - 1], near.shape)
    return jnp.stack([near, prev, far]).transpose(0, 3, 1, 2).astype(F32)


def _dsa_kernel(qa_ref, iq_ref, aux_ref, iklo_ref, ikhi_ref, ckv_ref, wuk_ref, wuv_ref, bias_ref, g_ref,
                o_ref, iqs_ref, iwb_ref, key_ref, qlat_ref, m_ref, l_ref, acc_ref, *, tq, topk):
    i = pl.program_id(1)
    nh = A_HEADS
    npair = IDX_HEADS // 2

    for p in range(npair):
        iqs_ref[p * tq:(p + 1) * tq, :] = iq_ref[:, p * LANES:(p + 1) * LANES]
    iw = aux_ref[:, IDX_DIM:IDX_DIM + IDX_HEADS] * (IDX_HEADS ** -0.5 * IDX_DIM ** -0.5)
    for h in range(IDX_HEADS):
        iwb_ref[h] = jnp.broadcast_to(iw[:, h:h + 1], (tq, LANES))
    for h in range(nh):
        ql = lax.dot_general(qa_ref[:, h * A_HEAD_DIM:(h + 1) * A_HEAD_DIM], wuk_ref[h], NT_DIMS,
                             preferred_element_type=F32)
        qlat_ref[h * tq:(h + 1) * tq, :] = (ql * (A_HEAD_DIM ** -0.5)).astype(BF16)

    row = lax.broadcasted_iota(I32, (tq, tq), 0) + i * tq
    col = lax.broadcasted_iota(I32, (tq, tq), 1)
    nrep = tq // LANES

    def score_body(kc, carry):
        off = pl.multiple_of(kc * tq, tq)
        klo = iklo_ref[0, pl.ds(off, tq), :]
        khi = ikhi_ref[0, pl.ds(off, tq), :]
        iqs = iqs_ref[...]
        se = lax.dot_general(iqs, klo, NT_DIMS, preferred_element_type=F32)
        so = lax.dot_general(iqs, khi, NT_DIMS, preferred_element_type=F32)
        acc = jnp.zeros((tq, tq), F32)
        for p in range(npair):
            we = jnp.concatenate([iwb_ref[2 * p]] * nrep, axis=1)
            wo = jnp.concatenate([iwb_ref[2 * p + 1]] * nrep, axis=1)
            acc = acc + jnp.maximum(se[p * tq:(p + 1) * tq], 0.0) * we
            acc = acc + jnp.maximum(so[p * tq:(p + 1) * tq], 0.0) * wo
        bits = lax.bitcast_convert_type(acc, I32)
        key = jnp.where(bits >= 0, bits, bits ^ jnp.int32(0x7FFFFFFF))
        key_ref[kc] = jnp.where(col + off <= row, key, jnp.int32(KEY_NEG_INF))
        return carry

    lax.fori_loop(0, i + 1, score_body, 0)

    def count_ge(cand):
        def body(kc, c):
            hit = jnp.where(key_ref[kc] >= cand, 1.0, 0.0)
            for r in range(nrep):
                c = c + hit[:, r * LANES:(r + 1) * LANES]
            return c
        c = lax.fori_loop(0, i + 1, body, jnp.zeros((tq, LANES), F32))
        return jnp.sum(c, axis=1, keepdims=True)

    kf = float(topk)
    thr = jnp.where(count_ge(jnp.zeros((tq, 1), I32)) >= kf, jnp.int32(0), jnp.int32(INT_MIN))

    def bit_body(j, thr):
        cand = thr | lax.shift_left(jnp.int32(1), 30 - j)
        return jnp.where(count_ge(cand) >= kf, cand, thr)

    thr = lax.fori_loop(0, 31, bit_body, thr)

    m_ref[...] = jnp.full(m_ref.shape, -jnp.inf, F32)
    l_ref[...] = jnp.zeros(l_ref.shape, F32)
    acc_ref[...] = jnp.zeros(acc_ref.shape, F32)

    def att_body(kc, carry):
        off = pl.multiple_of(kc * tq, tq)
        ckv = ckv_ref[0, pl.ds(off, tq), :]
        s = lax.dot_general(qlat_ref[...], ckv, NT_DIMS, preferred_element_type=F32)
        key = key_ref[kc]
        sel = (key >= thr) & (key > jnp.int32(KEY_NEG_INF))
        bias = bias_ref[jnp.minimum(i - kc, 2)]
        s3 = jnp.where(sel[None], s.reshape(nh, tq, tq) + bias, -jnp.inf)
        s = s3.reshape(nh * tq, tq)
        m_old = m_ref[...]
        m_new = jnp.maximum(m_old, jnp.max(s, axis=1, keepdims=True))
        m_safe = jnp.where(m_new == -jnp.inf, 0.0, m_new)
        alpha = jnp.exp(m_old - m_safe)
        p = jnp.exp(s - m_safe)
        l_ref[...] = alpha * l_ref[...] + jnp.sum(p, axis=1, keepdims=True)
        acc_ref[...] = alpha * acc_ref[...] + jnp.dot(p.astype(BF16), ckv, preferred_element_type=F32)
        m_ref[...] = m_new
        return carry

    lax.fori_loop(0, i + 1, att_body, 0)

    o_lat = (acc_ref[...] / l_ref[...]).astype(BF16)
    outs = []
    for h in range(nh):
        outs.append(jnp.dot(o_lat[h * tq:(h + 1) * tq], wuv_ref[h], preferred_element_type=F32))
    o = jnp.concatenate(outs, axis=1)
    o = o * lax.rsqrt(jnp.mean(o * o, axis=-1, keepdims=True) + EPS) * g_ref[...]
    o_ref[...] = o.astype(o_ref.dtype)


def _dsa(g1, aux, ik_lo, ik_hi, ckv_n, w_uk, w_uv, bias_tab, g, bsz, L, tq):
    t = bsz * L
    nq = L // tq
    topk = min(IDX_TOPK_MAX, L // 4)
    aux_blk = A_KV_RANK // LANES
    kern = functools.partial(_dsa_kernel, tq=tq, topk=topk)
    width = A_HEADS * A_HEAD_DIM
    return pl.pallas_call(
        kern,
        grid=(bsz, nq),
        in_specs=[pl.BlockSpec((tq, width), lambda b, i: (b * nq + i, 0)),
                  pl.BlockSpec((tq, IDX_HEADS * IDX_DIM), lambda b, i: (b * nq + i, 1)),
                  pl.BlockSpec((tq, LANES), lambda b, i: (b * nq + i, aux_blk)),
                  pl.BlockSpec((1, L, LANES), lambda b, i: (b, 0, 0)),
                  pl.BlockSpec((1, L, LANES), lambda b, i: (b, 0, 0)),
                  pl.BlockSpec((1, L, A_KV_RANK), lambda b, i: (b, 0, 0)),
                  pl.BlockSpec((A_HEADS, A_KV_RANK, A_HEAD_DIM), lambda b, i: (0, 0, 0)),
                  pl.BlockSpec((A_HEADS, A_KV_RANK, A_HEAD_DIM), lambda b, i: (0, 0, 0)),
                  pl.BlockSpec((3, A_HEADS, tq, tq), lambda b, i: (0, 0, 0, 0)),
                  pl.BlockSpec((1, width), lambda b, i: (0, 0))],
        out_specs=pl.BlockSpec((tq, width), lambda b, i: (b * nq + i, 0)),
        out_shape=jax.ShapeDtypeStruct((t, width), BF16),
        scratch_shapes=[pltpu.VMEM((IDX_HEADS // 2 * tq, LANES), BF16),
                        pltpu.VMEM((IDX_HEADS, tq, LANES), F32),
                        pltpu.VMEM((nq, tq, tq), I32),
                        pltpu.VMEM((A_HEADS * tq, A_KV_RANK), BF16),
                        pltpu.VMEM((A_HEADS * tq, 1), F32),
                        pltpu.VMEM((A_HEADS * tq, 1), F32),
                        pltpu.VMEM((A_HEADS * tq, A_KV_RANK), F32)],
        compiler_params=_cparams(("parallel", "arbitrary")),
    )(g1, g1, aux, ik_lo.reshape(bsz, L, LANES), ik_hi.reshape(bsz, L, LANES),
      ckv_n.reshape(bsz, L, A_KV_RANK), w_uk, w_uv, bias_tab, g.reshape(1, width))


def _hgrn_kernel(q_ref, i_ref, gate_ref, f_ref, lbl_ref, ng_ref, o_ref, st_ref, *, chunk, rblk):
    @pl.when(pl.program_id(1) == 0)
    def _():
        st_ref[...] = jnp.zeros(st_ref.shape, F32)

    ll = lbl_ref[...]
    ex = jnp.exp(ll - jnp.max(ll, axis=0, keepdims=True))
    lb_all = ex[0:1] / jnp.sum(ex, axis=0, keepdims=True)

    r_i = lax.broadcasted_iota(I32, (chunk, chunk), 0)
    c_i = lax.broadcasted_iota(I32, (chunk, chunk), 1)
    tri = jnp.where(r_i >= c_i, 1.0, 0.0).astype(BF16)
    row_k = lax.broadcasted_iota(I32, (chunk, B_DIM), 0)

    for h in range(B_HEADS):
        sl = slice(h * B_DIM, (h + 1) * B_DIM)
        lb = lb_all[:, sl]
        f = lb + (1.0 - lb) * jax.nn.sigmoid(f_ref[:, sl])
        lf = jnp.log(f)
        kk = 1.0 - f
        l1 = lf.astype(BF16)
        r1 = lf - l1.astype(F32)
        l2 = r1.astype(BF16)
        l3 = (r1 - l2.astype(F32)).astype(BF16)
        cs = jnp.dot(tri, jnp.concatenate([l1, l2, l3], axis=1), preferred_element_type=F32)
        bc = cs[:, :B_DIM] + cs[:, B_DIM:2 * B_DIM] + cs[:, 2 * B_DIM:]

        q = q_ref[:, sl].astype(F32)
        v = i_ref[:, sl]
        st = st_ref[h]
        o = lax.dot_general((q * jnp.exp(bc)).astype(BF16), st.astype(BF16), NT_DIMS,
                            preferred_element_type=F32)
        parts = []
        for r in range(chunk // rblk):
            lo, hi = r * rblk, (r + 1) * rblk
            base = bc[lo - 1:lo] if r > 0 else jnp.zeros((1, B_DIM), F32)
            qt = (q[lo:hi] * jnp.exp(bc[lo:hi] - base)).astype(BF16)
            kt = (kk * jnp.exp(jnp.where(row_k < hi, base - bc, 0.0))).astype(BF16)
            parts.append(lax.dot_general(qt, kt, NT_DIMS, preferred_element_type=F32))
        sc = jnp.where(c_i <= r_i, jnp.concatenate(parts, axis=0), 0.0).astype(BF16)
        o = o + jnp.dot(sc, v, preferred_element_type=F32)

        last = bc[chunk - 1:chunk]
        kd = (kk * jnp.exp(last - bc)).astype(BF16)
        st_ref[h] = st * jnp.exp(last) + lax.dot_general(v, kd, TN_DIMS, preferred_element_type=F32)

        y = o * lax.rsqrt(jnp.mean(o * o, axis=-1, keepdims=True) + EPS) * ng_ref[:, sl]
        o_ref[:, sl] = (y * _silu(gate_ref[:, sl].astype(F32))).astype(o_ref.dtype)


def _hgrn(g1, fb, lb_logits, ng, bsz, L, chunk=128, rblk=16):
    t = bsz * L
    nc = L // chunk
    width = B_HEADS * B_DIM
    kern = functools.partial(_hgrn_kernel, chunk=chunk, rblk=rblk)
    return pl.pallas_call(
        kern,
        grid=(bsz, nc),
        in_specs=[pl.BlockSpec((chunk, width), lambda b, c: (b * nc + c, 2)),
                  pl.BlockSpec((chunk, width), lambda b, c: (b * nc + c, 3)),
                  pl.BlockSpec((chunk, width), lambda b, c: (b * nc + c, 4)),
                  pl.BlockSpec((chunk, width), lambda b, c: (b * nc + c, 0)),
                  pl.BlockSpec(lb_logits.shape, lambda b, c: (0, 0)),
                  pl.BlockSpec((1, width), lambda b, c: (0, 0))],
        out_specs=pl.BlockSpec((chunk, width), lambda b, c: (b * nc + c, 0)),
        out_shape=jax.ShapeDtypeStruct((t, width), BF16),
        scratch_shapes=[pltpu.VMEM((B_HEADS, B_DIM, B_DIM), F32)],
        compiler_params=_cparams(("parallel", "arbitrary")),
    )(g1, g1, g1, fb, lb_logits, ng.reshape(1, width))


def _out_kernel(oa_ref, ob_ref, x_ref, wa_ref, wb_ref, mod_ref, g_ref, wr_ref, x1_ref, h2_ref, lg_ref):
    mix = jnp.dot(oa_ref[...], wa_ref[...], preferred_element_type=F32)
    mix = mix + jnp.dot(ob_ref[...], wb_ref[...], preferred_element_type=F32)
    x1 = x_ref[...] + mod_ref[0, 2:3, :] * mix
    x1_ref[...] = x1
    y = x1 * lax.rsqrt(jnp.mean(x1 * x1, axis=-1, keepdims=True) + EPS) * g_ref[...]
    h2 = y * (1.0 + mod_ref[0, 4:5, :]) + mod_ref[0, 3:4, :]
    h2_ref[...] = _pack_halves(h2)
    lg_ref[...] = lax.dot_general(wr_ref[...], h2.astype(BF16), NT_DIMS, preferred_element_type=F32)


def _out(oa, ob, x2d, w_out, mod3, g, w_router_t, L, tm=256):
    t, d = x2d.shape
    half = oa.shape[1]
    ne = w_router_t.shape[0]
    return pl.pallas_call(
        _out_kernel,
        grid=(t // tm,),
        in_specs=[pl.BlockSpec((tm, half), lambda i: (i, 0)),
                  pl.BlockSpec((tm, half), lambda i: (i, 0)),
                  pl.BlockSpec((tm, d), lambda i: (i, 0)),
                  pl.BlockSpec((half, d), lambda i: (0, 0)),
                  pl.BlockSpec((half, d), lambda i: (1, 0)),
                  pl.BlockSpec((1, 6, d), lambda i: (i * tm // L, 0, 0)),
                  pl.BlockSpec((1, d), lambda i: (0, 0)),
                  pl.BlockSpec((ne, d), lambda i: (0, 0))],
        out_specs=[pl.BlockSpec((tm, d), lambda i: (i, 0)),
                   pl.BlockSpec((tm, d // 2), lambda i: (i, 0)),
                   pl.BlockSpec((ne, tm), lambda i: (0, i))],
        out_shape=[jax.ShapeDtypeStruct((t, d), F32),
                   jax.ShapeDtypeStruct((t, d // 2), U32),
                   jax.ShapeDtypeStruct((ne, t), F32)],
        compiler_params=_cparams(("parallel",)),
    )(oa, ob, x2d, w_out, w_out, mod3, g.reshape(1, d), w_router_t)


def _rows_to_tile(rows, nrow):
    n = rows[0].shape[1]
    ridx = lax.broadcasted_iota(I32, (nrow, n), 0)
    out = jnp.zeros((nrow, n), rows[0].dtype)
    for r, v in enumerate(rows):
        out = jnp.where(ridx == r, jnp.broadcast_to(v, (nrow, n)), out)
    return out


def _route_kernel(lg_ref, rb_ref, eidx_ref, ew_ref, rank_ref, cnt_ref, run_ref):
    @pl.when(pl.program_id(0) == 0)
    def _():
        run_ref[...] = jnp.zeros(run_ref.shape, F32)

    ne, tt = lg_ref.shape
    per = ne // N_GROUPS
    sc = jax.nn.sigmoid(lg_ref[...])
    ch = sc + rb_ref[...]
    neg = -jnp.inf

    sub = lax.broadcasted_iota(I32, (per, tt), 0).astype(F32)
    gsc = []
    for g in range(N_GROUPS):
        cg = ch[g * per:(g + 1) * per]
        m1 = jnp.max(cg, axis=0, keepdims=True)
        first = jnp.min(jnp.where(cg == m1, sub, float(per)), axis=0, keepdims=True)
        m2 = jnp.max(jnp.where(sub == first, neg, cg), axis=0, keepdims=True)
        gsc.append(m1 + m2)
    grp = _rows_to_tile(gsc, N_GROUPS)

    gid = lax.broadcasted_iota(I32, (N_GROUPS, tt), 0).astype(F32)
    gsel = jnp.zeros((N_GROUPS, tt), F32)
    for _ in range(TOPK_GROUPS):
        mx = jnp.max(grp, axis=0, keepdims=True)
        gi = jnp.min(jnp.where(grp == mx, gid, float(N_GROUPS)), axis=0, keepdims=True)
        pick = gid == gi
        gsel = jnp.where(pick, 1.0, gsel)
        grp = jnp.where(pick, neg, grp)

    eid = lax.broadcasted_iota(I32, (ne, tt), 0).astype(F32)
    cm = jnp.full((ne, tt), neg, F32)
    for g in range(N_GROUPS):
        in_g = (eid >= float(g * per)) & (eid < float((g + 1) * per))
        cm = jnp.where(in_g & (jnp.broadcast_to(gsel[g:g + 1], (ne, tt)) > 0.5), ch, cm)

    idx_rows, w_rows = [], []
    onehot = jnp.zeros((ne, tt), F32)
    for _ in range(TOP_K):
        mx = jnp.max(cm, axis=0, keepdims=True)
        ei = jnp.min(jnp.where(cm == mx, eid, float(ne)), axis=0, keepdims=True)
        pick = eid == ei
        idx_rows.append(ei)
        w_rows.append(jnp.sum(jnp.where(pick, sc, 0.0), axis=0, keepdims=True))
        onehot = jnp.where(pick, 1.0, onehot)
        cm = jnp.where(pick, neg, cm)
    wsum = w_rows[0]
    for w in w_rows[1:]:
        wsum = wsum + w
    w_rows = [w / wsum * ROUTED_SCALE for w in w_rows]

    a_i = lax.broadcasted_iota(I32, (tt, tt), 0)
    b_i = lax.broadcasted_iota(I32, (tt, tt), 1)
    upper = jnp.where(a_i < b_i, 1.0, 0.0).astype(BF16)
    rank_full = jnp.dot(onehot.astype(BF16), upper, preferred_element_type=F32) + run_ref[...]
    r_rows = [jnp.sum(jnp.where(eid == ei, rank_full, 0.0), axis=0, keepdims=True) for ei in idx_rows]
    run = run_ref[...] + jnp.sum(onehot, axis=1, keepdims=True)
    run_ref[...] = run

    eidx_ref[...] = _rows_to_tile(idx_rows, TOP_K).astype(I32)
    ew_ref[...] = _rows_to_tile(w_rows, TOP_K)
    rank_ref[...] = _rows_to_tile(r_rows, TOP_K).astype(I32)
    cnt_ref[...] = jnp.broadcast_to(run, cnt_ref.shape)


def _route(logits_t, router_bias, tt=512):
    ne, t = logits_t.shape
    return pl.pallas_call(
        _route_kernel,
        grid=(t // tt,),
        in_specs=[pl.BlockSpec((ne, tt), lambda i: (0, i)),
                  pl.BlockSpec((ne, 1), lambda i: (0, 0))],
        out_specs=[pl.BlockSpec((TOP_K, tt), lambda i: (0, i)),
                   pl.BlockSpec((TOP_K, tt), lambda i: (0, i)),
                   pl.BlockSpec((TOP_K, tt), lambda i: (0, i)),
                   pl.BlockSpec((ne, LANES), lambda i: (0, 0))],
        out_shape=[jax.ShapeDtypeStruct((TOP_K, t), I32),
                   jax.ShapeDtypeStruct((TOP_K, t), F32),
                   jax.ShapeDtypeStruct((TOP_K, t), I32),
                   jax.ShapeDtypeStruct((ne, LANES), F32)],
        scratch_shapes=[pltpu.VMEM((ne, 1), F32)],
        compiler_params=_cparams(("arbitrary",)),
    )(logits_t, router_bias.reshape(ne, 1))


def _dest_kernel(eidx_ref, rank_ref, ps_ref, o_ref):
    ne = ps_ref.shape[0]
    tt = eidx_ref.shape[1]
    eid = lax.broadcasted_iota(I32, (ne, tt), 0)
    ps = jnp.broadcast_to(ps_ref[...], (ne, tt))
    rows = []
    for k in range(TOP_K):
        start = jnp.sum(jnp.where(eid == eidx_ref[k:k + 1, :], ps, 0.0), axis=0, keepdims=True)
        rows.append(start + rank_ref[k:k + 1, :].astype(F32))
    o_ref[...] = _rows_to_tile(rows, TOP_K).astype(I32)


def _dest(eidx, rank, pad_start, tt=2048):
    t = eidx.shape[1]
    tt = min(tt, t)
    ne = pad_start.shape[0]
    return pl.pallas_call(
        _dest_kernel,
        grid=(t // tt,),
        in_specs=[pl.BlockSpec((TOP_K, tt), lambda i: (0, i)),
                  pl.BlockSpec((TOP_K, tt), lambda i: (0, i)),
                  pl.BlockSpec((ne, 1), lambda i: (0, 0))],
        out_specs=pl.BlockSpec((TOP_K, tt), lambda i: (0, i)),
        out_shape=jax.ShapeDtypeStruct((TOP_K, t), I32),
        compiler_params=_cparams(("parallel",)),
    )(eidx, rank, pad_start.astype(F32).reshape(ne, 1))


def _dispatch_kernel(pend_ref, padded_ref, dest_ref, h_ref, xs_ref, zbuf_ref, zsem, sem, *, td, bm):
    i = pl.program_id(0)

    def tail_copy(e):
        start = pl.multiple_of(pend_ref[e] - bm, bm)
        return pltpu.make_async_copy(zbuf_ref, xs_ref.at[pl.ds(start, bm)], zsem)

    @pl.when(i == 0)
    def _():
        zbuf_ref[...] = jnp.zeros(zbuf_ref.shape, zbuf_ref.dtype)

        def start_body(e, c):
            @pl.when(padded_ref[e] > 0)
            def _():
                tail_copy(e).start()
            return c

        def wait_body(e, c):
            @pl.when(padded_ref[e] > 0)
            def _():
                tail_copy(e).wait()
            return c

        lax.fori_loop(0, N_EXPERTS, start_body, 0)
        lax.fori_loop(0, N_EXPERTS, wait_body, 0)

        def unused_copy(b):
            return pltpu.make_async_copy(zbuf_ref, xs_ref.at[pl.ds(pl.multiple_of(b * bm, bm), bm)], zsem)

        def ustart_body(b, c):
            unused_copy(b).start()
            return c

        def uwait_body(b, c):
            unused_copy(b).wait()
            return c

        first_unused = pend_ref[N_EXPERTS - 1] // bm
        lax.fori_loop(first_unused, xs_ref.shape[0] // bm, ustart_body, 0)
        lax.fori_loop(first_unused, xs_ref.shape[0] // bm, uwait_body, 0)

    def tok_body(j, c):
        src = h_ref.at[i * td + j]
        for k in range(TOP_K):
            pltpu.make_async_copy(src, xs_ref.at[dest_ref[j * TOP_K + k]], sem).start()
        return c

    lax.fori_loop(0, td, tok_body, 0)
    for _ in range(TOP_K):
        pltpu.make_async_copy(h_ref.at[pl.ds(0, td)], xs_ref.at[pl.ds(0, td)], sem).wait()


def _dispatch(pad_end, padded, dest_flat, h2p, n_rows, bm, td=256):
    t, w = h2p.shape
    kern = functools.partial(_dispatch_kernel, td=td, bm=bm)
    return pl.pallas_call(
        kern,
        grid_spec=pltpu.PrefetchScalarGridSpec(
            num_scalar_prefetch=2,
            grid=(t // td,),
            in_specs=[pl.BlockSpec((td * TOP_K,), lambda i, *_: (i,), memory_space=pltpu.SMEM),
                      pl.BlockSpec(memory_space=pl.ANY)],
            out_specs=pl.BlockSpec(memory_space=pl.ANY),
            scratch_shapes=[pltpu.VMEM((bm, w), U32),
                            pltpu.SemaphoreType.DMA(()),
                            pltpu.SemaphoreType.DMA(())]),
        out_shape=jax.ShapeDtypeStruct((n_rows, w), U32),
        compiler_params=_cparams(("arbitrary",)),
    )(pad_end, padded, dest_flat, h2p)


def _ffn(xw, wgu_ref, wd_ref):
    half = xw.shape[1]
    f = wd_ref.shape[0]
    left, right = _unpack_halves(xw)
    gu = jnp.dot(left.astype(BF16), wgu_ref[:half, :], preferred_element_type=F32)
    gu = gu + jnp.dot(right.astype(BF16), wgu_ref[half:, :], preferred_element_type=F32)
    act = (_silu(gu[:, :f]) * gu[:, f:]).astype(BF16)
    return jnp.dot(act, wd_ref[...], preferred_element_type=F32)


def _expert_kernel(blk_ref, eid_ref, nvb_ref, x_ref, wgu_ref, wd_ref, o_ref):
    @pl.when(pl.program_id(0) < nvb_ref[0])
    def _():
        o_ref[...] = _pack_halves(_ffn(x_ref[...], wgu_ref.at[0], wd_ref.at[0]))

    @pl.when(pl.program_id(0) >= nvb_ref[0])
    def _():
        o_ref[...] = jnp.zeros(o_ref.shape, o_ref.dtype)


def _experts(blk, eid, nvb, xs, wgu, wd, bm):
    n_rows, w = xs.shape
    ne, d, f2 = wgu.shape
    return pl.pallas_call(
        _expert_kernel,
        grid_spec=pltpu.PrefetchScalarGridSpec(
            num_scalar_prefetch=3,
            grid=(n_rows // bm,),
            in_specs=[pl.BlockSpec((bm, w), lambda i, blk, eid, nvb: (blk[i], 0)),
                      pl.BlockSpec((1, d, f2), lambda i, blk, eid, nvb: (eid[i], 0, 0)),
                      pl.BlockSpec((1, f2 // 2, d), lambda i, blk, eid, nvb: (eid[i], 0, 0))],
            out_specs=pl.BlockSpec((bm, w), lambda i, blk, eid, nvb: (i, 0))),
        out_shape=jax.ShapeDtypeStruct((n_rows, w), U32),
        compiler_params=_cparams(("arbitrary",)),
    )(blk, eid, nvb, xs, wgu, wd)


def _combine_kernel(dest_ref, y_ref, h_ref, x1_ref, ew_ref, wgu_ref, wd_ref, mod_ref, g_ref, o_ref,
                    gbuf_ref, sem, *, tc):
    def tok_body(j, c):
        for k in range(TOP_K):
            pltpu.make_async_copy(y_ref.at[dest_ref[j * TOP_K + k]], gbuf_ref.at[k, j], sem).start()
        return c

    lax.fori_loop(0, tc, tok_body, 0)
    shared = _ffn(h_ref[...], wgu_ref, wd_ref)
    for k in range(TOP_K):
        pltpu.make_async_copy(y_ref.at[pl.ds(0, tc)], gbuf_ref.at[k], sem).wait()

    half = h_ref.shape[1]
    ew = ew_ref[...]
    left = shared[:, :half]
    right = shared[:, half:]
    for k in range(TOP_K):
        yl, yr = _unpack_halves(gbuf_ref[k])
        wk = ew[:, k:k + 1]
        left = left + wk * yl
        right = right + wk * yr
    x2 = x1_ref[...] + mod_ref[0, 5:6, :] * jnp.concatenate([left, right], axis=1)
    o_ref[...] = x2 * lax.rsqrt(jnp.mean(x2 * x2, axis=-1, keepdims=True) + EPS) * g_ref[...]


def _combine(dest_flat, y, h2p, x1, ew_t, wsgu, wsd, mod3, g, L, tc=256):
    t, d = x1.shape
    w = h2p.shape[1]
    kern = functools.partial(_combine_kernel, tc=tc)
    return pl.pallas_call(
        kern,
        grid=(t // tc,),
        in_specs=[pl.BlockSpec((tc * TOP_K,), lambda i: (i,), memory_space=pltpu.SMEM),
                  pl.BlockSpec(memory_space=pl.ANY),
                  pl.BlockSpec((tc, w), lambda i: (i, 0)),
                  pl.BlockSpec((tc, d), lambda i: (i, 0)),
                  pl.BlockSpec((tc, TOP_K), lambda i: (i, 0)),
                  pl.BlockSpec(wsgu.shape, lambda i: (0, 0)),
                  pl.BlockSpec(wsd.shape, lambda i: (0, 0)),
                  pl.BlockSpec((1, 6, d), lambda i: (i * tc // L, 0, 0)),
                  pl.BlockSpec((1, d), lambda i: (0, 0))],
        out_specs=pl.BlockSpec((tc, d), lambda i: (i, 0)),
        out_shape=jax.ShapeDtypeStruct((t, d), F32),
        scratch_shapes=[pltpu.VMEM((TOP_K, tc, w), U32),
                        pltpu.SemaphoreType.DMA(())],
        compiler_params=_cparams(("arbitrary",)),
    )(dest_flat, y, h2p, x1, ew_t, wsgu, wsd, mod3, g.reshape(1, d))


def _split_cols(w, sizes):
    out, off = [], 0
    for s in sizes:
        out.append(w[:, off:off + s])
        off += s
    return out


def kernel(x, c, w_ada, b_ada, norm1_g, w_in, ckv_norm_g, idx_k_norm_g, w_uk, w_uv, rel_bias, lb_logits,
           attn_out_norm_g, hgrn_out_norm_g, w_out, norm2_g, w_router, router_bias, w_e_gate, w_e_up,
           w_e_down, w_s_gate, w_s_up, w_s_down, final_norm_g):
    bsz, L, d = x.shape
    t = bsz * L
    assert w_ada.shape[0] == 1, "single-layer block"
    a_width = A_HEADS * A_HEAD_DIM
    b_width = B_HEADS * B_DIM
    sizes = (a_width, A_KV_RANK, IDX_HEADS * IDX_DIM, IDX_DIM, IDX_HEADS, b_width, b_width, b_width, b_width)
    assert w_in.shape[2] == sum(sizes)

    wq_a, wckv, wiq, wik, wiw, wq_b, wf_b, wi_b, wg_b = _split_cols(w_in[0], sizes)
    w_main = jnp.concatenate([wq_a, wiq, wq_b, wi_b, wg_b], axis=1).astype(BF16)
    w_f = wf_b.astype(BF16)
    aux_pad = LANES - IDX_DIM - IDX_HEADS
    w_aux = jnp.concatenate([wckv, wik, wiw, jnp.zeros((d, aux_pad), F32)], axis=1).astype(BF16)

    mod3 = _ada(c, w_ada[0], b_ada[0]).reshape(bsz, 6, d)
    h1 = _norm1(x, mod3, norm1_g[0]).reshape(t, d)
    g1 = _matmul(h1, w_main, BF16, tm=1024, tn=512)
    fb = _matmul(h1, w_f, F32, tm=1024, tn=512)
    aux = _matmul(h1, w_aux, F32, tm=1024, tn=w_aux.shape[1])
    ckv_n, ik_lo, ik_hi = _kvnorm(aux, ckv_norm_g[0], idx_k_norm_g[0])

    tq = min(256, L)
    o_a = _dsa(g1, aux, ik_lo, ik_hi, ckv_n, w_uk[0].astype(BF16), w_uv[0].astype(BF16),
               _bias_tables(rel_bias, tq), attn_out_norm_g[0], bsz, L, tq)
    o_b = _hgrn(g1, fb, lb_logits, hgrn_out_norm_g[0], bsz, L)

    x1, h2p, logits_t = _out(o_a, o_b, x.reshape(t, d), w_out[0].astype(BF16), mod3, norm2_g[0],
                             w_router[0].T.astype(BF16), L)

    eidx, ew, rank, cnt = _route(logits_t, router_bias[0])

    bm = 256
    counts = cnt[:, 0].astype(I32)
    padded = (counts + bm - 1) // bm * bm
    pad_end = jnp.cumsum(padded)
    pad_start = pad_end - padded
    n_rows = (t * TOP_K + N_EXPERTS * (bm - 1) + bm - 1) // bm * bm
    nb = n_rows // bm
    nvb = pad_end[-1] // bm
    blk = jnp.minimum(jnp.arange(nb, dtype=I32), nvb - 1)
    eid = jnp.minimum(jnp.searchsorted(pad_end, blk * bm, side='right'), N_EXPERTS - 1).astype(I32)

    dest = _dest(eidx, rank, pad_start)
    dest_flat = dest.T.reshape(t * TOP_K)
    xs = _dispatch(pad_end.astype(I32), padded.astype(I32), dest_flat, h2p, n_rows, bm)

    wgu = jnp.concatenate([w_e_gate[0], w_e_up[0]], axis=-1).astype(BF16)
    y = _experts(blk, eid, nvb.reshape(1).astype(I32), xs, wgu, w_e_down[0].astype(BF16), bm)

    wsgu = jnp.concatenate([w_s_gate[0], w_s_up[0]], axis=-1).astype(BF16)
    out = _combine(dest_flat, y, h2p, x1, ew.T, wsgu, w_s_down[0].astype(BF16), mod3, final_norm_g, L)
    return out.reshape(bsz, L, d)
```

```python
import functools
import math

import numpy as np
import jax
import jax.numpy as jnp
from jax import lax
from jax.experimental import pallas as pl
from jax.experimental.pallas import tpu as pltpu

F32 = jnp.float32
BF16 = jnp.bfloat16
I32 = jnp.int32
U32 = jnp.uint32

EPS = 1e-6
A_HEADS = 8
A_HEAD_DIM = 128
A_KV_RANK = 256
IDX_HEADS = 16
IDX_DIM = 64
IDX_TOPK_MAX = 256
B_HEADS = 8
B_DIM = 128
REL_BUCKETS = 32
REL_MAX_DIST = 128
N_EXPERTS = 64
TOP_K = 8
N_GROUPS = 8
TOPK_GROUPS = 4
ROUTED_SCALE = 2.5

VMEM_LIMIT_BYTES = 56 * 1024 * 1024
LANES = 128

NT_DIMS = (((1,), (1,)), ((), ()))
TN_DIMS = (((0,), (0,)), ((), ()))

INT_MIN = -2 ** 31
KEY_NEG_INF = -2139095041


def _cparams(sem):
    return pltpu.CompilerParams(dimension_semantics=sem, vmem_limit_bytes=VMEM_LIMIT_BYTES)


def _silu(v):
    return v * jax.nn.sigmoid(v)


def _pack_halves(v):
    n = v.shape[1] // 2
    lo = lax.bitcast_convert_type(v[:, :n].astype(BF16).astype(F32), U32)
    hi = lax.bitcast_convert_type(v[:, n:].astype(BF16).astype(F32), U32)
    return lax.shift_right_logical(lo, jnp.uint32(16)) | (hi & jnp.uint32(0xFFFF0000))


def _unpack_halves(w):
    left = lax.bitcast_convert_type(lax.shift_left(w, jnp.uint32(16)), F32)
    right = lax.bitcast_convert_type(w & jnp.uint32(0xFFFF0000), F32)
    return left, right


def _ada_kernel(c_ref, w_ref, b_ref, o_ref):
    a = _silu(c_ref[...]).astype(BF16)
    o_ref[...] = jnp.dot(a, w_ref[...].astype(BF16), preferred_element_type=F32) + b_ref[...]


def _ada(c, w, b, tn=1024):
    bsz, d = c.shape
    n = w.shape[1]
    return pl.pallas_call(
        _ada_kernel,
        grid=(n // tn,),
        in_specs=[pl.BlockSpec((bsz, d), lambda j: (0, 0)),
                  pl.BlockSpec((d, tn), lambda j: (0, j)),
                  pl.BlockSpec((1, tn), lambda j: (0, j))],
        out_specs=pl.BlockSpec((bsz, tn), lambda j: (0, j)),
        out_shape=jax.ShapeDtypeStruct((bsz, n), F32),
        compiler_params=_cparams(("arbitrary",)),
    )(c, w, b.reshape(1, n))


def _norm1_kernel(x_ref, mod_ref, g_ref, o_ref):
    x = x_ref[0]
    y = x * lax.rsqrt(jnp.mean(x * x, axis=-1, keepdims=True) + EPS) * g_ref[...]
    sh = mod_ref[0, 0:1, :]
    sc = mod_ref[0, 1:2, :]
    o_ref[0] = (y * (1.0 + sc) + sh).astype(o_ref.dtype)


def _norm1(x, mod3, g, tm=512):
    bsz, L, d = x.shape
    return pl.pallas_call(
        _norm1_kernel,
        grid=(bsz, L // tm),
        in_specs=[pl.BlockSpec((1, tm, d), lambda b, i: (b, i, 0)),
                  pl.BlockSpec((1, 6, d), lambda b, i: (b, 0, 0)),
                  pl.BlockSpec((1, d), lambda b, i: (0, 0))],
        out_specs=pl.BlockSpec((1, tm, d), lambda b, i: (b, i, 0)),
        out_shape=jax.ShapeDtypeStruct((bsz, L, d), BF16),
        compiler_params=_cparams(("parallel", "parallel")),
    )(x, mod3, g.reshape(1, d))


def _mm_kernel(a_ref, w_ref, o_ref):
    o_ref[...] = jnp.dot(a_ref[...], w_ref[...], preferred_element_type=F32).astype(o_ref.dtype)


def _matmul(a, w, out_dtype, tm, tn):
    m, k = a.shape
    n = w.shape[1]
    return pl.pallas_call(
        _mm_kernel,
        grid=(m // tm, n // tn),
        in_specs=[pl.BlockSpec((tm, k), lambda i, j: (i, 0)),
                  pl.BlockSpec((k, tn), lambda i, j: (0, j))],
        out_specs=pl.BlockSpec((tm, tn), lambda i, j: (i, j)),
        out_shape=jax.ShapeDtypeStruct((m, n), out_dtype),
        compiler_params=_cparams(("parallel", "arbitrary")),
    )(a, w)


def _kvnorm_kernel(aux_ref, gc_ref, gk_ref, ckv_ref, iklo_ref, ikhi_ref):
    ckv = aux_ref[:, :A_KV_RANK]
    ckv_ref[...] = (ckv * lax.rsqrt(jnp.mean(ckv * ckv, axis=-1, keepdims=True) + EPS)
                    * gc_ref[...]).astype(BF16)
    v = aux_ref[:, A_KV_RANK:A_KV_RANK + LANES]
    lane = lax.broadcasted_iota(I32, v.shape, 1)
    ik = jnp.where(lane < IDX_DIM, v, 0.0)
    ms = jnp.sum(ik * ik, axis=-1, keepdims=True) * (1.0 / IDX_DIM)
    ikn = ik * lax.rsqrt(ms + EPS) * gk_ref[...]
    iklo_ref[...] = ikn.astype(BF16)
    ikhi_ref[...] = pltpu.roll(ikn, IDX_DIM, 1).astype(BF16)


def _kvnorm(aux, gc, gk, tm=1024):
    t = aux.shape[0]
    gk_pad = jnp.concatenate([gk, jnp.zeros((LANES - IDX_DIM,), F32)]).reshape(1, LANES)
    return pl.pallas_call(
        _kvnorm_kernel,
        grid=(t // tm,),
        in_specs=[pl.BlockSpec((tm, aux.shape[1]), lambda i: (i, 0)),
                  pl.BlockSpec((1, A_KV_RANK), lambda i: (0, 0)),
                  pl.BlockSpec((1, LANES), lambda i: (0, 0))],
        out_specs=[pl.BlockSpec((tm, A_KV_RANK), lambda i: (i, 0)),
                   pl.BlockSpec((tm, LANES), lambda i: (i, 0)),
                   pl.BlockSpec((tm, LANES), lambda i: (i, 0))],
        out_shape=[jax.ShapeDtypeStruct((t, A_KV_RANK), BF16),
                   jax.ShapeDtypeStruct((t, LANES), BF16),
                   jax.ShapeDtypeStruct((t, LANES), BF16)],
        compiler_params=_cparams(("parallel",)),
    )(aux, gc.reshape(1, A_KV_RANK), gk_pad)


def _t5_bucket(rel):
    n = jnp.maximum(rel, 0)
    max_exact = REL_BUCKETS // 2
    n_large = jnp.maximum(n, max_exact).astype(F32)
    large = max_exact + (jnp.log(n_large / max_exact) / math.log(REL_MAX_DIST / max_exact)
                         * (REL_BUCKETS - max_exact)).astype(I32)
    large = jnp.minimum(large, REL_BUCKETS - 1)
    return jnp.where(n < max_exact, n, large)


def _bias_tables(rel_bias, tq):
    assert tq + 1 >= REL_MAX_DIST
    i = jnp.arange(tq, dtype=I32)[:, None]
    j = jnp.arange(tq, dtype=I32)[None, :]
    def lookup(rel):
        hot = jax.nn.one_hot(_t5_bucket(rel), REL_BUCKETS, dtype=F32)
        return jnp.einsum('ijb,bh->hij', hot, rel_bias.astype(F32), precision=lax.Precision.HIGHEST)

    near = lookup(i - j)
    prev = lookup(tq + i - j)
    far = jnp.broadcast_to(rel_bias[REL_BUCKETS - 1].astype(F32)[:, None, None], near.shape)
    return jnp.stack([near, prev, far])


def _dsa_kernel(qa_ref, iq_ref, aux_ref, iklo_ref, ikhi_ref, ckv_ref, wuk_ref, wuv_ref, bias_ref, g_ref,
                o_ref, iqs_ref, iwb_ref, key_ref, qlat_ref, m_ref, l_ref, acc_ref, *, tq, topk):
    i = pl.program_id(1)
    nh = A_HEADS
    npair = IDX_HEADS // 2

    for p in range(npair):
        iqs_ref[p * tq:(p + 1) * tq, :] = iq_ref[:, p * LANES:(p + 1) * LANES]
    iw = aux_ref[:, IDX_DIM:IDX_DIM + IDX_HEADS] * (IDX_HEADS ** -0.5 * IDX_DIM ** -0.5)
    for h in range(IDX_HEADS):
        iwb_ref[h] = jnp.broadcast_to(iw[:, h:h + 1], (tq, LANES))
    for h in range(nh):
        ql = lax.dot_general(qa_ref[:, h * A_HEAD_DIM:(h + 1) * A_HEAD_DIM], wuk_ref[h], NT_DIMS,
                             preferred_element_type=F32)
        qlat_ref[h * tq:(h + 1) * tq, :] = (ql * (A_HEAD_DIM ** -0.5)).astype(BF16)

    row = lax.broadcasted_iota(I32, (tq, tq), 0) + i * tq
    col = lax.broadcasted_iota(I32, (tq, tq), 1)
    nrep = tq // LANES

    def score_body(kc, carry):
        off = pl.multiple_of(kc * tq, tq)
        klo = iklo_ref[0, pl.ds(off, tq), :]
        khi = ikhi_ref[0, pl.ds(off, tq), :]
        iqs = iqs_ref[...]
        se = lax.dot_general(iqs, klo, NT_DIMS, preferred_element_type=F32)
        so = lax.dot_general(iqs, khi, NT_DIMS, preferred_element_type=F32)
        acc = jnp.zeros((tq, tq), F32)
        for p in range(npair):
            we = jnp.concatenate([iwb_ref[2 * p]] * nrep, axis=1)
            wo = jnp.concatenate([iwb_ref[2 * p + 1]] * nrep, axis=1)
            acc = acc + jnp.maximum(se[p * tq:(p + 1) * tq], 0.0) * we
            acc = acc + jnp.maximum(so[p * tq:(p + 1) * tq], 0.0) * wo
        bits = lax.bitcast_convert_type(acc, I32)
        key = jnp.where(bits >= 0, bits, bits ^ jnp.int32(0x7FFFFFFF))
        key_ref[kc] = jnp.where(col + off <= row, key, jnp.int32(KEY_NEG_INF))
        return carry

    lax.fori_loop(0, i + 1, score_body, 0)

    def count_ge(cand):
        def body(kc, c):
            hit = jnp.where(key_ref[kc] >= cand, 1.0, 0.0)
            for r in range(nrep):
                c = c + hit[:, r * LANES:(r + 1) * LANES]
            return c
        c = lax.fori_loop(0, i + 1, body, jnp.zeros((tq, LANES), F32))
        return jnp.sum(c, axis=1, keepdims=True)

    kf = float(topk)
    thr = jnp.where(count_ge(jnp.zeros((tq, 1), I32)) >= kf, jnp.int32(0), jnp.int32(INT_MIN))

    def bit_body(j, thr):
        cand = thr | lax.shift_left(jnp.int32(1), 30 - j)
        return jnp.where(count_ge(cand) >= kf, cand, thr)

    thr = lax.fori_loop(0, 31, bit_body, thr)

    m_ref[...] = jnp.full(m_ref.shape, -jnp.inf, F32)
    l_ref[...] = jnp.zeros(l_ref.shape, F32)
    acc_ref[...] = jnp.zeros(acc_ref.shape, F32)

    def att_body(kc, carry):
        off = pl.multiple_of(kc * tq, tq)
        ckv = ckv_ref[0, pl.ds(off, tq), :]
        s = lax.dot_general(qlat_ref[...], ckv, NT_DIMS, preferred_element_type=F32)
        key = key_ref[kc]
        sel = (key >= thr) & (key > jnp.int32(KEY_NEG_INF))
        bias = bias_ref[jnp.minimum(i - kc, 2)]
        s3 = jnp.where(sel[None], s.reshape(nh, tq, tq) + bias, -jnp.inf)
        s = s3.reshape(nh * tq, tq)
        m_old = m_ref[...]
        m_new = jnp.maximum(m_old, jnp.max(s, axis=1, keepdims=True))
        m_safe = jnp.where(m_new == -jnp.inf, 0.0, m_new)
        alpha = jnp.exp(m_old - m_safe)
        p = jnp.exp(s - m_safe)
        l_ref[...] = alpha * l_ref[...] + jnp.sum(p, axis=1, keepdims=True)
        acc_ref[...] = alpha * acc_ref[...] + jnp.dot(p.astype(BF16), ckv, preferred_element_type=F32)
        m_ref[...] = m_new
        return carry

    lax.fori_loop(0, i + 1, att_body, 0)

    o_lat = (acc_ref[...] / l_ref[...]).astype(BF16)
    outs = []
    for h in range(nh):
        outs.append(jnp.dot(o_lat[h * tq:(h + 1) * tq], wuv_ref[h], preferred_element_type=F32))
    o = jnp.concatenate(outs, axis=1)
    o = o * lax.rsqrt(jnp.mean(o * o, axis=-1, keepdims=True) + EPS) * g_ref[...]
    o_ref[...] = o.astype(o_ref.dtype)


def _dsa(g1, aux, ik_lo, ik_hi, ckv_n, w_uk, w_uv, bias_tab, g, bsz, L, tq):
    t = bsz * L
    nq = L // tq
    topk = min(IDX_TOPK_MAX, L // 4)
    aux_blk = A_KV_RANK // LANES
    kern = functools.partial(_dsa_kernel, tq=tq, topk=topk)
    width = A_HEADS * A_HEAD_DIM
    return pl.pallas_call(
        kern,
        grid=(bsz, nq),
        in_specs=[pl.BlockSpec((tq, width), lambda b, i: (b * nq + i, 0)),
                  pl.BlockSpec((tq, IDX_HEADS * IDX_DIM), lambda b, i: (b * nq + i, 1)),
                  pl.BlockSpec((tq, LANES), lambda b, i: (b * nq + i, aux_blk)),
                  pl.BlockSpec((1, L, LANES), lambda b, i: (b, 0, 0)),
                  pl.BlockSpec((1, L, LANES), lambda b, i: (b, 0, 0)),
                  pl.BlockSpec((1, L, A_KV_RANK), lambda b, i: (b, 0, 0)),
                  pl.BlockSpec((A_HEADS, A_KV_RANK, A_HEAD_DIM), lambda b, i: (0, 0, 0)),
                  pl.BlockSpec((A_HEADS, A_KV_RANK, A_HEAD_DIM), lambda b, i: (0, 0, 0)),
                  pl.BlockSpec((3, A_HEADS, tq, tq), lambda b, i: (0, 0, 0, 0)),
                  pl.BlockSpec((1, width), lambda b, i: (0, 0))],
        out_specs=pl.BlockSpec((tq, width), lambda b, i: (b * nq + i, 0)),
        out_shape=jax.ShapeDtypeStruct((t, width), BF16),
        scratch_shapes=[pltpu.VMEM((IDX_HEADS // 2 * tq, LANES), BF16),
                        pltpu.VMEM((IDX_HEADS, tq, LANES), F32),
                        pltpu.VMEM((nq, tq, tq), I32),
                        pltpu.VMEM((A_HEADS * tq, A_KV_RANK), BF16),
                        pltpu.VMEM((A_HEADS * tq, 1), F32),
                        pltpu.VMEM((A_HEADS * tq, 1), F32),
                        pltpu.VMEM((A_HEADS * tq, A_KV_RANK), F32)],
        compiler_params=_cparams(("parallel", "arbitrary")),
    )(g1, g1, aux, ik_lo.reshape(bsz, L, LANES), ik_hi.reshape(bsz, L, LANES),
      ckv_n.reshape(bsz, L, A_KV_RANK), w_uk, w_uv, bias_tab, g.reshape(1, width))


def _hgrn_kernel(q_ref, i_ref, gate_ref, f_ref, lbl_ref, ng_ref, o_ref, st_ref, *, chunk, rblk):
    @pl.when(pl.program_id(1) == 0)
    def _():
        st_ref[...] = jnp.zeros(st_ref.shape, F32)

    ll = lbl_ref[...]
    ex = jnp.exp(ll - jnp.max(ll, axis=0, keepdims=True))
    lb_all = ex[0:1] / jnp.sum(ex, axis=0, keepdims=True)

    r_i = lax.broadcasted_iota(I32, (chunk, chunk), 0)
    c_i = lax.broadcasted_iota(I32, (chunk, chunk), 1)
    tri = jnp.where(r_i >= c_i, 1.0, 0.0).astype(BF16)
    row_k = lax.broadcasted_iota(I32, (chunk, B_DIM), 0)

    for h in range(B_HEADS):
        sl = slice(h * B_DIM, (h + 1) * B_DIM)
        lb = lb_all[:, sl]
        f = lb + (1.0 - lb) * jax.nn.sigmoid(f_ref[:, sl])
        lf = jnp.log(f)
        kk = 1.0 - f
        l1 = lf.astype(BF16)
        r1 = lf - l1.astype(F32)
        l2 = r1.astype(BF16)
        l3 = (r1 - l2.astype(F32)).astype(BF16)
        cs = jnp.dot(tri, jnp.concatenate([l1, l2, l3], axis=1), preferred_element_type=F32)
        bc = cs[:, :B_DIM] + cs[:, B_DIM:2 * B_DIM] + cs[:, 2 * B_DIM:]

        q = q_ref[:, sl].astype(F32)
        v = i_ref[:, sl]
        st = st_ref[h]
        o = lax.dot_general((q * jnp.exp(bc)).astype(BF16), st.astype(BF16), NT_DIMS,
                            preferred_element_type=F32)
        parts = []
        for r in range(chunk // rblk):
            lo, hi = r * rblk, (r + 1) * rblk
            base = bc[lo - 1:lo] if r > 0 else jnp.zeros((1, B_DIM), F32)
            qt = (q[lo:hi] * jnp.exp(bc[lo:hi] - base)).astype(BF16)
            kt = (kk * jnp.exp(jnp.where(row_k < hi, base - bc, 0.0))).astype(BF16)
            parts.append(lax.dot_general(qt, kt, NT_DIMS, preferred_element_type=F32))
        sc = jnp.where(c_i <= r_i, jnp.concatenate(parts, axis=0), 0.0).astype(BF16)
        o = o + jnp.dot(sc, v, preferred_element_type=F32)

        last = bc[chunk - 1:chunk]
        kd = (kk * jnp.exp(last - bc)).astype(BF16)
        st_ref[h] = st * jnp.exp(last) + lax.dot_general(v, kd, TN_DIMS, preferred_element_type=F32)

        y = o * lax.rsqrt(jnp.mean(o * o, axis=-1, keepdims=True) + EPS) * ng_ref[:, sl]
        o_ref[:, sl] = (y * _silu(gate_ref[:, sl].astype(F32))).astype(o_ref.dtype)


def _hgrn(g1, fb, lb_logits, ng, bsz, L, chunk=128, rblk=16):
    t = bsz * L
    nc = L // chunk
    width = B_HEADS * B_DIM
    kern = functools.partial(_hgrn_kernel, chunk=chunk, rblk=rblk)
    return pl.pallas_call(
        kern,
        grid=(bsz, nc),
        in_specs=[pl.BlockSpec((chunk, width), lambda b, c: (b * nc + c, 2)),
                  pl.BlockSpec((chunk, width), lambda b, c: (b * nc + c, 3)),
                  pl.BlockSpec((chunk, width), lambda b, c: (b * nc + c, 4)),
                  pl.BlockSpec((chunk, width), lambda b, c: (b * nc + c, 0)),
                  pl.BlockSpec(lb_logits.shape, lambda b, c: (0, 0)),
                  pl.BlockSpec((1, width), lambda b, c: (0, 0))],
        out_specs=pl.BlockSpec((chunk, width), lambda b, c: (b * nc + c, 0)),
        out_shape=jax.ShapeDtypeStruct((t, width), BF16),
        scratch_shapes=[pltpu.VMEM((B_HEADS, B_DIM, B_DIM), F32)],
        compiler_params=_cparams(("parallel", "arbitrary")),
    )(g1, g1, g1, fb, lb_logits, ng.reshape(1, width))


def _out_kernel(oa_ref, ob_ref, x_ref, wa_ref, wb_ref, mod_ref, g_ref, wr_ref, x1_ref, h2_ref, lg_ref):
    mix = jnp.dot(oa_ref[...], wa_ref[...], preferred_element_type=F32)
    mix = mix + jnp.dot(ob_ref[...], wb_ref[...], preferred_element_type=F32)
    x1 = x_ref[...] + mod_ref[0, 2:3, :] * mix
    x1_ref[...] = x1
    y = x1 * lax.rsqrt(jnp.mean(x1 * x1, axis=-1, keepdims=True) + EPS) * g_ref[...]
    h2 = y * (1.0 + mod_ref[0, 4:5, :]) + mod_ref[0, 3:4, :]
    h2_ref[...] = _pack_halves(h2)
    lg_ref[...] = lax.dot_general(wr_ref[...], h2.astype(BF16), NT_DIMS, preferred_element_type=F32)


def _out(oa, ob, x2d, w_out, mod3, g, w_router_t, L, tm=256):
    t, d = x2d.shape
    half = oa.shape[1]
    ne = w_router_t.shape[0]
    return pl.pallas_call(
        _out_kernel,
        grid=(t // tm,),
        in_specs=[pl.BlockSpec((tm, half), lambda i: (i, 0)),
                  pl.BlockSpec((tm, half), lambda i: (i, 0)),
                  pl.BlockSpec((tm, d), lambda i: (i, 0)),
                  pl.BlockSpec((half, d), lambda i: (0, 0)),
                  pl.BlockSpec((half, d), lambda i: (1, 0)),
                  pl.BlockSpec((1, 6, d), lambda i: (i * tm // L, 0, 0)),
                  pl.BlockSpec((1, d), lambda i: (0, 0)),
                  pl.BlockSpec((ne, d), lambda i: (0, 0))],
        out_specs=[pl.BlockSpec((tm, d), lambda i: (i, 0)),
                   pl.BlockSpec((tm, d // 2), lambda i: (i, 0)),
                   pl.BlockSpec((ne, tm), lambda i: (0, i))],
        out_shape=[jax.ShapeDtypeStruct((t, d), F32),
                   jax.ShapeDtypeStruct((t, d // 2), U32),
                   jax.ShapeDtypeStruct((ne, t), F32)],
        compiler_params=_cparams(("parallel",)),
    )(oa, ob, x2d, w_out, w_out, mod3, g.reshape(1, d), w_router_t)


def _rows_to_tile(rows, nrow):
    n = rows[0].shape[1]
    ridx = lax.broadcasted_iota(I32, (nrow, n), 0)
    out = jnp.zeros((nrow, n), rows[0].dtype)
    for r, v in enumerate(rows):
        out = jnp.where(ridx == r, jnp.broadcast_to(v, (nrow, n)), out)
    return out


def _route_kernel(lg_ref, rb_ref, eidx_ref, ew_ref, rank_ref, cnt_ref, run_ref):
    @pl.when(pl.program_id(0) == 0)
    def _():
        run_ref[...] = jnp.zeros(run_ref.shape, F32)

    ne, tt = lg_ref.shape
    per = ne // N_GROUPS
    sc = jax.nn.sigmoid(lg_ref[...])
    ch = sc + rb_ref[...]
    neg = -jnp.inf

    sub = lax.broadcasted_iota(I32, (per, tt), 0).astype(F32)
    gsc = []
    for g in range(N_GROUPS):
        cg = ch[g * per:(g + 1) * per]
        m1 = jnp.max(cg, axis=0, keepdims=True)
        first = jnp.min(jnp.where(cg == m1, sub, float(per)), axis=0, keepdims=True)
        m2 = jnp.max(jnp.where(sub == first, neg, cg), axis=0, keepdims=True)
        gsc.append(m1 + m2)
    grp = _rows_to_tile(gsc, N_GROUPS)

    gid = lax.broadcasted_iota(I32, (N_GROUPS, tt), 0).astype(F32)
    gsel = jnp.zeros((N_GROUPS, tt), F32)
    for _ in range(TOPK_GROUPS):
        mx = jnp.max(grp, axis=0, keepdims=True)
        gi = jnp.min(jnp.where(grp == mx, gid, float(N_GROUPS)), axis=0, keepdims=True)
        pick = gid == gi
        gsel = jnp.where(pick, 1.0, gsel)
        grp = jnp.where(pick, neg, grp)

    eid = lax.broadcasted_iota(I32, (ne, tt), 0).astype(F32)
    cm = jnp.full((ne, tt), neg, F32)
    for g in range(N_GROUPS):
        in_g = (eid >= float(g * per)) & (eid < float((g + 1) * per))
        cm = jnp.where(in_g & (jnp.broadcast_to(gsel[g:g + 1], (ne, tt)) > 0.5), ch, cm)

    idx_rows, w_rows = [], []
    onehot = jnp.zeros((ne, tt), F32)
    for _ in range(TOP_K):
        mx = jnp.max(cm, axis=0, keepdims=True)
        ei = jnp.min(jnp.where(cm == mx, eid, float(ne)), axis=0, keepdims=True)
        pick = eid == ei
        idx_rows.append(ei)
        w_rows.append(jnp.sum(jnp.where(pick, sc, 0.0), axis=0, keepdims=True))
        onehot = jnp.where(pick, 1.0, onehot)
        cm = jnp.where(pick, neg, cm)
    wsum = w_rows[0]
    for w in w_rows[1:]:
        wsum = wsum + w
    w_rows = [w / wsum * ROUTED_SCALE for w in w_rows]

    a_i = lax.broadcasted_iota(I32, (tt, tt), 0)
    b_i = lax.broadcasted_iota(I32, (tt, tt), 1)
    upper = jnp.where(a_i < b_i, 1.0, 0.0).astype(BF16)
    rank_full = jnp.dot(onehot.astype(BF16), upper, preferred_element_type=F32) + run_ref[...]
    r_rows = [jnp.sum(jnp.where(eid == ei, rank_full, 0.0), axis=0, keepdims=True) for ei in idx_rows]
    run = run_ref[...] + jnp.sum(onehot, axis=1, keepdims=True)
    run_ref[...] = run

    eidx_ref[...] = _rows_to_tile(idx_rows, TOP_K).astype(I32)
    ew_ref[...] = _rows_to_tile(w_rows, TOP_K)
    rank_ref[...] = _rows_to_tile(r_rows, TOP_K).astype(I32)
    cnt_ref[...] = jnp.broadcast_to(run, cnt_ref.shape)


def _route(logits_t, router_bias, tt=512):
    ne, t = logits_t.shape
    return pl.pallas_call(
        _route_kernel,
        grid=(t // tt,),
        in_specs=[pl.BlockSpec((ne, tt), lambda i: (0, i)),
                  pl.BlockSpec((ne, 1), lambda i: (0, 0))],
        out_specs=[pl.BlockSpec((TOP_K, tt), lambda i: (0, i)),
                   pl.BlockSpec((TOP_K, tt), lambda i: (0, i)),
                   pl.BlockSpec((TOP_K, tt), lambda i: (0, i)),
                   pl.BlockSpec((ne, LANES), lambda i: (0, 0))],
        out_shape=[jax.ShapeDtypeStruct((TOP_K, t), I32),
                   jax.ShapeDtypeStruct((TOP_K, t), F32),
                   jax.ShapeDtypeStruct((TOP_K, t), I32),
                   jax.ShapeDtypeStruct((ne, LANES), F32)],
        scratch_shapes=[pltpu.VMEM((ne, 1), F32)],
        compiler_params=_cparams(("arbitrary",)),
    )(logits_t, router_bias.reshape(ne, 1))


def _dest_kernel(eidx_ref, rank_ref, ps_ref, o_ref):
    ne = ps_ref.shape[0]
    tt = eidx_ref.shape[1]
    eid = lax.broadcasted_iota(I32, (ne, tt), 0)
    ps = jnp.broadcast_to(ps_ref[...], (ne, tt))
    rows = []
    for k in range(TOP_K):
        start = jnp.sum(jnp.where(eid == eidx_ref[k:k + 1, :], ps, 0.0), axis=0, keepdims=True)
        rows.append(start + rank_ref[k:k + 1, :].astype(F32))
    o_ref[...] = _rows_to_tile(rows, TOP_K).astype(I32)


def _dest(eidx, rank, pad_start, tt=2048):
    t = eidx.shape[1]
    tt = min(tt, t)
    ne = pad_start.shape[0]
    return pl.pallas_call(
        _dest_kernel,
        grid=(t // tt,),
        in_specs=[pl.BlockSpec((TOP_K, tt), lambda i: (0, i)),
                  pl.BlockSpec((TOP_K, tt), lambda i: (0, i)),
                  pl.BlockSpec((ne, 1), lambda i: (0, 0))],
        out_specs=pl.BlockSpec((TOP_K, tt), lambda i: (0, i)),
        out_shape=jax.ShapeDtypeStruct((TOP_K, t), I32),
        compiler_params=_cparams(("parallel",)),
    )(eidx, rank, pad_start.astype(F32).reshape(ne, 1))


def _dispatch_kernel(pend_ref, padded_ref, dest_ref, h_ref, xs_ref, zbuf_ref, zsem, sem, *, td, bm):
    i = pl.program_id(0)

    def tail_copy(e):
        start = pl.multiple_of(pend_ref[e] - bm, bm)
        return pltpu.make_async_copy(zbuf_ref, xs_ref.at[pl.ds(start, bm)], zsem)

    @pl.when(i == 0)
    def _():
        zbuf_ref[...] = jnp.zeros(zbuf_ref.shape, zbuf_ref.dtype)

        def start_body(e, c):
            @pl.when(padded_ref[e] > 0)
            def _():
                tail_copy(e).start()
            return c

        def wait_body(e, c):
            @pl.when(padded_ref[e] > 0)
            def _():
                tail_copy(e).wait()
            return c

        lax.fori_loop(0, N_EXPERTS, start_body, 0)
        lax.fori_loop(0, N_EXPERTS, wait_body, 0)

        def unused_copy(b):
            return pltpu.make_async_copy(zbuf_ref, xs_ref.at[pl.ds(pl.multiple_of(b * bm, bm), bm)], zsem)

        def ustart_body(b, c):
            unused_copy(b).start()
            return c

        def uwait_body(b, c):
            unused_copy(b).wait()
            return c

        first_unused = pend_ref[N_EXPERTS - 1] // bm
        lax.fori_loop(first_unused, xs_ref.shape[0] // bm, ustart_body, 0)
        lax.fori_loop(first_unused, xs_ref.shape[0] // bm, uwait_body, 0)

    def tok_body(j, c):
        src = h_ref.at[j]
        for k in range(TOP_K):
            pltpu.make_async_copy(src, xs_ref.at[dest_ref[j * TOP_K + k]], sem).start()
        return c

    lax.fori_loop(0, td, tok_body, 0)
    for _ in range(TOP_K):
        pltpu.make_async_copy(h_ref, xs_ref.at[pl.ds(0, td)], sem).wait()


def _dispatch(pad_end, padded, dest_flat, h2p, n_rows, bm, td=256):
    t, w = h2p.shape
    kern = functools.partial(_dispatch_kernel, td=td, bm=bm)
    return pl.pallas_call(
        kern,
        grid_spec=pltpu.PrefetchScalarGridSpec(
            num_scalar_prefetch=2,
            grid=(t // td,),
            in_specs=[pl.BlockSpec((td * TOP_K,), lambda i, *_: (i,), memory_space=pltpu.SMEM),
                      pl.BlockSpec((td, w), lambda i, *_: (i, 0))],
            out_specs=pl.BlockSpec(memory_space=pl.ANY),
            scratch_shapes=[pltpu.VMEM((bm, w), U32),
                            pltpu.SemaphoreType.DMA(()),
                            pltpu.SemaphoreType.DMA(())]),
        out_shape=jax.ShapeDtypeStruct((n_rows, w), U32),
        compiler_params=_cparams(("arbitrary",)),
    )(pad_end, padded, dest_flat, h2p)


def _ffn(xw, wgu_ref, wd_ref):
    half = xw.shape[1]
    f = wd_ref.shape[0]
    left, right = _unpack_halves(xw)
    gu = jnp.dot(left.astype(BF16), wgu_ref[:half, :], preferred_element_type=F32)
    gu = gu + jnp.dot(right.astype(BF16), wgu_ref[half:, :], preferred_element_type=F32)
    act = (_silu(gu[:, :f]) * gu[:, f:]).astype(BF16)
    return jnp.dot(act, wd_ref[...], preferred_element_type=F32)


def _expert_kernel(blk_ref, eid_ref, nvb_ref, x_ref, wgu_ref, wd_ref, o_ref):
    @pl.when(pl.program_id(0) < nvb_ref[0])
    def _():
        o_ref[...] = _pack_halves(_ffn(x_ref[...], wgu_ref.at[0], wd_ref.at[0]))

    @pl.when(pl.program_id(0) >= nvb_ref[0])
    def _():
        o_ref[...] = jnp.zeros(o_ref.shape, o_ref.dtype)


def _experts(blk, eid, nvb, xs, wgu, wd, bm):
    n_rows, w = xs.shape
    ne, d, f2 = wgu.shape
    return pl.pallas_call(
        _expert_kernel,
        grid_spec=pltpu.PrefetchScalarGridSpec(
            num_scalar_prefetch=3,
            grid=(n_rows // bm,),
            in_specs=[pl.BlockSpec((bm, w), lambda i, blk, eid, nvb: (blk[i], 0)),
                      pl.BlockSpec((1, d, f2), lambda i, blk, eid, nvb: (eid[i], 0, 0)),
                      pl.BlockSpec((1, f2 // 2, d), lambda i, blk, eid, nvb: (eid[i], 0, 0))],
            out_specs=pl.BlockSpec((bm, w), lambda i, blk, eid, nvb: (i, 0))),
        out_shape=jax.ShapeDtypeStruct((n_rows, w), U32),
        compiler_params=_cparams(("arbitrary",)),
    )(blk, eid, nvb, xs, wgu, wd)


def _combine_kernel(dest_ref, y_ref, h_ref, x1_ref, ew_ref, wgu_ref, wd_ref, mod_ref, g_ref, o_ref,
                    gbuf_ref, sem, *, tc):
    def tok_body(j, c):
        for k in range(TOP_K):
            pltpu.make_async_copy(y_ref.at[dest_ref[j * TOP_K + k]], gbuf_ref.at[k, j], sem).start()
        return c

    lax.fori_loop(0, tc, tok_body, 0)
    shared = _ffn(h_ref[...], wgu_ref, wd_ref)
    for k in range(TOP_K):
        pltpu.make_async_copy(y_ref.at[pl.ds(0, tc)], gbuf_ref.at[k], sem).wait()

    half = h_ref.shape[1]
    ew = ew_ref[...]
    left = shared[:, :half]
    right = shared[:, half:]
    for k in range(TOP_K):
        yl, yr = _unpack_halves(gbuf_ref[k])
        wk = ew[:, k:k + 1]
        left = left + wk * yl
        right = right + wk * yr
    x2 = x1_ref[...] + mod_ref[0, 5:6, :] * jnp.concatenate([left, right], axis=1)
    o_ref[...] = x2 * lax.rsqrt(jnp.mean(x2 * x2, axis=-1, keepdims=True) + EPS) * g_ref[...]


def _combine(dest_flat, y, h2p, x1, ew_t, wsgu, wsd, mod3, g, L, tc=256):
    t, d = x1.shape
    w = h2p.shape[1]
    kern = functools.partial(_combine_kernel, tc=tc)
    return pl.pallas_call(
        kern,
        grid=(t // tc,),
        in_specs=[pl.BlockSpec((tc * TOP_K,), lambda i: (i,), memory_space=pltpu.SMEM),
                  pl.BlockSpec(memory_space=pl.ANY),
                  pl.BlockSpec((tc, w), lambda i: (i, 0)),
                  pl.BlockSpec((tc, d), lambda i: (i, 0)),
                  pl.BlockSpec((tc, TOP_K), lambda i: (i, 0)),
                  pl.BlockSpec(wsgu.shape, lambda i: (0, 0)),
                  pl.BlockSpec(wsd.shape, lambda i: (0, 0)),
                  pl.BlockSpec((1, 6, d), lambda i: (i * tc // L, 0, 0)),
                  pl.BlockSpec((1, d), lambda i: (0, 0))],
        out_specs=pl.BlockSpec((tc, d), lambda i: (i, 0)),
        out_shape=jax.ShapeDtypeStruct((t, d), F32),
        scratch_shapes=[pltpu.VMEM((TOP_K, tc, w), U32),
                        pltpu.SemaphoreType.DMA(())],
        compiler_params=_cparams(("arbitrary",)),
    )(dest_flat, y, h2p, x1, ew_t, wsgu, wsd, mod3, g.reshape(1, d))


def _split_cols(w, sizes):
    out, off = [], 0
    for s in sizes:
        out.append(w[:, off:off + s])
        off += s
    return out


def kernel(x, c, w_ada, b_ada, norm1_g, w_in, ckv_norm_g, idx_k_norm_g, w_uk, w_uv, rel_bias, lb_logits,
           attn_out_norm_g, hgrn_out_norm_g, w_out, norm2_g, w_router, router_bias, w_e_gate, w_e_up,
           w_e_down, w_s_gate, w_s_up, w_s_down, final_norm_g):
    bsz, L, d = x.shape
    t = bsz * L
    assert w_ada.shape[0] == 1, "single-layer block"
    a_width = A_HEADS * A_HEAD_DIM
    b_width = B_HEADS * B_DIM
    sizes = (a_width, A_KV_RANK, IDX_HEADS * IDX_DIM, IDX_DIM, IDX_HEADS, b_width, b_width, b_width, b_width)
    assert w_in.shape[2] == sum(sizes)

    wq_a, wckv, wiq, wik, wiw, wq_b, wf_b, wi_b, wg_b = _split_cols(w_in[0], sizes)
    w_main = jnp.concatenate([wq_a, wiq, wq_b, wi_b, wg_b], axis=1).astype(BF16)
    w_f = wf_b.astype(BF16)
    aux_pad = LANES - IDX_DIM - IDX_HEADS
    w_aux = jnp.concatenate([wckv, wik, wiw, jnp.zeros((d, aux_pad), F32)], axis=1).astype(BF16)

    mod3 = _ada(c, w_ada[0], b_ada[0]).reshape(bsz, 6, d)
    h1 = _norm1(x, mod3, norm1_g[0]).reshape(t, d)
    g1 = _matmul(h1, w_main, BF16, tm=1024, tn=512)
    fb = _matmul(h1, w_f, F32, tm=1024, tn=512)
    aux = _matmul(h1, w_aux, F32, tm=1024, tn=w_aux.shape[1])
    ckv_n, ik_lo, ik_hi = _kvnorm(aux, ckv_norm_g[0], idx_k_norm_g[0])

    tq = min(256, L)
    o_a = _dsa(g1, aux, ik_lo, ik_hi, ckv_n, w_uk[0].astype(BF16), w_uv[0].astype(BF16),
               _bias_tables(rel_bias, tq), attn_out_norm_g[0], bsz, L, tq)
    o_b = _hgrn(g1, fb, lb_logits, hgrn_out_norm_g[0], bsz, L)

    x1, h2p, logits_t = _out(o_a, o_b, x.reshape(t, d), w_out[0].astype(BF16), mod3, norm2_g[0],
                             w_router[0].T.astype(BF16), L)

    eidx, ew, rank, cnt = _route(logits_t, router_bias[0])

    bm = 256
    counts = cnt[:, 0].astype(I32)
    padded = (counts + bm - 1) // bm * bm
    pad_end = jnp.cumsum(padded)
    pad_start = pad_end - padded
    n_rows = (t * TOP_K + N_EXPERTS * (bm - 1) + bm - 1) // bm * bm
    nb = n_rows // bm
    nvb = pad_end[-1] // bm
    blk = jnp.minimum(jnp.arange(nb, dtype=I32), nvb - 1)
    eid = jnp.minimum(jnp.sum((pad_end[None, :] <= (blk * bm)[:, None]).astype(I32), axis=1), N_EXPERTS - 1)

    dest = _dest(eidx, rank, pad_start)
    dest_flat = dest.T.reshape(t * TOP_K)
    xs = _dispatch(pad_end.astype(I32), padded.astype(I32), dest_flat, h2p, n_rows, bm)

    wgu = jnp.concatenate([w_e_gate[0], w_e_up[0]], axis=-1).astype(BF16)
    y = _experts(blk, eid, nvb.reshape(1).astype(I32), xs, wgu, w_e_down[0].astype(BF16), bm)

    wsgu = jnp.concatenate([w_s_gate[0], w_s_up[0]], axis=-1).astype(BF16)
    out = _combine(dest_flat, y, h2p, x1, ew.T, wsgu, w_s_down[0].astype(BF16), mod3, final_norm_g, L)
    return out.reshape(bsz, L, d)
```

```python
import functools
import math

import numpy as np
import jax
import jax.numpy as jnp
from jax import lax
from jax.experimental import pallas as pl
from jax.experimental.pallas import tpu as pltpu

F32 = jnp.float32
BF16 = jnp.bfloat16
I32 = jnp.int32
U32 = jnp.uint32

EPS = 1e-6
A_HEADS = 8
A_HEAD_DIM = 128
A_KV_RANK = 256
IDX_HEADS = 16
IDX_DIM = 64
IDX_TOPK_MAX = 256
B_HEADS = 8
B_DIM = 128
REL_BUCKETS = 32
REL_MAX_DIST = 128
N_EXPERTS = 64
TOP_K = 8
N_GROUPS = 8
TOPK_GROUPS = 4
ROUTED_SCALE = 2.5

VMEM_LIMIT_BYTES = 56 * 1024 * 1024
LANES = 128

NT_DIMS = (((1,), (1,)), ((), ()))
TN_DIMS = (((0,), (0,)), ((), ()))

LOG2E = math.log2(math.e)
INT_MIN = -2 ** 31
KEY_NEG_INF = -2139095041


def _cparams(sem):
    return pltpu.CompilerParams(dimension_semantics=sem, vmem_limit_bytes=VMEM_LIMIT_BYTES)


def _silu(v):
    return v * jax.nn.sigmoid(v)


def _pack_halves(v):
    n = v.shape[1] // 2
    lo = lax.bitcast_convert_type(v[:, :n].astype(BF16).astype(F32), U32)
    hi = lax.bitcast_convert_type(v[:, n:].astype(BF16).astype(F32), U32)
    return lax.shift_right_logical(lo, jnp.uint32(16)) | (hi & jnp.uint32(0xFFFF0000))


def _unpack_halves(w):
    left = lax.bitcast_convert_type(lax.shift_left(w, jnp.uint32(16)), F32)
    right = lax.bitcast_convert_type(w & jnp.uint32(0xFFFF0000), F32)
    return left, right


def _ada_kernel(c_ref, w_ref, b_ref, o_ref):
    a = _silu(c_ref[...]).astype(BF16)
    o_ref[...] = jnp.dot(a, w_ref[...].astype(BF16), preferred_element_type=F32) + b_ref[...]


def _ada(c, w, b, tn=1024):
    bsz, d = c.shape
    n = w.shape[1]
    return pl.pallas_call(
        _ada_kernel,
        grid=(n // tn,),
        in_specs=[pl.BlockSpec((bsz, d), lambda j: (0, 0)),
                  pl.BlockSpec((d, tn), lambda j: (0, j)),
                  pl.BlockSpec((1, tn), lambda j: (0, j))],
        out_specs=pl.BlockSpec((bsz, tn), lambda j: (0, j)),
        out_shape=jax.ShapeDtypeStruct((bsz, n), F32),
        compiler_params=_cparams(("arbitrary",)),
    )(c, w, b.reshape(1, n))


def _norm1_kernel(x_ref, mod_ref, g_ref, o_ref):
    x = x_ref[0]
    y = x * lax.rsqrt(jnp.mean(x * x, axis=-1, keepdims=True) + EPS) * g_ref[...]
    sh = mod_ref[0, 0:1, :]
    sc = mod_ref[0, 1:2, :]
    o_ref[0] = (y * (1.0 + sc) + sh).astype(o_ref.dtype)


def _norm1(x, mod3, g, tm=512):
    bsz, L, d = x.shape
    return pl.pallas_call(
        _norm1_kernel,
        grid=(bsz, L // tm),
        in_specs=[pl.BlockSpec((1, tm, d), lambda b, i: (b, i, 0)),
                  pl.BlockSpec((1, 6, d), lambda b, i: (b, 0, 0)),
                  pl.BlockSpec((1, d), lambda b, i: (0, 0))],
        out_specs=pl.BlockSpec((1, tm, d), lambda b, i: (b, i, 0)),
        out_shape=jax.ShapeDtypeStruct((bsz, L, d), BF16),
        compiler_params=_cparams(("parallel", "parallel")),
    )(x, mod3, g.reshape(1, d))


def _mm_kernel(a_ref, w_ref, o_ref):
    o_ref[...] = jnp.dot(a_ref[...], w_ref[...], preferred_element_type=F32).astype(o_ref.dtype)


def _matmul(a, w, out_dtype, tm, tn):
    m, k = a.shape
    n = w.shape[1]
    return pl.pallas_call(
        _mm_kernel,
        grid=(m // tm, n // tn),
        in_specs=[pl.BlockSpec((tm, k), lambda i, j: (i, 0)),
                  pl.BlockSpec((k, tn), lambda i, j: (0, j))],
        out_specs=pl.BlockSpec((tm, tn), lambda i, j: (i, j)),
        out_shape=jax.ShapeDtypeStruct((m, n), out_dtype),
        compiler_params=_cparams(("parallel", "arbitrary")),
    )(a, w)


def _kvnorm_kernel(aux_ref, gc_ref, gk_ref, ckv_ref, iklo_ref, ikhi_ref):
    ckv = aux_ref[:, :A_KV_RANK]
    ckv_ref[...] = (ckv * lax.rsqrt(jnp.mean(ckv * ckv, axis=-1, keepdims=True) + EPS)
                    * gc_ref[...]).astype(BF16)
    v = aux_ref[:, A_KV_RANK:A_KV_RANK + LANES]
    lane = lax.broadcasted_iota(I32, v.shape, 1)
    ik = jnp.where(lane < IDX_DIM, v, 0.0)
    ms = jnp.sum(ik * ik, axis=-1, keepdims=True) * (1.0 / IDX_DIM)
    ikn = ik * lax.rsqrt(ms + EPS) * gk_ref[...]
    iklo_ref[...] = ikn.astype(BF16)
    ikhi_ref[...] = pltpu.roll(ikn, IDX_DIM, 1).astype(BF16)


def _kvnorm(aux, gc, gk, tm=1024):
    t = aux.shape[0]
    gk_pad = jnp.concatenate([gk, jnp.zeros((LANES - IDX_DIM,), F32)]).reshape(1, LANES)
    return pl.pallas_call(
        _kvnorm_kernel,
        grid=(t // tm,),
        in_specs=[pl.BlockSpec((tm, aux.shape[1]), lambda i: (i, 0)),
                  pl.BlockSpec((1, A_KV_RANK), lambda i: (0, 0)),
                  pl.BlockSpec((1, LANES), lambda i: (0, 0))],
        out_specs=[pl.BlockSpec((tm, A_KV_RANK), lambda i: (i, 0)),
                   pl.BlockSpec((tm, LANES), lambda i: (i, 0)),
                   pl.BlockSpec((tm, LANES), lambda i: (i, 0))],
        out_shape=[jax.ShapeDtypeStruct((t, A_KV_RANK), BF16),
                   jax.ShapeDtypeStruct((t, LANES), BF16),
                   jax.ShapeDtypeStruct((t, LANES), BF16)],
        compiler_params=_cparams(("parallel",)),
    )(aux, gc.reshape(1, A_KV_RANK), gk_pad)


def _t5_bucket(rel):
    n = jnp.maximum(rel, 0)
    max_exact = REL_BUCKETS // 2
    n_large = jnp.maximum(n, max_exact).astype(F32)
    large = max_exact + (jnp.log(n_large / max_exact) / math.log(REL_MAX_DIST / max_exact)
                         * (REL_BUCKETS - max_exact)).astype(I32)
    large = jnp.minimum(large, REL_BUCKETS - 1)
    return jnp.where(n < max_exact, n, large)


def _bias_tables(rel_bias, tq):
    assert tq + 1 >= REL_MAX_DIST
    nh = rel_bias.shape[1]
    dist = jnp.maximum(jnp.arange(3 * tq + 1, dtype=I32) - tq, 0)
    v = rel_bias.astype(F32)[_t5_bucket(dist)].T * LOG2E
    n = v.shape[1]
    x = jnp.broadcast_to(v[:, None, :], (nh, tq, n)).reshape(nh, tq * n)[:, :tq * (n - 1)].reshape(nh, tq, n - 1)
    near = x[:, :, tq:2 * tq]
    prev = x[:, :, 2 * tq:3 * tq]
    far = jnp.broadcast_to(v[:, n - 1][:, None, None], near.shape)
    return jnp.stack([near, prev, far])


def _dsa_kernel(qa_ref, iq_ref, aux_ref, iklo_ref, ikhi_ref, ckv_ref, ckvt_ref, wuk_ref, wuvt_ref, bias_ref,
                g_ref, o_ref, iqt_ref, iwt_ref, key_ref, qlt_ref, m_ref, l_ref, acc_ref, *, tq, topk):
    i = pl.program_id(1)
    nh = A_HEADS
    npair = IDX_HEADS // 2

    r_i = lax.broadcasted_iota(I32, (LANES, LANES), 0)
    c_i = lax.broadcasted_iota(I32, (LANES, LANES), 1)
    eye = jnp.where(r_i == c_i, 1.0, 0.0).astype(BF16)
    for p in range(npair):
        iqt_ref[:, p * tq:(p + 1) * tq] = lax.dot_general(
            eye, iq_ref[:, p * LANES:(p + 1) * LANES], NT_DIMS, preferred_element_type=F32).astype(BF16)
    iwt_ref[...] = (jnp.transpose(aux_ref[...])[IDX_DIM:IDX_DIM + IDX_HEADS, :]
                    * (IDX_HEADS ** -0.5 * IDX_DIM ** -0.5))
    for h in range(nh):
        ql = lax.dot_general(wuk_ref[h], qa_ref[:, h * A_HEAD_DIM:(h + 1) * A_HEAD_DIM], NT_DIMS,
                             preferred_element_type=F32)
        qlt_ref[:, h * tq:(h + 1) * tq] = (ql * (A_HEAD_DIM ** -0.5 * LOG2E)).astype(BF16)

    kpos = lax.broadcasted_iota(I32, (tq, tq), 0)
    qpos = lax.broadcasted_iota(I32, (tq, tq), 1) + i * tq

    def score_body(kc, carry):
        off = pl.multiple_of(kc * tq, tq)
        klo = iklo_ref[0, pl.ds(off, tq), :]
        khi = ikhi_ref[0, pl.ds(off, tq), :]
        acc = jnp.zeros((tq, tq), F32)
        for p in range(npair):
            rhs = iqt_ref[:, p * tq:(p + 1) * tq]
            se = jnp.dot(klo, rhs, preferred_element_type=F32)
            so = jnp.dot(khi, rhs, preferred_element_type=F32)
            acc = acc + jnp.maximum(se, 0.0) * iwt_ref[2 * p:2 * p + 1, :]
            acc = acc + jnp.maximum(so, 0.0) * iwt_ref[2 * p + 1:2 * p + 2, :]
        bits = lax.bitcast_convert_type(acc, I32)
        key = jnp.where(bits >= 0, bits, bits ^ jnp.int32(0x7FFFFFFF))
        key_ref[kc] = jnp.where(kpos + off <= qpos, key, jnp.int32(KEY_NEG_INF))
        return carry

    lax.fori_loop(0, i + 1, score_body, 0)

    def count_ge(cand):
        def body(kc, c):
            hit = jnp.where(key_ref[kc] >= cand, 1.0, 0.0)
            return c + jnp.sum(hit.reshape(tq // 8, 8, tq), axis=0)
        c = lax.fori_loop(0, i + 1, body, jnp.zeros((8, tq), F32))
        return jnp.sum(c, axis=0, keepdims=True)

    kf = float(topk)
    thr = jnp.where(count_ge(jnp.zeros((1, tq), I32)) >= kf, jnp.int32(0), jnp.int32(INT_MIN))

    def bit_body(j, thr):
        cand = thr | lax.shift_left(jnp.int32(1), 30 - j)
        return jnp.where(count_ge(cand) >= kf, cand, thr)

    thr = lax.fori_loop(0, 31, bit_body, thr)

    m_ref[...] = jnp.full(m_ref.shape, -jnp.inf, F32)
    l_ref[...] = jnp.zeros(l_ref.shape, F32)
    acc_ref[...] = jnp.zeros(acc_ref.shape, F32)

    def att_body(kc, carry):
        off = pl.multiple_of(kc * tq, tq)
        ckv = ckv_ref[0, pl.ds(off, tq), :]
        ckvt = ckvt_ref[0, kc]
        key = key_ref[kc]
        madd = jnp.where((key >= thr) & (key > jnp.int32(KEY_NEG_INF)), 0.0, -jnp.inf)
        d = jnp.minimum(i - kc, 2)
        for h in range(nh):
            s = jnp.dot(ckv, qlt_ref[:, h * tq:(h + 1) * tq], preferred_element_type=F32)
            s = s + (bias_ref[d, h] + madd)
            m_old = m_ref[h:h + 1, :]
            m_new = jnp.maximum(m_old, jnp.max(s, axis=0, keepdims=True))
            m_safe = jnp.where(m_new == -jnp.inf, 0.0, m_new)
            alpha = jnp.exp2(m_old - m_safe)
            p = jnp.exp2(s - m_safe)
            l_ref[h:h + 1, :] = alpha * l_ref[h:h + 1, :] + jnp.sum(p, axis=0, keepdims=True)
            acc_ref[h] = alpha * acc_ref[h] + jnp.dot(ckvt, p.astype(BF16), preferred_element_type=F32)
            m_ref[h:h + 1, :] = m_new
        return carry

    lax.fori_loop(0, i + 1, att_body, 0)

    outs = []
    for h in range(nh):
        o_lat = (acc_ref[h] / l_ref[h:h + 1, :]).astype(BF16)
        outs.append(jnp.transpose(jnp.dot(wuvt_ref[h], o_lat, preferred_element_type=F32)))
    o = jnp.concatenate(outs, axis=1)
    o = o * lax.rsqrt(jnp.mean(o * o, axis=-1, keepdims=True) + EPS) * g_ref[...]
    o_ref[...] = o.astype(o_ref.dtype)


def _dsa(g1, aux, ik_lo, ik_hi, ckv_n, w_uk, w_uv, bias_tab, g, bsz, L, tq):
    t = bsz * L
    nq = L // tq
    topk = min(IDX_TOPK_MAX, L // 4)
    aux_blk = A_KV_RANK // LANES
    kern = functools.partial(_dsa_kernel, tq=tq, topk=topk)
    width = A_HEADS * A_HEAD_DIM
    ckv3 = ckv_n.reshape(bsz, L, A_KV_RANK)
    ckvt = ckv_n.reshape(bsz, nq, tq, A_KV_RANK).transpose(0, 1, 3, 2)
    return pl.pallas_call(
        kern,
        grid=(bsz, nq),
        in_specs=[pl.BlockSpec((tq, width), lambda b, i: (b * nq + i, 0)),
                  pl.BlockSpec((tq, IDX_HEADS * IDX_DIM), lambda b, i: (b * nq + i, 1)),
                  pl.BlockSpec((tq, LANES), lambda b, i: (b * nq + i, aux_blk)),
                  pl.BlockSpec((1, L, LANES), lambda b, i: (b, 0, 0)),
                  pl.BlockSpec((1, L, LANES), lambda b, i: (b, 0, 0)),
                  pl.BlockSpec((1, L, A_KV_RANK), lambda b, i: (b, 0, 0)),
                  pl.BlockSpec((1, nq, A_KV_RANK, tq), lambda b, i: (b, 0, 0, 0)),
                  pl.BlockSpec((A_HEADS, A_KV_RANK, A_HEAD_DIM), lambda b, i: (0, 0, 0)),
                  pl.BlockSpec((A_HEADS, A_HEAD_DIM, A_KV_RANK), lambda b, i: (0, 0, 0)),
                  pl.BlockSpec((3, A_HEADS, tq, tq), lambda b, i: (0, 0, 0, 0)),
                  pl.BlockSpec((1, width), lambda b, i: (0, 0))],
        out_specs=pl.BlockSpec((tq, width), lambda b, i: (b * nq + i, 0)),
        out_shape=jax.ShapeDtypeStruct((t, width), BF16),
        scratch_shapes=[pltpu.VMEM((LANES, IDX_HEADS // 2 * tq), BF16),
                        pltpu.VMEM((IDX_HEADS, tq), F32),
                        pltpu.VMEM((nq, tq, tq), I32),
                        pltpu.VMEM((A_KV_RANK, A_HEADS * tq), BF16),
                        pltpu.VMEM((A_HEADS, tq), F32),
                        pltpu.VMEM((A_HEADS, tq), F32),
                        pltpu.VMEM((A_HEADS, A_KV_RANK, tq), F32)],
        compiler_params=_cparams(("parallel", "arbitrary")),
    )(g1, g1, aux, ik_lo.reshape(bsz, L, LANES), ik_hi.reshape(bsz, L, LANES),
      ckv3, ckvt, w_uk, jnp.transpose(w_uv, (0, 2, 1)), bias_tab, g.reshape(1, width))


def _hgrn_kernel(q_ref, i_ref, gate_ref, f_ref, lbl_ref, ng_ref, o_ref, st_ref, *, chunk, rblk):
    @pl.when(pl.program_id(1) == 0)
    def _():
        st_ref[...] = jnp.zeros(st_ref.shape, F32)

    ll = lbl_ref[...]
    ex = jnp.exp(ll - jnp.max(ll, axis=0, keepdims=True))
    lb_all = ex[0:1] / jnp.sum(ex, axis=0, keepdims=True)

    r_i = lax.broadcasted_iota(I32, (chunk, chunk), 0)
    c_i = lax.broadcasted_iota(I32, (chunk, chunk), 1)
    tri = jnp.where(r_i >= c_i, 1.0, 0.0).astype(BF16)
    row_k = lax.broadcasted_iota(I32, (chunk, B_DIM), 0)

    for h in range(B_HEADS):
        sl = slice(h * B_DIM, (h + 1) * B_DIM)
        lb = lb_all[:, sl]
        f = lb + (1.0 - lb) * jax.nn.sigmoid(f_ref[:, sl])
        lf = jnp.log(f)
        kk = 1.0 - f
        l1 = lf.astype(BF16)
        r1 = lf - l1.astype(F32)
        l2 = r1.astype(BF16)
        l3 = (r1 - l2.astype(F32)).astype(BF16)
        cs = jnp.dot(tri, jnp.concatenate([l1, l2, l3], axis=1), preferred_element_type=F32)
        bc = cs[:, :B_DIM] + cs[:, B_DIM:2 * B_DIM] + cs[:, 2 * B_DIM:]

        q = q_ref[:, sl].astype(F32)
        v = i_ref[:, sl]
        st = st_ref[h]
        o = lax.dot_general((q * jnp.exp(bc)).astype(BF16), st.astype(BF16), NT_DIMS,
                            preferred_element_type=F32)
        parts = []
        for r in range(chunk // rblk):
            lo, hi = r * rblk, (r + 1) * rblk
            base = bc[lo - 1:lo] if r > 0 else jnp.zeros((1, B_DIM), F32)
            qt = (q[lo:hi] * jnp.exp(bc[lo:hi] - base)).astype(BF16)
            kt = (kk * jnp.exp(jnp.where(row_k < hi, base - bc, 0.0))).astype(BF16)
            parts.append(lax.dot_general(qt, kt, NT_DIMS, preferred_element_type=F32))
        sc = jnp.where(c_i <= r_i, jnp.concatenate(parts, axis=0), 0.0).astype(BF16)
        o = o + jnp.dot(sc, v, preferred_element_type=F32)

        last = bc[chunk - 1:chunk]
        kd = (kk * jnp.exp(last - bc)).astype(BF16)
        st_ref[h] = st * jnp.exp(last) + lax.dot_general(v, kd, TN_DIMS, preferred_element_type=F32)

        y = o * lax.rsqrt(jnp.mean(o * o, axis=-1, keepdims=True) + EPS) * ng_ref[:, sl]
        o_ref[:, sl] = (y * _silu(gate_ref[:, sl].astype(F32))).astype(o_ref.dtype)


def _hgrn(g1, fb, lb_logits, ng, bsz, L, chunk=128, rblk=16):
    t = bsz * L
    nc = L // chunk
    width = B_HEADS * B_DIM
    kern = functools.partial(_hgrn_kernel, chunk=chunk, rblk=rblk)
    return pl.pallas_call(
        kern,
        grid=(bsz, nc),
        in_specs=[pl.BlockSpec((chunk, width), lambda b, c: (b * nc + c, 2)),
                  pl.BlockSpec((chunk, width), lambda b, c: (b * nc + c, 3)),
                  pl.BlockSpec((chunk, width), lambda b, c: (b * nc + c, 4)),
                  pl.BlockSpec((chunk, width), lambda b, c: (b * nc + c, 0)),
                  pl.BlockSpec(lb_logits.shape, lambda b, c: (0, 0)),
                  pl.BlockSpec((1, width), lambda b, c: (0, 0))],
        out_specs=pl.BlockSpec((chunk, width), lambda b, c: (b * nc + c, 0)),
        out_shape=jax.ShapeDtypeStruct((t, width), BF16),
        scratch_shapes=[pltpu.VMEM((B_HEADS, B_DIM, B_DIM), F32)],
        compiler_params=_cparams(("parallel", "arbitrary")),
    )(g1, g1, g1, fb, lb_logits, ng.reshape(1, width))


def _out_kernel(oa_ref, ob_ref, x_ref, wa_ref, wb_ref, mod_ref, g_ref, wr_ref, x1_ref, h2_ref, lg_ref):
    mix = jnp.dot(oa_ref[...], wa_ref[...], preferred_element_type=F32)
    mix = mix + jnp.dot(ob_ref[...], wb_ref[...], preferred_element_type=F32)
    x1 = x_ref[...] + mod_ref[0, 2:3, :] * mix
    x1_ref[...] = x1
    y = x1 * lax.rsqrt(jnp.mean(x1 * x1, axis=-1, keepdims=True) + EPS) * g_ref[...]
    h2 = y * (1.0 + mod_ref[0, 4:5, :]) + mod_ref[0, 3:4, :]
    h2_ref[...] = _pack_halves(h2)
    lg_ref[...] = lax.dot_general(wr_ref[...], h2.astype(BF16), NT_DIMS, preferred_element_type=F32)


def _out(oa, ob, x2d, w_out, mod3, g, w_router_t, L, tm=256):
    t, d = x2d.shape
    half = oa.shape[1]
    ne = w_router_t.shape[0]
    return pl.pallas_call(
        _out_kernel,
        grid=(t // tm,),
        in_specs=[pl.BlockSpec((tm, half), lambda i: (i, 0)),
                  pl.BlockSpec((tm, half), lambda i: (i, 0)),
                  pl.BlockSpec((tm, d), lambda i: (i, 0)),
                  pl.BlockSpec((half, d), lambda i: (0, 0)),
                  pl.BlockSpec((half, d), lambda i: (1, 0)),
                  pl.BlockSpec((1, 6, d), lambda i: (i * tm // L, 0, 0)),
                  pl.BlockSpec((1, d), lambda i: (0, 0)),
                  pl.BlockSpec((ne, d), lambda i: (0, 0))],
        out_specs=[pl.BlockSpec((tm, d), lambda i: (i, 0)),
                   pl.BlockSpec((tm, d // 2), lambda i: (i, 0)),
                   pl.BlockSpec((ne, tm), lambda i: (0, i))],
        out_shape=[jax.ShapeDtypeStruct((t, d), F32),
                   jax.ShapeDtypeStruct((t, d // 2), U32),
                   jax.ShapeDtypeStruct((ne, t), F32)],
        compiler_params=_cparams(("parallel",)),
    )(oa, ob, x2d, w_out, w_out, mod3, g.reshape(1, d), w_router_t)


def _rows_to_tile(rows, nrow):
    n = rows[0].shape[1]
    ridx = lax.broadcasted_iota(I32, (nrow, n), 0)
    out = jnp.zeros((nrow, n), rows[0].dtype)
    for r, v in enumerate(rows):
        out = jnp.where(ridx == r, jnp.broadcast_to(v, (nrow, n)), out)
    return out


def _route_kernel(lg_ref, rb_ref, eidx_ref, ew_ref, rank_ref, cnt_ref, run_ref):
    @pl.when(pl.program_id(0) == 0)
    def _():
        run_ref[...] = jnp.zeros(run_ref.shape, F32)

    ne, tt = lg_ref.shape
    per = ne // N_GROUPS
    sc = jax.nn.sigmoid(lg_ref[...])
    ch = sc + rb_ref[...]
    neg = -jnp.inf

    sub = lax.broadcasted_iota(I32, (per, tt), 0).astype(F32)
    gsc = []
    for g in range(N_GROUPS):
        cg = ch[g * per:(g + 1) * per]
        m1 = jnp.max(cg, axis=0, keepdims=True)
        first = jnp.min(jnp.where(cg == m1, sub, float(per)), axis=0, keepdims=True)
        m2 = jnp.max(jnp.where(sub == first, neg, cg), axis=0, keepdims=True)
        gsc.append(m1 + m2)
    grp = _rows_to_tile(gsc, N_GROUPS)

    gid = lax.broadcasted_iota(I32, (N_GROUPS, tt), 0).astype(F32)
    gsel = jnp.zeros((N_GROUPS, tt), F32)
    for _ in range(TOPK_GROUPS):
        mx = jnp.max(grp, axis=0, keepdims=True)
        gi = jnp.min(jnp.where(grp == mx, gid, float(N_GROUPS)), axis=0, keepdims=True)
        pick = gid == gi
        gsel = jnp.where(pick, 1.0, gsel)
        grp = jnp.where(pick, neg, grp)

    eid = lax.broadcasted_iota(I32, (ne, tt), 0).astype(F32)
    cm = jnp.full((ne, tt), neg, F32)
    for g in range(N_GROUPS):
        in_g = (eid >= float(g * per)) & (eid < float((g + 1) * per))
        cm = jnp.where(in_g & (jnp.broadcast_to(gsel[g:g + 1], (ne, tt)) > 0.5), ch, cm)

    idx_rows, w_rows = [], []
    onehot = jnp.zeros((ne, tt), F32)
    for _ in range(TOP_K):
        mx = jnp.max(cm, axis=0, keepdims=True)
        ei = jnp.min(jnp.where(cm == mx, eid, float(ne)), axis=0, keepdims=True)
        pick = eid == ei
        idx_rows.append(ei)
        w_rows.append(jnp.sum(jnp.where(pick, sc, 0.0), axis=0, keepdims=True))
        onehot = jnp.where(pick, 1.0, onehot)
        cm = jnp.where(pick, neg, cm)
    wsum = w_rows[0]
    for w in w_rows[1:]:
        wsum = wsum + w
    w_rows = [w / wsum * ROUTED_SCALE for w in w_rows]

    a_i = lax.broadcasted_iota(I32, (tt, tt), 0)
    b_i = lax.broadcasted_iota(I32, (tt, tt), 1)
    upper = jnp.where(a_i < b_i, 1.0, 0.0).astype(BF16)
    rank_full = jnp.dot(onehot.astype(BF16), upper, preferred_element_type=F32) + run_ref[...]
    r_rows = [jnp.sum(jnp.where(eid == ei, rank_full, 0.0), axis=0, keepdims=True) for ei in idx_rows]
    run = run_ref[...] + jnp.sum(onehot, axis=1, keepdims=True)
    run_ref[...] = run

    eidx_ref[...] = _rows_to_tile(idx_rows, TOP_K).astype(I32)
    ew_ref[...] = _rows_to_tile(w_rows, TOP_K)
    rank_ref[...] = _rows_to_tile(r_rows, TOP_K).astype(I32)
    cnt_ref[...] = jnp.broadcast_to(run, cnt_ref.shape)


def _route(logits_t, router_bias, tt=512):
    ne, t = logits_t.shape
    return pl.pallas_call(
        _route_kernel,
        grid=(t // tt,),
        in_specs=[pl.BlockSpec((ne, tt), lambda i: (0, i)),
                  pl.BlockSpec((ne, 1), lambda i: (0, 0))],
        out_specs=[pl.BlockSpec((TOP_K, tt), lambda i: (0, i)),
                   pl.BlockSpec((TOP_K, tt), lambda i: (0, i)),
                   pl.BlockSpec((TOP_K, tt), lambda i: (0, i)),
                   pl.BlockSpec((ne, LANES), lambda i: (0, 0))],
        out_shape=[jax.ShapeDtypeStruct((TOP_K, t), I32),
                   jax.ShapeDtypeStruct((TOP_K, t), F32),
                   jax.ShapeDtypeStruct((TOP_K, t), I32),
                   jax.ShapeDtypeStruct((ne, LANES), F32)],
        scratch_shapes=[pltpu.VMEM((ne, 1), F32)],
        compiler_params=_cparams(("arbitrary",)),
    )(logits_t, router_bias.reshape(ne, 1))


def _dest_kernel(eidx_ref, rank_ref, ps_ref, o_ref):
    ne = ps_ref.shape[0]
    tt = eidx_ref.shape[1]
    eid = lax.broadcasted_iota(I32, (ne, tt), 0)
    ps = jnp.broadcast_to(ps_ref[...], (ne, tt))
    rows = []
    for k in range(TOP_K):
        start = jnp.sum(jnp.where(eid == eidx_ref[k:k + 1, :], ps, 0.0), axis=0, keepdims=True)
        rows.append(start + rank_ref[k:k + 1, :].astype(F32))
    o_ref[...] = _rows_to_tile(rows, TOP_K).astype(I32)


def _dest(eidx, rank, pad_start, tt=2048):
    t = eidx.shape[1]
    tt = min(tt, t)
    ne = pad_start.shape[0]
    return pl.pallas_call(
        _dest_kernel,
        grid=(t // tt,),
        in_specs=[pl.BlockSpec((TOP_K, tt), lambda i: (0, i)),
                  pl.BlockSpec((TOP_K, tt), lambda i: (0, i)),
                  pl.BlockSpec((ne, 1), lambda i: (0, 0))],
        out_specs=pl.BlockSpec((TOP_K, tt), lambda i: (0, i)),
        out_shape=jax.ShapeDtypeStruct((TOP_K, t), I32),
        compiler_params=_cparams(("parallel",)),
    )(eidx, rank, pad_start.astype(F32).reshape(ne, 1))


def _dispatch_kernel(pend_ref, padded_ref, dest_ref, h_ref, xs_ref, zbuf_ref, zsem, sem, *, td, bm):
    i = pl.program_id(0)

    def tail_copy(e):
        start = pl.multiple_of(pend_ref[e] - bm, bm)
        return pltpu.make_async_copy(zbuf_ref, xs_ref.at[pl.ds(start, bm)], zsem)

    @pl.when(i == 0)
    def _():
        zbuf_ref[...] = jnp.zeros(zbuf_ref.shape, zbuf_ref.dtype)

        def start_body(e, c):
            @pl.when(padded_ref[e] > 0)
            def _():
                tail_copy(e).start()
            return c

        def wait_body(e, c):
            @pl.when(padded_ref[e] > 0)
            def _():
                tail_copy(e).wait()
            return c

        lax.fori_loop(0, N_EXPERTS, start_body, 0)
        lax.fori_loop(0, N_EXPERTS, wait_body, 0)

        def unused_copy(b):
            return pltpu.make_async_copy(zbuf_ref, xs_ref.at[pl.ds(pl.multiple_of(b * bm, bm), bm)], zsem)

        def ustart_body(b, c):
            unused_copy(b).start()
            return c

        def uwait_body(b, c):
            unused_copy(b).wait()
            return c

        first_unused = pend_ref[N_EXPERTS - 1] // bm
        lax.fori_loop(first_unused, xs_ref.shape[0] // bm, ustart_body, 0)
        lax.fori_loop(first_unused, xs_ref.shape[0] // bm, uwait_body, 0)

    def tok_body(j, c):
        src = h_ref.at[j]
        for k in range(TOP_K):
            pltpu.make_async_copy(src, xs_ref.at[dest_ref[j * TOP_K + k]], sem).start()
        return c

    lax.fori_loop(0, td, tok_body, 0)
    for _ in range(TOP_K):
        pltpu.make_async_copy(h_ref, xs_ref.at[pl.ds(0, td)], sem).wait()


def _dispatch(pad_end, padded, dest_flat, h2p, n_rows, bm, td=256):
    t, w = h2p.shape
    kern = functools.partial(_dispatch_kernel, td=td, bm=bm)
    return pl.pallas_call(
        kern,
        grid_spec=pltpu.PrefetchScalarGridSpec(
            num_scalar_prefetch=2,
            grid=(t // td,),
            in_specs=[pl.BlockSpec((td * TOP_K,), lambda i, *_: (i,), memory_space=pltpu.SMEM),
                      pl.BlockSpec((td, w), lambda i, *_: (i, 0))],
            out_specs=pl.BlockSpec(memory_space=pl.ANY),
            scratch_shapes=[pltpu.VMEM((bm, w), U32),
                            pltpu.SemaphoreType.DMA(()),
                            pltpu.SemaphoreType.DMA(())]),
        out_shape=jax.ShapeDtypeStruct((n_rows, w), U32),
        compiler_params=_cparams(("arbitrary",)),
    )(pad_end, padded, dest_flat, h2p)


def _ffn(xw, wgu_ref, wd_ref):
    half = xw.shape[1]
    f = wd_ref.shape[0]
    left, right = _unpack_halves(xw)
    gu = jnp.dot(left.astype(BF16), wgu_ref[:half, :], preferred_element_type=F32)
    gu = gu + jnp.dot(right.astype(BF16), wgu_ref[half:, :], preferred_element_type=F32)
    act = (_silu(gu[:, :f]) * gu[:, f:]).astype(BF16)
    return jnp.dot(act, wd_ref[...], preferred_element_type=F32)


def _expert_kernel(blk_ref, eid_ref, nvb_ref, x_ref, wgu_ref, wd_ref, o_ref):
    @pl.when(pl.program_id(0) < nvb_ref[0])
    def _():
        o_ref[...] = _pack_halves(_ffn(x_ref[...], wgu_ref.at[0], wd_ref.at[0]))

    @pl.when(pl.program_id(0) >= nvb_ref[0])
    def _():
        o_ref[...] = jnp.zeros(o_ref.shape, o_ref.dtype)


def _experts(blk, eid, nvb, xs, wgu, wd, bm):
    n_rows, w = xs.shape
    ne, d, f2 = wgu.shape
    return pl.pallas_call(
        _expert_kernel,
        grid_spec=pltpu.PrefetchScalarGridSpec(
            num_scalar_prefetch=3,
            grid=(n_rows // bm,),
            in_specs=[pl.BlockSpec((bm, w), lambda i, blk, eid, nvb: (blk[i], 0)),
                      pl.BlockSpec((1, d, f2), lambda i, blk, eid, nvb: (eid[i], 0, 0)),
                      pl.BlockSpec((1, f2 // 2, d), lambda i, blk, eid, nvb: (eid[i], 0, 0))],
            out_specs=pl.BlockSpec((bm, w), lambda i, blk, eid, nvb: (i, 0))),
        out_shape=jax.ShapeDtypeStruct((n_rows, w), U32),
        compiler_params=_cparams(("arbitrary",)),
    )(blk, eid, nvb, xs, wgu, wd)


def _combine_kernel(dest_ref, y_ref, h_ref, x1_ref, ew_ref, wgu_ref, wd_ref, mod_ref, g_ref, o_ref,
                    gbuf_ref, sem, *, tc):
    def tok_body(j, c):
        for k in range(TOP_K):
            pltpu.make_async_copy(y_ref.at[dest_ref[j * TOP_K + k]], gbuf_ref.at[k, j], sem).start()
        return c

    lax.fori_loop(0, tc, tok_body, 0)
    shared = _ffn(h_ref[...], wgu_ref, wd_ref)
    for k in range(TOP_K):
        pltpu.make_async_copy(y_ref.at[pl.ds(0, tc)], gbuf_ref.at[k], sem).wait()

    half = h_ref.shape[1]
    ew = ew_ref[...]
    left = shared[:, :half]
    right = shared[:, half:]
    for k in range(TOP_K):
        yl, yr = _unpack_halves(gbuf_ref[k])
        wk = ew[:, k:k + 1]
        left = left + wk * yl
        right = right + wk * yr
    x2 = x1_ref[...] + mod_ref[0, 5:6, :] * jnp.concatenate([left, right], axis=1)
    o_ref[...] = x2 * lax.rsqrt(jnp.mean(x2 * x2, axis=-1, keepdims=True) + EPS) * g_ref[...]


def _combine(dest_flat, y, h2p, x1, ew_t, wsgu, wsd, mod3, g, L, tc=256):
    t, d = x1.shape
    w = h2p.shape[1]
    kern = functools.partial(_combine_kernel, tc=tc)
    return pl.pallas_call(
        kern,
        grid=(t // tc,),
        in_specs=[pl.BlockSpec((tc * TOP_K,), lambda i: (i,), memory_space=pltpu.SMEM),
                  pl.BlockSpec(memory_space=pl.ANY),
                  pl.BlockSpec((tc, w), lambda i: (i, 0)),
                  pl.BlockSpec((tc, d), lambda i: (i, 0)),
                  pl.BlockSpec((tc, TOP_K), lambda i: (i, 0)),
                  pl.BlockSpec(wsgu.shape, lambda i: (0, 0)),
                  pl.BlockSpec(wsd.shape, lambda i: (0, 0)),
                  pl.BlockSpec((1, 6, d), lambda i: (i * tc // L, 0, 0)),
                  pl.BlockSpec((1, d), lambda i: (0, 0))],
        out_specs=pl.BlockSpec((tc, d), lambda i: (i, 0)),
        out_shape=jax.ShapeDtypeStruct((t, d), F32),
        scratch_shapes=[pltpu.VMEM((TOP_K, tc, w), U32),
                        pltpu.SemaphoreType.DMA(())],
        compiler_params=_cparams(("arbitrary",)),
    )(dest_flat, y, h2p, x1, ew_t, wsgu, wsd, mod3, g.reshape(1, d))


def _split_cols(w, sizes):
    out, off = [], 0
    for s in sizes:
        out.append(w[:, off:off + s])
        off += s
    return out


def kernel(x, c, w_ada, b_ada, norm1_g, w_in, ckv_norm_g, idx_k_norm_g, w_uk, w_uv, rel_bias, lb_logits,
           attn_out_norm_g, hgrn_out_norm_g, w_out, norm2_g, w_router, router_bias, w_e_gate, w_e_up,
           w_e_down, w_s_gate, w_s_up, w_s_down, final_norm_g):
    bsz, L, d = x.shape
    t = bsz * L
    assert w_ada.shape[0] == 1, "single-layer block"
    a_width = A_HEADS * A_HEAD_DIM
    b_width = B_HEADS * B_DIM
    sizes = (a_width, A_KV_RANK, IDX_HEADS * IDX_DIM, IDX_DIM, IDX_HEADS, b_width, b_width, b_width, b_width)
    assert w_in.shape[2] == sum(sizes)

    wq_a, wckv, wiq, wik, wiw, wq_b, wf_b, wi_b, wg_b = _split_cols(w_in[0], sizes)
    w_main = jnp.concatenate([wq_a, wiq, wq_b, wi_b, wg_b], axis=1).astype(BF16)
    w_f = wf_b.astype(BF16)
    aux_pad = LANES - IDX_DIM - IDX_HEADS
    w_aux = jnp.concatenate([wckv, wik, wiw, jnp.zeros((d, aux_pad), F32)], axis=1).astype(BF16)

    mod3 = _ada(c, w_ada[0], b_ada[0]).reshape(bsz, 6, d)
    h1 = _norm1(x, mod3, norm1_g[0]).reshape(t, d)
    g1 = _matmul(h1, w_main, BF16, tm=1024, tn=512)
    fb = _matmul(h1, w_f, F32, tm=1024, tn=512)
    aux = _matmul(h1, w_aux, F32, tm=1024, tn=w_aux.shape[1])
    ckv_n, ik_lo, ik_hi = _kvnorm(aux, ckv_norm_g[0], idx_k_norm_g[0])

    tq = min(256, L)
    o_a = _dsa(g1, aux, ik_lo, ik_hi, ckv_n, w_uk[0].astype(BF16), w_uv[0].astype(BF16),
               _bias_tables(rel_bias, tq), attn_out_norm_g[0], bsz, L, tq)
    o_b = _hgrn(g1, fb, lb_logits, hgrn_out_norm_g[0], bsz, L)

    x1, h2p, logits_t = _out(o_a, o_b, x.reshape(t, d), w_out[0].astype(BF16), mod3, norm2_g[0],
                             w_router[0].T.astype(BF16), L)

    eidx, ew, rank, cnt = _route(logits_t, router_bias[0])

    bm = 256
    counts = cnt[:, 0].astype(I32)
    padded = (counts + bm - 1) // bm * bm
    pad_end = jnp.cumsum(padded)
    pad_start = pad_end - padded
    n_rows = (t * TOP_K + N_EXPERTS * (bm - 1) + bm - 1) // bm * bm
    nb = n_rows // bm
    nvb = pad_end[-1] // bm
    blk = jnp.minimum(jnp.arange(nb, dtype=I32), nvb - 1)
    eid = jnp.minimum(jnp.sum((pad_end[None, :] <= (blk * bm)[:, None]).astype(I32), axis=1), N_EXPERTS - 1)

    dest = _dest(eidx, rank, pad_start)
    dest_flat = dest.T.reshape(t * TOP_K)
    xs = _dispatch(pad_end.astype(I32), padded.astype(I32), dest_flat, h2p, n_rows, bm)

    wgu = jnp.concatenate([w_e_gate[0], w_e_up[0]], axis=-1).astype(BF16)
    y = _experts(blk, eid, nvb.reshape(1).astype(I32), xs, wgu, w_e_down[0].astype(BF16), bm)

    wsgu = jnp.concatenate([w_s_gate[0], w_s_up[0]], axis=-1).astype(BF16)
    out = _combine(dest_flat, y, h2p, x1, ew.T, wsgu, w_s_down[0].astype(BF16), mod3, final_norm_g, L)
    return out.reshape(bsz, L, d)
```

```python
import functools
import math

import numpy as np
import jax
import jax.numpy as jnp
from jax import lax
from jax.experimental import pallas as pl
from jax.experimental.pallas import tpu as pltpu

F32 = jnp.float32
BF16 = jnp.bfloat16
I32 = jnp.int32
U32 = jnp.uint32

EPS = 1e-6
A_HEADS = 8
A_HEAD_DIM = 128
A_KV_RANK = 256
IDX_HEADS = 16
IDX_DIM = 64
IDX_TOPK_MAX = 256
B_HEADS = 8
B_DIM = 128
REL_BUCKETS = 32
REL_MAX_DIST = 128
N_EXPERTS = 64
TOP_K = 8
N_GROUPS = 8
TOPK_GROUPS = 4
ROUTED_SCALE = 2.5

VMEM_LIMIT_BYTES = 56 * 1024 * 1024
LANES = 128

NT_DIMS = (((1,), (1,)), ((), ()))
TN_DIMS = (((0,), (0,)), ((), ()))

LOG2E = math.log2(math.e)
INT_MIN = -2 ** 31
KEY_NEG_INF = -2139095041


def _cparams(sem):
    return pltpu.CompilerParams(dimension_semantics=sem, vmem_limit_bytes=VMEM_LIMIT_BYTES)


def _silu(v):
    return v * jax.nn.sigmoid(v)


def _pack_halves(v):
    n = v.shape[1] // 2
    lo = lax.bitcast_convert_type(v[:, :n].astype(BF16).astype(F32), U32)
    hi = lax.bitcast_convert_type(v[:, n:].astype(BF16).astype(F32), U32)
    return lax.shift_right_logical(lo, jnp.uint32(16)) | (hi & jnp.uint32(0xFFFF0000))


def _unpack_halves(w):
    left = lax.bitcast_convert_type(lax.shift_left(w, jnp.uint32(16)), F32)
    right = lax.bitcast_convert_type(w & jnp.uint32(0xFFFF0000), F32)
    return left, right


def _ada_kernel(c_ref, w_ref, b_ref, o_ref):
    a = _silu(c_ref[...]).astype(BF16)
    o_ref[...] = jnp.dot(a, w_ref[...].astype(BF16), preferred_element_type=F32) + b_ref[...]


def _ada(c, w, b, tn=1024):
    bsz, d = c.shape
    n = w.shape[1]
    return pl.pallas_call(
        _ada_kernel,
        grid=(n // tn,),
        in_specs=[pl.BlockSpec((bsz, d), lambda j: (0, 0)),
                  pl.BlockSpec((d, tn), lambda j: (0, j)),
                  pl.BlockSpec((1, tn), lambda j: (0, j))],
        out_specs=pl.BlockSpec((bsz, tn), lambda j: (0, j)),
        out_shape=jax.ShapeDtypeStruct((bsz, n), F32),
        compiler_params=_cparams(("arbitrary",)),
    )(c, w, b.reshape(1, n))


def _norm1_kernel(x_ref, mod_ref, g_ref, o_ref):
    x = x_ref[0]
    y = x * lax.rsqrt(jnp.mean(x * x, axis=-1, keepdims=True) + EPS) * g_ref[...]
    sh = mod_ref[0, 0:1, :]
    sc = mod_ref[0, 1:2, :]
    o_ref[0] = (y * (1.0 + sc) + sh).astype(o_ref.dtype)


def _norm1(x, mod3, g, tm=512):
    bsz, L, d = x.shape
    return pl.pallas_call(
        _norm1_kernel,
        grid=(bsz, L // tm),
        in_specs=[pl.BlockSpec((1, tm, d), lambda b, i: (b, i, 0)),
                  pl.BlockSpec((1, 6, d), lambda b, i: (b, 0, 0)),
                  pl.BlockSpec((1, d), lambda b, i: (0, 0))],
        out_specs=pl.BlockSpec((1, tm, d), lambda b, i: (b, i, 0)),
        out_shape=jax.ShapeDtypeStruct((bsz, L, d), BF16),
        compiler_params=_cparams(("parallel", "parallel")),
    )(x, mod3, g.reshape(1, d))


def _mm_kernel(a_ref, w_ref, o_ref):
    o_ref[...] = jnp.dot(a_ref[...], w_ref[...], preferred_element_type=F32).astype(o_ref.dtype)


def _matmul(a, w, out_dtype, tm, tn):
    m, k = a.shape
    n = w.shape[1]
    return pl.pallas_call(
        _mm_kernel,
        grid=(m // tm, n // tn),
        in_specs=[pl.BlockSpec((tm, k), lambda i, j: (i, 0)),
                  pl.BlockSpec((k, tn), lambda i, j: (0, j))],
        out_specs=pl.BlockSpec((tm, tn), lambda i, j: (i, j)),
        out_shape=jax.ShapeDtypeStruct((m, n), out_dtype),
        compiler_params=_cparams(("parallel", "arbitrary")),
    )(a, w)


def _kvnorm_kernel(aux_ref, gc_ref, gk_ref, ckv_ref, iklo_ref, ikhi_ref):
    ckv = aux_ref[:, :A_KV_RANK]
    ckv_ref[...] = (ckv * lax.rsqrt(jnp.mean(ckv * ckv, axis=-1, keepdims=True) + EPS)
                    * gc_ref[...]).astype(BF16)
    v = aux_ref[:, A_KV_RANK:A_KV_RANK + LANES]
    lane = lax.broadcasted_iota(I32, v.shape, 1)
    ik = jnp.where(lane < IDX_DIM, v, 0.0)
    ms = jnp.sum(ik * ik, axis=-1, keepdims=True) * (1.0 / IDX_DIM)
    ikn = ik * lax.rsqrt(ms + EPS) * gk_ref[...]
    iklo_ref[...] = ikn.astype(BF16)
    ikhi_ref[...] = pltpu.roll(ikn, IDX_DIM, 1).astype(BF16)


def _kvnorm(aux, gc, gk, tm=1024):
    t = aux.shape[0]
    gk_pad = jnp.concatenate([gk, jnp.zeros((LANES - IDX_DIM,), F32)]).reshape(1, LANES)
    return pl.pallas_call(
        _kvnorm_kernel,
        grid=(t // tm,),
        in_specs=[pl.BlockSpec((tm, aux.shape[1]), lambda i: (i, 0)),
                  pl.BlockSpec((1, A_KV_RANK), lambda i: (0, 0)),
                  pl.BlockSpec((1, LANES), lambda i: (0, 0))],
        out_specs=[pl.BlockSpec((tm, A_KV_RANK), lambda i: (i, 0)),
                   pl.BlockSpec((tm, LANES), lambda i: (i, 0)),
                   pl.BlockSpec((tm, LANES), lambda i: (i, 0))],
        out_shape=[jax.ShapeDtypeStruct((t, A_KV_RANK), BF16),
                   jax.ShapeDtypeStruct((t, LANES), BF16),
                   jax.ShapeDtypeStruct((t, LANES), BF16)],
        compiler_params=_cparams(("parallel",)),
    )(aux, gc.reshape(1, A_KV_RANK), gk_pad)


def _t5_bucket(rel):
    n = jnp.maximum(rel, 0)
    max_exact = REL_BUCKETS // 2
    n_large = jnp.maximum(n, max_exact).astype(F32)
    large = max_exact + (jnp.log(n_large / max_exact) / math.log(REL_MAX_DIST / max_exact)
                         * (REL_BUCKETS - max_exact)).astype(I32)
    large = jnp.minimum(large, REL_BUCKETS - 1)
    return jnp.where(n < max_exact, n, large)


def _bias_tables(rel_bias, tq):
    assert tq + 1 >= REL_MAX_DIST
    nh = rel_bias.shape[1]
    dist = jnp.maximum(jnp.arange(3 * tq + 1, dtype=I32) - tq, 0)
    v = rel_bias.astype(F32)[_t5_bucket(dist)].T * LOG2E
    n = v.shape[1]
    x = jnp.broadcast_to(v[:, None, :], (nh, tq, n)).reshape(nh, tq * n)[:, :tq * (n - 1)].reshape(nh, tq, n - 1)
    near = x[:, :, tq:2 * tq]
    prev = x[:, :, 2 * tq:3 * tq]
    far = jnp.broadcast_to(v[:, n - 1][:, None, None], near.shape)
    return jnp.stack([near, prev, far])


def _dsa_kernel(qa_ref, iq_ref, aux_ref, iklo_ref, ikhi_ref, ckv_ref, ckvt_ref, wuk_ref, wuvt_ref, bias_ref,
                g_ref, o_ref, iqt_ref, iwt_ref, key_ref, qlt_ref, m_ref, l_ref, acc_ref, *, tq, topk):
    i = pl.program_id(1)
    nh = A_HEADS
    npair = IDX_HEADS // 2

    r_i = lax.broadcasted_iota(I32, (LANES, LANES), 0)
    c_i = lax.broadcasted_iota(I32, (LANES, LANES), 1)
    eye = jnp.where(r_i == c_i, 1.0, 0.0).astype(BF16)
    for p in range(npair):
        iqt_ref[:, p * tq:(p + 1) * tq] = lax.dot_general(
            eye, iq_ref[:, p * LANES:(p + 1) * LANES], NT_DIMS, preferred_element_type=F32).astype(BF16)
    iwt_ref[...] = (jnp.transpose(aux_ref[...])[IDX_DIM:IDX_DIM + IDX_HEADS, :]
                    * (IDX_HEADS ** -0.5 * IDX_DIM ** -0.5))
    for h in range(nh):
        ql = lax.dot_general(wuk_ref[h], qa_ref[:, h * A_HEAD_DIM:(h + 1) * A_HEAD_DIM], NT_DIMS,
                             preferred_element_type=F32)
        qlt_ref[:, h * tq:(h + 1) * tq] = (ql * (A_HEAD_DIM ** -0.5 * LOG2E)).astype(BF16)

    kpos = lax.broadcasted_iota(I32, (tq, tq), 0)
    qpos = lax.broadcasted_iota(I32, (tq, tq), 1) + i * tq

    def score_body(kc, carry):
        off = pl.multiple_of(kc * tq, tq)
        klo = iklo_ref[0, pl.ds(off, tq), :]
        khi = ikhi_ref[0, pl.ds(off, tq), :]
        acc = jnp.zeros((tq, tq), F32)
        for p in range(npair):
            rhs = iqt_ref[:, p * tq:(p + 1) * tq]
            se = jnp.dot(klo, rhs, preferred_element_type=F32)
            so = jnp.dot(khi, rhs, preferred_element_type=F32)
            acc = acc + jnp.maximum(se, 0.0) * iwt_ref[2 * p:2 * p + 1, :]
            acc = acc + jnp.maximum(so, 0.0) * iwt_ref[2 * p + 1:2 * p + 2, :]
        bits = lax.bitcast_convert_type(acc, I32)
        key = jnp.where(bits >= 0, bits, bits ^ jnp.int32(0x7FFFFFFF))
        key_ref[kc] = jnp.where(kpos + off <= qpos, key, jnp.int32(KEY_NEG_INF))
        return carry

    lax.fori_loop(0, i + 1, score_body, 0)

    def count_ge(cand):
        def body(kc, c):
            hit = jnp.where(key_ref[kc] >= cand, 1.0, 0.0)
            return c + jnp.sum(hit.reshape(tq // 8, 8, tq), axis=0)
        c = lax.fori_loop(0, i + 1, body, jnp.zeros((8, tq), F32))
        return jnp.sum(c, axis=0, keepdims=True)

    kf = float(topk)
    thr = jnp.where(count_ge(jnp.zeros((1, tq), I32)) >= kf, jnp.int32(0), jnp.int32(INT_MIN))

    def bit_body(j, thr):
        cand = thr | lax.shift_left(jnp.int32(1), 30 - j)
        return jnp.where(count_ge(cand) >= kf, cand, thr)

    thr = lax.fori_loop(0, 31, bit_body, thr)

    m_ref[...] = jnp.full(m_ref.shape, -jnp.inf, F32)
    l_ref[...] = jnp.zeros(l_ref.shape, F32)
    acc_ref[...] = jnp.zeros(acc_ref.shape, F32)

    def att_body(kc, carry):
        off = pl.multiple_of(kc * tq, tq)
        ckv = ckv_ref[0, pl.ds(off, tq), :]
        ckvt = ckvt_ref[0, kc]
        key = key_ref[kc]
        madd = jnp.where((key >= thr) & (key > jnp.int32(KEY_NEG_INF)), 0.0, -jnp.inf)
        d = jnp.minimum(i - kc, 2)
        for h in range(nh):
            s = jnp.dot(ckv, qlt_ref[:, h * tq:(h + 1) * tq], preferred_element_type=F32)
            s = s + (bias_ref[d, h] + madd)
            m_old = m_ref[h:h + 1, :]
            m_new = jnp.maximum(m_old, jnp.max(s, axis=0, keepdims=True))
            m_safe = jnp.where(m_new == -jnp.inf, 0.0, m_new)
            alpha = jnp.exp2(m_old - m_safe)
            p = jnp.exp2(s - m_safe)
            l_ref[h:h + 1, :] = alpha * l_ref[h:h + 1, :] + jnp.sum(p, axis=0, keepdims=True)
            acc_ref[h] = alpha * acc_ref[h] + jnp.dot(ckvt, p.astype(BF16), preferred_element_type=F32)
            m_ref[h:h + 1, :] = m_new
        return carry

    lax.fori_loop(0, i + 1, att_body, 0)

    outs = []
    for h in range(nh):
        o_lat = (acc_ref[h] / l_ref[h:h + 1, :]).astype(BF16)
        outs.append(jnp.transpose(jnp.dot(wuvt_ref[h], o_lat, preferred_element_type=F32)))
    o = jnp.concatenate(outs, axis=1)
    o = o * lax.rsqrt(jnp.mean(o * o, axis=-1, keepdims=True) + EPS) * g_ref[...]
    o_ref[...] = o.astype(o_ref.dtype)


def _dsa(g1, aux, ik_lo, ik_hi, ckv_n, w_uk, w_uv, bias_tab, g, bsz, L, tq):
    t = bsz * L
    nq = L // tq
    topk = min(IDX_TOPK_MAX, L // 4)
    aux_blk = A_KV_RANK // LANES
    kern = functools.partial(_dsa_kernel, tq=tq, topk=topk)
    width = A_HEADS * A_HEAD_DIM
    ckv3 = ckv_n.reshape(bsz, L, A_KV_RANK)
    ckvt = ckv_n.reshape(bsz, nq, tq, A_KV_RANK).transpose(0, 1, 3, 2)
    return pl.pallas_call(
        kern,
        grid=(bsz, nq),
        in_specs=[pl.BlockSpec((tq, width), lambda b, i: (b * nq + i, 0)),
                  pl.BlockSpec((tq, IDX_HEADS * IDX_DIM), lambda b, i: (b * nq + i, 1)),
                  pl.BlockSpec((tq, LANES), lambda b, i: (b * nq + i, aux_blk)),
                  pl.BlockSpec((1, L, LANES), lambda b, i: (b, 0, 0)),
                  pl.BlockSpec((1, L, LANES), lambda b, i: (b, 0, 0)),
                  pl.BlockSpec((1, L, A_KV_RANK), lambda b, i: (b, 0, 0)),
                  pl.BlockSpec((1, nq, A_KV_RANK, tq), lambda b, i: (b, 0, 0, 0)),
                  pl.BlockSpec((A_HEADS, A_KV_RANK, A_HEAD_DIM), lambda b, i: (0, 0, 0)),
                  pl.BlockSpec((A_HEADS, A_HEAD_DIM, A_KV_RANK), lambda b, i: (0, 0, 0)),
                  pl.BlockSpec((3, A_HEADS, tq, tq), lambda b, i: (0, 0, 0, 0)),
                  pl.BlockSpec((1, width), lambda b, i: (0, 0))],
        out_specs=pl.BlockSpec((tq, width), lambda b, i: (b * nq + i, 0)),
        out_shape=jax.ShapeDtypeStruct((t, width), BF16),
        scratch_shapes=[pltpu.VMEM((LANES, IDX_HEADS // 2 * tq), BF16),
                        pltpu.VMEM((IDX_HEADS, tq), F32),
                        pltpu.VMEM((nq, tq, tq), I32),
                        pltpu.VMEM((A_KV_RANK, A_HEADS * tq), BF16),
                        pltpu.VMEM((A_HEADS, tq), F32),
                        pltpu.VMEM((A_HEADS, tq), F32),
                        pltpu.VMEM((A_HEADS, A_KV_RANK, tq), F32)],
        compiler_params=_cparams(("parallel", "arbitrary")),
    )(g1, g1, aux, ik_lo.reshape(bsz, L, LANES), ik_hi.reshape(bsz, L, LANES),
      ckv3, ckvt, w_uk, jnp.transpose(w_uv, (0, 2, 1)), bias_tab, g.reshape(1, width))


def _hgrn_kernel(q_ref, i_ref, gate_ref, f_ref, lbl_ref, ng_ref, o_ref, st_ref, *, chunk, rblk):
    @pl.when(pl.program_id(1) == 0)
    def _():
        st_ref[...] = jnp.zeros(st_ref.shape, F32)

    ll = lbl_ref[...]
    ex = jnp.exp(ll - jnp.max(ll, axis=0, keepdims=True))
    lb_all = ex[0:1] / jnp.sum(ex, axis=0, keepdims=True)

    r_i = lax.broadcasted_iota(I32, (chunk, chunk), 0)
    c_i = lax.broadcasted_iota(I32, (chunk, chunk), 1)
    tri = jnp.where(r_i >= c_i, 1.0, 0.0).astype(BF16)
    row_k = lax.broadcasted_iota(I32, (chunk, B_DIM), 0)

    for h in range(B_HEADS):
        sl = slice(h * B_DIM, (h + 1) * B_DIM)
        lb = lb_all[:, sl]
        f = lb + (1.0 - lb) * jax.nn.sigmoid(f_ref[:, sl])
        lf = jnp.log(f)
        kk = 1.0 - f
        l1 = lf.astype(BF16)
        r1 = lf - l1.astype(F32)
        l2 = r1.astype(BF16)
        l3 = (r1 - l2.astype(F32)).astype(BF16)
        cs = jnp.dot(tri, jnp.concatenate([l1, l2, l3], axis=1), preferred_element_type=F32)
        bc = cs[:, :B_DIM] + cs[:, B_DIM:2 * B_DIM] + cs[:, 2 * B_DIM:]

        q = q_ref[:, sl].astype(F32)
        v = i_ref[:, sl]
        st = st_ref[h]
        o = lax.dot_general((q * jnp.exp(bc)).astype(BF16), st.astype(BF16), NT_DIMS,
                            preferred_element_type=F32)
        parts = []
        for r in range(chunk // rblk):
            lo, hi = r * rblk, (r + 1) * rblk
            base = bc[lo - 1:lo] if r > 0 else jnp.zeros((1, B_DIM), F32)
            qt = (q[lo:hi] * jnp.exp(bc[lo:hi] - base)).astype(BF16)
            kt = (kk * jnp.exp(jnp.where(row_k < hi, base - bc, 0.0))).astype(BF16)
            parts.append(lax.dot_general(qt, kt, NT_DIMS, preferred_element_type=F32))
        sc = jnp.where(c_i <= r_i, jnp.concatenate(parts, axis=0), 0.0).astype(BF16)
        o = o + jnp.dot(sc, v, preferred_element_type=F32)

        last = bc[chunk - 1:chunk]
        kd = (kk * jnp.exp(last - bc)).astype(BF16)
        st_ref[h] = st * jnp.exp(last) + lax.dot_general(v, kd, TN_DIMS, preferred_element_type=F32)

        y = o * lax.rsqrt(jnp.mean(o * o, axis=-1, keepdims=True) + EPS) * ng_ref[:, sl]
        o_ref[:, sl] = (y * _silu(gate_ref[:, sl].astype(F32))).astype(o_ref.dtype)


def _hgrn(g1, fb, lb_logits, ng, bsz, L, chunk=128, rblk=16):
    t = bsz * L
    nc = L // chunk
    width = B_HEADS * B_DIM
    kern = functools.partial(_hgrn_kernel, chunk=chunk, rblk=rblk)
    return pl.pallas_call(
        kern,
        grid=(bsz, nc),
        in_specs=[pl.BlockSpec((chunk, width), lambda b, c: (b * nc + c, 2)),
                  pl.BlockSpec((chunk, width), lambda b, c: (b * nc + c, 3)),
                  pl.BlockSpec((chunk, width), lambda b, c: (b * nc + c, 4)),
                  pl.BlockSpec((chunk, width), lambda b, c: (b * nc + c, 0)),
                  pl.BlockSpec(lb_logits.shape, lambda b, c: (0, 0)),
                  pl.BlockSpec((1, width), lambda b, c: (0, 0))],
        out_specs=pl.BlockSpec((chunk, width), lambda b, c: (b * nc + c, 0)),
        out_shape=jax.ShapeDtypeStruct((t, width), BF16),
        scratch_shapes=[pltpu.VMEM((B_HEADS, B_DIM, B_DIM), F32)],
        compiler_params=_cparams(("parallel", "arbitrary")),
    )(g1, g1, g1, fb, lb_logits, ng.reshape(1, width))


def _out_kernel(oa_ref, ob_ref, x_ref, wa_ref, wb_ref, mod_ref, g_ref, wr_ref, x1_ref, h2_ref, lg_ref):
    mix = jnp.dot(oa_ref[...], wa_ref[...], preferred_element_type=F32)
    mix = mix + jnp.dot(ob_ref[...], wb_ref[...], preferred_element_type=F32)
    x1 = x_ref[...] + mod_ref[0, 2:3, :] * mix
    x1_ref[...] = x1
    y = x1 * lax.rsqrt(jnp.mean(x1 * x1, axis=-1, keepdims=True) + EPS) * g_ref[...]
    h2 = y * (1.0 + mod_ref[0, 4:5, :]) + mod_ref[0, 3:4, :]
    h2_ref[...] = _pack_halves(h2)
    lg_ref[...] = lax.dot_general(wr_ref[...], h2.astype(BF16), NT_DIMS, preferred_element_type=F32)


def _out(oa, ob, x2d, w_out, mod3, g, w_router_t, L, tm=256):
    t, d = x2d.shape
    half = oa.shape[1]
    ne = w_router_t.shape[0]
    return pl.pallas_call(
        _out_kernel,
        grid=(t // tm,),
        in_specs=[pl.BlockSpec((tm, half), lambda i: (i, 0)),
                  pl.BlockSpec((tm, half), lambda i: (i, 0)),
                  pl.BlockSpec((tm, d), lambda i: (i, 0)),
                  pl.BlockSpec((half, d), lambda i: (0, 0)),
                  pl.BlockSpec((half, d), lambda i: (1, 0)),
                  pl.BlockSpec((1, 6, d), lambda i: (i * tm // L, 0, 0)),
                  pl.BlockSpec((1, d), lambda i: (0, 0)),
                  pl.BlockSpec((ne, d), lambda i: (0, 0))],
        out_specs=[pl.BlockSpec((tm, d), lambda i: (i, 0)),
                   pl.BlockSpec((tm, d // 2), lambda i: (i, 0)),
                   pl.BlockSpec((ne, tm), lambda i: (0, i))],
        out_shape=[jax.ShapeDtypeStruct((t, d), F32),
                   jax.ShapeDtypeStruct((t, d // 2), U32),
                   jax.ShapeDtypeStruct((ne, t), F32)],
        compiler_params=_cparams(("parallel",)),
    )(oa, ob, x2d, w_out, w_out, mod3, g.reshape(1, d), w_router_t)


def _rows_to_tile(rows, nrow):
    n = rows[0].shape[1]
    ridx = lax.broadcasted_iota(I32, (nrow, n), 0)
    out = jnp.zeros((nrow, n), rows[0].dtype)
    for r, v in enumerate(rows):
        out = jnp.where(ridx == r, jnp.broadcast_to(v, (nrow, n)), out)
    return out


def _route_kernel(lg_ref, rb_ref, eidx_ref, ew_ref, rank_ref, cnt_ref, run_ref):
    @pl.when(pl.program_id(0) == 0)
    def _():
        run_ref[...] = jnp.zeros(run_ref.shape, F32)

    ne, tt = lg_ref.shape
    per = ne // N_GROUPS
    sc = jax.nn.sigmoid(lg_ref[...])
    ch = sc + rb_ref[...]
    neg = -jnp.inf

    sub = lax.broadcasted_iota(I32, (per, tt), 0).astype(F32)
    gsc = []
    for g in range(N_GROUPS):
        cg = ch[g * per:(g + 1) * per]
        m1 = jnp.max(cg, axis=0, keepdims=True)
        first = jnp.min(jnp.where(cg == m1, sub, float(per)), axis=0, keepdims=True)
        m2 = jnp.max(jnp.where(sub == first, neg, cg), axis=0, keepdims=True)
        gsc.append(m1 + m2)
    grp = _rows_to_tile(gsc, N_GROUPS)

    gid = lax.broadcasted_iota(I32, (N_GROUPS, tt), 0).astype(F32)
    gsel = jnp.zeros((N_GROUPS, tt), F32)
    for _ in range(TOPK_GROUPS):
        mx = jnp.max(grp, axis=0, keepdims=True)
        gi = jnp.min(jnp.where(grp == mx, gid, float(N_GROUPS)), axis=0, keepdims=True)
        pick = gid == gi
        gsel = jnp.where(pick, 1.0, gsel)
        grp = jnp.where(pick, neg, grp)

    eid = lax.broadcasted_iota(I32, (ne, tt), 0).astype(F32)
    cm = jnp.full((ne, tt), neg, F32)
    for g in range(N_GROUPS):
        in_g = (eid >= float(g * per)) & (eid < float((g + 1) * per))
        cm = jnp.where(in_g & (jnp.broadcast_to(gsel[g:g + 1], (ne, tt)) > 0.5), ch, cm)

    idx_rows, w_rows = [], []
    onehot = jnp.zeros((ne, tt), F32)
    for _ in range(TOP_K):
        mx = jnp.max(cm, axis=0, keepdims=True)
        ei = jnp.min(jnp.where(cm == mx, eid, float(ne)), axis=0, keepdims=True)
        pick = eid == ei
        idx_rows.append(ei)
        w_rows.append(jnp.sum(jnp.where(pick, sc, 0.0), axis=0, keepdims=True))
        onehot = jnp.where(pick, 1.0, onehot)
        cm = jnp.where(pick, neg, cm)
    wsum = w_rows[0]
    for w in w_rows[1:]:
        wsum = wsum + w
    w_rows = [w / wsum * ROUTED_SCALE for w in w_rows]

    a_i = lax.broadcasted_iota(I32, (tt, tt), 0)
    b_i = lax.broadcasted_iota(I32, (tt, tt), 1)
    upper = jnp.where(a_i < b_i, 1.0, 0.0).astype(BF16)
    rank_full = jnp.dot(onehot.astype(BF16), upper, preferred_element_type=F32) + run_ref[...]
    r_rows = [jnp.sum(jnp.where(eid == ei, rank_full, 0.0), axis=0, keepdims=True) for ei in idx_rows]
    run = run_ref[...] + jnp.sum(onehot, axis=1, keepdims=True)
    run_ref[...] = run

    eidx_ref[...] = _rows_to_tile(idx_rows, TOP_K).astype(I32)
    ew_ref[...] = _rows_to_tile(w_rows, TOP_K)
    rank_ref[...] = _rows_to_tile(r_rows, TOP_K).astype(I32)
    cnt_ref[...] = jnp.broadcast_to(run, cnt_ref.shape)


def _route(logits_t, router_bias, tt=512):
    ne, t = logits_t.shape
    return pl.pallas_call(
        _route_kernel,
        grid=(t // tt,),
        in_specs=[pl.BlockSpec((ne, tt), lambda i: (0, i)),
                  pl.BlockSpec((ne, 1), lambda i: (0, 0))],
        out_specs=[pl.BlockSpec((TOP_K, tt), lambda i: (0, i)),
                   pl.BlockSpec((TOP_K, tt), lambda i: (0, i)),
                   pl.BlockSpec((TOP_K, tt), lambda i: (0, i)),
                   pl.BlockSpec((ne, LANES), lambda i: (0, 0))],
        out_shape=[jax.ShapeDtypeStruct((TOP_K, t), I32),
                   jax.ShapeDtypeStruct((TOP_K, t), F32),
                   jax.ShapeDtypeStruct((TOP_K, t), I32),
                   jax.ShapeDtypeStruct((ne, LANES), F32)],
        scratch_shapes=[pltpu.VMEM((ne, 1), F32)],
        compiler_params=_cparams(("arbitrary",)),
    )(logits_t, router_bias.reshape(ne, 1))


def _dest_kernel(eidx_ref, rank_ref, ps_ref, o_ref):
    ne = ps_ref.shape[0]
    tt = eidx_ref.shape[1]
    eid = lax.broadcasted_iota(I32, (ne, tt), 0)
    ps = jnp.broadcast_to(ps_ref[...], (ne, tt))
    rows = []
    for k in range(TOP_K):
        start = jnp.sum(jnp.where(eid == eidx_ref[k:k + 1, :], ps, 0.0), axis=0, keepdims=True)
        rows.append(start + rank_ref[k:k + 1, :].astype(F32))
    o_ref[...] = _rows_to_tile(rows, TOP_K).astype(I32)


def _dest(eidx, rank, pad_start, tt=2048):
    t = eidx.shape[1]
    tt = min(tt, t)
    ne = pad_start.shape[0]
    return pl.pallas_call(
        _dest_kernel,
        grid=(t // tt,),
        in_specs=[pl.BlockSpec((TOP_K, tt), lambda i: (0, i)),
                  pl.BlockSpec((TOP_K, tt), lambda i: (0, i)),
                  pl.BlockSpec((ne, 1), lambda i: (0, 0))],
        out_specs=pl.BlockSpec((TOP_K, tt), lambda i: (0, i)),
        out_shape=jax.ShapeDtypeStruct((TOP_K, t), I32),
        compiler_params=_cparams(("parallel",)),
    )(eidx, rank, pad_start.astype(F32).reshape(ne, 1))


def _dispatch_kernel(pend_ref, padded_ref, dest_ref, h_ref, xs_ref, zbuf_ref, zsem, sem, *, td, bm):
    i = pl.program_id(0)

    def tail_copy(e):
        start = pl.multiple_of(pend_ref[e] - bm, bm)
        return pltpu.make_async_copy(zbuf_ref, xs_ref.at[pl.ds(start, bm)], zsem)

    @pl.when(i == 0)
    def _():
        zbuf_ref[...] = jnp.zeros(zbuf_ref.shape, zbuf_ref.dtype)

        def start_body(e, c):
            @pl.when(padded_ref[e] > 0)
            def _():
                tail_copy(e).start()
            return c

        def wait_body(e, c):
            @pl.when(padded_ref[e] > 0)
            def _():
                tail_copy(e).wait()
            return c

        lax.fori_loop(0, N_EXPERTS, start_body, 0)
        lax.fori_loop(0, N_EXPERTS, wait_body, 0)

        def unused_copy(b):
            return pltpu.make_async_copy(zbuf_ref, xs_ref.at[pl.ds(pl.multiple_of(b * bm, bm), bm)], zsem)

        def ustart_body(b, c):
            unused_copy(b).start()
            return c

        def uwait_body(b, c):
            unused_copy(b).wait()
            return c

        first_unused = pend_ref[N_EXPERTS - 1] // bm
        lax.fori_loop(first_unused, xs_ref.shape[0] // bm, ustart_body, 0)
        lax.fori_loop(first_unused, xs_ref.shape[0] // bm, uwait_body, 0)

    def tok_body(j, c):
        src = h_ref.at[j]
        for k in range(TOP_K):
            pltpu.make_async_copy(src, xs_ref.at[dest_ref[j * TOP_K + k]], sem).start()
        return c

    lax.fori_loop(0, td, tok_body, 0)
    for _ in range(TOP_K):
        pltpu.make_async_copy(h_ref, xs_ref.at[pl.ds(0, td)], sem).wait()


def _dispatch(pad_end, padded, dest_flat, h2p, n_rows, bm, td=256):
    t, w = h2p.shape
    kern = functools.partial(_dispatch_kernel, td=td, bm=bm)
    return pl.pallas_call(
        kern,
        grid_spec=pltpu.PrefetchScalarGridSpec(
            num_scalar_prefetch=2,
            grid=(t // td,),
            in_specs=[pl.BlockSpec((td * TOP_K,), lambda i, *_: (i,), memory_space=pltpu.SMEM),
                      pl.BlockSpec((td, w), lambda i, *_: (i, 0))],
            out_specs=pl.BlockSpec(memory_space=pl.ANY),
            scratch_shapes=[pltpu.VMEM((bm, w), U32),
                            pltpu.SemaphoreType.DMA(()),
                            pltpu.SemaphoreType.DMA(())]),
        out_shape=jax.ShapeDtypeStruct((n_rows, w), U32),
        compiler_params=_cparams(("arbitrary",)),
    )(pad_end, padded, dest_flat, h2p)


def _ffn(xw, wg_ref, wu_ref, wd_ref):
    half = xw.shape[1]
    left, right = _unpack_halves(xw)
    left = left.astype(BF16)
    right = right.astype(BF16)

    def proj(w_ref):
        return (jnp.dot(left, w_ref[:half, :], preferred_element_type=F32)
                + jnp.dot(right, w_ref[half:, :], preferred_element_type=F32))

    act = (_silu(proj(wg_ref)) * proj(wu_ref)).astype(BF16)
    return jnp.dot(act, wd_ref[...], preferred_element_type=F32)


def _expert_kernel(blk_ref, eid_ref, nvb_ref, x_ref, wg_ref, wu_ref, wd_ref, o_ref, wg_s, wu_s, wd_s):
    i = pl.program_id(0)

    @pl.when((i == 0) | (eid_ref[i] != eid_ref[jnp.maximum(i - 1, 0)]))
    def _():
        wg_s[...] = wg_ref[0].astype(BF16)
        wu_s[...] = wu_ref[0].astype(BF16)
        wd_s[...] = wd_ref[0].astype(BF16)

    @pl.when(i < nvb_ref[0])
    def _():
        o_ref[...] = _pack_halves(_ffn(x_ref[...], wg_s, wu_s, wd_s))

    @pl.when(i >= nvb_ref[0])
    def _():
        o_ref[...] = jnp.zeros(o_ref.shape, o_ref.dtype)


def _experts(blk, eid, nvb, xs, wg, wu, wd, bm):
    n_rows, w = xs.shape
    ne, d, f = wg.shape
    return pl.pallas_call(
        _expert_kernel,
        grid_spec=pltpu.PrefetchScalarGridSpec(
            num_scalar_prefetch=3,
            grid=(n_rows // bm,),
            in_specs=[pl.BlockSpec((bm, w), lambda i, blk, eid, nvb: (blk[i], 0)),
                      pl.BlockSpec((1, d, f), lambda i, blk, eid, nvb: (eid[i], 0, 0)),
                      pl.BlockSpec((1, d, f), lambda i, blk, eid, nvb: (eid[i], 0, 0)),
                      pl.BlockSpec((1, f, d), lambda i, blk, eid, nvb: (eid[i], 0, 0))],
            out_specs=pl.BlockSpec((bm, w), lambda i, blk, eid, nvb: (i, 0)),
            scratch_shapes=[pltpu.VMEM((d, f), BF16), pltpu.VMEM((d, f), BF16), pltpu.VMEM((f, d), BF16)]),
        out_shape=jax.ShapeDtypeStruct((n_rows, w), U32),
        compiler_params=_cparams(("arbitrary",)),
    )(blk, eid, nvb, xs, wg, wu, wd)


def _combine_kernel(dest_ref, y_ref, h_ref, x1_ref, ew_ref, wg_ref, wu_ref, wd_ref, mod_ref, g_ref, o_ref,
                    gbuf_ref, sem, *, tc):
    def tok_body(j, c):
        for k in range(TOP_K):
            pltpu.make_async_copy(y_ref.at[dest_ref[j * TOP_K + k]], gbuf_ref.at[k, j], sem).start()
        return c

    lax.fori_loop(0, tc, tok_body, 0)
    shared = _ffn(h_ref[...], wg_ref, wu_ref, wd_ref)
    for k in range(TOP_K):
        pltpu.make_async_copy(y_ref.at[pl.ds(0, tc)], gbuf_ref.at[k], sem).wait()

    half = h_ref.shape[1]
    ew = ew_ref[...]
    left = shared[:, :half]
    right = shared[:, half:]
    for k in range(TOP_K):
        yl, yr = _unpack_halves(gbuf_ref[k])
        wk = ew[:, k:k + 1]
        left = left + wk * yl
        right = right + wk * yr
    x2 = x1_ref[...] + mod_ref[0, 5:6, :] * jnp.concatenate([left, right], axis=1)
    o_ref[...] = x2 * lax.rsqrt(jnp.mean(x2 * x2, axis=-1, keepdims=True) + EPS) * g_ref[...]


def _combine(dest_flat, y, h2p, x1, ew_t, wsg, wsu, wsd, mod3, g, L, tc=256):
    t, d = x1.shape
    w = h2p.shape[1]
    kern = functools.partial(_combine_kernel, tc=tc)
    return pl.pallas_call(
        kern,
        grid=(t // tc,),
        in_specs=[pl.BlockSpec((tc * TOP_K,), lambda i: (i,), memory_space=pltpu.SMEM),
                  pl.BlockSpec(memory_space=pl.ANY),
                  pl.BlockSpec((tc, w), lambda i: (i, 0)),
                  pl.BlockSpec((tc, d), lambda i: (i, 0)),
                  pl.BlockSpec((tc, TOP_K), lambda i: (i, 0)),
                  pl.BlockSpec(wsg.shape, lambda i: (0, 0)),
                  pl.BlockSpec(wsu.shape, lambda i: (0, 0)),
                  pl.BlockSpec(wsd.shape, lambda i: (0, 0)),
                  pl.BlockSpec((1, 6, d), lambda i: (i * tc // L, 0, 0)),
                  pl.BlockSpec((1, d), lambda i: (0, 0))],
        out_specs=pl.BlockSpec((tc, d), lambda i: (i, 0)),
        out_shape=jax.ShapeDtypeStruct((t, d), F32),
        scratch_shapes=[pltpu.VMEM((TOP_K, tc, w), U32),
                        pltpu.SemaphoreType.DMA(())],
        compiler_params=_cparams(("arbitrary",)),
    )(dest_flat, y, h2p, x1, ew_t, wsg, wsu, wsd, mod3, g.reshape(1, d))


def _split_cols(w, sizes):
    out, off = [], 0
    for s in sizes:
        out.append(w[:, off:off + s])
        off += s
    return out


def kernel(x, c, w_ada, b_ada, norm1_g, w_in, ckv_norm_g, idx_k_norm_g, w_uk, w_uv, rel_bias, lb_logits,
           attn_out_norm_g, hgrn_out_norm_g, w_out, norm2_g, w_router, router_bias, w_e_gate, w_e_up,
           w_e_down, w_s_gate, w_s_up, w_s_down, final_norm_g):
    bsz, L, d = x.shape
    t = bsz * L
    assert w_ada.shape[0] == 1, "single-layer block"
    a_width = A_HEADS * A_HEAD_DIM
    b_width = B_HEADS * B_DIM
    sizes = (a_width, A_KV_RANK, IDX_HEADS * IDX_DIM, IDX_DIM, IDX_HEADS, b_width, b_width, b_width, b_width)
    assert w_in.shape[2] == sum(sizes)

    wq_a, wckv, wiq, wik, wiw, wq_b, wf_b, wi_b, wg_b = _split_cols(w_in[0], sizes)
    w_main = jnp.concatenate([wq_a, wiq, wq_b, wi_b, wg_b], axis=1).astype(BF16)
    w_f = wf_b.astype(BF16)
    aux_pad = LANES - IDX_DIM - IDX_HEADS
    w_aux = jnp.concatenate([wckv, wik, wiw, jnp.zeros((d, aux_pad), F32)], axis=1).astype(BF16)

    mod3 = _ada(c, w_ada[0], b_ada[0]).reshape(bsz, 6, d)
    h1 = _norm1(x, mod3, norm1_g[0]).reshape(t, d)
    g1 = _matmul(h1, w_main, BF16, tm=1024, tn=512)
    fb = _matmul(h1, w_f, F32, tm=1024, tn=512)
    aux = _matmul(h1, w_aux, F32, tm=1024, tn=w_aux.shape[1])
    ckv_n, ik_lo, ik_hi = _kvnorm(aux, ckv_norm_g[0], idx_k_norm_g[0])

    tq = min(256, L)
    o_a = _dsa(g1, aux, ik_lo, ik_hi, ckv_n, w_uk[0].astype(BF16), w_uv[0].astype(BF16),
               _bias_tables(rel_bias, tq), attn_out_norm_g[0], bsz, L, tq)
    o_b = _hgrn(g1, fb, lb_logits, hgrn_out_norm_g[0], bsz, L)

    x1, h2p, logits_t = _out(o_a, o_b, x.reshape(t, d), w_out[0].astype(BF16), mod3, norm2_g[0],
                             w_router[0].T.astype(BF16), L)

    eidx, ew, rank, cnt = _route(logits_t, router_bias[0])

    bm = 256
    counts = cnt[:, 0].astype(I32)
    padded = (counts + bm - 1) // bm * bm
    pad_end = jnp.cumsum(padded)
    pad_start = pad_end - padded
    n_rows = (t * TOP_K + N_EXPERTS * (bm - 1) + bm - 1) // bm * bm
    nb = n_rows // bm
    nvb = pad_end[-1] // bm
    blk = jnp.minimum(jnp.arange(nb, dtype=I32), nvb - 1)
    eid = jnp.minimum(jnp.sum((pad_end[None, :] <= (blk * bm)[:, None]).astype(I32), axis=1), N_EXPERTS - 1)

    dest = _dest(eidx, rank, pad_start)
    dest_flat = dest.T.reshape(t * TOP_K)
    xs = _dispatch(pad_end.astype(I32), padded.astype(I32), dest_flat, h2p, n_rows, bm)

    y = _experts(blk, eid, nvb.reshape(1).astype(I32), xs, w_e_gate[0], w_e_up[0], w_e_down[0], bm)

    out = _combine(dest_flat, y, h2p, x1, ew.T, w_s_gate[0].astype(BF16), w_s_up[0].astype(BF16),
                   w_s_down[0].astype(BF16), mod3, final_norm_g, L)
    return out.reshape(bsz, L, d)
```

```python
import functools
import math

import numpy as np
import jax
import jax.numpy as jnp
from jax import lax
from jax.experimental import pallas as pl
from jax.experimental.pallas import tpu as pltpu

F32 = jnp.float32
BF16 = jnp.bfloat16
I32 = jnp.int32
U32 = jnp.uint32

EPS = 1e-6
A_HEADS = 8
A_HEAD_DIM = 128
A_KV_RANK = 256
IDX_HEADS = 16
IDX_DIM = 64
IDX_TOPK_MAX = 256
B_HEADS = 8
B_DIM = 128
REL_BUCKETS = 32
REL_MAX_DIST = 128
N_EXPERTS = 64
TOP_K = 8
N_GROUPS = 8
TOPK_GROUPS = 4
ROUTED_SCALE = 2.5

VMEM_LIMIT_BYTES = 56 * 1024 * 1024
LANES = 128

NT_DIMS = (((1,), (1,)), ((), ()))
TN_DIMS = (((0,), (0,)), ((), ()))

LOG2E = math.log2(math.e)
INT_MIN = -2 ** 31
KEY_NEG_INF = -2139095041


def _cparams(sem):
    return pltpu.CompilerParams(dimension_semantics=sem, vmem_limit_bytes=VMEM_LIMIT_BYTES)


def _silu(v):
    return v * jax.nn.sigmoid(v)


def _pack_halves(v):
    n = v.shape[1] // 2
    lo = lax.bitcast_convert_type(v[:, :n].astype(BF16).astype(F32), U32)
    hi = lax.bitcast_convert_type(v[:, n:].astype(BF16).astype(F32), U32)
    return lax.shift_right_logical(lo, jnp.uint32(16)) | (hi & jnp.uint32(0xFFFF0000))


def _unpack_halves(w):
    left = lax.bitcast_convert_type(lax.shift_left(w, jnp.uint32(16)), F32)
    right = lax.bitcast_convert_type(w & jnp.uint32(0xFFFF0000), F32)
    return left, right


def _ada_kernel(c_ref, w_ref, b_ref, o_ref):
    a = _silu(c_ref[...]).astype(BF16)
    o_ref[...] = jnp.dot(a, w_ref[...].astype(BF16), preferred_element_type=F32) + b_ref[...]


def _ada(c, w, b, tn=1024):
    bsz, d = c.shape
    n = w.shape[1]
    return pl.pallas_call(
        _ada_kernel,
        grid=(n // tn,),
        in_specs=[pl.BlockSpec((bsz, d), lambda j: (0, 0)),
                  pl.BlockSpec((d, tn), lambda j: (0, j)),
                  pl.BlockSpec((1, tn), lambda j: (0, j))],
        out_specs=pl.BlockSpec((bsz, tn), lambda j: (0, j)),
        out_shape=jax.ShapeDtypeStruct((bsz, n), F32),
        compiler_params=_cparams(("arbitrary",)),
    )(c, w, b.reshape(1, n))


def _norm1_kernel(x_ref, mod_ref, g_ref, o_ref):
    x = x_ref[0]
    y = x * lax.rsqrt(jnp.mean(x * x, axis=-1, keepdims=True) + EPS) * g_ref[...]
    sh = mod_ref[0, 0:1, :]
    sc = mod_ref[0, 1:2, :]
    o_ref[0] = (y * (1.0 + sc) + sh).astype(o_ref.dtype)


def _norm1(x, mod3, g, tm=512):
    bsz, L, d = x.shape
    return pl.pallas_call(
        _norm1_kernel,
        grid=(bsz, L // tm),
        in_specs=[pl.BlockSpec((1, tm, d), lambda b, i: (b, i, 0)),
                  pl.BlockSpec((1, 6, d), lambda b, i: (b, 0, 0)),
                  pl.BlockSpec((1, d), lambda b, i: (0, 0))],
        out_specs=pl.BlockSpec((1, tm, d), lambda b, i: (b, i, 0)),
        out_shape=jax.ShapeDtypeStruct((bsz, L, d), BF16),
        compiler_params=_cparams(("parallel", "parallel")),
    )(x, mod3, g.reshape(1, d))


def _mm_kernel(a_ref, w_ref, o_ref):
    o_ref[...] = jnp.dot(a_ref[...], w_ref[...], preferred_element_type=F32).astype(o_ref.dtype)


def _matmul(a, w, out_dtype, tm, tn):
    m, k = a.shape
    n = w.shape[1]
    return pl.pallas_call(
        _mm_kernel,
        grid=(m // tm, n // tn),
        in_specs=[pl.BlockSpec((tm, k), lambda i, j: (i, 0)),
                  pl.BlockSpec((k, tn), lambda i, j: (0, j))],
        out_specs=pl.BlockSpec((tm, tn), lambda i, j: (i, j)),
        out_shape=jax.ShapeDtypeStruct((m, n), out_dtype),
        compiler_params=_cparams(("parallel", "arbitrary")),
    )(a, w)


def _kvnorm_kernel(aux_ref, gc_ref, gk_ref, ckv_ref, iklo_ref, ikhi_ref):
    ckv = aux_ref[:, :A_KV_RANK]
    ckv_ref[...] = (ckv * lax.rsqrt(jnp.mean(ckv * ckv, axis=-1, keepdims=True) + EPS)
                    * gc_ref[...]).astype(BF16)
    v = aux_ref[:, A_KV_RANK:A_KV_RANK + LANES]
    lane = lax.broadcasted_iota(I32, v.shape, 1)
    ik = jnp.where(lane < IDX_DIM, v, 0.0)
    ms = jnp.sum(ik * ik, axis=-1, keepdims=True) * (1.0 / IDX_DIM)
    ikn = ik * lax.rsqrt(ms + EPS) * gk_ref[...]
    iklo_ref[...] = ikn.astype(BF16)
    ikhi_ref[...] = pltpu.roll(ikn, IDX_DIM, 1).astype(BF16)


def _kvnorm(aux, gc, gk, tm=1024):
    t = aux.shape[0]
    gk_pad = jnp.concatenate([gk, jnp.zeros((LANES - IDX_DIM,), F32)]).reshape(1, LANES)
    return pl.pallas_call(
        _kvnorm_kernel,
        grid=(t // tm,),
        in_specs=[pl.BlockSpec((tm, aux.shape[1]), lambda i: (i, 0)),
                  pl.BlockSpec((1, A_KV_RANK), lambda i: (0, 0)),
                  pl.BlockSpec((1, LANES), lambda i: (0, 0))],
        out_specs=[pl.BlockSpec((tm, A_KV_RANK), lambda i: (i, 0)),
                   pl.BlockSpec((tm, LANES), lambda i: (i, 0)),
                   pl.BlockSpec((tm, LANES), lambda i: (i, 0))],
        out_shape=[jax.ShapeDtypeStruct((t, A_KV_RANK), BF16),
                   jax.ShapeDtypeStruct((t, LANES), BF16),
                   jax.ShapeDtypeStruct((t, LANES), BF16)],
        compiler_params=_cparams(("parallel",)),
    )(aux, gc.reshape(1, A_KV_RANK), gk_pad)


def _t5_bucket(rel):
    n = jnp.maximum(rel, 0)
    max_exact = REL_BUCKETS // 2
    n_large = jnp.maximum(n, max_exact).astype(F32)
    large = max_exact + (jnp.log(n_large / max_exact) / math.log(REL_MAX_DIST / max_exact)
                         * (REL_BUCKETS - max_exact)).astype(I32)
    large = jnp.minimum(large, REL_BUCKETS - 1)
    return jnp.where(n < max_exact, n, large)


def _bias_tables(rel_bias, tq):
    assert tq + 1 >= REL_MAX_DIST
    nh = rel_bias.shape[1]
    dist = jnp.maximum(jnp.arange(3 * tq + 1, dtype=I32) - tq, 0)
    v = rel_bias.astype(F32)[_t5_bucket(dist)].T * LOG2E
    n = v.shape[1]
    x = jnp.broadcast_to(v[:, None, :], (nh, tq, n)).reshape(nh, tq * n)[:, :tq * (n - 1)].reshape(nh, tq, n - 1)
    near = x[:, :, tq:2 * tq]
    prev = x[:, :, 2 * tq:3 * tq]
    far = jnp.broadcast_to(v[:, n - 1][:, None, None], near.shape)
    return jnp.stack([near, prev, far])


def _dsa_kernel(qa_ref, iq_ref, aux_ref, iklo_ref, ikhi_ref, ckv_ref, ckvt_ref, wuk_ref, wuvt_ref, bias_ref,
                g_ref, o_ref, iqt_ref, iwt_ref, key_ref, qlt_ref, m_ref, l_ref, acc_ref, tie_ref, madd_ref,
                *, tq, topk):
    i = pl.program_id(1)
    nh = A_HEADS
    npair = IDX_HEADS // 2

    r_i = lax.broadcasted_iota(I32, (LANES, LANES), 0)
    c_i = lax.broadcasted_iota(I32, (LANES, LANES), 1)
    eye = jnp.where(r_i == c_i, 1.0, 0.0).astype(BF16)
    for p in range(npair):
        iqt_ref[:, p * tq:(p + 1) * tq] = lax.dot_general(
            eye, iq_ref[:, p * LANES:(p + 1) * LANES], NT_DIMS, preferred_element_type=F32).astype(BF16)
    iwt_ref[...] = (jnp.transpose(aux_ref[...])[IDX_DIM:IDX_DIM + IDX_HEADS, :]
                    * (IDX_HEADS ** -0.5 * IDX_DIM ** -0.5))
    for h in range(nh):
        ql = lax.dot_general(wuk_ref[h], qa_ref[:, h * A_HEAD_DIM:(h + 1) * A_HEAD_DIM], NT_DIMS,
                             preferred_element_type=F32)
        qlt_ref[:, h * tq:(h + 1) * tq] = (ql * (A_HEAD_DIM ** -0.5 * LOG2E)).astype(BF16)

    kpos = lax.broadcasted_iota(I32, (tq, tq), 0)
    qpos = lax.broadcasted_iota(I32, (tq, tq), 1) + i * tq

    def score_body(kc, carry):
        off = pl.multiple_of(kc * tq, tq)
        klo = iklo_ref[0, pl.ds(off, tq), :]
        khi = ikhi_ref[0, pl.ds(off, tq), :]
        acc = jnp.zeros((tq, tq), F32)
        for p in range(npair):
            rhs = iqt_ref[:, p * tq:(p + 1) * tq]
            se = jnp.dot(klo, rhs, preferred_element_type=F32)
            so = jnp.dot(khi, rhs, preferred_element_type=F32)
            acc = acc + jnp.maximum(se, 0.0) * iwt_ref[2 * p:2 * p + 1, :]
            acc = acc + jnp.maximum(so, 0.0) * iwt_ref[2 * p + 1:2 * p + 2, :]
        bits = lax.bitcast_convert_type(acc, I32)
        key = jnp.where(bits >= 0, bits, bits ^ jnp.int32(0x7FFFFFFF))
        key_ref[kc] = jnp.where(kpos + off <= qpos, key, jnp.int32(KEY_NEG_INF))
        return carry

    lax.fori_loop(0, i + 1, score_body, 0)

    def count_ge(cand):
        def body(kc, c):
            hit = jnp.where(key_ref[kc] >= cand, 1.0, 0.0)
            return c + jnp.sum(hit.reshape(tq // 8, 8, tq), axis=0)
        c = lax.fori_loop(0, i + 1, body, jnp.zeros((8, tq), F32))
        return jnp.sum(c, axis=0, keepdims=True)

    kf = float(topk)
    thr = jnp.where(count_ge(jnp.zeros((1, tq), I32)) >= kf, jnp.int32(0), jnp.int32(INT_MIN))

    def bit_body(j, thr):
        cand = thr | lax.shift_left(jnp.int32(1), 30 - j)
        return jnp.where(count_ge(cand) >= kf, cand, thr)

    thr = lax.fori_loop(0, 31, bit_body, thr)

    def count_gt():
        def body(kc, c):
            hit = jnp.where(key_ref[kc] > thr, 1.0, 0.0)
            return c + jnp.sum(hit.reshape(tq // 8, 8, tq), axis=0)
        c = lax.fori_loop(0, i + 1, body, jnp.zeros((8, tq), F32))
        return jnp.sum(c, axis=0, keepdims=True)

    need = kf - count_gt()
    tied = (count_ge(thr) > kf) & (thr > jnp.int32(KEY_NEG_INF))
    has_tie = jnp.max(jnp.where(tied, 1.0, 0.0)) > 0.0
    tie_ref[...] = jnp.zeros(tie_ref.shape, F32)

    m_ref[...] = jnp.full(m_ref.shape, -jnp.inf, F32)
    l_ref[...] = jnp.zeros(l_ref.shape, F32)
    acc_ref[...] = jnp.zeros(acc_ref.shape, F32)

    def att_body(kc, carry):
        off = pl.multiple_of(kc * tq, tq)
        ckv = ckv_ref[0, pl.ds(off, tq), :]
        ckvt = ckvt_ref[0, kc]
        key = key_ref[kc]
        causal = key > jnp.int32(KEY_NEG_INF)

        @pl.when(jnp.logical_not(has_tie))
        def _():
            madd_ref[...] = jnp.where((key >= thr) & causal, 0.0, -jnp.inf)

        @pl.when(has_tie)
        def _():
            eq = key == thr
            eqf = jnp.where(eq, 1.0, 0.0)
            before = (lax.broadcasted_iota(I32, (tq, tq), 1) < lax.broadcasted_iota(I32, (tq, tq), 0))
            rank = jnp.dot(jnp.where(before, 1.0, 0.0).astype(BF16), eqf.astype(BF16),
                           preferred_element_type=F32) + tie_ref[...]
            keep = (key > thr) | (eq & (rank < need))
            madd_ref[...] = jnp.where(keep & causal, 0.0, -jnp.inf)
            tie_ref[...] = tie_ref[...] + jnp.sum(eqf, axis=0, keepdims=True)

        madd = madd_ref[...]
        d = jnp.minimum(i - kc, 2)
        for h in range(nh):
            s = jnp.dot(ckv, qlt_ref[:, h * tq:(h + 1) * tq], preferred_element_type=F32)
            s = s + (bias_ref[d, h] + madd)
            m_old = m_ref[h:h + 1, :]
            m_new = jnp.maximum(m_old, jnp.max(s, axis=0, keepdims=True))
            m_safe = jnp.where(m_new == -jnp.inf, 0.0, m_new)
            alpha = jnp.exp2(m_old - m_safe)
            p = jnp.exp2(s - m_safe)
            l_ref[h:h + 1, :] = alpha * l_ref[h:h + 1, :] + jnp.sum(p, axis=0, keepdims=True)
            acc_ref[h] = alpha * acc_ref[h] + jnp.dot(ckvt, p.astype(BF16), preferred_element_type=F32)
            m_ref[h:h + 1, :] = m_new
        return carry

    lax.fori_loop(0, i + 1, att_body, 0)

    outs = []
    for h in range(nh):
        o_lat = (acc_ref[h] / l_ref[h:h + 1, :]).astype(BF16)
        outs.append(jnp.transpose(jnp.dot(wuvt_ref[h], o_lat, preferred_element_type=F32)))
    o = jnp.concatenate(outs, axis=1)
    o = o * lax.rsqrt(jnp.mean(o * o, axis=-1, keepdims=True) + EPS) * g_ref[...]
    o_ref[...] = o.astype(o_ref.dtype)


def _dsa(g1, aux, ik_lo, ik_hi, ckv_n, w_uk, w_uv, bias_tab, g, bsz, L, tq):
    t = bsz * L
    nq = L // tq
    topk = min(IDX_TOPK_MAX, L // 4)
    aux_blk = A_KV_RANK // LANES
    kern = functools.partial(_dsa_kernel, tq=tq, topk=topk)
    width = A_HEADS * A_HEAD_DIM
    ckv3 = ckv_n.reshape(bsz, L, A_KV_RANK)
    ckvt = ckv_n.reshape(bsz, nq, tq, A_KV_RANK).transpose(0, 1, 3, 2)
    return pl.pallas_call(
        kern,
        grid=(bsz, nq),
        in_specs=[pl.BlockSpec((tq, width), lambda b, i: (b * nq + i, 0)),
                  pl.BlockSpec((tq, IDX_HEADS * IDX_DIM), lambda b, i: (b * nq + i, 1)),
                  pl.BlockSpec((tq, LANES), lambda b, i: (b * nq + i, aux_blk)),
                  pl.BlockSpec((1, L, LANES), lambda b, i: (b, 0, 0)),
                  pl.BlockSpec((1, L, LANES), lambda b, i: (b, 0, 0)),
                  pl.BlockSpec((1, L, A_KV_RANK), lambda b, i: (b, 0, 0)),
                  pl.BlockSpec((1, nq, A_KV_RANK, tq), lambda b, i: (b, 0, 0, 0)),
                  pl.BlockSpec((A_HEADS, A_KV_RANK, A_HEAD_DIM), lambda b, i: (0, 0, 0)),
                  pl.BlockSpec((A_HEADS, A_HEAD_DIM, A_KV_RANK), lambda b, i: (0, 0, 0)),
                  pl.BlockSpec((3, A_HEADS, tq, tq), lambda b, i: (0, 0, 0, 0)),
                  pl.BlockSpec((1, width), lambda b, i: (0, 0))],
        out_specs=pl.BlockSpec((tq, width), lambda b, i: (b * nq + i, 0)),
        out_shape=jax.ShapeDtypeStruct((t, width), BF16),
        scratch_shapes=[pltpu.VMEM((LANES, IDX_HEADS // 2 * tq), BF16),
                        pltpu.VMEM((IDX_HEADS, tq), F32),
                        pltpu.VMEM((nq, tq, tq), I32),
                        pltpu.VMEM((A_KV_RANK, A_HEADS * tq), BF16),
                        pltpu.VMEM((A_HEADS, tq), F32),
                        pltpu.VMEM((A_HEADS, tq), F32),
                        pltpu.VMEM((A_HEADS, A_KV_RANK, tq), F32),
                        pltpu.VMEM((1, tq), F32),
                        pltpu.VMEM((tq, tq), F32)],
        compiler_params=_cparams(("parallel", "arbitrary")),
    )(g1, g1, aux, ik_lo.reshape(bsz, L, LANES), ik_hi.reshape(bsz, L, LANES),
      ckv3, ckvt, w_uk, jnp.transpose(w_uv, (0, 2, 1)), bias_tab, g.reshape(1, width))


def _hgrn_kernel(q_ref, i_ref, gate_ref, f_ref, lbl_ref, ng_ref, o_ref, st_ref, *, chunk, rblk):
    @pl.when(pl.program_id(1) == 0)
    def _():
        st_ref[...] = jnp.zeros(st_ref.shape, F32)

    ll = lbl_ref[...]
    ex = jnp.exp(ll - jnp.max(ll, axis=0, keepdims=True))
    lb_all = ex[0:1] / jnp.sum(ex, axis=0, keepdims=True)

    r_i = lax.broadcasted_iota(I32, (chunk, chunk), 0)
    c_i = lax.broadcasted_iota(I32, (chunk, chunk), 1)
    tri = jnp.where(r_i >= c_i, 1.0, 0.0).astype(BF16)
    row_k = lax.broadcasted_iota(I32, (chunk, B_DIM), 0)

    for h in range(B_HEADS):
        sl = slice(h * B_DIM, (h + 1) * B_DIM)
        lb = lb_all[:, sl]
        f = lb + (1.0 - lb) * jax.nn.sigmoid(f_ref[:, sl])
        lf = jnp.log(f)
        kk = 1.0 - f
        l1 = lf.astype(BF16)
        r1 = lf - l1.astype(F32)
        l2 = r1.astype(BF16)
        l3 = (r1 - l2.astype(F32)).astype(BF16)
        cs = jnp.dot(tri, jnp.concatenate([l1, l2, l3], axis=1), preferred_element_type=F32)
        bc = cs[:, :B_DIM] + cs[:, B_DIM:2 * B_DIM] + cs[:, 2 * B_DIM:]

        q = q_ref[:, sl].astype(F32)
        v = i_ref[:, sl]
        st = st_ref[h]
        o = lax.dot_general((q * jnp.exp(bc)).astype(BF16), st.astype(BF16), NT_DIMS,
                            preferred_element_type=F32)
        parts = []
        for r in range(chunk // rblk):
            lo, hi = r * rblk, (r + 1) * rblk
            base = bc[lo - 1:lo] if r > 0 else jnp.zeros((1, B_DIM), F32)
            qt = (q[lo:hi] * jnp.exp(bc[lo:hi] - base)).astype(BF16)
            kt = (kk * jnp.exp(jnp.where(row_k < hi, base - bc, 0.0))).astype(BF16)
            parts.append(lax.dot_general(qt, kt, NT_DIMS, preferred_element_type=F32))
        sc = jnp.where(c_i <= r_i, jnp.concatenate(parts, axis=0), 0.0).astype(BF16)
        o = o + jnp.dot(sc, v, preferred_element_type=F32)

        last = bc[chunk - 1:chunk]
        kd = (kk * jnp.exp(last - bc)).astype(BF16)
        st_ref[h] = st * jnp.exp(last) + lax.dot_general(v, kd, TN_DIMS, preferred_element_type=F32)

        y = o * lax.rsqrt(jnp.mean(o * o, axis=-1, keepdims=True) + EPS) * ng_ref[:, sl]
        o_ref[:, sl] = (y * _silu(gate_ref[:, sl].astype(F32))).astype(o_ref.dtype)


def _hgrn(g1, fb, lb_logits, ng, bsz, L, chunk=128, rblk=16):
    t = bsz * L
    nc = L // chunk
    width = B_HEADS * B_DIM
    kern = functools.partial(_hgrn_kernel, chunk=chunk, rblk=rblk)
    return pl.pallas_call(
        kern,
        grid=(bsz, nc),
        in_specs=[pl.BlockSpec((chunk, width), lambda b, c: (b * nc + c, 2)),
                  pl.BlockSpec((chunk, width), lambda b, c: (b * nc + c, 3)),
                  pl.BlockSpec((chunk, width), lambda b, c: (b * nc + c, 4)),
                  pl.BlockSpec((chunk, width), lambda b, c: (b * nc + c, 0)),
                  pl.BlockSpec(lb_logits.shape, lambda b, c: (0, 0)),
                  pl.BlockSpec((1, width), lambda b, c: (0, 0))],
        out_specs=pl.BlockSpec((chunk, width), lambda b, c: (b * nc + c, 0)),
        out_shape=jax.ShapeDtypeStruct((t, width), BF16),
        scratch_shapes=[pltpu.VMEM((B_HEADS, B_DIM, B_DIM), F32)],
        compiler_params=_cparams(("parallel", "arbitrary")),
    )(g1, g1, g1, fb, lb_logits, ng.reshape(1, width))


def _out_kernel(oa_ref, ob_ref, x_ref, wa_ref, wb_ref, mod_ref, g_ref, wr_ref, x1_ref, h2_ref, lg_ref):
    mix = jnp.dot(oa_ref[...], wa_ref[...], preferred_element_type=F32)
    mix = mix + jnp.dot(ob_ref[...], wb_ref[...], preferred_element_type=F32)
    x1 = x_ref[...] + mod_ref[0, 2:3, :] * mix
    x1_ref[...] = x1
    y = x1 * lax.rsqrt(jnp.mean(x1 * x1, axis=-1, keepdims=True) + EPS) * g_ref[...]
    h2 = y * (1.0 + mod_ref[0, 4:5, :]) + mod_ref[0, 3:4, :]
    h2_ref[...] = _pack_halves(h2)
    lg_ref[...] = lax.dot_general(wr_ref[...], h2.astype(BF16), NT_DIMS, preferred_element_type=F32)


def _out(oa, ob, x2d, w_out, mod3, g, w_router_t, L, tm=256):
    t, d = x2d.shape
    half = oa.shape[1]
    ne = w_router_t.shape[0]
    return pl.pallas_call(
        _out_kernel,
        grid=(t // tm,),
        in_specs=[pl.BlockSpec((tm, half), lambda i: (i, 0)),
                  pl.BlockSpec((tm, half), lambda i: (i, 0)),
                  pl.BlockSpec((tm, d), lambda i: (i, 0)),
                  pl.BlockSpec((half, d), lambda i: (0, 0)),
                  pl.BlockSpec((half, d), lambda i: (1, 0)),
                  pl.BlockSpec((1, 6, d), lambda i: (i * tm // L, 0, 0)),
                  pl.BlockSpec((1, d), lambda i: (0, 0)),
                  pl.BlockSpec((ne, d), lambda i: (0, 0))],
        out_specs=[pl.BlockSpec((tm, d), lambda i: (i, 0)),
                   pl.BlockSpec((tm, d // 2), lambda i: (i, 0)),
                   pl.BlockSpec((ne, tm), lambda i: (0, i))],
        out_shape=[jax.ShapeDtypeStruct((t, d), F32),
                   jax.ShapeDtypeStruct((t, d // 2), U32),
                   jax.ShapeDtypeStruct((ne, t), F32)],
        compiler_params=_cparams(("parallel",)),
    )(oa, ob, x2d, w_out, w_out, mod3, g.reshape(1, d), w_router_t)


def _rows_to_tile(rows, nrow):
    n = rows[0].shape[1]
    ridx = lax.broadcasted_iota(I32, (nrow, n), 0)
    out = jnp.zeros((nrow, n), rows[0].dtype)
    for r, v in enumerate(rows):
        out = jnp.where(ridx == r, jnp.broadcast_to(v, (nrow, n)), out)
    return out


def _route_kernel(lg_ref, rb_ref, eidx_ref, ew_ref, rank_ref, cnt_ref, run_ref):
    @pl.when(pl.program_id(0) == 0)
    def _():
        run_ref[...] = jnp.zeros(run_ref.shape, F32)

    ne, tt = lg_ref.shape
    per = ne // N_GROUPS
    sc = jax.nn.sigmoid(lg_ref[...])
    ch = sc + rb_ref[...]
    neg = -jnp.inf

    sub = lax.broadcasted_iota(I32, (per, tt), 0).astype(F32)
    gsc = []
    for g in range(N_GROUPS):
        cg = ch[g * per:(g + 1) * per]
        m1 = jnp.max(cg, axis=0, keepdims=True)
        first = jnp.min(jnp.where(cg == m1, sub, float(per)), axis=0, keepdims=True)
        m2 = jnp.max(jnp.where(sub == first, neg, cg), axis=0, keepdims=True)
        gsc.append(m1 + m2)
    grp = _rows_to_tile(gsc, N_GROUPS)

    gid = lax.broadcasted_iota(I32, (N_GROUPS, tt), 0).astype(F32)
    gsel = jnp.zeros((N_GROUPS, tt), F32)
    for _ in range(TOPK_GROUPS):
        mx = jnp.max(grp, axis=0, keepdims=True)
        gi = jnp.min(jnp.where(grp == mx, gid, float(N_GROUPS)), axis=0, keepdims=True)
        pick = gid == gi
        gsel = jnp.where(pick, 1.0, gsel)
        grp = jnp.where(pick, neg, grp)

    eid = lax.broadcasted_iota(I32, (ne, tt), 0).astype(F32)
    cm = jnp.full((ne, tt), neg, F32)
    for g in range(N_GROUPS):
        in_g = (eid >= float(g * per)) & (eid < float((g + 1) * per))
        cm = jnp.where(in_g & (jnp.broadcast_to(gsel[g:g + 1], (ne, tt)) > 0.5), ch, cm)

    idx_rows, w_rows = [], []
    onehot = jnp.zeros((ne, tt), F32)
    for _ in range(TOP_K):
        mx = jnp.max(cm, axis=0, keepdims=True)
        ei = jnp.min(jnp.where(cm == mx, eid, float(ne)), axis=0, keepdims=True)
        pick = eid == ei
        idx_rows.append(ei)
        w_rows.append(jnp.sum(jnp.where(pick, sc, 0.0), axis=0, keepdims=True))
        onehot = jnp.where(pick, 1.0, onehot)
        cm = jnp.where(pick, neg, cm)
    wsum = w_rows[0]
    for w in w_rows[1:]:
        wsum = wsum + w
    w_rows = [w / wsum * ROUTED_SCALE for w in w_rows]

    a_i = lax.broadcasted_iota(I32, (tt, tt), 0)
    b_i = lax.broadcasted_iota(I32, (tt, tt), 1)
    upper = jnp.where(a_i < b_i, 1.0, 0.0).astype(BF16)
    rank_full = jnp.dot(onehot.astype(BF16), upper, preferred_element_type=F32) + run_ref[...]
    r_rows = [jnp.sum(jnp.where(eid == ei, rank_full, 0.0), axis=0, keepdims=True) for ei in idx_rows]
    run = run_ref[...] + jnp.sum(onehot, axis=1, keepdims=True)
    run_ref[...] = run

    eidx_ref[...] = _rows_to_tile(idx_rows, TOP_K).astype(I32)
    ew_ref[...] = _rows_to_tile(w_rows, TOP_K)
    rank_ref[...] = _rows_to_tile(r_rows, TOP_K).astype(I32)
    cnt_ref[...] = jnp.broadcast_to(run, cnt_ref.shape)


def _route(logits_t, router_bias, tt=512):
    ne, t = logits_t.shape
    return pl.pallas_call(
        _route_kernel,
        grid=(t // tt,),
        in_specs=[pl.BlockSpec((ne, tt), lambda i: (0, i)),
                  pl.BlockSpec((ne, 1), lambda i: (0, 0))],
        out_specs=[pl.BlockSpec((TOP_K, tt), lambda i: (0, i)),
                   pl.BlockSpec((TOP_K, tt), lambda i: (0, i)),
                   pl.BlockSpec((TOP_K, tt), lambda i: (0, i)),
                   pl.BlockSpec((ne, LANES), lambda i: (0, 0))],
        out_shape=[jax.ShapeDtypeStruct((TOP_K, t), I32),
                   jax.ShapeDtypeStruct((TOP_K, t), F32),
                   jax.ShapeDtypeStruct((TOP_K, t), I32),
                   jax.ShapeDtypeStruct((ne, LANES), F32)],
        scratch_shapes=[pltpu.VMEM((ne, 1), F32)],
        compiler_params=_cparams(("arbitrary",)),
    )(logits_t, router_bias.reshape(ne, 1))


def _dest_kernel(eidx_ref, rank_ref, ps_ref, o_ref):
    ne = ps_ref.shape[0]
    tt = eidx_ref.shape[1]
    eid = lax.broadcasted_iota(I32, (ne, tt), 0)
    ps = jnp.broadcast_to(ps_ref[...], (ne, tt))
    rows = []
    for k in range(TOP_K):
        start = jnp.sum(jnp.where(eid == eidx_ref[k:k + 1, :], ps, 0.0), axis=0, keepdims=True)
        rows.append(start + rank_ref[k:k + 1, :].astype(F32))
    o_ref[...] = _rows_to_tile(rows, TOP_K).astype(I32)


def _dest(eidx, rank, pad_start, tt=2048):
    t = eidx.shape[1]
    tt = min(tt, t)
    ne = pad_start.shape[0]
    return pl.pallas_call(
        _dest_kernel,
        grid=(t // tt,),
        in_specs=[pl.BlockSpec((TOP_K, tt), lambda i: (0, i)),
                  pl.BlockSpec((TOP_K, tt), lambda i: (0, i)),
                  pl.BlockSpec((ne, 1), lambda i: (0, 0))],
        out_specs=pl.BlockSpec((TOP_K, tt), lambda i: (0, i)),
        out_shape=jax.ShapeDtypeStruct((TOP_K, t), I32),
        compiler_params=_cparams(("parallel",)),
    )(eidx, rank, pad_start.astype(F32).reshape(ne, 1))


def _dispatch_kernel(pend_ref, padded_ref, dest_ref, h_ref, xs_ref, zbuf_ref, zsem, sem, *, td, bm):
    i = pl.program_id(0)

    def tail_copy(e):
        start = pl.multiple_of(pend_ref[e] - bm, bm)
        return pltpu.make_async_copy(zbuf_ref, xs_ref.at[pl.ds(start, bm)], zsem)

    @pl.when(i == 0)
    def _():
        zbuf_ref[...] = jnp.zeros(zbuf_ref.shape, zbuf_ref.dtype)

        def start_body(e, c):
            @pl.when(padded_ref[e] > 0)
            def _():
                tail_copy(e).start()
            return c

        def wait_body(e, c):
            @pl.when(padded_ref[e] > 0)
            def _():
                tail_copy(e).wait()
            return c

        lax.fori_loop(0, N_EXPERTS, start_body, 0)
        lax.fori_loop(0, N_EXPERTS, wait_body, 0)

        def unused_copy(b):
            return pltpu.make_async_copy(zbuf_ref, xs_ref.at[pl.ds(pl.multiple_of(b * bm, bm), bm)], zsem)

        def ustart_body(b, c):
            unused_copy(b).start()
            return c

        def uwait_body(b, c):
            unused_copy(b).wait()
            return c

        first_unused = pend_ref[N_EXPERTS - 1] // bm
        lax.fori_loop(first_unused, xs_ref.shape[0] // bm, ustart_body, 0)
        lax.fori_loop(first_unused, xs_ref.shape[0] // bm, uwait_body, 0)

    def tok_body(j, c):
        src = h_ref.at[j]
        for k in range(TOP_K):
            pltpu.make_async_copy(src, xs_ref.at[dest_ref[j * TOP_K + k]], sem).start()
        return c

    lax.fori_loop(0, td, tok_body, 0)
    for _ in range(TOP_K):
        pltpu.make_async_copy(h_ref, xs_ref.at[pl.ds(0, td)], sem).wait()


def _dispatch(pad_end, padded, dest_flat, h2p, n_rows, bm, td=256):
    t, w = h2p.shape
    kern = functools.partial(_dispatch_kernel, td=td, bm=bm)
    return pl.pallas_call(
        kern,
        grid_spec=pltpu.PrefetchScalarGridSpec(
            num_scalar_prefetch=2,
            grid=(t // td,),
            in_specs=[pl.BlockSpec((td * TOP_K,), lambda i, *_: (i,), memory_space=pltpu.SMEM),
                      pl.BlockSpec((td, w), lambda i, *_: (i, 0))],
            out_specs=pl.BlockSpec(memory_space=pl.ANY),
            scratch_shapes=[pltpu.VMEM((bm, w), U32),
                            pltpu.SemaphoreType.DMA(()),
                            pltpu.SemaphoreType.DMA(())]),
        out_shape=jax.ShapeDtypeStruct((n_rows, w), U32),
        compiler_params=_cparams(("arbitrary",)),
    )(pad_end, padded, dest_flat, h2p)


def _ffn(xw, wg_ref, wu_ref, wd_ref):
    half = xw.shape[1]
    left, right = _unpack_halves(xw)
    left = left.astype(BF16)
    right = right.astype(BF16)

    def proj(w_ref):
        return (jnp.dot(left, w_ref[:half, :], preferred_element_type=F32)
                + jnp.dot(right, w_ref[half:, :], preferred_element_type=F32))

    act = (_silu(proj(wg_ref)) * proj(wu_ref)).astype(BF16)
    return jnp.dot(act, wd_ref[...], preferred_element_type=F32)


def _expert_kernel(blk_ref, eid_ref, first_ref, slot_ref, nxt_ref, more_ref, nvb_ref,
                   x_ref, wg_hbm, wu_hbm, wd_hbm, o_ref,
                   wg_f, wu_f, wd_f, wg_s, wu_s, wd_s, sems):
    i = pl.program_id(0)

    def weight_copies(e, slot):
        return (pltpu.make_async_copy(wg_hbm.at[e], wg_f.at[slot], sems.at[slot]),
                pltpu.make_async_copy(wu_hbm.at[e], wu_f.at[slot], sems.at[slot]),
                pltpu.make_async_copy(wd_hbm.at[e], wd_f.at[slot], sems.at[slot]))

    @pl.when(i == 0)
    def _():
        for cp in weight_copies(eid_ref[0], 0):
            cp.start()

    @pl.when(first_ref[i] == 1)
    def _():
        slot = slot_ref[i]
        for cp in weight_copies(eid_ref[i], slot):
            cp.wait()

        @pl.when(more_ref[i] == 1)
        def _():
            for cp in weight_copies(nxt_ref[i], 1 - slot):
                cp.start()

        wg_s[...] = wg_f[slot].astype(BF16)
        wu_s[...] = wu_f[slot].astype(BF16)
        wd_s[...] = wd_f[slot].astype(BF16)

    @pl.when(i < nvb_ref[0])
    def _():
        o_ref[...] = _pack_halves(_ffn(x_ref[...], wg_s, wu_s, wd_s))

    @pl.when(i >= nvb_ref[0])
    def _():
        o_ref[...] = jnp.zeros(o_ref.shape, o_ref.dtype)


def _experts(blk, eid, first, slot, nxt, more, nvb, xs, wg, wu, wd, bm):
    n_rows, w = xs.shape
    ne, d, f = wg.shape
    return pl.pallas_call(
        _expert_kernel,
        grid_spec=pltpu.PrefetchScalarGridSpec(
            num_scalar_prefetch=7,
            grid=(n_rows // bm,),
            in_specs=[pl.BlockSpec((bm, w), lambda i, blk, *_: (blk[i], 0)),
                      pl.BlockSpec(memory_space=pl.ANY),
                      pl.BlockSpec(memory_space=pl.ANY),
                      pl.BlockSpec(memory_space=pl.ANY)],
            out_specs=pl.BlockSpec((bm, w), lambda i, *_: (i, 0)),
            scratch_shapes=[pltpu.VMEM((2, d, f), F32), pltpu.VMEM((2, d, f), F32), pltpu.VMEM((2, f, d), F32),
                            pltpu.VMEM((d, f), BF16), pltpu.VMEM((d, f), BF16), pltpu.VMEM((f, d), BF16),
                            pltpu.SemaphoreType.DMA((2,))]),
        out_shape=jax.ShapeDtypeStruct((n_rows, w), U32),
        compiler_params=_cparams(("arbitrary",)),
    )(blk, eid, first, slot, nxt, more, nvb, xs, wg, wu, wd)


def _combine_kernel(dest_ref, y_ref, h_ref, x1_ref, ew_ref, wg_ref, wu_ref, wd_ref, mod_ref, g_ref, o_ref,
                    gbuf_ref, sem, *, tc):
    def tok_body(j, c):
        for k in range(TOP_K):
            pltpu.make_async_copy(y_ref.at[dest_ref[j * TOP_K + k]], gbuf_ref.at[k, j], sem).start()
        return c

    lax.fori_loop(0, tc, tok_body, 0)
    shared = _ffn(h_ref[...], wg_ref, wu_ref, wd_ref)
    for k in range(TOP_K):
        pltpu.make_async_copy(y_ref.at[pl.ds(0, tc)], gbuf_ref.at[k], sem).wait()

    half = h_ref.shape[1]
    ew = ew_ref[...]
    left = shared[:, :half]
    right = shared[:, half:]
    for k in range(TOP_K):
        yl, yr = _unpack_halves(gbuf_ref[k])
        wk = ew[:, k:k + 1]
        left = left + wk * yl
        right = right + wk * yr
    x2 = x1_ref[...] + mod_ref[0, 5:6, :] * jnp.concatenate([left, right], axis=1)
    o_ref[...] = x2 * lax.rsqrt(jnp.mean(x2 * x2, axis=-1, keepdims=True) + EPS) * g_ref[...]


def _combine(dest_flat, y, h2p, x1, ew_t, wsg, wsu, wsd, mod3, g, L, tc=256):
    t, d = x1.shape
    w = h2p.shape[1]
    kern = functools.partial(_combine_kernel, tc=tc)
    return pl.pallas_call(
        kern,
        grid=(t // tc,),
        in_specs=[pl.BlockSpec((tc * TOP_K,), lambda i: (i,), memory_space=pltpu.SMEM),
                  pl.BlockSpec(memory_space=pl.ANY),
                  pl.BlockSpec((tc, w), lambda i: (i, 0)),
                  pl.BlockSpec((tc, d), lambda i: (i, 0)),
                  pl.BlockSpec((tc, TOP_K), lambda i: (i, 0)),
                  pl.BlockSpec(wsg.shape, lambda i: (0, 0)),
                  pl.BlockSpec(wsu.shape, lambda i: (0, 0)),
                  pl.BlockSpec(wsd.shape, lambda i: (0, 0)),
                  pl.BlockSpec((1, 6, d), lambda i: (i * tc // L, 0, 0)),
                  pl.BlockSpec((1, d), lambda i: (0, 0))],
        out_specs=pl.BlockSpec((tc, d), lambda i: (i, 0)),
        out_shape=jax.ShapeDtypeStruct((t, d), F32),
        scratch_shapes=[pltpu.VMEM((TOP_K, tc, w), U32),
                        pltpu.SemaphoreType.DMA(())],
        compiler_params=_cparams(("arbitrary",)),
    )(dest_flat, y, h2p, x1, ew_t, wsg, wsu, wsd, mod3, g.reshape(1, d))


def _split_cols(w, sizes):
    out, off = [], 0
    for s in sizes:
        out.append(w[:, off:off + s])
        off += s
    return out


def kernel(x, c, w_ada, b_ada, norm1_g, w_in, ckv_norm_g, idx_k_norm_g, w_uk, w_uv, rel_bias, lb_logits,
           attn_out_norm_g, hgrn_out_norm_g, w_out, norm2_g, w_router, router_bias, w_e_gate, w_e_up,
           w_e_down, w_s_gate, w_s_up, w_s_down, final_norm_g):
    bsz, L, d = x.shape
    t = bsz * L
    assert w_ada.shape[0] == 1, "single-layer block"
    a_width = A_HEADS * A_HEAD_DIM
    b_width = B_HEADS * B_DIM
    sizes = (a_width, A_KV_RANK, IDX_HEADS * IDX_DIM, IDX_DIM, IDX_HEADS, b_width, b_width, b_width, b_width)
    assert w_in.shape[2] == sum(sizes)

    wq_a, wckv, wiq, wik, wiw, wq_b, wf_b, wi_b, wg_b = _split_cols(w_in[0], sizes)
    w_main = jnp.concatenate([wq_a, wiq, wq_b, wi_b, wg_b], axis=1).astype(BF16)
    w_f = wf_b.astype(BF16)
    aux_pad = LANES - IDX_DIM - IDX_HEADS
    w_aux = jnp.concatenate([wckv, wik, wiw, jnp.zeros((d, aux_pad), F32)], axis=1).astype(BF16)

    mod3 = _ada(c, w_ada[0], b_ada[0]).reshape(bsz, 6, d)
    h1 = _norm1(x, mod3, norm1_g[0]).reshape(t, d)
    g1 = _matmul(h1, w_main, BF16, tm=1024, tn=512)
    fb = _matmul(h1, w_f, F32, tm=1024, tn=512)
    aux = _matmul(h1, w_aux, F32, tm=1024, tn=w_aux.shape[1])
    ckv_n, ik_lo, ik_hi = _kvnorm(aux, ckv_norm_g[0], idx_k_norm_g[0])

    tq = min(256, L)
    o_a = _dsa(g1, aux, ik_lo, ik_hi, ckv_n, w_uk[0].astype(BF16), w_uv[0].astype(BF16),
               _bias_tables(rel_bias, tq), attn_out_norm_g[0], bsz, L, tq)
    o_b = _hgrn(g1, fb, lb_logits, hgrn_out_norm_g[0], bsz, L)

    x1, h2p, logits_t = _out(o_a, o_b, x.reshape(t, d), w_out[0].astype(BF16), mod3, norm2_g[0],
                             w_router[0].T.astype(BF16), L)

    eidx, ew, rank, cnt = _route(logits_t, router_bias[0])

    bm = 256
    counts = cnt[:, 0].astype(I32)
    padded = (counts + bm - 1) // bm * bm
    pad_end = jnp.cumsum(padded)
    pad_start = pad_end - padded
    n_rows = (t * TOP_K + N_EXPERTS * (bm - 1) + bm - 1) // bm * bm
    nb = n_rows // bm
    nvb = pad_end[-1] // bm
    blk = jnp.minimum(jnp.arange(nb, dtype=I32), nvb - 1)
    eid = jnp.minimum(jnp.sum((pad_end[None, :] <= (blk * bm)[:, None]).astype(I32), axis=1), N_EXPERTS - 1)
    ar = jnp.arange(nb, dtype=I32)
    first = ((ar < nvb) & ((ar == 0) | (eid != jnp.roll(eid, 1)))).astype(I32)
    slot = (jnp.cumsum(first) - 1) % 2
    nxt_blk = pad_end[eid] // bm
    more = (nxt_blk < nvb).astype(I32)
    nxt = eid[jnp.minimum(nxt_blk, nb - 1)]

    dest = _dest(eidx, rank, pad_start)
    dest_flat = dest.T.reshape(t * TOP_K)
    xs = _dispatch(pad_end.astype(I32), padded.astype(I32), dest_flat, h2p, n_rows, bm)

    y = _experts(blk, eid, first, slot.astype(I32), nxt.astype(I32), more, nvb.reshape(1).astype(I32), xs,
                 w_e_gate[0], w_e_up[0], w_e_down[0], bm)

    out = _combine(dest_flat, y, h2p, x1, ew.T, w_s_gate[0].astype(BF16), w_s_up[0].astype(BF16),
                   w_s_down[0].astype(BF16), mod3, final_norm_g, L)
    return out.reshape(bsz, L, d)
```

```python
import functools
import math

import numpy as np
import jax
import jax.numpy as jnp
from jax import lax
from jax.experimental import pallas as pl
from jax.experimental.pallas import tpu as pltpu

F32 = jnp.float32
BF16 = jnp.bfloat16
I32 = jnp.int32
U32 = jnp.uint32

EPS = 1e-6
A_HEADS = 8
A_HEAD_DIM = 128
A_KV_RANK = 256
IDX_HEADS = 16
IDX_DIM = 64
IDX_TOPK_MAX = 256
B_HEADS = 8
B_DIM = 128
REL_BUCKETS = 32
REL_MAX_DIST = 128
N_EXPERTS = 64
TOP_K = 8
N_GROUPS = 8
TOPK_GROUPS = 4
ROUTED_SCALE = 2.5

VMEM_LIMIT_BYTES = 56 * 1024 * 1024
LANES = 128
SUBLANES = 8

NT_DIMS = (((1,), (1,)), ((), ()))
TN_DIMS = (((0,), (0,)), ((), ()))

LOG2E = math.log2(math.e)
INT_MIN = -2 ** 31
KEY_NEG_INF = -2139095041


def _cparams(sem):
    return pltpu.CompilerParams(dimension_semantics=sem, vmem_limit_bytes=VMEM_LIMIT_BYTES)


def _silu(v):
    return v * jax.nn.sigmoid(v)


def _pack_halves(v):
    n = v.shape[1] // 2
    lo = lax.bitcast_convert_type(v[:, :n].astype(BF16).astype(F32), U32)
    hi = lax.bitcast_convert_type(v[:, n:].astype(BF16).astype(F32), U32)
    return lax.shift_right_logical(lo, jnp.uint32(16)) | (hi & jnp.uint32(0xFFFF0000))


def _unpack_halves(w):
    left = lax.bitcast_convert_type(lax.shift_left(w, jnp.uint32(16)), F32)
    right = lax.bitcast_convert_type(w & jnp.uint32(0xFFFF0000), F32)
    return left, right


def _ada_kernel(c_ref, w_ref, b_ref, o_ref):
    a = _silu(c_ref[...]).astype(BF16)
    o_ref[...] = jnp.dot(a, w_ref[...].astype(BF16), preferred_element_type=F32) + b_ref[...]


def _ada(c, w, b, tn=1024):
    bsz, d = c.shape
    n = w.shape[1]
    return pl.pallas_call(
        _ada_kernel,
        grid=(n // tn,),
        in_specs=[pl.BlockSpec((bsz, d), lambda j: (0, 0)),
                  pl.BlockSpec((d, tn), lambda j: (0, j)),
                  pl.BlockSpec((1, tn), lambda j: (0, j))],
        out_specs=pl.BlockSpec((bsz, tn), lambda j: (0, j)),
        out_shape=jax.ShapeDtypeStruct((bsz, n), F32),
        compiler_params=_cparams(("arbitrary",)),
    )(c, w, b.reshape(1, n))


def _norm1_kernel(x_ref, mod_ref, g_ref, o_ref):
    x = x_ref[0]
    y = x * lax.rsqrt(jnp.mean(x * x, axis=-1, keepdims=True) + EPS) * g_ref[...]
    sh = mod_ref[0, 0:1, :]
    sc = mod_ref[0, 1:2, :]
    o_ref[0] = (y * (1.0 + sc) + sh).astype(o_ref.dtype)


def _norm1(x, mod3, g, tm=512):
    bsz, L, d = x.shape
    return pl.pallas_call(
        _norm1_kernel,
        grid=(bsz, L // tm),
        in_specs=[pl.BlockSpec((1, tm, d), lambda b, i: (b, i, 0)),
                  pl.BlockSpec((1, 6, d), lambda b, i: (b, 0, 0)),
                  pl.BlockSpec((1, d), lambda b, i: (0, 0))],
        out_specs=pl.BlockSpec((1, tm, d), lambda b, i: (b, i, 0)),
        out_shape=jax.ShapeDtypeStruct((bsz, L, d), BF16),
        compiler_params=_cparams(("parallel", "parallel")),
    )(x, mod3, g.reshape(1, d))


def _mm_kernel(a_ref, w_ref, o_ref):
    o_ref[...] = jnp.dot(a_ref[...], w_ref[...], preferred_element_type=F32).astype(o_ref.dtype)


def _matmul(a, w, out_dtype, tm, tn):
    m, k = a.shape
    n = w.shape[1]
    return pl.pallas_call(
        _mm_kernel,
        grid=(m // tm, n // tn),
        in_specs=[pl.BlockSpec((tm, k), lambda i, j: (i, 0)),
                  pl.BlockSpec((k, tn), lambda i, j: (0, j))],
        out_specs=pl.BlockSpec((tm, tn), lambda i, j: (i, j)),
        out_shape=jax.ShapeDtypeStruct((m, n), out_dtype),
        compiler_params=_cparams(("parallel", "arbitrary")),
    )(a, w)


def _kvnorm_kernel(aux_ref, gc_ref, gk_ref, ckv_ref, iklo_ref, ikhi_ref):
    ckv = aux_ref[:, :A_KV_RANK]
    ckv_ref[...] = (ckv * lax.rsqrt(jnp.mean(ckv * ckv, axis=-1, keepdims=True) + EPS)
                    * gc_ref[...]).astype(BF16)
    v = aux_ref[:, A_KV_RANK:A_KV_RANK + LANES]
    lane = lax.broadcasted_iota(I32, v.shape, 1)
    ik = jnp.where(lane < IDX_DIM, v, 0.0)
    ms = jnp.sum(ik * ik, axis=-1, keepdims=True) * (1.0 / IDX_DIM)
    ikn = ik * lax.rsqrt(ms + EPS) * gk_ref[...]
    iklo_ref[...] = ikn.astype(BF16)
    ikhi_ref[...] = pltpu.roll(ikn, IDX_DIM, 1).astype(BF16)


def _kvnorm(aux, gc, gk, tm=1024):
    t = aux.shape[0]
    gk_pad = jnp.concatenate([gk, jnp.zeros((LANES - IDX_DIM,), F32)]).reshape(1, LANES)
    return pl.pallas_call(
        _kvnorm_kernel,
        grid=(t // tm,),
        in_specs=[pl.BlockSpec((tm, aux.shape[1]), lambda i: (i, 0)),
                  pl.BlockSpec((1, A_KV_RANK), lambda i: (0, 0)),
                  pl.BlockSpec((1, LANES), lambda i: (0, 0))],
        out_specs=[pl.BlockSpec((tm, A_KV_RANK), lambda i: (i, 0)),
                   pl.BlockSpec((tm, LANES), lambda i: (i, 0)),
                   pl.BlockSpec((tm, LANES), lambda i: (i, 0))],
        out_shape=[jax.ShapeDtypeStruct((t, A_KV_RANK), BF16),
                   jax.ShapeDtypeStruct((t, LANES), BF16),
                   jax.ShapeDtypeStruct((t, LANES), BF16)],
        compiler_params=_cparams(("parallel",)),
    )(aux, gc.reshape(1, A_KV_RANK), gk_pad)


def _t5_bucket(rel):
    n = jnp.maximum(rel, 0)
    max_exact = REL_BUCKETS // 2
    n_large = jnp.maximum(n, max_exact).astype(F32)
    large = max_exact + (jnp.log(n_large / max_exact) / math.log(REL_MAX_DIST / max_exact)
                         * (REL_BUCKETS - max_exact)).astype(I32)
    large = jnp.minimum(large, REL_BUCKETS - 1)
    return jnp.where(n < max_exact, n, large)


def _bias_tables(rel_bias, tq):
    assert tq + 1 >= REL_MAX_DIST
    nh = rel_bias.shape[1]
    dist = jnp.maximum(jnp.arange(3 * tq + 1, dtype=I32) - tq, 0)
    v = rel_bias.astype(F32)[_t5_bucket(dist)].T * LOG2E
    n = v.shape[1]
    x = jnp.broadcast_to(v[:, None, :], (nh, tq, n)).reshape(nh, tq * n)[:, :tq * (n - 1)].reshape(nh, tq, n - 1)
    near = x[:, :, tq:2 * tq]
    prev = x[:, :, 2 * tq:3 * tq]
    far = jnp.broadcast_to(v[:, n - 1][:, None, None], near.shape)
    return jnp.stack([near, prev, far])


def _dsa_kernel(qa_ref, iq_ref, aux_ref, iklo_ref, ikhi_ref, ckv_ref, ckvt_ref, wuk_ref, wuvt_ref, bias_ref,
                g_ref, o_ref, iqt_ref, iwt_ref, key_ref, qlt_ref, m_ref, l_ref, acc_ref, tie_ref, madd_ref,
                *, tq, topk):
    i = pl.program_id(1)
    nh = A_HEADS
    npair = IDX_HEADS // 2

    r_i = lax.broadcasted_iota(I32, (LANES, LANES), 0)
    c_i = lax.broadcasted_iota(I32, (LANES, LANES), 1)
    eye = jnp.where(r_i == c_i, 1.0, 0.0).astype(BF16)
    for p in range(npair):
        iqt_ref[:, p * tq:(p + 1) * tq] = lax.dot_general(
            eye, iq_ref[:, p * LANES:(p + 1) * LANES], NT_DIMS, preferred_element_type=F32).astype(BF16)
    iwt_ref[...] = (jnp.transpose(aux_ref[...])[IDX_DIM:IDX_DIM + IDX_HEADS, :]
                    * (IDX_HEADS ** -0.5 * IDX_DIM ** -0.5))
    for h in range(nh):
        ql = lax.dot_general(wuk_ref[h], qa_ref[:, h * A_HEAD_DIM:(h + 1) * A_HEAD_DIM], NT_DIMS,
                             preferred_element_type=F32)
        qlt_ref[:, h * tq:(h + 1) * tq] = (ql * (A_HEAD_DIM ** -0.5 * LOG2E)).astype(BF16)

    kpos = lax.broadcasted_iota(I32, (tq, tq), 0)
    qpos = lax.broadcasted_iota(I32, (tq, tq), 1) + i * tq

    def score_body(kc, carry):
        off = pl.multiple_of(kc * tq, tq)
        klo = iklo_ref[0, pl.ds(off, tq), :]
        khi = ikhi_ref[0, pl.ds(off, tq), :]
        acc = jnp.zeros((tq, tq), F32)
        for p in range(npair):
            rhs = iqt_ref[:, p * tq:(p + 1) * tq]
            se = jnp.dot(klo, rhs, preferred_element_type=F32)
            so = jnp.dot(khi, rhs, preferred_element_type=F32)
            acc = acc + jnp.maximum(se, 0.0) * iwt_ref[2 * p:2 * p + 1, :]
            acc = acc + jnp.maximum(so, 0.0) * iwt_ref[2 * p + 1:2 * p + 2, :]
        bits = lax.bitcast_convert_type(acc, I32)
        key = jnp.where(bits >= 0, bits, bits ^ jnp.int32(0x7FFFFFFF))
        key_ref[kc] = jnp.where(kpos + off <= qpos, key, jnp.int32(KEY_NEG_INF))
        return carry

    lax.fori_loop(0, i + 1, score_body, 0)

    def count_ge(cand):
        def body(kc, c):
            hit = jnp.where(key_ref[kc] >= cand, 1.0, 0.0)
            return c + jnp.sum(hit.reshape(tq // 8, 8, tq), axis=0)
        c = lax.fori_loop(0, i + 1, body, jnp.zeros((8, tq), F32))
        return jnp.sum(c, axis=0, keepdims=True)

    kf = float(topk)
    thr = jnp.where(count_ge(jnp.zeros((1, tq), I32)) >= kf, jnp.int32(0), jnp.int32(INT_MIN))

    def bit_body(j, thr):
        cand = thr | lax.shift_left(jnp.int32(1), 30 - j)
        return jnp.where(count_ge(cand) >= kf, cand, thr)

    thr = lax.fori_loop(0, 31, bit_body, thr)

    def count_gt():
        def body(kc, c):
            hit = jnp.where(key_ref[kc] > thr, 1.0, 0.0)
            return c + jnp.sum(hit.reshape(tq // 8, 8, tq), axis=0)
        c = lax.fori_loop(0, i + 1, body, jnp.zeros((8, tq), F32))
        return jnp.sum(c, axis=0, keepdims=True)

    need = kf - count_gt()
    tied = (count_ge(thr) > kf) & (thr > jnp.int32(KEY_NEG_INF))
    has_tie = jnp.max(jnp.where(tied, 1.0, 0.0)) > 0.0
    tie_ref[...] = jnp.zeros(tie_ref.shape, F32)

    m_ref[...] = jnp.full(m_ref.shape, -jnp.inf, F32)
    l_ref[...] = jnp.zeros(l_ref.shape, F32)
    acc_ref[...] = jnp.zeros(acc_ref.shape, F32)

    def att_body(kc, carry):
        off = pl.multiple_of(kc * tq, tq)
        ckv = ckv_ref[0, pl.ds(off, tq), :]
        ckvt = ckvt_ref[0, kc]
        key = key_ref[kc]
        causal = key > jnp.int32(KEY_NEG_INF)

        @pl.when(jnp.logical_not(has_tie))
        def _():
            madd_ref[...] = jnp.where((key >= thr) & causal, 0.0, -jnp.inf)

        @pl.when(has_tie)
        def _():
            eq = key == thr
            eqf = jnp.where(eq, 1.0, 0.0)
            before = (lax.broadcasted_iota(I32, (tq, tq), 1) < lax.broadcasted_iota(I32, (tq, tq), 0))
            rank = jnp.dot(jnp.where(before, 1.0, 0.0).astype(BF16), eqf.astype(BF16),
                           preferred_element_type=F32) + tie_ref[...]
            keep = (key > thr) | (eq & (rank < need))
            madd_ref[...] = jnp.where(keep & causal, 0.0, -jnp.inf)
            tie_ref[...] = tie_ref[...] + jnp.sum(eqf, axis=0, keepdims=True)

        madd = madd_ref[...]
        d = jnp.minimum(i - kc, 2)
        for h in range(nh):
            s = jnp.dot(ckv, qlt_ref[:, h * tq:(h + 1) * tq], preferred_element_type=F32)
            s = s + (bias_ref[d, h] + madd)
            m_old = m_ref[h:h + 1, :]
            m_new = jnp.maximum(m_old, jnp.max(s, axis=0, keepdims=True))
            m_safe = jnp.where(m_new == -jnp.inf, 0.0, m_new)
            alpha = jnp.exp2(m_old - m_safe)
            p = jnp.exp2(s - m_safe)
            l_ref[h:h + 1, :] = alpha * l_ref[h:h + 1, :] + jnp.sum(p, axis=0, keepdims=True)
            acc_ref[h] = alpha * acc_ref[h] + jnp.dot(ckvt, p.astype(BF16), preferred_element_type=F32)
            m_ref[h:h + 1, :] = m_new
        return carry

    lax.fori_loop(0, i + 1, att_body, 0)

    outs = []
    for h in range(nh):
        o_lat = (acc_ref[h] / l_ref[h:h + 1, :]).astype(BF16)
        outs.append(jnp.transpose(jnp.dot(wuvt_ref[h], o_lat, preferred_element_type=F32)))
    o = jnp.concatenate(outs, axis=1)
    o = o * lax.rsqrt(jnp.mean(o * o, axis=-1, keepdims=True) + EPS) * g_ref[...]
    o_ref[...] = o.astype(o_ref.dtype)


def _dsa(g1, aux, ik_lo, ik_hi, ckv_n, w_uk, w_uv, bias_tab, g, bsz, L, tq):
    t = bsz * L
    nq = L // tq
    topk = min(IDX_TOPK_MAX, L // 4)
    aux_blk = A_KV_RANK // LANES
    kern = functools.partial(_dsa_kernel, tq=tq, topk=topk)
    width = A_HEADS * A_HEAD_DIM
    ckv3 = ckv_n.reshape(bsz, L, A_KV_RANK)
    ckvt = ckv_n.reshape(bsz, nq, tq, A_KV_RANK).transpose(0, 1, 3, 2)
    return pl.pallas_call(
        kern,
        grid=(bsz, nq),
        in_specs=[pl.BlockSpec((tq, width), lambda b, i: (b * nq + i, 0)),
                  pl.BlockSpec((tq, IDX_HEADS * IDX_DIM), lambda b, i: (b * nq + i, 1)),
                  pl.BlockSpec((tq, LANES), lambda b, i: (b * nq + i, aux_blk)),
                  pl.BlockSpec((1, L, LANES), lambda b, i: (b, 0, 0)),
                  pl.BlockSpec((1, L, LANES), lambda b, i: (b, 0, 0)),
                  pl.BlockSpec((1, L, A_KV_RANK), lambda b, i: (b, 0, 0)),
                  pl.BlockSpec((1, nq, A_KV_RANK, tq), lambda b, i: (b, 0, 0, 0)),
                  pl.BlockSpec((A_HEADS, A_KV_RANK, A_HEAD_DIM), lambda b, i: (0, 0, 0)),
                  pl.BlockSpec((A_HEADS, A_HEAD_DIM, A_KV_RANK), lambda b, i: (0, 0, 0)),
                  pl.BlockSpec((3, A_HEADS, tq, tq), lambda b, i: (0, 0, 0, 0)),
                  pl.BlockSpec((1, width), lambda b, i: (0, 0))],
        out_specs=pl.BlockSpec((tq, width), lambda b, i: (b * nq + i, 0)),
        out_shape=jax.ShapeDtypeStruct((t, width), BF16),
        scratch_shapes=[pltpu.VMEM((LANES, IDX_HEADS // 2 * tq), BF16),
                        pltpu.VMEM((IDX_HEADS, tq), F32),
                        pltpu.VMEM((nq, tq, tq), I32),
                        pltpu.VMEM((A_KV_RANK, A_HEADS * tq), BF16),
                        pltpu.VMEM((A_HEADS, tq), F32),
                        pltpu.VMEM((A_HEADS, tq), F32),
                        pltpu.VMEM((A_HEADS, A_KV_RANK, tq), F32),
                        pltpu.VMEM((1, tq), F32),
                        pltpu.VMEM((tq, tq), F32)],
        compiler_params=_cparams(("parallel", "arbitrary")),
    )(g1, g1, aux, ik_lo.reshape(bsz, L, LANES), ik_hi.reshape(bsz, L, LANES),
      ckv3, ckvt, w_uk, jnp.transpose(w_uv, (0, 2, 1)), bias_tab, g.reshape(1, width))


def _hgrn_kernel(q_ref, i_ref, gate_ref, f_ref, lbl_ref, ng_ref, o_ref, st_ref, *, chunk, rblk):
    @pl.when(pl.program_id(1) == 0)
    def _():
        st_ref[...] = jnp.zeros(st_ref.shape, F32)

    ll = lbl_ref[...]
    ex = jnp.exp(ll - jnp.max(ll, axis=0, keepdims=True))
    lb_all = ex[0:1] / jnp.sum(ex, axis=0, keepdims=True)

    r_i = lax.broadcasted_iota(I32, (chunk, chunk), 0)
    c_i = lax.broadcasted_iota(I32, (chunk, chunk), 1)
    tri = jnp.where(r_i >= c_i, 1.0, 0.0).astype(BF16)
    row_k = lax.broadcasted_iota(I32, (chunk, B_DIM), 0)

    for h in range(B_HEADS):
        sl = slice(h * B_DIM, (h + 1) * B_DIM)
        lb = lb_all[:, sl]
        f = lb + (1.0 - lb) * jax.nn.sigmoid(f_ref[:, sl])
        lf = jnp.log(f)
        kk = 1.0 - f
        l1 = lf.astype(BF16)
        r1 = lf - l1.astype(F32)
        l2 = r1.astype(BF16)
        l3 = (r1 - l2.astype(F32)).astype(BF16)
        cs = jnp.dot(tri, jnp.concatenate([l1, l2, l3], axis=1), preferred_element_type=F32)
        bc = cs[:, :B_DIM] + cs[:, B_DIM:2 * B_DIM] + cs[:, 2 * B_DIM:]

        q = q_ref[:, sl].astype(F32)
        v = i_ref[:, sl]
        st = st_ref[h]
        o = lax.dot_general((q * jnp.exp(bc)).astype(BF16), st.astype(BF16), NT_DIMS,
                            preferred_element_type=F32)
        parts = []
        for r in range(chunk // rblk):
            lo, hi = r * rblk, (r + 1) * rblk
            base = bc[lo - 1:lo] if r > 0 else jnp.zeros((1, B_DIM), F32)
            qt = (q[lo:hi] * jnp.exp(bc[lo:hi] - base)).astype(BF16)
            kt = (kk * jnp.exp(jnp.where(row_k < hi, base - bc, 0.0))).astype(BF16)
            parts.append(lax.dot_general(qt, kt, NT_DIMS, preferred_element_type=F32))
        sc = jnp.where(c_i <= r_i, jnp.concatenate(parts, axis=0), 0.0).astype(BF16)
        o = o + jnp.dot(sc, v, preferred_element_type=F32)

        last = bc[chunk - 1:chunk]
        kd = (kk * jnp.exp(last - bc)).astype(BF16)
        st_ref[h] = st * jnp.exp(last) + lax.dot_general(v, kd, TN_DIMS, preferred_element_type=F32)

        y = o * lax.rsqrt(jnp.mean(o * o, axis=-1, keepdims=True) + EPS) * ng_ref[:, sl]
        o_ref[:, sl] = (y * _silu(gate_ref[:, sl].astype(F32))).astype(o_ref.dtype)


def _hgrn(g1, fb, lb_logits, ng, bsz, L, chunk=128, rblk=16):
    t = bsz * L
    nc = L // chunk
    width = B_HEADS * B_DIM
    kern = functools.partial(_hgrn_kernel, chunk=chunk, rblk=rblk)
    return pl.pallas_call(
        kern,
        grid=(bsz, nc),
        in_specs=[pl.BlockSpec((chunk, width), lambda b, c: (b * nc + c, 2)),
                  pl.BlockSpec((chunk, width), lambda b, c: (b * nc + c, 3)),
                  pl.BlockSpec((chunk, width), lambda b, c: (b * nc + c, 4)),
                  pl.BlockSpec((chunk, width), lambda b, c: (b * nc + c, 0)),
                  pl.BlockSpec(lb_logits.shape, lambda b, c: (0, 0)),
                  pl.BlockSpec((1, width), lambda b, c: (0, 0))],
        out_specs=pl.BlockSpec((chunk, width), lambda b, c: (b * nc + c, 0)),
        out_shape=jax.ShapeDtypeStruct((t, width), BF16),
        scratch_shapes=[pltpu.VMEM((B_HEADS, B_DIM, B_DIM), F32)],
        compiler_params=_cparams(("parallel", "arbitrary")),
    )(g1, g1, g1, fb, lb_logits, ng.reshape(1, width))


def _out_kernel(oa_ref, ob_ref, x_ref, wa_ref, wb_ref, mod_ref, g_ref, wr_ref, x1_ref, h2_ref, lg_ref):
    mix = jnp.dot(oa_ref[...], wa_ref[...], preferred_element_type=F32)
    mix = mix + jnp.dot(ob_ref[...], wb_ref[...], preferred_element_type=F32)
    x1 = x_ref[...] + mod_ref[0, 2:3, :] * mix
    x1_ref[...] = x1
    y = x1 * lax.rsqrt(jnp.mean(x1 * x1, axis=-1, keepdims=True) + EPS) * g_ref[...]
    h2 = y * (1.0 + mod_ref[0, 4:5, :]) + mod_ref[0, 3:4, :]
    h2_ref[...] = _pack_halves(h2)
    lg_ref[...] = lax.dot_general(wr_ref[...], h2.astype(BF16), NT_DIMS, preferred_element_type=F32)


def _out(oa, ob, x2d, w_out, mod3, g, w_router_t, L, tm=256):
    t, d = x2d.shape
    half = oa.shape[1]
    ne = w_router_t.shape[0]
    return pl.pallas_call(
        _out_kernel,
        grid=(t // tm,),
        in_specs=[pl.BlockSpec((tm, half), lambda i: (i, 0)),
                  pl.BlockSpec((tm, half), lambda i: (i, 0)),
                  pl.BlockSpec((tm, d), lambda i: (i, 0)),
                  pl.BlockSpec((half, d), lambda i: (0, 0)),
                  pl.BlockSpec((half, d), lambda i: (1, 0)),
                  pl.BlockSpec((1, 6, d), lambda i: (i * tm // L, 0, 0)),
                  pl.BlockSpec((1, d), lambda i: (0, 0)),
                  pl.BlockSpec((ne, d), lambda i: (0, 0))],
        out_specs=[pl.BlockSpec((tm, d), lambda i: (i, 0)),
                   pl.BlockSpec((tm, d // 2), lambda i: (i, 0)),
                   pl.BlockSpec((ne, tm), lambda i: (0, i))],
        out_shape=[jax.ShapeDtypeStruct((t, d), F32),
                   jax.ShapeDtypeStruct((t, d // 2), U32),
                   jax.ShapeDtypeStruct((ne, t), F32)],
        compiler_params=_cparams(("parallel",)),
    )(oa, ob, x2d, w_out, w_out, mod3, g.reshape(1, d), w_router_t)


def _rows_to_tile(rows, nrow):
    n = rows[0].shape[1]
    ridx = lax.broadcasted_iota(I32, (nrow, n), 0)
    out = jnp.zeros((nrow, n), rows[0].dtype)
    for r, v in enumerate(rows):
        out = jnp.where(ridx == r, jnp.broadcast_to(v, (nrow, n)), out)
    return out


def _route_kernel(lg_ref, rb_ref, eidx_ref, ew_ref, rank_ref, cnt_ref, run_ref):
    @pl.when(pl.program_id(0) == 0)
    def _():
        run_ref[...] = jnp.zeros(run_ref.shape, F32)

    ne, tt = lg_ref.shape
    per = ne // N_GROUPS
    sc = jax.nn.sigmoid(lg_ref[...])
    ch = sc + rb_ref[...]
    neg = -jnp.inf

    sub = lax.broadcasted_iota(I32, (per, tt), 0).astype(F32)
    gsc = []
    for g in range(N_GROUPS):
        cg = ch[g * per:(g + 1) * per]
        m1 = jnp.max(cg, axis=0, keepdims=True)
        first = jnp.min(jnp.where(cg == m1, sub, float(per)), axis=0, keepdims=True)
        m2 = jnp.max(jnp.where(sub == first, neg, cg), axis=0, keepdims=True)
        gsc.append(m1 + m2)
    grp = _rows_to_tile(gsc, N_GROUPS)

    gid = lax.broadcasted_iota(I32, (N_GROUPS, tt), 0).astype(F32)
    gsel = jnp.zeros((N_GROUPS, tt), F32)
    for _ in range(TOPK_GROUPS):
        mx = jnp.max(grp, axis=0, keepdims=True)
        gi = jnp.min(jnp.where(grp == mx, gid, float(N_GROUPS)), axis=0, keepdims=True)
        pick = gid == gi
        gsel = jnp.where(pick, 1.0, gsel)
        grp = jnp.where(pick, neg, grp)

    eid = lax.broadcasted_iota(I32, (ne, tt), 0).astype(F32)
    cm = jnp.full((ne, tt), neg, F32)
    for g in range(N_GROUPS):
        in_g = (eid >= float(g * per)) & (eid < float((g + 1) * per))
        cm = jnp.where(in_g & (jnp.broadcast_to(gsel[g:g + 1], (ne, tt)) > 0.5), ch, cm)

    idx_rows, w_rows = [], []
    onehot = jnp.zeros((ne, tt), F32)
    for _ in range(TOP_K):
        mx = jnp.max(cm, axis=0, keepdims=True)
        ei = jnp.min(jnp.where(cm == mx, eid, float(ne)), axis=0, keepdims=True)
        pick = eid == ei
        idx_rows.append(ei)
        w_rows.append(jnp.sum(jnp.where(pick, sc, 0.0), axis=0, keepdims=True))
        onehot = jnp.where(pick, 1.0, onehot)
        cm = jnp.where(pick, neg, cm)
    wsum = w_rows[0]
    for w in w_rows[1:]:
        wsum = wsum + w
    w_rows = [w / wsum * ROUTED_SCALE for w in w_rows]

    a_i = lax.broadcasted_iota(I32, (tt, tt), 0)
    b_i = lax.broadcasted_iota(I32, (tt, tt), 1)
    upper = jnp.where(a_i < b_i, 1.0, 0.0).astype(BF16)
    rank_full = jnp.dot(onehot.astype(BF16), upper, preferred_element_type=F32) + run_ref[...]
    r_rows = [jnp.sum(jnp.where(eid == ei, rank_full, 0.0), axis=0, keepdims=True) for ei in idx_rows]
    run = run_ref[...] + jnp.sum(onehot, axis=1, keepdims=True)
    run_ref[...] = run

    eidx_ref[...] = _rows_to_tile(idx_rows, TOP_K).astype(I32)
    ew_ref[...] = _rows_to_tile(w_rows, TOP_K)
    rank_ref[...] = _rows_to_tile(r_rows, TOP_K).astype(I32)
    cnt_ref[...] = jnp.broadcast_to(run, cnt_ref.shape)


def _route(logits_t, router_bias, tt=512):
    ne, t = logits_t.shape
    return pl.pallas_call(
        _route_kernel,
        grid=(t // tt,),
        in_specs=[pl.BlockSpec((ne, tt), lambda i: (0, i)),
                  pl.BlockSpec((ne, 1), lambda i: (0, 0))],
        out_specs=[pl.BlockSpec((TOP_K, tt), lambda i: (0, i)),
                   pl.BlockSpec((TOP_K, tt), lambda i: (0, i)),
                   pl.BlockSpec((TOP_K, tt), lambda i: (0, i)),
                   pl.BlockSpec((ne, LANES), lambda i: (0, 0))],
        out_shape=[jax.ShapeDtypeStruct((TOP_K, t), I32),
                   jax.ShapeDtypeStruct((TOP_K, t), F32),
                   jax.ShapeDtypeStruct((TOP_K, t), I32),
                   jax.ShapeDtypeStruct((ne, LANES), F32)],
        scratch_shapes=[pltpu.VMEM((ne, 1), F32)],
        compiler_params=_cparams(("arbitrary",)),
    )(logits_t, router_bias.reshape(ne, 1))


def _dest_kernel(eidx_ref, rank_ref, ps_ref, o_ref):
    ne = ps_ref.shape[0]
    tt = eidx_ref.shape[1]
    eid = lax.broadcasted_iota(I32, (ne, tt), 0)
    ps = jnp.broadcast_to(ps_ref[...], (ne, tt))
    rows = []
    for k in range(TOP_K):
        start = jnp.sum(jnp.where(eid == eidx_ref[k:k + 1, :], ps, 0.0), axis=0, keepdims=True)
        rows.append(start + rank_ref[k:k + 1, :].astype(F32))
    o_ref[...] = _rows_to_tile(rows, TOP_K).astype(I32)


def _dest(eidx, rank, pad_start, tt=2048):
    t = eidx.shape[1]
    tt = min(tt, t)
    ne = pad_start.shape[0]
    return pl.pallas_call(
        _dest_kernel,
        grid=(t // tt,),
        in_specs=[pl.BlockSpec((TOP_K, tt), lambda i: (0, i)),
                  pl.BlockSpec((TOP_K, tt), lambda i: (0, i)),
                  pl.BlockSpec((ne, 1), lambda i: (0, 0))],
        out_specs=pl.BlockSpec((TOP_K, tt), lambda i: (0, i)),
        out_shape=jax.ShapeDtypeStruct((TOP_K, t), I32),
        compiler_params=_cparams(("parallel",)),
    )(eidx, rank, pad_start.astype(F32).reshape(ne, 1))


def _dispatch_kernel(pend_ref, padded_ref, dest_ref, h_ref, xs_ref, zbuf_ref, zsem, sem, *, td, bm):
    i = pl.program_id(0)

    def tail_copy(e):
        start = pl.multiple_of(pend_ref[e] - bm, bm)
        return pltpu.make_async_copy(zbuf_ref, xs_ref.at[pl.ds(start, bm)], zsem)

    @pl.when(i == 0)
    def _():
        zbuf_ref[...] = jnp.zeros(zbuf_ref.shape, zbuf_ref.dtype)

        def start_body(e, c):
            @pl.when(padded_ref[e] > 0)
            def _():
                tail_copy(e).start()
            return c

        def wait_body(e, c):
            @pl.when(padded_ref[e] > 0)
            def _():
                tail_copy(e).wait()
            return c

        lax.fori_loop(0, N_EXPERTS, start_body, 0)
        lax.fori_loop(0, N_EXPERTS, wait_body, 0)

        def unused_copy(b):
            return pltpu.make_async_copy(zbuf_ref, xs_ref.at[pl.ds(pl.multiple_of(b * bm, bm), bm)], zsem)

        def ustart_body(b, c):
            unused_copy(b).start()
            return c

        def uwait_body(b, c):
            unused_copy(b).wait()
            return c

        first_unused = pend_ref[N_EXPERTS - 1] // bm
        lax.fori_loop(first_unused, xs_ref.shape[0] // bm, ustart_body, 0)
        lax.fori_loop(first_unused, xs_ref.shape[0] // bm, uwait_body, 0)

    def group_body(g, c):
        base = pl.multiple_of(g * SUBLANES, SUBLANES)
        for jj in range(SUBLANES):
            src = h_ref.at[base + jj]
            for k in range(TOP_K):
                dst = xs_ref.at[dest_ref[(base + jj) * TOP_K + k]]
                pltpu.make_async_copy(src, dst, sem).start(priority=k % 2)
        return c

    lax.fori_loop(0, td // SUBLANES, group_body, 0)
    for _ in range(TOP_K):
        pltpu.make_async_copy(h_ref, xs_ref.at[pl.ds(0, td)], sem).wait()


def _dispatch(pad_end, padded, dest_flat, h2p, n_rows, bm, td=256):
    t, w = h2p.shape
    kern = functools.partial(_dispatch_kernel, td=td, bm=bm)
    return pl.pallas_call(
        kern,
        grid_spec=pltpu.PrefetchScalarGridSpec(
            num_scalar_prefetch=2,
            grid=(t // td,),
            in_specs=[pl.BlockSpec((td * TOP_K,), lambda i, *_: (i,), memory_space=pltpu.SMEM),
                      pl.BlockSpec((td, w), lambda i, *_: (i, 0))],
            out_specs=pl.BlockSpec(memory_space=pl.ANY),
            scratch_shapes=[pltpu.VMEM((bm, w), U32),
                            pltpu.SemaphoreType.DMA(()),
                            pltpu.SemaphoreType.DMA(())]),
        out_shape=jax.ShapeDtypeStruct((n_rows, w), U32),
        compiler_params=_cparams(("arbitrary",)),
    )(pad_end, padded, dest_flat, h2p)


def _ffn(xw, wg_ref, wu_ref, wd_ref):
    half = xw.shape[1]
    left, right = _unpack_halves(xw)
    left = left.astype(BF16)
    right = right.astype(BF16)

    def proj(w_ref):
        return (jnp.dot(left, w_ref[:half, :], preferred_element_type=F32)
                + jnp.dot(right, w_ref[half:, :], preferred_element_type=F32))

    act = (_silu(proj(wg_ref)) * proj(wu_ref)).astype(BF16)
    return jnp.dot(act, wd_ref[...], preferred_element_type=F32)


def _expert_kernel(blk_ref, eid_ref, first_ref, slot_ref, nxt_ref, more_ref, nvb_ref,
                   x_ref, wg_hbm, wu_hbm, wd_hbm, o_ref,
                   wg_f, wu_f, wd_f, wg_s, wu_s, wd_s, sems):
    i = pl.program_id(0)

    def weight_copies(e, slot):
        return (pltpu.make_async_copy(wg_hbm.at[e], wg_f.at[slot], sems.at[slot]),
                pltpu.make_async_copy(wu_hbm.at[e], wu_f.at[slot], sems.at[slot]),
                pltpu.make_async_copy(wd_hbm.at[e], wd_f.at[slot], sems.at[slot]))

    @pl.when(i == 0)
    def _():
        for cp in weight_copies(eid_ref[0], 0):
            cp.start()

    @pl.when(first_ref[i] == 1)
    def _():
        slot = slot_ref[i]
        for cp in weight_copies(eid_ref[i], slot):
            cp.wait()

        @pl.when(more_ref[i] == 1)
        def _():
            for cp in weight_copies(nxt_ref[i], 1 - slot):
                cp.start()

        wg_s[...] = wg_f[slot].astype(BF16)
        wu_s[...] = wu_f[slot].astype(BF16)
        wd_s[...] = wd_f[slot].astype(BF16)

    @pl.when(i < nvb_ref[0])
    def _():
        o_ref[...] = _pack_halves(_ffn(x_ref[...], wg_s, wu_s, wd_s))

    @pl.when(i >= nvb_ref[0])
    def _():
        o_ref[...] = jnp.zeros(o_ref.shape, o_ref.dtype)


def _experts(blk, eid, first, slot, nxt, more, nvb, xs, wg, wu, wd, bm):
    n_rows, w = xs.shape
    ne, d, f = wg.shape
    return pl.pallas_call(
        _expert_kernel,
        grid_spec=pltpu.PrefetchScalarGridSpec(
            num_scalar_prefetch=7,
            grid=(n_rows // bm,),
            in_specs=[pl.BlockSpec((bm, w), lambda i, blk, *_: (blk[i], 0)),
                      pl.BlockSpec(memory_space=pl.ANY),
                      pl.BlockSpec(memory_space=pl.ANY),
                      pl.BlockSpec(memory_space=pl.ANY)],
            out_specs=pl.BlockSpec((bm, w), lambda i, *_: (i, 0)),
            scratch_shapes=[pltpu.VMEM((2, d, f), F32), pltpu.VMEM((2, d, f), F32), pltpu.VMEM((2, f, d), F32),
                            pltpu.VMEM((d, f), BF16), pltpu.VMEM((d, f), BF16), pltpu.VMEM((f, d), BF16),
                            pltpu.SemaphoreType.DMA((2,))]),
        out_shape=jax.ShapeDtypeStruct((n_rows, w), U32),
        compiler_params=_cparams(("arbitrary",)),
    )(blk, eid, first, slot, nxt, more, nvb, xs, wg, wu, wd)


def _combine_kernel(dest_ref, y_ref, h_ref, x1_ref, ew_ref, wg_ref, wu_ref, wd_ref, mod_ref, g_ref, o_ref,
                    gbuf_ref, sem, *, tc):
    def group_body(g, c):
        base = pl.multiple_of(g * SUBLANES, SUBLANES)
        for jj in range(SUBLANES):
            for k in range(TOP_K):
                src = y_ref.at[dest_ref[(base + jj) * TOP_K + k]]
                pltpu.make_async_copy(src, gbuf_ref.at[k, base + jj], sem).start(priority=k % 2)
        return c

    lax.fori_loop(0, tc // SUBLANES, group_body, 0)
    shared = _ffn(h_ref[...], wg_ref, wu_ref, wd_ref)
    for k in range(TOP_K):
        pltpu.make_async_copy(y_ref.at[pl.ds(0, tc)], gbuf_ref.at[k], sem).wait()

    half = h_ref.shape[1]
    ew = ew_ref[...]
    left = shared[:, :half]
    right = shared[:, half:]
    for k in range(TOP_K):
        yl, yr = _unpack_halves(gbuf_ref[k])
        wk = ew[:, k:k + 1]
        left = left + wk * yl
        right = right + wk * yr
    x2 = x1_ref[...] + mod_ref[0, 5:6, :] * jnp.concatenate([left, right], axis=1)
    o_ref[...] = x2 * lax.rsqrt(jnp.mean(x2 * x2, axis=-1, keepdims=True) + EPS) * g_ref[...]


def _combine(dest_flat, y, h2p, x1, ew_t, wsg, wsu, wsd, mod3, g, L, tc=256):
    t, d = x1.shape
    w = h2p.shape[1]
    kern = functools.partial(_combine_kernel, tc=tc)
    return pl.pallas_call(
        kern,
        grid=(t // tc,),
        in_specs=[pl.BlockSpec((tc * TOP_K,), lambda i: (i,), memory_space=pltpu.SMEM),
                  pl.BlockSpec(memory_space=pl.ANY),
                  pl.BlockSpec((tc, w), lambda i: (i, 0)),
                  pl.BlockSpec((tc, d), lambda i: (i, 0)),
                  pl.BlockSpec((tc, TOP_K), lambda i: (i, 0)),
                  pl.BlockSpec(wsg.shape, lambda i: (0, 0)),
                  pl.BlockSpec(wsu.shape, lambda i: (0, 0)),
                  pl.BlockSpec(wsd.shape, lambda i: (0, 0)),
                  pl.BlockSpec((1, 6, d), lambda i: (i * tc // L, 0, 0)),
                  pl.BlockSpec((1, d), lambda i: (0, 0))],
        out_specs=pl.BlockSpec((tc, d), lambda i: (i, 0)),
        out_shape=jax.ShapeDtypeStruct((t, d), F32),
        scratch_shapes=[pltpu.VMEM((TOP_K, tc, w), U32),
                        pltpu.SemaphoreType.DMA(())],
        compiler_params=_cparams(("arbitrary",)),
    )(dest_flat, y, h2p, x1, ew_t, wsg, wsu, wsd, mod3, g.reshape(1, d))


def _split_cols(w, sizes):
    out, off = [], 0
    for s in sizes:
        out.append(w[:, off:off + s])
        off += s
    return out


def kernel(x, c, w_ada, b_ada, norm1_g, w_in, ckv_norm_g, idx_k_norm_g, w_uk, w_uv, rel_bias, lb_logits,
           attn_out_norm_g, hgrn_out_norm_g, w_out, norm2_g, w_router, router_bias, w_e_gate, w_e_up,
           w_e_down, w_s_gate, w_s_up, w_s_down, final_norm_g):
    bsz, L, d = x.shape
    t = bsz * L
    assert w_ada.shape[0] == 1, "single-layer block"
    a_width = A_HEADS * A_HEAD_DIM
    b_width = B_HEADS * B_DIM
    sizes = (a_width, A_KV_RANK, IDX_HEADS * IDX_DIM, IDX_DIM, IDX_HEADS, b_width, b_width, b_width, b_width)
    assert w_in.shape[2] == sum(sizes)

    wq_a, wckv, wiq, wik, wiw, wq_b, wf_b, wi_b, wg_b = _split_cols(w_in[0], sizes)
    w_main = jnp.concatenate([wq_a, wiq, wq_b, wi_b, wg_b], axis=1).astype(BF16)
    w_f = wf_b.astype(BF16)
    aux_pad = LANES - IDX_DIM - IDX_HEADS
    w_aux = jnp.concatenate([wckv, wik, wiw, jnp.zeros((d, aux_pad), F32)], axis=1).astype(BF16)

    mod3 = _ada(c, w_ada[0], b_ada[0]).reshape(bsz, 6, d)
    h1 = _norm1(x, mod3, norm1_g[0]).reshape(t, d)
    g1 = _matmul(h1, w_main, BF16, tm=1024, tn=512)
    fb = _matmul(h1, w_f, F32, tm=1024, tn=512)
    aux = _matmul(h1, w_aux, F32, tm=1024, tn=w_aux.shape[1])
    ckv_n, ik_lo, ik_hi = _kvnorm(aux, ckv_norm_g[0], idx_k_norm_g[0])

    tq = min(256, L)
    o_a = _dsa(g1, aux, ik_lo, ik_hi, ckv_n, w_uk[0].astype(BF16), w_uv[0].astype(BF16),
               _bias_tables(rel_bias, tq), attn_out_norm_g[0], bsz, L, tq)
    o_b = _hgrn(g1, fb, lb_logits, hgrn_out_norm_g[0], bsz, L)

    x1, h2p, logits_t = _out(o_a, o_b, x.reshape(t, d), w_out[0].astype(BF16), mod3, norm2_g[0],
                             w_router[0].T.astype(BF16), L)

    eidx, ew, rank, cnt = _route(logits_t, router_bias[0])

    bm = 256
    counts = cnt[:, 0].astype(I32)
    padded = (counts + bm - 1) // bm * bm
    pad_end = jnp.cumsum(padded)
    pad_start = pad_end - padded
    n_rows = (t * TOP_K + N_EXPERTS * (bm - 1) + bm - 1) // bm * bm
    nb = n_rows // bm
    nvb = pad_end[-1] // bm
    blk = jnp.minimum(jnp.arange(nb, dtype=I32), nvb - 1)
    eid = jnp.minimum(jnp.sum((pad_end[None, :] <= (blk * bm)[:, None]).astype(I32), axis=1), N_EXPERTS - 1)
    ar = jnp.arange(nb, dtype=I32)
    first = ((ar < nvb) & ((ar == 0) | (eid != jnp.roll(eid, 1)))).astype(I32)
    slot = (jnp.cumsum(first) - 1) % 2
    nxt_blk = pad_end[eid] // bm
    more = (nxt_blk < nvb).astype(I32)
    nxt = eid[jnp.minimum(nxt_blk, nb - 1)]

    dest = _dest(eidx, rank, pad_start)
    dest_flat = dest.T.reshape(t * TOP_K)
    xs = _dispatch(pad_end.astype(I32), padded.astype(I32), dest_flat, h2p, n_rows, bm)

    y = _experts(blk, eid, first, slot.astype(I32), nxt.astype(I32), more, nvb.reshape(1).astype(I32), xs,
                 w_e_gate[0], w_e_up[0], w_e_down[0], bm)

    out = _combine(dest_flat, y, h2p, x1, ew.T, w_s_gate[0].astype(BF16), w_s_up[0].astype(BF16),
                   w_s_down[0].astype(BF16), mod3, final_norm_g, L)
    return out.reshape(bsz, L, d)
```

```python
import functools
import math

import numpy as np
import jax
import jax.numpy as jnp
from jax import lax
from jax.experimental import pallas as pl
from jax.experimental.pallas import tpu as pltpu

F32 = jnp.float32
BF16 = jnp.bfloat16
I32 = jnp.int32
U32 = jnp.uint32

EPS = 1e-6
A_HEADS = 8
A_HEAD_DIM = 128
A_KV_RANK = 256
IDX_HEADS = 16
IDX_DIM = 64
IDX_TOPK_MAX = 256
B_HEADS = 8
B_DIM = 128
REL_BUCKETS = 32
REL_MAX_DIST = 128
N_EXPERTS = 64
TOP_K = 8
N_GROUPS = 8
TOPK_GROUPS = 4
ROUTED_SCALE = 2.5

VMEM_LIMIT_BYTES = 56 * 1024 * 1024
LANES = 128
SUBLANES = 8

NT_DIMS = (((1,), (1,)), ((), ()))
TN_DIMS = (((0,), (0,)), ((), ()))

LOG2E = math.log2(math.e)
INT_MIN = -2 ** 31
KEY_NEG_INF = -2139095041


def _cparams(sem):
    return pltpu.CompilerParams(dimension_semantics=sem, vmem_limit_bytes=VMEM_LIMIT_BYTES)


def _silu(v):
    return v * jax.nn.sigmoid(v)


def _pack_halves(v):
    n = v.shape[1] // 2
    lo = lax.bitcast_convert_type(v[:, :n].astype(BF16).astype(F32), U32)
    hi = lax.bitcast_convert_type(v[:, n:].astype(BF16).astype(F32), U32)
    return lax.shift_right_logical(lo, jnp.uint32(16)) | (hi & jnp.uint32(0xFFFF0000))


def _unpack_halves(w):
    left = lax.bitcast_convert_type(lax.shift_left(w, jnp.uint32(16)), F32)
    right = lax.bitcast_convert_type(w & jnp.uint32(0xFFFF0000), F32)
    return left, right


def _ada_kernel(c_ref, w_ref, b_ref, o_ref):
    a = _silu(c_ref[...]).astype(BF16)
    o_ref[...] = jnp.dot(a, w_ref[...].astype(BF16), preferred_element_type=F32) + b_ref[...]


def _ada(c, w, b, tn=1024):
    bsz, d = c.shape
    n = w.shape[1]
    return pl.pallas_call(
        _ada_kernel,
        grid=(n // tn,),
        in_specs=[pl.BlockSpec((bsz, d), lambda j: (0, 0)),
                  pl.BlockSpec((d, tn), lambda j: (0, j)),
                  pl.BlockSpec((1, tn), lambda j: (0, j))],
        out_specs=pl.BlockSpec((bsz, tn), lambda j: (0, j)),
        out_shape=jax.ShapeDtypeStruct((bsz, n), F32),
        compiler_params=_cparams(("arbitrary",)),
    )(c, w, b.reshape(1, n))


def _norm1_kernel(x_ref, mod_ref, g_ref, o_ref):
    x = x_ref[0]
    y = x * lax.rsqrt(jnp.mean(x * x, axis=-1, keepdims=True) + EPS) * g_ref[...]
    sh = mod_ref[0, 0:1, :]
    sc = mod_ref[0, 1:2, :]
    o_ref[0] = (y * (1.0 + sc) + sh).astype(o_ref.dtype)


def _norm1(x, mod3, g, tm=512):
    bsz, L, d = x.shape
    return pl.pallas_call(
        _norm1_kernel,
        grid=(bsz, L // tm),
        in_specs=[pl.BlockSpec((1, tm, d), lambda b, i: (b, i, 0)),
                  pl.BlockSpec((1, 6, d), lambda b, i: (b, 0, 0)),
                  pl.BlockSpec((1, d), lambda b, i: (0, 0))],
        out_specs=pl.BlockSpec((1, tm, d), lambda b, i: (b, i, 0)),
        out_shape=jax.ShapeDtypeStruct((bsz, L, d), BF16),
        compiler_params=_cparams(("parallel", "parallel")),
    )(x, mod3, g.reshape(1, d))


def _mm_kernel(a_ref, w_ref, o_ref):
    o_ref[...] = jnp.dot(a_ref[...], w_ref[...], preferred_element_type=F32).astype(o_ref.dtype)


def _matmul(a, w, out_dtype, tm, tn):
    m, k = a.shape
    n = w.shape[1]
    return pl.pallas_call(
        _mm_kernel,
        grid=(m // tm, n // tn),
        in_specs=[pl.BlockSpec((tm, k), lambda i, j: (i, 0)),
                  pl.BlockSpec((k, tn), lambda i, j: (0, j))],
        out_specs=pl.BlockSpec((tm, tn), lambda i, j: (i, j)),
        out_shape=jax.ShapeDtypeStruct((m, n), out_dtype),
        compiler_params=_cparams(("parallel", "arbitrary")),
    )(a, w)


def _kvnorm_kernel(aux_ref, gc_ref, gk_ref, ckv_ref, iklo_ref, ikhi_ref):
    ckv = aux_ref[:, :A_KV_RANK]
    ckv_ref[...] = (ckv * lax.rsqrt(jnp.mean(ckv * ckv, axis=-1, keepdims=True) + EPS)
                    * gc_ref[...]).astype(BF16)
    v = aux_ref[:, A_KV_RANK:A_KV_RANK + LANES]
    lane = lax.broadcasted_iota(I32, v.shape, 1)
    ik = jnp.where(lane < IDX_DIM, v, 0.0)
    ms = jnp.sum(ik * ik, axis=-1, keepdims=True) * (1.0 / IDX_DIM)
    ikn = ik * lax.rsqrt(ms + EPS) * gk_ref[...]
    iklo_ref[...] = ikn.astype(BF16)
    ikhi_ref[...] = pltpu.roll(ikn, IDX_DIM, 1).astype(BF16)


def _kvnorm(aux, gc, gk, tm=1024):
    t = aux.shape[0]
    gk_pad = jnp.concatenate([gk, jnp.zeros((LANES - IDX_DIM,), F32)]).reshape(1, LANES)
    return pl.pallas_call(
        _kvnorm_kernel,
        grid=(t // tm,),
        in_specs=[pl.BlockSpec((tm, aux.shape[1]), lambda i: (i, 0)),
                  pl.BlockSpec((1, A_KV_RANK), lambda i: (0, 0)),
                  pl.BlockSpec((1, LANES), lambda i: (0, 0))],
        out_specs=[pl.BlockSpec((tm, A_KV_RANK), lambda i: (i, 0)),
                   pl.BlockSpec((tm, LANES), lambda i: (i, 0)),
                   pl.BlockSpec((tm, LANES), lambda i: (i, 0))],
        out_shape=[jax.ShapeDtypeStruct((t, A_KV_RANK), BF16),
                   jax.ShapeDtypeStruct((t, LANES), BF16),
                   jax.ShapeDtypeStruct((t, LANES), BF16)],
        compiler_params=_cparams(("parallel",)),
    )(aux, gc.reshape(1, A_KV_RANK), gk_pad)


def _t5_bucket(rel):
    n = jnp.maximum(rel, 0)
    max_exact = REL_BUCKETS // 2
    n_large = jnp.maximum(n, max_exact).astype(F32)
    large = max_exact + (jnp.log(n_large / max_exact) / math.log(REL_MAX_DIST / max_exact)
                         * (REL_BUCKETS - max_exact)).astype(I32)
    large = jnp.minimum(large, REL_BUCKETS - 1)
    return jnp.where(n < max_exact, n, large)


def _bias_tables(rel_bias, tq):
    assert tq + 1 >= REL_MAX_DIST
    nh = rel_bias.shape[1]
    dist = jnp.maximum(jnp.arange(3 * tq + 1, dtype=I32) - tq, 0)
    v = rel_bias.astype(F32)[_t5_bucket(dist)].T * LOG2E
    n = v.shape[1]
    x = jnp.broadcast_to(v[:, None, :], (nh, tq, n)).reshape(nh, tq * n)[:, :tq * (n - 1)].reshape(nh, tq, n - 1)
    near = x[:, :, tq:2 * tq]
    prev = x[:, :, 2 * tq:3 * tq]
    far = jnp.broadcast_to(v[:, n - 1][:, None, None], near.shape)
    return jnp.stack([near, prev, far])


def _dsa_kernel(qa_ref, iq_ref, aux_ref, iklo_ref, ikhi_ref, ckv_ref, ckvt_ref, wuk_ref, wuvt_ref, bias_ref,
                g_ref, o_ref, iqt_ref, iwt_ref, key_ref, qlt_ref, m_ref, l_ref, acc_ref, tie_ref, madd_ref,
                *, tq, topk):
    i = pl.program_id(1)
    nh = A_HEADS
    npair = IDX_HEADS // 2

    r_i = lax.broadcasted_iota(I32, (LANES, LANES), 0)
    c_i = lax.broadcasted_iota(I32, (LANES, LANES), 1)
    eye = jnp.where(r_i == c_i, 1.0, 0.0).astype(BF16)
    for p in range(npair):
        iqt_ref[:, p * tq:(p + 1) * tq] = lax.dot_general(
            eye, iq_ref[:, p * LANES:(p + 1) * LANES], NT_DIMS, preferred_element_type=F32).astype(BF16)
    iwt_ref[...] = (jnp.transpose(aux_ref[...])[IDX_DIM:IDX_DIM + IDX_HEADS, :]
                    * (IDX_HEADS ** -0.5 * IDX_DIM ** -0.5))
    for h in range(nh):
        ql = lax.dot_general(wuk_ref[h], qa_ref[:, h * A_HEAD_DIM:(h + 1) * A_HEAD_DIM], NT_DIMS,
                             preferred_element_type=F32)
        qlt_ref[:, h * tq:(h + 1) * tq] = (ql * (A_HEAD_DIM ** -0.5 * LOG2E)).astype(BF16)

    kpos = lax.broadcasted_iota(I32, (tq, tq), 0)
    qpos = lax.broadcasted_iota(I32, (tq, tq), 1) + i * tq

    def score_body(kc, carry):
        off = pl.multiple_of(kc * tq, tq)
        klo = iklo_ref[0, pl.ds(off, tq), :]
        khi = ikhi_ref[0, pl.ds(off, tq), :]
        acc = jnp.zeros((tq, tq), F32)
        for p in range(npair):
            rhs = iqt_ref[:, p * tq:(p + 1) * tq]
            se = jnp.dot(klo, rhs, preferred_element_type=F32)
            so = jnp.dot(khi, rhs, preferred_element_type=F32)
            acc = acc + jnp.maximum(se, 0.0) * iwt_ref[2 * p:2 * p + 1, :]
            acc = acc + jnp.maximum(so, 0.0) * iwt_ref[2 * p + 1:2 * p + 2, :]
        bits = lax.bitcast_convert_type(acc, I32)
        key = jnp.where(bits >= 0, bits, bits ^ jnp.int32(0x7FFFFFFF))
        key_ref[kc] = jnp.where(kpos + off <= qpos, key, jnp.int32(KEY_NEG_INF))
        return carry

    lax.fori_loop(0, i + 1, score_body, 0)

    def count_ge(cand):
        def body(kc, c):
            hit = jnp.where(key_ref[kc] >= cand, 1.0, 0.0)
            return c + jnp.sum(hit.reshape(tq // 8, 8, tq), axis=0)
        c = lax.fori_loop(0, i + 1, body, jnp.zeros((8, tq), F32))
        return jnp.sum(c, axis=0, keepdims=True)

    kf = float(topk)
    thr = jnp.where(count_ge(jnp.zeros((1, tq), I32)) >= kf, jnp.int32(0), jnp.int32(INT_MIN))

    def bit_body(j, thr):
        cand = thr | lax.shift_left(jnp.int32(1), 30 - j)
        return jnp.where(count_ge(cand) >= kf, cand, thr)

    thr = lax.fori_loop(0, 31, bit_body, thr)

    def count_gt():
        def body(kc, c):
            hit = jnp.where(key_ref[kc] > thr, 1.0, 0.0)
            return c + jnp.sum(hit.reshape(tq // 8, 8, tq), axis=0)
        c = lax.fori_loop(0, i + 1, body, jnp.zeros((8, tq), F32))
        return jnp.sum(c, axis=0, keepdims=True)

    need = kf - count_gt()
    tied = (count_ge(thr) > kf) & (thr > jnp.int32(KEY_NEG_INF))
    has_tie = jnp.max(jnp.where(tied, 1.0, 0.0)) > 0.0
    tie_ref[...] = jnp.zeros(tie_ref.shape, F32)

    m_ref[...] = jnp.full(m_ref.shape, -jnp.inf, F32)
    l_ref[...] = jnp.zeros(l_ref.shape, F32)
    acc_ref[...] = jnp.zeros(acc_ref.shape, F32)

    def att_body(kc, carry):
        off = pl.multiple_of(kc * tq, tq)
        ckv = ckv_ref[0, pl.ds(off, tq), :]
        ckvt = ckvt_ref[0, kc]
        key = key_ref[kc]
        causal = key > jnp.int32(KEY_NEG_INF)

        @pl.when(jnp.logical_not(has_tie))
        def _():
            madd_ref[...] = jnp.where((key >= thr) & causal, 0.0, -jnp.inf)

        @pl.when(has_tie)
        def _():
            eq = key == thr
            eqf = jnp.where(eq, 1.0, 0.0)
            before = (lax.broadcasted_iota(I32, (tq, tq), 1) < lax.broadcasted_iota(I32, (tq, tq), 0))
            rank = jnp.dot(jnp.where(before, 1.0, 0.0).astype(BF16), eqf.astype(BF16),
                           preferred_element_type=F32) + tie_ref[...]
            keep = (key > thr) | (eq & (rank < need))
            madd_ref[...] = jnp.where(keep & causal, 0.0, -jnp.inf)
            tie_ref[...] = tie_ref[...] + jnp.sum(eqf, axis=0, keepdims=True)

        madd = madd_ref[...]
        d = jnp.minimum(i - kc, 2)
        for h in range(nh):
            s = jnp.dot(ckv, qlt_ref[:, h * tq:(h + 1) * tq], preferred_element_type=F32)
            s = s + (bias_ref[d, h] + madd)
            m_old = m_ref[h:h + 1, :]
            m_new = jnp.maximum(m_old, jnp.max(s, axis=0, keepdims=True))
            m_safe = jnp.where(m_new == -jnp.inf, 0.0, m_new)
            alpha = jnp.exp2(m_old - m_safe)
            p = jnp.exp2(s - m_safe)
            l_ref[h:h + 1, :] = alpha * l_ref[h:h + 1, :] + jnp.sum(p, axis=0, keepdims=True)
            acc_ref[h] = alpha * acc_ref[h] + jnp.dot(ckvt, p.astype(BF16), preferred_element_type=F32)
            m_ref[h:h + 1, :] = m_new
        return carry

    lax.fori_loop(0, i + 1, att_body, 0)

    outs = []
    for h in range(nh):
        o_lat = (acc_ref[h] / l_ref[h:h + 1, :]).astype(BF16)
        outs.append(jnp.transpose(jnp.dot(wuvt_ref[h], o_lat, preferred_element_type=F32)))
    o = jnp.concatenate(outs, axis=1)
    o = o * lax.rsqrt(jnp.mean(o * o, axis=-1, keepdims=True) + EPS) * g_ref[...]
    o_ref[...] = o.astype(o_ref.dtype)


def _dsa(g1, aux, ik_lo, ik_hi, ckv_n, w_uk, w_uv, bias_tab, g, bsz, L, tq):
    t = bsz * L
    nq = L // tq
    topk = min(IDX_TOPK_MAX, L // 4)
    aux_blk = A_KV_RANK // LANES
    kern = functools.partial(_dsa_kernel, tq=tq, topk=topk)
    width = A_HEADS * A_HEAD_DIM
    ckv3 = ckv_n.reshape(bsz, L, A_KV_RANK)
    ckvt = ckv_n.reshape(bsz, nq, tq, A_KV_RANK).transpose(0, 1, 3, 2)
    return pl.pallas_call(
        kern,
        grid=(bsz, nq),
        in_specs=[pl.BlockSpec((tq, width), lambda b, i: (b * nq + i, 0)),
                  pl.BlockSpec((tq, IDX_HEADS * IDX_DIM), lambda b, i: (b * nq + i, 1)),
                  pl.BlockSpec((tq, LANES), lambda b, i: (b * nq + i, aux_blk)),
                  pl.BlockSpec((1, L, LANES), lambda b, i: (b, 0, 0)),
                  pl.BlockSpec((1, L, LANES), lambda b, i: (b, 0, 0)),
                  pl.BlockSpec((1, L, A_KV_RANK), lambda b, i: (b, 0, 0)),
                  pl.BlockSpec((1, nq, A_KV_RANK, tq), lambda b, i: (b, 0, 0, 0)),
                  pl.BlockSpec((A_HEADS, A_KV_RANK, A_HEAD_DIM), lambda b, i: (0, 0, 0)),
                  pl.BlockSpec((A_HEADS, A_HEAD_DIM, A_KV_RANK), lambda b, i: (0, 0, 0)),
                  pl.BlockSpec((3, A_HEADS, tq, tq), lambda b, i: (0, 0, 0, 0)),
                  pl.BlockSpec((1, width), lambda b, i: (0, 0))],
        out_specs=pl.BlockSpec((tq, width), lambda b, i: (b * nq + i, 0)),
        out_shape=jax.ShapeDtypeStruct((t, width), BF16),
        scratch_shapes=[pltpu.VMEM((LANES, IDX_HEADS // 2 * tq), BF16),
                        pltpu.VMEM((IDX_HEADS, tq), F32),
                        pltpu.VMEM((nq, tq, tq), I32),
                        pltpu.VMEM((A_KV_RANK, A_HEADS * tq), BF16),
                        pltpu.VMEM((A_HEADS, tq), F32),
                        pltpu.VMEM((A_HEADS, tq), F32),
                        pltpu.VMEM((A_HEADS, A_KV_RANK, tq), F32),
                        pltpu.VMEM((1, tq), F32),
                        pltpu.VMEM((tq, tq), F32)],
        compiler_params=_cparams(("parallel", "arbitrary")),
    )(g1, g1, aux, ik_lo.reshape(bsz, L, LANES), ik_hi.reshape(bsz, L, LANES),
      ckv3, ckvt, w_uk, jnp.transpose(w_uv, (0, 2, 1)), bias_tab, g.reshape(1, width))


def _hgrn_kernel(q_ref, i_ref, gate_ref, f_ref, lbl_ref, ng_ref, o_ref, st_ref, *, chunk, rblk):
    @pl.when(pl.program_id(1) == 0)
    def _():
        st_ref[...] = jnp.zeros(st_ref.shape, F32)

    ll = lbl_ref[...]
    ex = jnp.exp(ll - jnp.max(ll, axis=0, keepdims=True))
    lb_all = ex[0:1] / jnp.sum(ex, axis=0, keepdims=True)

    r_i = lax.broadcasted_iota(I32, (chunk, chunk), 0)
    c_i = lax.broadcasted_iota(I32, (chunk, chunk), 1)
    tri = jnp.where(r_i >= c_i, 1.0, 0.0).astype(BF16)
    row_k = lax.broadcasted_iota(I32, (chunk, B_DIM), 0)

    for h in range(B_HEADS):
        sl = slice(h * B_DIM, (h + 1) * B_DIM)
        lb = lb_all[:, sl]
        f = lb + (1.0 - lb) * jax.nn.sigmoid(f_ref[:, sl])
        lf = jnp.log(f)
        kk = 1.0 - f
        l1 = lf.astype(BF16)
        r1 = lf - l1.astype(F32)
        l2 = r1.astype(BF16)
        l3 = (r1 - l2.astype(F32)).astype(BF16)
        cs = jnp.dot(tri, jnp.concatenate([l1, l2, l3], axis=1), preferred_element_type=F32)
        bc = cs[:, :B_DIM] + cs[:, B_DIM:2 * B_DIM] + cs[:, 2 * B_DIM:]

        q = q_ref[:, sl].astype(F32)
        v = i_ref[:, sl]
        st = st_ref[h]
        o = lax.dot_general((q * jnp.exp(bc)).astype(BF16), st.astype(BF16), NT_DIMS,
                            preferred_element_type=F32)
        parts = []
        for r in range(chunk // rblk):
            lo, hi = r * rblk, (r + 1) * rblk
            base = bc[lo - 1:lo] if r > 0 else jnp.zeros((1, B_DIM), F32)
            qt = (q[lo:hi] * jnp.exp(bc[lo:hi] - base)).astype(BF16)
            kt = (kk * jnp.exp(jnp.where(row_k < hi, base - bc, 0.0))).astype(BF16)
            parts.append(lax.dot_general(qt, kt, NT_DIMS, preferred_element_type=F32))
        sc = jnp.where(c_i <= r_i, jnp.concatenate(parts, axis=0), 0.0).astype(BF16)
        o = o + jnp.dot(sc, v, preferred_element_type=F32)

        last = bc[chunk - 1:chunk]
        kd = (kk * jnp.exp(last - bc)).astype(BF16)
        st_ref[h] = st * jnp.exp(last) + lax.dot_general(v, kd, TN_DIMS, preferred_element_type=F32)

        y = o * lax.rsqrt(jnp.mean(o * o, axis=-1, keepdims=True) + EPS) * ng_ref[:, sl]
        o_ref[:, sl] = (y * _silu(gate_ref[:, sl].astype(F32))).astype(o_ref.dtype)


def _hgrn(g1, fb, lb_logits, ng, bsz, L, chunk=128, rblk=16):
    t = bsz * L
    nc = L // chunk
    width = B_HEADS * B_DIM
    kern = functools.partial(_hgrn_kernel, chunk=chunk, rblk=rblk)
    return pl.pallas_call(
        kern,
        grid=(bsz, nc),
        in_specs=[pl.BlockSpec((chunk, width), lambda b, c: (b * nc + c, 2)),
                  pl.BlockSpec((chunk, width), lambda b, c: (b * nc + c, 3)),
                  pl.BlockSpec((chunk, width), lambda b, c: (b * nc + c, 4)),
                  pl.BlockSpec((chunk, width), lambda b, c: (b * nc + c, 0)),
                  pl.BlockSpec(lb_logits.shape, lambda b, c: (0, 0)),
                  pl.BlockSpec((1, width), lambda b, c: (0, 0))],
        out_specs=pl.BlockSpec((chunk, width), lambda b, c: (b * nc + c, 0)),
        out_shape=jax.ShapeDtypeStruct((t, width), BF16),
        scratch_shapes=[pltpu.VMEM((B_HEADS, B_DIM, B_DIM), F32)],
        compiler_params=_cparams(("parallel", "arbitrary")),
    )(g1, g1, g1, fb, lb_logits, ng.reshape(1, width))


def _out_kernel(oa_ref, ob_ref, x_ref, wa_ref, wb_ref, mod_ref, g_ref, wr_ref, x1_ref, h2_ref, lg_ref):
    mix = jnp.dot(oa_ref[...], wa_ref[...], preferred_element_type=F32)
    mix = mix + jnp.dot(ob_ref[...], wb_ref[...], preferred_element_type=F32)
    x1 = x_ref[...] + mod_ref[0, 2:3, :] * mix
    x1_ref[...] = x1
    y = x1 * lax.rsqrt(jnp.mean(x1 * x1, axis=-1, keepdims=True) + EPS) * g_ref[...]
    h2 = y * (1.0 + mod_ref[0, 4:5, :]) + mod_ref[0, 3:4, :]
    h2_ref[...] = _pack_halves(h2)
    lg_ref[...] = lax.dot_general(wr_ref[...], h2.astype(BF16), NT_DIMS, preferred_element_type=F32)


def _out(oa, ob, x2d, w_out, mod3, g, w_router_t, L, tm=256):
    t, d = x2d.shape
    half = oa.shape[1]
    ne = w_router_t.shape[0]
    return pl.pallas_call(
        _out_kernel,
        grid=(t // tm,),
        in_specs=[pl.BlockSpec((tm, half), lambda i: (i, 0)),
                  pl.BlockSpec((tm, half), lambda i: (i, 0)),
                  pl.BlockSpec((tm, d), lambda i: (i, 0)),
                  pl.BlockSpec((half, d), lambda i: (0, 0)),
                  pl.BlockSpec((half, d), lambda i: (1, 0)),
                  pl.BlockSpec((1, 6, d), lambda i: (i * tm // L, 0, 0)),
                  pl.BlockSpec((1, d), lambda i: (0, 0)),
                  pl.BlockSpec((ne, d), lambda i: (0, 0))],
        out_specs=[pl.BlockSpec((tm, d), lambda i: (i, 0)),
                   pl.BlockSpec((tm, d // 2), lambda i: (i, 0)),
                   pl.BlockSpec((ne, tm), lambda i: (0, i))],
        out_shape=[jax.ShapeDtypeStruct((t, d), F32),
                   jax.ShapeDtypeStruct((t, d // 2), U32),
                   jax.ShapeDtypeStruct((ne, t), F32)],
        compiler_params=_cparams(("parallel",)),
    )(oa, ob, x2d, w_out, w_out, mod3, g.reshape(1, d), w_router_t)


def _rows_to_tile(rows, nrow):
    n = rows[0].shape[1]
    ridx = lax.broadcasted_iota(I32, (nrow, n), 0)
    out = jnp.zeros((nrow, n), rows[0].dtype)
    for r, v in enumerate(rows):
        out = jnp.where(ridx == r, jnp.broadcast_to(v, (nrow, n)), out)
    return out


def _route_kernel(lg_ref, rb_ref, eidx_ref, ew_ref, rank_ref, cnt_ref, run_ref):
    @pl.when(pl.program_id(0) == 0)
    def _():
        run_ref[...] = jnp.zeros(run_ref.shape, F32)

    ne, tt = lg_ref.shape
    per = ne // N_GROUPS
    sc = jax.nn.sigmoid(lg_ref[...])
    ch = sc + rb_ref[...]
    neg = -jnp.inf

    sub = lax.broadcasted_iota(I32, (per, tt), 0).astype(F32)
    gsc = []
    for g in range(N_GROUPS):
        cg = ch[g * per:(g + 1) * per]
        m1 = jnp.max(cg, axis=0, keepdims=True)
        first = jnp.min(jnp.where(cg == m1, sub, float(per)), axis=0, keepdims=True)
        m2 = jnp.max(jnp.where(sub == first, neg, cg), axis=0, keepdims=True)
        gsc.append(m1 + m2)
    grp = _rows_to_tile(gsc, N_GROUPS)

    gid = lax.broadcasted_iota(I32, (N_GROUPS, tt), 0).astype(F32)
    gsel = jnp.zeros((N_GROUPS, tt), F32)
    for _ in range(TOPK_GROUPS):
        mx = jnp.max(grp, axis=0, keepdims=True)
        gi = jnp.min(jnp.where(grp == mx, gid, float(N_GROUPS)), axis=0, keepdims=True)
        pick = gid == gi
        gsel = jnp.where(pick, 1.0, gsel)
        grp = jnp.where(pick, neg, grp)

    eid = lax.broadcasted_iota(I32, (ne, tt), 0).astype(F32)
    cm = jnp.full((ne, tt), neg, F32)
    for g in range(N_GROUPS):
        in_g = (eid >= float(g * per)) & (eid < float((g + 1) * per))
        cm = jnp.where(in_g & (jnp.broadcast_to(gsel[g:g + 1], (ne, tt)) > 0.5), ch, cm)

    idx_rows, w_rows = [], []
    onehot = jnp.zeros((ne, tt), F32)
    for _ in range(TOP_K):
        mx = jnp.max(cm, axis=0, keepdims=True)
        ei = jnp.min(jnp.where(cm == mx, eid, float(ne)), axis=0, keepdims=True)
        pick = eid == ei
        idx_rows.append(ei)
        w_rows.append(jnp.sum(jnp.where(pick, sc, 0.0), axis=0, keepdims=True))
        onehot = jnp.where(pick, 1.0, onehot)
        cm = jnp.where(pick, neg, cm)
    wsum = w_rows[0]
    for w in w_rows[1:]:
        wsum = wsum + w
    w_rows = [w / wsum * ROUTED_SCALE for w in w_rows]

    a_i = lax.broadcasted_iota(I32, (tt, tt), 0)
    b_i = lax.broadcasted_iota(I32, (tt, tt), 1)
    upper = jnp.where(a_i < b_i, 1.0, 0.0).astype(BF16)
    rank_full = jnp.dot(onehot.astype(BF16), upper, preferred_element_type=F32) + run_ref[...]
    r_rows = [jnp.sum(jnp.where(eid == ei, rank_full, 0.0), axis=0, keepdims=True) for ei in idx_rows]
    run = run_ref[...] + jnp.sum(onehot, axis=1, keepdims=True)
    run_ref[...] = run

    eidx_ref[...] = _rows_to_tile(idx_rows, TOP_K).astype(I32)
    ew_ref[...] = _rows_to_tile(w_rows, TOP_K)
    rank_ref[...] = _rows_to_tile(r_rows, TOP_K).astype(I32)
    cnt_ref[...] = jnp.broadcast_to(run, cnt_ref.shape)


def _route(logits_t, router_bias, tt=512):
    ne, t = logits_t.shape
    return pl.pallas_call(
        _route_kernel,
        grid=(t // tt,),
        in_specs=[pl.BlockSpec((ne, tt), lambda i: (0, i)),
                  pl.BlockSpec((ne, 1), lambda i: (0, 0))],
        out_specs=[pl.BlockSpec((TOP_K, tt), lambda i: (0, i)),
                   pl.BlockSpec((TOP_K, tt), lambda i: (0, i)),
                   pl.BlockSpec((TOP_K, tt), lambda i: (0, i)),
                   pl.BlockSpec((ne, LANES), lambda i: (0, 0))],
        out_shape=[jax.ShapeDtypeStruct((TOP_K, t), I32),
                   jax.ShapeDtypeStruct((TOP_K, t), F32),
                   jax.ShapeDtypeStruct((TOP_K, t), I32),
                   jax.ShapeDtypeStruct((ne, LANES), F32)],
        scratch_shapes=[pltpu.VMEM((ne, 1), F32)],
        compiler_params=_cparams(("arbitrary",)),
    )(logits_t, router_bias.reshape(ne, 1))


def _dest_kernel(eidx_ref, rank_ref, ps_ref, o_ref):
    ne = ps_ref.shape[0]
    tt = eidx_ref.shape[1]
    eid = lax.broadcasted_iota(I32, (ne, tt), 0)
    ps = jnp.broadcast_to(ps_ref[...], (ne, tt))
    rows = []
    for k in range(TOP_K):
        start = jnp.sum(jnp.where(eid == eidx_ref[k:k + 1, :], ps, 0.0), axis=0, keepdims=True)
        rows.append(start + rank_ref[k:k + 1, :].astype(F32))
    o_ref[...] = _rows_to_tile(rows, TOP_K).astype(I32)


def _dest(eidx, rank, pad_start, tt=2048):
    t = eidx.shape[1]
    tt = min(tt, t)
    ne = pad_start.shape[0]
    return pl.pallas_call(
        _dest_kernel,
        grid=(t // tt,),
        in_specs=[pl.BlockSpec((TOP_K, tt), lambda i: (0, i)),
                  pl.BlockSpec((TOP_K, tt), lambda i: (0, i)),
                  pl.BlockSpec((ne, 1), lambda i: (0, 0))],
        out_specs=pl.BlockSpec((TOP_K, tt), lambda i: (0, i)),
        out_shape=jax.ShapeDtypeStruct((TOP_K, t), I32),
        compiler_params=_cparams(("parallel",)),
    )(eidx, rank, pad_start.astype(F32).reshape(ne, 1))


def _dispatch_kernel(pend_ref, padded_ref, dest_ref, h_ref, xs_ref, zbuf_ref, zsem, sem, *, td, bm):
    i = pl.program_id(0)

    def tail_copy(e):
        start = pl.multiple_of(pend_ref[e] - bm, bm)
        return pltpu.make_async_copy(zbuf_ref, xs_ref.at[pl.ds(start, bm)], zsem)

    @pl.when(i == 0)
    def _():
        zbuf_ref[...] = jnp.zeros(zbuf_ref.shape, zbuf_ref.dtype)

        def start_body(e, c):
            @pl.when(padded_ref[e] > 0)
            def _():
                tail_copy(e).start()
            return c

        def wait_body(e, c):
            @pl.when(padded_ref[e] > 0)
            def _():
                tail_copy(e).wait()
            return c

        lax.fori_loop(0, N_EXPERTS, start_body, 0)
        lax.fori_loop(0, N_EXPERTS, wait_body, 0)

        def unused_copy(b):
            return pltpu.make_async_copy(zbuf_ref, xs_ref.at[pl.ds(pl.multiple_of(b * bm, bm), bm)], zsem)

        def ustart_body(b, c):
            unused_copy(b).start()
            return c

        def uwait_body(b, c):
            unused_copy(b).wait()
            return c

        first_unused = pend_ref[N_EXPERTS - 1] // bm
        lax.fori_loop(first_unused, xs_ref.shape[0] // bm, ustart_body, 0)
        lax.fori_loop(first_unused, xs_ref.shape[0] // bm, uwait_body, 0)

    for j in range(td):
        for k in range(TOP_K):
            dst = xs_ref.at[dest_ref[j * TOP_K + k]]
            pltpu.make_async_copy(h_ref.at[j], dst, sem).start(priority=k % 2)
    for _ in range(TOP_K):
        pltpu.make_async_copy(h_ref, xs_ref.at[pl.ds(0, td)], sem).wait()


def _dispatch(pad_end, padded, dest_flat, h2p, n_rows, bm, td=256):
    t, w = h2p.shape
    kern = functools.partial(_dispatch_kernel, td=td, bm=bm)
    return pl.pallas_call(
        kern,
        grid_spec=pltpu.PrefetchScalarGridSpec(
            num_scalar_prefetch=2,
            grid=(t // td,),
            in_specs=[pl.BlockSpec((td * TOP_K,), lambda i, *_: (i,), memory_space=pltpu.SMEM),
                      pl.BlockSpec((td, w), lambda i, *_: (i, 0))],
            out_specs=pl.BlockSpec(memory_space=pl.ANY),
            scratch_shapes=[pltpu.VMEM((bm, w), U32),
                            pltpu.SemaphoreType.DMA(()),
                            pltpu.SemaphoreType.DMA(())]),
        out_shape=jax.ShapeDtypeStruct((n_rows, w), U32),
        compiler_params=_cparams(("arbitrary",)),
    )(pad_end, padded, dest_flat, h2p)


def _ffn(xw, wg_ref, wu_ref, wd_ref):
    half = xw.shape[1]
    left, right = _unpack_halves(xw)
    left = left.astype(BF16)
    right = right.astype(BF16)

    def proj(w_ref):
        return (jnp.dot(left, w_ref[:half, :], preferred_element_type=F32)
                + jnp.dot(right, w_ref[half:, :], preferred_element_type=F32))

    act = (_silu(proj(wg_ref)) * proj(wu_ref)).astype(BF16)
    return jnp.dot(act, wd_ref[...], preferred_element_type=F32)


def _expert_kernel(blk_ref, eid_ref, first_ref, slot_ref, nxt_ref, more_ref, nvb_ref,
                   x_ref, wg_hbm, wu_hbm, wd_hbm, o_ref,
                   wg_f, wu_f, wd_f, wg_s, wu_s, wd_s, sems):
    i = pl.program_id(0)

    def weight_copies(e, slot):
        return (pltpu.make_async_copy(wg_hbm.at[e], wg_f.at[slot], sems.at[slot]),
                pltpu.make_async_copy(wu_hbm.at[e], wu_f.at[slot], sems.at[slot]),
                pltpu.make_async_copy(wd_hbm.at[e], wd_f.at[slot], sems.at[slot]))

    @pl.when(i == 0)
    def _():
        for cp in weight_copies(eid_ref[0], 0):
            cp.start()

    @pl.when(first_ref[i] == 1)
    def _():
        slot = slot_ref[i]
        for cp in weight_copies(eid_ref[i], slot):
            cp.wait()

        @pl.when(more_ref[i] == 1)
        def _():
            for cp in weight_copies(nxt_ref[i], 1 - slot):
                cp.start()

        wg_s[...] = wg_f[slot].astype(BF16)
        wu_s[...] = wu_f[slot].astype(BF16)
        wd_s[...] = wd_f[slot].astype(BF16)

    @pl.when(i < nvb_ref[0])
    def _():
        o_ref[...] = _pack_halves(_ffn(x_ref[...], wg_s, wu_s, wd_s))

    @pl.when(i >= nvb_ref[0])
    def _():
        o_ref[...] = jnp.zeros(o_ref.shape, o_ref.dtype)


def _experts(blk, eid, first, slot, nxt, more, nvb, xs, wg, wu, wd, bm):
    n_rows, w = xs.shape
    ne, d, f = wg.shape
    return pl.pallas_call(
        _expert_kernel,
        grid_spec=pltpu.PrefetchScalarGridSpec(
            num_scalar_prefetch=7,
            grid=(n_rows // bm,),
            in_specs=[pl.BlockSpec((bm, w), lambda i, blk, *_: (blk[i], 0)),
                      pl.BlockSpec(memory_space=pl.ANY),
                      pl.BlockSpec(memory_space=pl.ANY),
                      pl.BlockSpec(memory_space=pl.ANY)],
            out_specs=pl.BlockSpec((bm, w), lambda i, *_: (i, 0)),
            scratch_shapes=[pltpu.VMEM((2, d, f), F32), pltpu.VMEM((2, d, f), F32), pltpu.VMEM((2, f, d), F32),
                            pltpu.VMEM((d, f), BF16), pltpu.VMEM((d, f), BF16), pltpu.VMEM((f, d), BF16),
                            pltpu.SemaphoreType.DMA((2,))]),
        out_shape=jax.ShapeDtypeStruct((n_rows, w), U32),
        compiler_params=_cparams(("arbitrary",)),
    )(blk, eid, first, slot, nxt, more, nvb, xs, wg, wu, wd)


def _combine_kernel(dest_ref, dnext_ref, y_ref, h_ref, x1_ref, ew_ref, wg_ref, wu_ref, wd_ref, mod_ref, g_ref,
                    o_ref, gbuf_a, gbuf_b, sems, *, tc):
    i = pl.program_id(0)
    last = pl.num_programs(0) - 1

    def issue(idx_ref, gbuf, sem):
        for j in range(tc):
            for k in range(TOP_K):
                src = y_ref.at[idx_ref[j * TOP_K + k]]
                pltpu.make_async_copy(src, gbuf.at[k, j], sem).start(priority=k % 2)

    def wait_all(gbuf, sem):
        for k in range(TOP_K):
            pltpu.make_async_copy(y_ref.at[pl.ds(0, tc)], gbuf.at[k], sem).wait()

    def compute(gbuf):
        shared = _ffn(h_ref[...], wg_ref, wu_ref, wd_ref)
        half = h_ref.shape[1]
        ew = ew_ref[...]
        left = shared[:, :half]
        right = shared[:, half:]
        for k in range(TOP_K):
            yl, yr = _unpack_halves(gbuf[k])
            wk = ew[:, k:k + 1]
            left = left + wk * yl
            right = right + wk * yr
        x2 = x1_ref[...] + mod_ref[0, 5:6, :] * jnp.concatenate([left, right], axis=1)
        o_ref[...] = x2 * lax.rsqrt(jnp.mean(x2 * x2, axis=-1, keepdims=True) + EPS) * g_ref[...]

    @pl.when(i == 0)
    def _():
        def group_body(g, c):
            base = pl.multiple_of(g * SUBLANES, SUBLANES)
            for jj in range(SUBLANES):
                for k in range(TOP_K):
                    src = y_ref.at[dest_ref[(base + jj) * TOP_K + k]]
                    pltpu.make_async_copy(src, gbuf_a.at[k, base + jj], sems.at[0]).start(priority=k % 2)
            return c

        lax.fori_loop(0, tc // SUBLANES, group_body, 0)

    @pl.when(i % 2 == 0)
    def _():
        wait_all(gbuf_a, sems.at[0])
        issue(dnext_ref, gbuf_b, sems.at[1])
        compute(gbuf_a)

    @pl.when(i % 2 == 1)
    def _():
        wait_all(gbuf_b, sems.at[1])
        issue(dnext_ref, gbuf_a, sems.at[0])
        compute(gbuf_b)

    @pl.when((i == last) & (i % 2 == 0))
    def _():
        wait_all(gbuf_b, sems.at[1])

    @pl.when((i == last) & (i % 2 == 1))
    def _():
        wait_all(gbuf_a, sems.at[0])


def _combine(dest_flat, y, h2p, x1, ew_t, wsg, wsu, wsd, mod3, g, L, tc=256):
    t, d = x1.shape
    w = h2p.shape[1]
    nt = t // tc
    kern = functools.partial(_combine_kernel, tc=tc)
    return pl.pallas_call(
        kern,
        grid=(nt,),
        in_specs=[pl.BlockSpec((tc * TOP_K,), lambda i: (i,), memory_space=pltpu.SMEM),
                  pl.BlockSpec((tc * TOP_K,), lambda i: (jnp.minimum(i + 1, nt - 1),), memory_space=pltpu.SMEM),
                  pl.BlockSpec(memory_space=pl.ANY),
                  pl.BlockSpec((tc, w), lambda i: (i, 0)),
                  pl.BlockSpec((tc, d), lambda i: (i, 0)),
                  pl.BlockSpec((tc, TOP_K), lambda i: (i, 0)),
                  pl.BlockSpec(wsg.shape, lambda i: (0, 0)),
                  pl.BlockSpec(wsu.shape, lambda i: (0, 0)),
                  pl.BlockSpec(wsd.shape, lambda i: (0, 0)),
                  pl.BlockSpec((1, 6, d), lambda i: (i * tc // L, 0, 0)),
                  pl.BlockSpec((1, d), lambda i: (0, 0))],
        out_specs=pl.BlockSpec((tc, d), lambda i: (i, 0)),
        out_shape=jax.ShapeDtypeStruct((t, d), F32),
        scratch_shapes=[pltpu.VMEM((TOP_K, tc, w), U32),
                        pltpu.VMEM((TOP_K, tc, w), U32),
                        pltpu.SemaphoreType.DMA((2,))],
        compiler_params=_cparams(("arbitrary",)),
    )(dest_flat, dest_flat, y, h2p, x1, ew_t, wsg, wsu, wsd, mod3, g.reshape(1, d))


def _split_cols(w, sizes):
    out, off = [], 0
    for s in sizes:
        out.append(w[:, off:off + s])
        off += s
    return out


def kernel(x, c, w_ada, b_ada, norm1_g, w_in, ckv_norm_g, idx_k_norm_g, w_uk, w_uv, rel_bias, lb_logits,
           attn_out_norm_g, hgrn_out_norm_g, w_out, norm2_g, w_router, router_bias, w_e_gate, w_e_up,
           w_e_down, w_s_gate, w_s_up, w_s_down, final_norm_g):
    bsz, L, d = x.shape
    t = bsz * L
    assert w_ada.shape[0] == 1, "single-layer block"
    a_width = A_HEADS * A_HEAD_DIM
    b_width = B_HEADS * B_DIM
    sizes = (a_width, A_KV_RANK, IDX_HEADS * IDX_DIM, IDX_DIM, IDX_HEADS, b_width, b_width, b_width, b_width)
    assert w_in.shape[2] == sum(sizes)

    wq_a, wckv, wiq, wik, wiw, wq_b, wf_b, wi_b, wg_b = _split_cols(w_in[0], sizes)
    w_main = jnp.concatenate([wq_a, wiq, wq_b, wi_b, wg_b], axis=1).astype(BF16)
    w_f = wf_b.astype(BF16)
    aux_pad = LANES - IDX_DIM - IDX_HEADS
    w_aux = jnp.concatenate([wckv, wik, wiw, jnp.zeros((d, aux_pad), F32)], axis=1).astype(BF16)

    mod3 = _ada(c, w_ada[0], b_ada[0]).reshape(bsz, 6, d)
    h1 = _norm1(x, mod3, norm1_g[0]).reshape(t, d)
    g1 = _matmul(h1, w_main, BF16, tm=1024, tn=512)
    fb = _matmul(h1, w_f, F32, tm=1024, tn=512)
    aux = _matmul(h1, w_aux, F32, tm=1024, tn=w_aux.shape[1])
    ckv_n, ik_lo, ik_hi = _kvnorm(aux, ckv_norm_g[0], idx_k_norm_g[0])

    tq = min(256, L)
    o_a = _dsa(g1, aux, ik_lo, ik_hi, ckv_n, w_uk[0].astype(BF16), w_uv[0].astype(BF16),
               _bias_tables(rel_bias, tq), attn_out_norm_g[0], bsz, L, tq)
    o_b = _hgrn(g1, fb, lb_logits, hgrn_out_norm_g[0], bsz, L)

    x1, h2p, logits_t = _out(o_a, o_b, x.reshape(t, d), w_out[0].astype(BF16), mod3, norm2_g[0],
                             w_router[0].T.astype(BF16), L)

    eidx, ew, rank, cnt = _route(logits_t, router_bias[0])

    bm = 256
    counts = cnt[:, 0].astype(I32)
    padded = (counts + bm - 1) // bm * bm
    pad_end = jnp.cumsum(padded)
    pad_start = pad_end - padded
    n_rows = (t * TOP_K + N_EXPERTS * (bm - 1) + bm - 1) // bm * bm
    nb = n_rows // bm
    nvb = pad_end[-1] // bm
    blk = jnp.minimum(jnp.arange(nb, dtype=I32), nvb - 1)
    eid = jnp.minimum(jnp.sum((pad_end[None, :] <= (blk * bm)[:, None]).astype(I32), axis=1), N_EXPERTS - 1)
    ar = jnp.arange(nb, dtype=I32)
    first = ((ar < nvb) & ((ar == 0) | (eid != jnp.roll(eid, 1)))).astype(I32)
    slot = (jnp.cumsum(first) - 1) % 2
    nxt_blk = pad_end[eid] // bm
    more = (nxt_blk < nvb).astype(I32)
    nxt = eid[jnp.minimum(nxt_blk, nb - 1)]

    dest = _dest(eidx, rank, pad_start)
    dest_flat = dest.T.reshape(t * TOP_K)
    xs = _dispatch(pad_end.astype(I32), padded.astype(I32), dest_flat, h2p, n_rows, bm)

    y = _experts(blk, eid, first, slot.astype(I32), nxt.astype(I32), more, nvb.reshape(1).astype(I32), xs,
                 w_e_gate[0], w_e_up[0], w_e_down[0], bm)

    out = _combine(dest_flat, y, h2p, x1, ew.T, w_s_gate[0].astype(BF16), w_s_up[0].astype(BF16),
                   w_s_down[0].astype(BF16), mod3, final_norm_g, L)
    return out.reshape(bsz, L, d)
```

```python
import functools
import math

import numpy as np
import jax
import jax.numpy as jnp
from jax import lax
from jax.experimental import pallas as pl
from jax.experimental.pallas import tpu as pltpu

F32 = jnp.float32
BF16 = jnp.bfloat16
I32 = jnp.int32
U32 = jnp.uint32

EPS = 1e-6
A_HEADS = 8
A_HEAD_DIM = 128
A_KV_RANK = 256
IDX_HEADS = 16
IDX_DIM = 64
IDX_TOPK_MAX = 256
B_HEADS = 8
B_DIM = 128
REL_BUCKETS = 32
REL_MAX_DIST = 128
N_EXPERTS = 64
TOP_K = 8
N_GROUPS = 8
TOPK_GROUPS = 4
ROUTED_SCALE = 2.5

VMEM_LIMIT_BYTES = 56 * 1024 * 1024
LANES = 128
SUBLANES = 8

NT_DIMS = (((1,), (1,)), ((), ()))
TN_DIMS = (((0,), (0,)), ((), ()))

LOG2E = math.log2(math.e)
INT_MIN = -2 ** 31
KEY_NEG_INF = -2139095041


def _cparams(sem):
    return pltpu.CompilerParams(dimension_semantics=sem, vmem_limit_bytes=VMEM_LIMIT_BYTES)


def _silu(v):
    return v * jax.nn.sigmoid(v)


def _pack_halves(v):
    n = v.shape[1] // 2
    lo = lax.bitcast_convert_type(v[:, :n].astype(BF16).astype(F32), U32)
    hi = lax.bitcast_convert_type(v[:, n:].astype(BF16).astype(F32), U32)
    return lax.shift_right_logical(lo, jnp.uint32(16)) | (hi & jnp.uint32(0xFFFF0000))


def _unpack_halves(w):
    left = lax.bitcast_convert_type(lax.shift_left(w, jnp.uint32(16)), F32)
    right = lax.bitcast_convert_type(w & jnp.uint32(0xFFFF0000), F32)
    return left, right


def _ada_kernel(c_ref, w_ref, b_ref, o_ref):
    a = _silu(c_ref[...]).astype(BF16)
    o_ref[...] = jnp.dot(a, w_ref[...].astype(BF16), preferred_element_type=F32) + b_ref[...]


def _ada(c, w, b, tn=1024):
    bsz, d = c.shape
    n = w.shape[1]
    return pl.pallas_call(
        _ada_kernel,
        grid=(n // tn,),
        in_specs=[pl.BlockSpec((bsz, d), lambda j: (0, 0)),
                  pl.BlockSpec((d, tn), lambda j: (0, j)),
                  pl.BlockSpec((1, tn), lambda j: (0, j))],
        out_specs=pl.BlockSpec((bsz, tn), lambda j: (0, j)),
        out_shape=jax.ShapeDtypeStruct((bsz, n), F32),
        compiler_params=_cparams(("arbitrary",)),
    )(c, w, b.reshape(1, n))


def _inproj_kernel(x_ref, mod_ref, g_ref, w_ref, main_ref, f_ref, aux_ref, h_ref, *, n_main, n_f):
    j = pl.program_id(1)

    @pl.when(j == 0)
    def _():
        x = x_ref[...]
        y = x * lax.rsqrt(jnp.mean(x * x, axis=-1, keepdims=True) + EPS) * g_ref[...]
        h_ref[...] = (y * (1.0 + mod_ref[0, 1:2, :]) + mod_ref[0, 0:1, :]).astype(BF16)

    res = jnp.dot(h_ref[...], w_ref[...], preferred_element_type=F32)

    @pl.when(j < n_main)
    def _():
        main_ref[...] = res.astype(main_ref.dtype)

    @pl.when((j >= n_main) & (j < n_main + n_f))
    def _():
        f_ref[...] = res

    @pl.when(j == n_main + n_f)
    def _():
        aux_ref[...] = res[:, :aux_ref.shape[1]]


def _inproj(x2d, mod3, g, w_all, n_main_cols, n_f_cols, n_aux_cols, L, tm=1024, tn=512):
    t, d = x2d.shape
    n_main = n_main_cols // tn
    n_f = n_f_cols // tn
    assert w_all.shape[1] == (n_main + n_f + 1) * tn and n_aux_cols <= tn
    kern = functools.partial(_inproj_kernel, n_main=n_main, n_f=n_f)
    return pl.pallas_call(
        kern,
        grid=(t // tm, n_main + n_f + 1),
        in_specs=[pl.BlockSpec((tm, d), lambda i, j: (i, 0)),
                  pl.BlockSpec((1, 6, d), lambda i, j: (i * tm // L, 0, 0)),
                  pl.BlockSpec((1, d), lambda i, j: (0, 0)),
                  pl.BlockSpec((d, tn), lambda i, j: (0, j))],
        out_specs=[pl.BlockSpec((tm, tn), lambda i, j: (i, jnp.minimum(j, n_main - 1))),
                   pl.BlockSpec((tm, tn), lambda i, j: (i, jnp.clip(j - n_main, 0, n_f - 1))),
                   pl.BlockSpec((tm, n_aux_cols), lambda i, j: (i, 0))],
        out_shape=[jax.ShapeDtypeStruct((t, n_main_cols), BF16),
                   jax.ShapeDtypeStruct((t, n_f_cols), F32),
                   jax.ShapeDtypeStruct((t, n_aux_cols), F32)],
        scratch_shapes=[pltpu.VMEM((tm, d), BF16)],
        compiler_params=_cparams(("parallel", "arbitrary")),
    )(x2d, mod3, g.reshape(1, d), w_all)


def _kvnorm_kernel(aux_ref, gc_ref, gk_ref, ckv_ref, iklo_ref, ikhi_ref):
    ckv = aux_ref[:, :A_KV_RANK]
    ckv_ref[...] = (ckv * lax.rsqrt(jnp.mean(ckv * ckv, axis=-1, keepdims=True) + EPS)
                    * gc_ref[...]).astype(BF16)
    v = aux_ref[:, A_KV_RANK:A_KV_RANK + LANES]
    lane = lax.broadcasted_iota(I32, v.shape, 1)
    ik = jnp.where(lane < IDX_DIM, v, 0.0)
    ms = jnp.sum(ik * ik, axis=-1, keepdims=True) * (1.0 / IDX_DIM)
    ikn = ik * lax.rsqrt(ms + EPS) * gk_ref[...]
    iklo_ref[...] = ikn.astype(BF16)
    ikhi_ref[...] = pltpu.roll(ikn, IDX_DIM, 1).astype(BF16)


def _kvnorm(aux, gc, gk, tm=1024):
    t = aux.shape[0]
    gk_pad = jnp.concatenate([gk, jnp.zeros((LANES - IDX_DIM,), F32)]).reshape(1, LANES)
    return pl.pallas_call(
        _kvnorm_kernel,
        grid=(t // tm,),
        in_specs=[pl.BlockSpec((tm, aux.shape[1]), lambda i: (i, 0)),
                  pl.BlockSpec((1, A_KV_RANK), lambda i: (0, 0)),
                  pl.BlockSpec((1, LANES), lambda i: (0, 0))],
        out_specs=[pl.BlockSpec((tm, A_KV_RANK), lambda i: (i, 0)),
                   pl.BlockSpec((tm, LANES), lambda i: (i, 0)),
                   pl.BlockSpec((tm, LANES), lambda i: (i, 0))],
        out_shape=[jax.ShapeDtypeStruct((t, A_KV_RANK), BF16),
                   jax.ShapeDtypeStruct((t, LANES), BF16),
                   jax.ShapeDtypeStruct((t, LANES), BF16)],
        compiler_params=_cparams(("parallel",)),
    )(aux, gc.reshape(1, A_KV_RANK), gk_pad)


def _t5_bucket(rel):
    n = jnp.maximum(rel, 0)
    max_exact = REL_BUCKETS // 2
    n_large = jnp.maximum(n, max_exact).astype(F32)
    large = max_exact + (jnp.log(n_large / max_exact) / math.log(REL_MAX_DIST / max_exact)
                         * (REL_BUCKETS - max_exact)).astype(I32)
    large = jnp.minimum(large, REL_BUCKETS - 1)
    return jnp.where(n < max_exact, n, large)


def _bias_tables(rel_bias, tq):
    assert tq + 1 >= REL_MAX_DIST
    nh = rel_bias.shape[1]
    dist = jnp.maximum(jnp.arange(3 * tq + 1, dtype=I32) - tq, 0)
    v = rel_bias.astype(F32)[_t5_bucket(dist)].T * LOG2E
    n = v.shape[1]
    x = jnp.broadcast_to(v[:, None, :], (nh, tq, n)).reshape(nh, tq * n)[:, :tq * (n - 1)].reshape(nh, tq, n - 1)
    near = x[:, :, tq:2 * tq]
    prev = x[:, :, 2 * tq:3 * tq]
    far = jnp.broadcast_to(v[:, n - 1][:, None, None], near.shape)
    return jnp.stack([near, prev, far])


def _dsa_kernel(qa_ref, iq_ref, aux_ref, iklo_ref, ikhi_ref, ckv_ref, ckvt_ref, wuk_ref, wuvt_ref, bias_ref,
                g_ref, o_ref, iqt_ref, iwt_ref, key_ref, qlt_ref, m_ref, l_ref, acc_ref, tie_ref, madd_ref,
                *, tq, topk):
    i = pl.program_id(1)
    nh = A_HEADS
    npair = IDX_HEADS // 2

    r_i = lax.broadcasted_iota(I32, (LANES, LANES), 0)
    c_i = lax.broadcasted_iota(I32, (LANES, LANES), 1)
    eye = jnp.where(r_i == c_i, 1.0, 0.0).astype(BF16)
    for p in range(npair):
        iqt_ref[:, p * tq:(p + 1) * tq] = lax.dot_general(
            eye, iq_ref[:, p * LANES:(p + 1) * LANES], NT_DIMS, preferred_element_type=F32).astype(BF16)
    iwt_ref[...] = (jnp.transpose(aux_ref[...])[IDX_DIM:IDX_DIM + IDX_HEADS, :]
                    * (IDX_HEADS ** -0.5 * IDX_DIM ** -0.5))
    for h in range(nh):
        ql = lax.dot_general(wuk_ref[h], qa_ref[:, h * A_HEAD_DIM:(h + 1) * A_HEAD_DIM], NT_DIMS,
                             preferred_element_type=F32)
        qlt_ref[:, h * tq:(h + 1) * tq] = (ql * (A_HEAD_DIM ** -0.5 * LOG2E)).astype(BF16)

    kpos = lax.broadcasted_iota(I32, (tq, tq), 0)
    qpos = lax.broadcasted_iota(I32, (tq, tq), 1) + i * tq

    def score_body(kc, carry):
        off = pl.multiple_of(kc * tq, tq)
        klo = iklo_ref[0, pl.ds(off, tq), :]
        khi = ikhi_ref[0, pl.ds(off, tq), :]
        acc = jnp.zeros((tq, tq), F32)
        for p in range(npair):
            rhs = iqt_ref[:, p * tq:(p + 1) * tq]
            se = jnp.dot(klo, rhs, preferred_element_type=F32)
            so = jnp.dot(khi, rhs, preferred_element_type=F32)
            acc = acc + jnp.maximum(se, 0.0) * iwt_ref[2 * p:2 * p + 1, :]
            acc = acc + jnp.maximum(so, 0.0) * iwt_ref[2 * p + 1:2 * p + 2, :]
        bits = lax.bitcast_convert_type(acc, I32)
        key = jnp.where(bits >= 0, bits, bits ^ jnp.int32(0x7FFFFFFF))
        key_ref[kc] = jnp.where(kpos + off <= qpos, key, jnp.int32(KEY_NEG_INF))
        return carry

    lax.fori_loop(0, i + 1, score_body, 0)

    def count_ge(cand):
        def body(kc, c):
            hit = jnp.where(key_ref[kc] >= cand, 1.0, 0.0)
            return c + jnp.sum(hit.reshape(tq // 8, 8, tq), axis=0)
        c = lax.fori_loop(0, i + 1, body, jnp.zeros((8, tq), F32))
        return jnp.sum(c, axis=0, keepdims=True)

    kf = float(topk)
    thr = jnp.where(count_ge(jnp.zeros((1, tq), I32)) >= kf, jnp.int32(0), jnp.int32(INT_MIN))

    def bit_body(j, thr):
        cand = thr | lax.shift_left(jnp.int32(1), 30 - j)
        return jnp.where(count_ge(cand) >= kf, cand, thr)

    thr = lax.fori_loop(0, 31, bit_body, thr)

    def count_gt_eq():
        def body(kc, c):
            key = key_ref[kc]
            gt = jnp.where(key > thr, 1.0, 0.0)
            eq = jnp.where(key == thr, 1.0, 0.0)
            return (c[0] + jnp.sum(gt.reshape(tq // 8, 8, tq), axis=0),
                    c[1] + jnp.sum(eq.reshape(tq // 8, 8, tq), axis=0))
        z = jnp.zeros((8, tq), F32)
        c = lax.fori_loop(0, i + 1, body, (z, z))
        return jnp.sum(c[0], axis=0, keepdims=True), jnp.sum(c[1], axis=0, keepdims=True)

    n_gt, n_eq = count_gt_eq()
    need = kf - n_gt
    tied = (n_gt + n_eq > kf) & (thr > jnp.int32(KEY_NEG_INF))
    has_tie = jnp.max(jnp.where(tied, 1.0, 0.0)) > 0.0
    tie_ref[...] = jnp.zeros(tie_ref.shape, F32)

    m_ref[...] = jnp.full(m_ref.shape, -jnp.inf, F32)
    l_ref[...] = jnp.zeros(l_ref.shape, F32)
    acc_ref[...] = jnp.zeros(acc_ref.shape, F32)

    def att_body(kc, carry):
        off = pl.multiple_of(kc * tq, tq)
        ckv = ckv_ref[0, pl.ds(off, tq), :]
        ckvt = ckvt_ref[0, kc]
        key = key_ref[kc]
        causal = key > jnp.int32(KEY_NEG_INF)

        @pl.when(jnp.logical_not(has_tie))
        def _():
            madd_ref[...] = jnp.where((key >= thr) & causal, 0.0, -jnp.inf)

        @pl.when(has_tie)
        def _():
            eq = key == thr
            eqf = jnp.where(eq, 1.0, 0.0)
            before = (lax.broadcasted_iota(I32, (tq, tq), 1) < lax.broadcasted_iota(I32, (tq, tq), 0))
            rank = jnp.dot(jnp.where(before, 1.0, 0.0).astype(BF16), eqf.astype(BF16),
                           preferred_element_type=F32) + tie_ref[...]
            keep = (key > thr) | (eq & (rank < need))
            madd_ref[...] = jnp.where(keep & causal, 0.0, -jnp.inf)
            tie_ref[...] = tie_ref[...] + jnp.sum(eqf, axis=0, keepdims=True)

        madd = madd_ref[...]
        d = jnp.minimum(i - kc, 2)
        for h in range(nh):
            s = jnp.dot(ckv, qlt_ref[:, h * tq:(h + 1) * tq], preferred_element_type=F32)
            s = s + (bias_ref[d, h] + madd)
            m_old = m_ref[h:h + 1, :]
            m_new = jnp.maximum(m_old, jnp.max(s, axis=0, keepdims=True))
            m_safe = jnp.where(m_new == -jnp.inf, 0.0, m_new)
            alpha = jnp.exp2(m_old - m_safe)
            p = jnp.exp2(s - m_safe)
            l_ref[h:h + 1, :] = alpha * l_ref[h:h + 1, :] + jnp.sum(p, axis=0, keepdims=True)
            acc_ref[h] = alpha * acc_ref[h] + jnp.dot(ckvt, p.astype(BF16), preferred_element_type=F32)
            m_ref[h:h + 1, :] = m_new
        return carry

    lax.fori_loop(0, i + 1, att_body, 0)

    outs = []
    for h in range(nh):
        o_lat = (acc_ref[h] / l_ref[h:h + 1, :]).astype(BF16)
        outs.append(jnp.transpose(jnp.dot(wuvt_ref[h], o_lat, preferred_element_type=F32)))
    o = jnp.concatenate(outs, axis=1)
    o = o * lax.rsqrt(jnp.mean(o * o, axis=-1, keepdims=True) + EPS) * g_ref[...]
    o_ref[...] = o.astype(o_ref.dtype)


def _dsa(g1, aux, ik_lo, ik_hi, ckv_n, w_uk, w_uv, bias_tab, g, bsz, L, tq):
    t = bsz * L
    nq = L // tq
    topk = min(IDX_TOPK_MAX, L // 4)
    aux_blk = A_KV_RANK // LANES
    kern = functools.partial(_dsa_kernel, tq=tq, topk=topk)
    width = A_HEADS * A_HEAD_DIM
    ckv3 = ckv_n.reshape(bsz, L, A_KV_RANK)
    ckvt = ckv_n.reshape(bsz, nq, tq, A_KV_RANK).transpose(0, 1, 3, 2)
    return pl.pallas_call(
        kern,
        grid=(bsz, nq),
        in_specs=[pl.BlockSpec((tq, width), lambda b, i: (b * nq + i, 0)),
                  pl.BlockSpec((tq, IDX_HEADS * IDX_DIM), lambda b, i: (b * nq + i, 1)),
                  pl.BlockSpec((tq, LANES), lambda b, i: (b * nq + i, aux_blk)),
                  pl.BlockSpec((1, L, LANES), lambda b, i: (b, 0, 0)),
                  pl.BlockSpec((1, L, LANES), lambda b, i: (b, 0, 0)),
                  pl.BlockSpec((1, L, A_KV_RANK), lambda b, i: (b, 0, 0)),
                  pl.BlockSpec((1, nq, A_KV_RANK, tq), lambda b, i: (b, 0, 0, 0)),
                  pl.BlockSpec((A_HEADS, A_KV_RANK, A_HEAD_DIM), lambda b, i: (0, 0, 0)),
                  pl.BlockSpec((A_HEADS, A_HEAD_DIM, A_KV_RANK), lambda b, i: (0, 0, 0)),
                  pl.BlockSpec((3, A_HEADS, tq, tq), lambda b, i: (0, 0, 0, 0)),
                  pl.BlockSpec((1, width), lambda b, i: (0, 0))],
        out_specs=pl.BlockSpec((tq, width), lambda b, i: (b * nq + i, 0)),
        out_shape=jax.ShapeDtypeStruct((t, width), BF16),
        scratch_shapes=[pltpu.VMEM((LANES, IDX_HEADS // 2 * tq), BF16),
                        pltpu.VMEM((IDX_HEADS, tq), F32),
                        pltpu.VMEM((nq, tq, tq), I32),
                        pltpu.VMEM((A_KV_RANK, A_HEADS * tq), BF16),
                        pltpu.VMEM((A_HEADS, tq), F32),
                        pltpu.VMEM((A_HEADS, tq), F32),
                        pltpu.VMEM((A_HEADS, A_KV_RANK, tq), F32),
                        pltpu.VMEM((1, tq), F32),
                        pltpu.VMEM((tq, tq), F32)],
        compiler_params=_cparams(("parallel", "arbitrary")),
    )(g1, g1, aux, ik_lo.reshape(bsz, L, LANES), ik_hi.reshape(bsz, L, LANES),
      ckv3, ckvt, w_uk, jnp.transpose(w_uv, (0, 2, 1)), bias_tab, g.reshape(1, width))


def _hgrn_kernel(q_ref, i_ref, gate_ref, f_ref, lbl_ref, ng_ref, o_ref, st_ref, *, chunk, rblk):
    @pl.when(pl.program_id(1) == 0)
    def _():
        st_ref[...] = jnp.zeros(st_ref.shape, F32)

    ll = lbl_ref[...]
    ex = jnp.exp(ll - jnp.max(ll, axis=0, keepdims=True))
    lb_all = ex[0:1] / jnp.sum(ex, axis=0, keepdims=True)

    r_i = lax.broadcasted_iota(I32, (chunk, chunk), 0)
    c_i = lax.broadcasted_iota(I32, (chunk, chunk), 1)
    tri = jnp.where(r_i >= c_i, 1.0, 0.0).astype(BF16)
    row_k = lax.broadcasted_iota(I32, (chunk, B_DIM), 0)

    for h in range(B_HEADS):
        sl = slice(h * B_DIM, (h + 1) * B_DIM)
        lb = lb_all[:, sl]
        f = lb + (1.0 - lb) * jax.nn.sigmoid(f_ref[:, sl])
        lf = jnp.log(f)
        kk = 1.0 - f
        l1 = lf.astype(BF16)
        r1 = lf - l1.astype(F32)
        l2 = r1.astype(BF16)
        l3 = (r1 - l2.astype(F32)).astype(BF16)
        cs = jnp.dot(tri, jnp.concatenate([l1, l2, l3], axis=1), preferred_element_type=F32)
        bc = cs[:, :B_DIM] + cs[:, B_DIM:2 * B_DIM] + cs[:, 2 * B_DIM:]

        q = q_ref[:, sl].astype(F32)
        v = i_ref[:, sl]
        st = st_ref[h]
        o = lax.dot_general((q * jnp.exp(bc)).astype(BF16), st.astype(BF16), NT_DIMS,
                            preferred_element_type=F32)
        parts = []
        for r in range(chunk // rblk):
            lo, hi = r * rblk, (r + 1) * rblk
            base = bc[lo - 1:lo] if r > 0 else jnp.zeros((1, B_DIM), F32)
            qt = (q[lo:hi] * jnp.exp(bc[lo:hi] - base)).astype(BF16)
            kt = (kk * jnp.exp(jnp.where(row_k < hi, base - bc, 0.0))).astype(BF16)
            parts.append(lax.dot_general(qt, kt, NT_DIMS, preferred_element_type=F32))
        sc = jnp.where(c_i <= r_i, jnp.concatenate(parts, axis=0), 0.0).astype(BF16)
        o = o + jnp.dot(sc, v, preferred_element_type=F32)

        last = bc[chunk - 1:chunk]
        kd = (kk * jnp.exp(last - bc)).astype(BF16)
        st_ref[h] = st * jnp.exp(last) + lax.dot_general(v, kd, TN_DIMS, preferred_element_type=F32)

        y = o * lax.rsqrt(jnp.mean(o * o, axis=-1, keepdims=True) + EPS) * ng_ref[:, sl]
        o_ref[:, sl] = (y * _silu(gate_ref[:, sl].astype(F32))).astype(o_ref.dtype)


def _hgrn(g1, fb, lb_logits, ng, bsz, L, chunk=128, rblk=16):
    t = bsz * L
    nc = L // chunk
    width = B_HEADS * B_DIM
    kern = functools.partial(_hgrn_kernel, chunk=chunk, rblk=rblk)
    return pl.pallas_call(
        kern,
        grid=(bsz, nc),
        in_specs=[pl.BlockSpec((chunk, width), lambda b, c: (b * nc + c, 2)),
                  pl.BlockSpec((chunk, width), lambda b, c: (b * nc + c, 3)),
                  pl.BlockSpec((chunk, width), lambda b, c: (b * nc + c, 4)),
                  pl.BlockSpec((chunk, width), lambda b, c: (b * nc + c, 0)),
                  pl.BlockSpec(lb_logits.shape, lambda b, c: (0, 0)),
                  pl.BlockSpec((1, width), lambda b, c: (0, 0))],
        out_specs=pl.BlockSpec((chunk, width), lambda b, c: (b * nc + c, 0)),
        out_shape=jax.ShapeDtypeStruct((t, width), BF16),
        scratch_shapes=[pltpu.VMEM((B_HEADS, B_DIM, B_DIM), F32)],
        compiler_params=_cparams(("parallel", "arbitrary")),
    )(g1, g1, g1, fb, lb_logits, ng.reshape(1, width))


def _out_kernel(oa_ref, ob_ref, x_ref, wa_ref, wb_ref, mod_ref, g_ref, wr_ref, x1_ref, h2_ref, lg_ref):
    mix = jnp.dot(oa_ref[...], wa_ref[...], preferred_element_type=F32)
    mix = mix + jnp.dot(ob_ref[...], wb_ref[...], preferred_element_type=F32)
    x1 = x_ref[...] + mod_ref[0, 2:3, :] * mix
    x1_ref[...] = x1
    y = x1 * lax.rsqrt(jnp.mean(x1 * x1, axis=-1, keepdims=True) + EPS) * g_ref[...]
    h2 = y * (1.0 + mod_ref[0, 4:5, :]) + mod_ref[0, 3:4, :]
    h2_ref[...] = _pack_halves(h2)
    lg_ref[...] = lax.dot_general(wr_ref[...], h2.astype(BF16), NT_DIMS, preferred_element_type=F32)


def _out(oa, ob, x2d, w_out, mod3, g, w_router_t, L, tm=256):
    t, d = x2d.shape
    half = oa.shape[1]
    ne = w_router_t.shape[0]
    return pl.pallas_call(
        _out_kernel,
        grid=(t // tm,),
        in_specs=[pl.BlockSpec((tm, half), lambda i: (i, 0)),
                  pl.BlockSpec((tm, half), lambda i: (i, 0)),
                  pl.BlockSpec((tm, d), lambda i: (i, 0)),
                  pl.BlockSpec((half, d), lambda i: (0, 0)),
                  pl.BlockSpec((half, d), lambda i: (1, 0)),
                  pl.BlockSpec((1, 6, d), lambda i: (i * tm // L, 0, 0)),
                  pl.BlockSpec((1, d), lambda i: (0, 0)),
                  pl.BlockSpec((ne, d), lambda i: (0, 0))],
        out_specs=[pl.BlockSpec((tm, d), lambda i: (i, 0)),
                   pl.BlockSpec((tm, d // 2), lambda i: (i, 0)),
                   pl.BlockSpec((ne, tm), lambda i: (0, i))],
        out_shape=[jax.ShapeDtypeStruct((t, d), F32),
                   jax.ShapeDtypeStruct((t, d // 2), U32),
                   jax.ShapeDtypeStruct((ne, t), F32)],
        compiler_params=_cparams(("parallel",)),
    )(oa, ob, x2d, w_out, w_out, mod3, g.reshape(1, d), w_router_t)


def _rows_to_tile(rows, nrow):
    n = rows[0].shape[1]
    ridx = lax.broadcasted_iota(I32, (nrow, n), 0)
    out = jnp.zeros((nrow, n), rows[0].dtype)
    for r, v in enumerate(rows):
        out = jnp.where(ridx == r, jnp.broadcast_to(v, (nrow, n)), out)
    return out


def _route_kernel(lg_ref, rb_ref, eidx_ref, ew_ref, rank_ref, cnt_ref, run_ref):
    @pl.when(pl.program_id(0) == 0)
    def _():
        run_ref[...] = jnp.zeros(run_ref.shape, F32)

    ne, tt = lg_ref.shape
    per = ne // N_GROUPS
    sc = jax.nn.sigmoid(lg_ref[...])
    ch = sc + rb_ref[...]
    neg = -jnp.inf

    sub = lax.broadcasted_iota(I32, (per, tt), 0).astype(F32)
    gsc = []
    for g in range(N_GROUPS):
        cg = ch[g * per:(g + 1) * per]
        m1 = jnp.max(cg, axis=0, keepdims=True)
        first = jnp.min(jnp.where(cg == m1, sub, float(per)), axis=0, keepdims=True)
        m2 = jnp.max(jnp.where(sub == first, neg, cg), axis=0, keepdims=True)
        gsc.append(m1 + m2)
    grp = _rows_to_tile(gsc, N_GROUPS)

    gid = lax.broadcasted_iota(I32, (N_GROUPS, tt), 0).astype(F32)
    gsel = jnp.zeros((N_GROUPS, tt), F32)
    for _ in range(TOPK_GROUPS):
        mx = jnp.max(grp, axis=0, keepdims=True)
        gi = jnp.min(jnp.where(grp == mx, gid, float(N_GROUPS)), axis=0, keepdims=True)
        pick = gid == gi
        gsel = jnp.where(pick, 1.0, gsel)
        grp = jnp.where(pick, neg, grp)

    eid = lax.broadcasted_iota(I32, (ne, tt), 0).astype(F32)
    cm = jnp.full((ne, tt), neg, F32)
    for g in range(N_GROUPS):
        in_g = (eid >= float(g * per)) & (eid < float((g + 1) * per))
        cm = jnp.where(in_g & (jnp.broadcast_to(gsel[g:g + 1], (ne, tt)) > 0.5), ch, cm)

    idx_rows, w_rows = [], []
    onehot = jnp.zeros((ne, tt), F32)
    for _ in range(TOP_K):
        mx = jnp.max(cm, axis=0, keepdims=True)
        ei = jnp.min(jnp.where(cm == mx, eid, float(ne)), axis=0, keepdims=True)
        pick = eid == ei
        idx_rows.append(ei)
        w_rows.append(jnp.sum(jnp.where(pick, sc, 0.0), axis=0, keepdims=True))
        onehot = jnp.where(pick, 1.0, onehot)
        cm = jnp.where(pick, neg, cm)
    wsum = w_rows[0]
    for w in w_rows[1:]:
        wsum = wsum + w
    w_rows = [w / wsum * ROUTED_SCALE for w in w_rows]

    a_i = lax.broadcasted_iota(I32, (tt, tt), 0)
    b_i = lax.broadcasted_iota(I32, (tt, tt), 1)
    upper = jnp.where(a_i < b_i, 1.0, 0.0).astype(BF16)
    rank_full = jnp.dot(onehot.astype(BF16), upper, preferred_element_type=F32) + run_ref[...]
    r_rows = [jnp.sum(jnp.where(eid == ei, rank_full, 0.0), axis=0, keepdims=True) for ei in idx_rows]
    run = run_ref[...] + jnp.sum(onehot, axis=1, keepdims=True)
    run_ref[...] = run

    eidx_ref[...] = _rows_to_tile(idx_rows, TOP_K).astype(I32)
    ew_ref[...] = _rows_to_tile(w_rows, TOP_K)
    rank_ref[...] = _rows_to_tile(r_rows, TOP_K).astype(I32)
    cnt_ref[...] = jnp.broadcast_to(run, cnt_ref.shape)


def _route(logits_t, router_bias, tt=512):
    ne, t = logits_t.shape
    return pl.pallas_call(
        _route_kernel,
        grid=(t // tt,),
        in_specs=[pl.BlockSpec((ne, tt), lambda i: (0, i)),
                  pl.BlockSpec((ne, 1), lambda i: (0, 0))],
        out_specs=[pl.BlockSpec((TOP_K, tt), lambda i: (0, i)),
                   pl.BlockSpec((TOP_K, tt), lambda i: (0, i)),
                   pl.BlockSpec((TOP_K, tt), lambda i: (0, i)),
                   pl.BlockSpec((ne, LANES), lambda i: (0, 0))],
        out_shape=[jax.ShapeDtypeStruct((TOP_K, t), I32),
                   jax.ShapeDtypeStruct((TOP_K, t), F32),
                   jax.ShapeDtypeStruct((TOP_K, t), I32),
                   jax.ShapeDtypeStruct((ne, LANES), F32)],
        scratch_shapes=[pltpu.VMEM((ne, 1), F32)],
        compiler_params=_cparams(("arbitrary",)),
    )(logits_t, router_bias.reshape(ne, 1))


def _dest_kernel(eidx_ref, rank_ref, ps_ref, o_ref):
    ne = ps_ref.shape[0]
    tt = eidx_ref.shape[1]
    eid = lax.broadcasted_iota(I32, (ne, tt), 0)
    ps = jnp.broadcast_to(ps_ref[...], (ne, tt))
    rows = []
    for k in range(TOP_K):
        start = jnp.sum(jnp.where(eid == eidx_ref[k:k + 1, :], ps, 0.0), axis=0, keepdims=True)
        rows.append(start + rank_ref[k:k + 1, :].astype(F32))
    o_ref[...] = _rows_to_tile(rows, TOP_K).astype(I32)


def _dest(eidx, rank, pad_start, tt=2048):
    t = eidx.shape[1]
    tt = min(tt, t)
    ne = pad_start.shape[0]
    return pl.pallas_call(
        _dest_kernel,
        grid=(t // tt,),
        in_specs=[pl.BlockSpec((TOP_K, tt), lambda i: (0, i)),
                  pl.BlockSpec((TOP_K, tt), lambda i: (0, i)),
                  pl.BlockSpec((ne, 1), lambda i: (0, 0))],
        out_specs=pl.BlockSpec((TOP_K, tt), lambda i: (0, i)),
        out_shape=jax.ShapeDtypeStruct((TOP_K, t), I32),
        compiler_params=_cparams(("parallel",)),
    )(eidx, rank, pad_start.astype(F32).reshape(ne, 1))


def _dispatch_kernel(pend_ref, padded_ref, dest_ref, h_ref, xs_ref, zbuf_ref, zsem, sem, *, td, bm):
    i = pl.program_id(0)

    def tail_copy(e):
        start = pl.multiple_of(pend_ref[e] - bm, bm)
        return pltpu.make_async_copy(zbuf_ref, xs_ref.at[pl.ds(start, bm)], zsem)

    @pl.when(i == 0)
    def _():
        zbuf_ref[...] = jnp.zeros(zbuf_ref.shape, zbuf_ref.dtype)

        def start_body(e, c):
            @pl.when(padded_ref[e] > 0)
            def _():
                tail_copy(e).start()
            return c

        def wait_body(e, c):
            @pl.when(padded_ref[e] > 0)
            def _():
                tail_copy(e).wait()
            return c

        lax.fori_loop(0, N_EXPERTS, start_body, 0)
        lax.fori_loop(0, N_EXPERTS, wait_body, 0)

        def unused_copy(b):
            return pltpu.make_async_copy(zbuf_ref, xs_ref.at[pl.ds(pl.multiple_of(b * bm, bm), bm)], zsem)

        def ustart_body(b, c):
            unused_copy(b).start()
            return c

        def uwait_body(b, c):
            unused_copy(b).wait()
            return c

        first_unused = pend_ref[N_EXPERTS - 1] // bm
        lax.fori_loop(first_unused, xs_ref.shape[0] // bm, ustart_body, 0)
        lax.fori_loop(first_unused, xs_ref.shape[0] // bm, uwait_body, 0)

    for j in range(td):
        for k in range(TOP_K):
            dst = xs_ref.at[dest_ref[j * TOP_K + k]]
            pltpu.make_async_copy(h_ref.at[j], dst, sem).start(priority=k % 2)
    for _ in range(TOP_K):
        pltpu.make_async_copy(h_ref, xs_ref.at[pl.ds(0, td)], sem).wait()


def _dispatch(pad_end, padded, dest_flat, h2p, n_rows, bm, td=256):
    t, w = h2p.shape
    kern = functools.partial(_dispatch_kernel, td=td, bm=bm)
    return pl.pallas_call(
        kern,
        grid_spec=pltpu.PrefetchScalarGridSpec(
            num_scalar_prefetch=2,
            grid=(t // td,),
            in_specs=[pl.BlockSpec((td * TOP_K,), lambda i, *_: (i,), memory_space=pltpu.SMEM),
                      pl.BlockSpec((td, w), lambda i, *_: (i, 0))],
            out_specs=pl.BlockSpec(memory_space=pl.ANY),
            scratch_shapes=[pltpu.VMEM((bm, w), U32),
                            pltpu.SemaphoreType.DMA(()),
                            pltpu.SemaphoreType.DMA(())]),
        out_shape=jax.ShapeDtypeStruct((n_rows, w), U32),
        compiler_params=_cparams(("arbitrary",)),
    )(pad_end, padded, dest_flat, h2p)


def _ffn(xw, wg_ref, wu_ref, wd_ref):
    half = xw.shape[1]
    left, right = _unpack_halves(xw)
    left = left.astype(BF16)
    right = right.astype(BF16)

    def proj(w_ref):
        return (jnp.dot(left, w_ref[:half, :], preferred_element_type=F32)
                + jnp.dot(right, w_ref[half:, :], preferred_element_type=F32))

    act = (_silu(proj(wg_ref)) * proj(wu_ref)).astype(BF16)
    return jnp.dot(act, wd_ref[...], preferred_element_type=F32)


def _expert_kernel(blk_ref, eid_ref, first_ref, slot_ref, nxt_ref, more_ref, nvb_ref,
                   x_ref, wg_hbm, wu_hbm, wd_hbm, o_ref,
                   wg_f, wu_f, wd_f, wg_s, wu_s, wd_s, sems):
    i = pl.program_id(0)

    def weight_copies(e, slot):
        return (pltpu.make_async_copy(wg_hbm.at[e], wg_f.at[slot], sems.at[slot]),
                pltpu.make_async_copy(wu_hbm.at[e], wu_f.at[slot], sems.at[slot]),
                pltpu.make_async_copy(wd_hbm.at[e], wd_f.at[slot], sems.at[slot]))

    @pl.when(i == 0)
    def _():
        for cp in weight_copies(eid_ref[0], 0):
            cp.start()

    @pl.when(first_ref[i] == 1)
    def _():
        slot = slot_ref[i]
        for cp in weight_copies(eid_ref[i], slot):
            cp.wait()

        @pl.when(more_ref[i] == 1)
        def _():
            for cp in weight_copies(nxt_ref[i], 1 - slot):
                cp.start()

        wg_s[...] = wg_f[slot].astype(BF16)
        wu_s[...] = wu_f[slot].astype(BF16)
        wd_s[...] = wd_f[slot].astype(BF16)

    @pl.when(i < nvb_ref[0])
    def _():
        o_ref[...] = _pack_halves(_ffn(x_ref[...], wg_s, wu_s, wd_s))

    @pl.when(i >= nvb_ref[0])
    def _():
        o_ref[...] = jnp.zeros(o_ref.shape, o_ref.dtype)


def _experts(blk, eid, first, slot, nxt, more, nvb, xs, wg, wu, wd, bm):
    n_rows, w = xs.shape
    ne, d, f = wg.shape
    return pl.pallas_call(
        _expert_kernel,
        grid_spec=pltpu.PrefetchScalarGridSpec(
            num_scalar_prefetch=7,
            grid=(n_rows // bm,),
            in_specs=[pl.BlockSpec((bm, w), lambda i, blk, *_: (blk[i], 0)),
                      pl.BlockSpec(memory_space=pl.ANY),
                      pl.BlockSpec(memory_space=pl.ANY),
                      pl.BlockSpec(memory_space=pl.ANY)],
            out_specs=pl.BlockSpec((bm, w), lambda i, *_: (i, 0)),
            scratch_shapes=[pltpu.VMEM((2, d, f), F32), pltpu.VMEM((2, d, f), F32), pltpu.VMEM((2, f, d), F32),
                            pltpu.VMEM((d, f), BF16), pltpu.VMEM((d, f), BF16), pltpu.VMEM((f, d), BF16),
                            pltpu.SemaphoreType.DMA((2,))]),
        out_shape=jax.ShapeDtypeStruct((n_rows, w), U32),
        compiler_params=_cparams(("arbitrary",)),
    )(blk, eid, first, slot, nxt, more, nvb, xs, wg, wu, wd)


def _combine_kernel(dest_ref, dnext_ref, y_ref, h_ref, x1_ref, ew_ref, wg_ref, wu_ref, wd_ref, mod_ref, g_ref,
                    o_ref, gbuf_a, gbuf_b, sems, *, tc):
    i = pl.program_id(0)
    last = pl.num_programs(0) - 1

    def issue(idx_ref, gbuf, sem):
        for j in range(tc):
            for k in range(TOP_K):
                src = y_ref.at[idx_ref[j * TOP_K + k]]
                pltpu.make_async_copy(src, gbuf.at[k, j], sem).start(priority=k % 2)

    def wait_all(gbuf, sem):
        for k in range(TOP_K):
            pltpu.make_async_copy(y_ref.at[pl.ds(0, tc)], gbuf.at[k], sem).wait()

    def compute(gbuf):
        shared = _ffn(h_ref[...], wg_ref, wu_ref, wd_ref)
        half = h_ref.shape[1]
        ew = ew_ref[...]
        left = shared[:, :half]
        right = shared[:, half:]
        for k in range(TOP_K):
            yl, yr = _unpack_halves(gbuf[k])
            wk = ew[:, k:k + 1]
            left = left + wk * yl
            right = right + wk * yr
        x2 = x1_ref[...] + mod_ref[0, 5:6, :] * jnp.concatenate([left, right], axis=1)
        o_ref[...] = x2 * lax.rsqrt(jnp.mean(x2 * x2, axis=-1, keepdims=True) + EPS) * g_ref[...]

    @pl.when(i == 0)
    def _():
        def group_body(g, c):
            base = pl.multiple_of(g * SUBLANES, SUBLANES)
            for jj in range(SUBLANES):
                for k in range(TOP_K):
                    src = y_ref.at[dest_ref[(base + jj) * TOP_K + k]]
                    pltpu.make_async_copy(src, gbuf_a.at[k, base + jj], sems.at[0]).start(priority=k % 2)
            return c

        lax.fori_loop(0, tc // SUBLANES, group_body, 0)

    @pl.when(i % 2 == 0)
    def _():
        wait_all(gbuf_a, sems.at[0])
        issue(dnext_ref, gbuf_b, sems.at[1])
        compute(gbuf_a)

    @pl.when(i % 2 == 1)
    def _():
        wait_all(gbuf_b, sems.at[1])
        issue(dnext_ref, gbuf_a, sems.at[0])
        compute(gbuf_b)

    @pl.when((i == last) & (i % 2 == 0))
    def _():
        wait_all(gbuf_b, sems.at[1])

    @pl.when((i == last) & (i % 2 == 1))
    def _():
        wait_all(gbuf_a, sems.at[0])


def _combine(dest_flat, y, h2p, x1, ew_t, wsg, wsu, wsd, mod3, g, L, tc=256):
    t, d = x1.shape
    w = h2p.shape[1]
    nt = t // tc
    kern = functools.partial(_combine_kernel, tc=tc)
    return pl.pallas_call(
        kern,
        grid=(nt,),
        in_specs=[pl.BlockSpec((tc * TOP_K,), lambda i: (i,), memory_space=pltpu.SMEM),
                  pl.BlockSpec((tc * TOP_K,), lambda i: (jnp.minimum(i + 1, nt - 1),), memory_space=pltpu.SMEM),
                  pl.BlockSpec(memory_space=pl.ANY),
                  pl.BlockSpec((tc, w), lambda i: (i, 0)),
                  pl.BlockSpec((tc, d), lambda i: (i, 0)),
                  pl.BlockSpec((tc, TOP_K), lambda i: (i, 0)),
                  pl.BlockSpec(wsg.shape, lambda i: (0, 0)),
                  pl.BlockSpec(wsu.shape, lambda i: (0, 0)),
                  pl.BlockSpec(wsd.shape, lambda i: (0, 0)),
                  pl.BlockSpec((1, 6, d), lambda i: (i * tc // L, 0, 0)),
                  pl.BlockSpec((1, d), lambda i: (0, 0))],
        out_specs=pl.BlockSpec((tc, d), lambda i: (i, 0)),
        out_shape=jax.ShapeDtypeStruct((t, d), F32),
        scratch_shapes=[pltpu.VMEM((TOP_K, tc, w), U32),
                        pltpu.VMEM((TOP_K, tc, w), U32),
                        pltpu.SemaphoreType.DMA((2,))],
        compiler_params=_cparams(("arbitrary",)),
    )(dest_flat, dest_flat, y, h2p, x1, ew_t, wsg, wsu, wsd, mod3, g.reshape(1, d))


def _split_cols(w, sizes):
    out, off = [], 0
    for s in sizes:
        out.append(w[:, off:off + s])
        off += s
    return out


def kernel(x, c, w_ada, b_ada, norm1_g, w_in, ckv_norm_g, idx_k_norm_g, w_uk, w_uv, rel_bias, lb_logits,
           attn_out_norm_g, hgrn_out_norm_g, w_out, norm2_g, w_router, router_bias, w_e_gate, w_e_up,
           w_e_down, w_s_gate, w_s_up, w_s_down, final_norm_g):
    bsz, L, d = x.shape
    t = bsz * L
    assert w_ada.shape[0] == 1, "single-layer block"
    a_width = A_HEADS * A_HEAD_DIM
    b_width = B_HEADS * B_DIM
    sizes = (a_width, A_KV_RANK, IDX_HEADS * IDX_DIM, IDX_DIM, IDX_HEADS, b_width, b_width, b_width, b_width)
    assert w_in.shape[2] == sum(sizes)

    wq_a, wckv, wiq, wik, wiw, wq_b, wf_b, wi_b, wg_b = _split_cols(w_in[0], sizes)
    w_main = jnp.concatenate([wq_a, wiq, wq_b, wi_b, wg_b], axis=1).astype(BF16)
    w_f = wf_b.astype(BF16)
    aux_pad = LANES - IDX_DIM - IDX_HEADS
    w_aux = jnp.concatenate([wckv, wik, wiw, jnp.zeros((d, aux_pad), F32)], axis=1).astype(BF16)

    mod3 = _ada(c, w_ada[0], b_ada[0]).reshape(bsz, 6, d)
    tn = 512
    w_all = jnp.concatenate([w_main, w_f, w_aux, jnp.zeros((d, tn - w_aux.shape[1]), BF16)], axis=1)
    g1, fb, aux = _inproj(x.reshape(t, d), mod3, norm1_g[0], w_all, w_main.shape[1], w_f.shape[1],
                          w_aux.shape[1], L, tm=min(1024, L), tn=tn)
    ckv_n, ik_lo, ik_hi = _kvnorm(aux, ckv_norm_g[0], idx_k_norm_g[0])

    tq = min(256, L)
    o_a = _dsa(g1, aux, ik_lo, ik_hi, ckv_n, w_uk[0].astype(BF16), w_uv[0].astype(BF16),
               _bias_tables(rel_bias, tq), attn_out_norm_g[0], bsz, L, tq)
    o_b = _hgrn(g1, fb, lb_logits, hgrn_out_norm_g[0], bsz, L)

    x1, h2p, logits_t = _out(o_a, o_b, x.reshape(t, d), w_out[0].astype(BF16), mod3, norm2_g[0],
                             w_router[0].T.astype(BF16), L)

    eidx, ew, rank, cnt = _route(logits_t, router_bias[0])

    bm = 256
    counts = cnt[:, 0].astype(I32)
    padded = (counts + bm - 1) // bm * bm
    pad_end = jnp.cumsum(padded)
    pad_start = pad_end - padded
    n_rows = (t * TOP_K + N_EXPERTS * (bm - 1) + bm - 1) // bm * bm
    nb = n_rows // bm
    nvb = pad_end[-1] // bm
    blk = jnp.minimum(jnp.arange(nb, dtype=I32), nvb - 1)
    eid = jnp.minimum(jnp.sum((pad_end[None, :] <= (blk * bm)[:, None]).astype(I32), axis=1), N_EXPERTS - 1)
    ar = jnp.arange(nb, dtype=I32)
    first = ((ar < nvb) & ((ar == 0) | (eid != jnp.roll(eid, 1)))).astype(I32)
    slot = (jnp.cumsum(first) - 1) % 2
    nxt_blk = pad_end[eid] // bm
    more = (nxt_blk < nvb).astype(I32)
    nxt = eid[jnp.minimum(nxt_blk, nb - 1)]

    dest = _dest(eidx, rank, pad_start)
    dest_flat = dest.T.reshape(t * TOP_K)
    xs = _dispatch(pad_end.astype(I32), padded.astype(I32), dest_flat, h2p, n_rows, bm)

    y = _experts(blk, eid, first, slot.astype(I32), nxt.astype(I32), more, nvb.reshape(1).astype(I32), xs,
                 w_e_gate[0], w_e_up[0], w_e_down[0], bm)

    out = _combine(dest_flat, y, h2p, x1, ew.T, w_s_gate[0].astype(BF16), w_s_up[0].astype(BF16),
                   w_s_down[0].astype(BF16), mod3, final_norm_g, L)
    return out.reshape(bsz, L, d)
```

```python
import functools
import math

import numpy as np
import jax
import jax.numpy as jnp
from jax import lax
from jax.experimental import pallas as pl
from jax.experimental.pallas import tpu as pltpu

F32 = jnp.float32
BF16 = jnp.bfloat16
I32 = jnp.int32
U32 = jnp.uint32

EPS = 1e-6
A_HEADS = 8
A_HEAD_DIM = 128
A_KV_RANK = 256
IDX_HEADS = 16
IDX_DIM = 64
IDX_TOPK_MAX = 256
B_HEADS = 8
B_DIM = 128
REL_BUCKETS = 32
REL_MAX_DIST = 128
N_EXPERTS = 64
TOP_K = 8
N_GROUPS = 8
TOPK_GROUPS = 4
ROUTED_SCALE = 2.5

VMEM_LIMIT_BYTES = 56 * 1024 * 1024
LANES = 128
SUBLANES = 8

NT_DIMS = (((1,), (1,)), ((), ()))
TN_DIMS = (((0,), (0,)), ((), ()))

HGRN_MAX_BLOCK_DECAY = 80.0
LOG2E = math.log2(math.e)
INT_MIN = -2 ** 31
KEY_NEG_INF = -2139095041


def _cparams(sem):
    return pltpu.CompilerParams(dimension_semantics=sem, vmem_limit_bytes=VMEM_LIMIT_BYTES)


def _silu(v):
    return v * jax.nn.sigmoid(v)


def _pack_halves(v):
    n = v.shape[1] // 2
    lo = lax.bitcast_convert_type(v[:, :n].astype(BF16).astype(F32), U32)
    hi = lax.bitcast_convert_type(v[:, n:].astype(BF16).astype(F32), U32)
    return lax.shift_right_logical(lo, jnp.uint32(16)) | (hi & jnp.uint32(0xFFFF0000))


def _unpack_halves(w):
    left = lax.bitcast_convert_type(lax.shift_left(w, jnp.uint32(16)), F32)
    right = lax.bitcast_convert_type(w & jnp.uint32(0xFFFF0000), F32)
    return left, right


def _ada_kernel(c_ref, w_ref, b_ref, o_ref):
    a = _silu(c_ref[...]).astype(BF16)
    o_ref[...] = jnp.dot(a, w_ref[...].astype(BF16), preferred_element_type=F32) + b_ref[...]


def _ada(c, w, b, tn=1024):
    bsz, d = c.shape
    n = w.shape[1]
    return pl.pallas_call(
        _ada_kernel,
        grid=(n // tn,),
        in_specs=[pl.BlockSpec((bsz, d), lambda j: (0, 0)),
                  pl.BlockSpec((d, tn), lambda j: (0, j)),
                  pl.BlockSpec((1, tn), lambda j: (0, j))],
        out_specs=pl.BlockSpec((bsz, tn), lambda j: (0, j)),
        out_shape=jax.ShapeDtypeStruct((bsz, n), F32),
        compiler_params=_cparams(("arbitrary",)),
    )(c, w, b.reshape(1, n))


def _norm1_kernel(x_ref, mod_ref, g_ref, o_ref):
    x = x_ref[0]
    y = x * lax.rsqrt(jnp.mean(x * x, axis=-1, keepdims=True) + EPS) * g_ref[...]
    sh = mod_ref[0, 0:1, :]
    sc = mod_ref[0, 1:2, :]
    o_ref[0] = (y * (1.0 + sc) + sh).astype(o_ref.dtype)


def _norm1(x, mod3, g, tm=512):
    bsz, L, d = x.shape
    return pl.pallas_call(
        _norm1_kernel,
        grid=(bsz, L // tm),
        in_specs=[pl.BlockSpec((1, tm, d), lambda b, i: (b, i, 0)),
                  pl.BlockSpec((1, 6, d), lambda b, i: (b, 0, 0)),
                  pl.BlockSpec((1, d), lambda b, i: (0, 0))],
        out_specs=pl.BlockSpec((1, tm, d), lambda b, i: (b, i, 0)),
        out_shape=jax.ShapeDtypeStruct((bsz, L, d), BF16),
        compiler_params=_cparams(("parallel", "parallel")),
    )(x, mod3, g.reshape(1, d))


def _mm_kernel(a_ref, w_ref, o_ref):
    o_ref[...] = jnp.dot(a_ref[...], w_ref[...], preferred_element_type=F32).astype(o_ref.dtype)


def _matmul(a, w, out_dtype, tm, tn):
    m, k = a.shape
    n = w.shape[1]
    return pl.pallas_call(
        _mm_kernel,
        grid=(m // tm, n // tn),
        in_specs=[pl.BlockSpec((tm, k), lambda i, j: (i, 0)),
                  pl.BlockSpec((k, tn), lambda i, j: (0, j))],
        out_specs=pl.BlockSpec((tm, tn), lambda i, j: (i, j)),
        out_shape=jax.ShapeDtypeStruct((m, n), out_dtype),
        compiler_params=_cparams(("parallel", "arbitrary")),
    )(a, w)


def _kvnorm_kernel(aux_ref, gc_ref, gk_ref, ckv_ref, iklo_ref, ikhi_ref):
    ckv = aux_ref[:, :A_KV_RANK]
    ckv_ref[...] = (ckv * lax.rsqrt(jnp.mean(ckv * ckv, axis=-1, keepdims=True) + EPS)
                    * gc_ref[...]).astype(BF16)
    v = aux_ref[:, A_KV_RANK:A_KV_RANK + LANES]
    lane = lax.broadcasted_iota(I32, v.shape, 1)
    ik = jnp.where(lane < IDX_DIM, v, 0.0)
    ms = jnp.sum(ik * ik, axis=-1, keepdims=True) * (1.0 / IDX_DIM)
    ikn = ik * lax.rsqrt(ms + EPS) * gk_ref[...]
    iklo_ref[...] = ikn.astype(BF16)
    ikhi_ref[...] = pltpu.roll(ikn, IDX_DIM, 1).astype(BF16)


def _kvnorm(aux, gc, gk, tm=1024):
    t = aux.shape[0]
    gk_pad = jnp.concatenate([gk, jnp.zeros((LANES - IDX_DIM,), F32)]).reshape(1, LANES)
    return pl.pallas_call(
        _kvnorm_kernel,
        grid=(t // tm,),
        in_specs=[pl.BlockSpec((tm, aux.shape[1]), lambda i: (i, 0)),
                  pl.BlockSpec((1, A_KV_RANK), lambda i: (0, 0)),
                  pl.BlockSpec((1, LANES), lambda i: (0, 0))],
        out_specs=[pl.BlockSpec((tm, A_KV_RANK), lambda i: (i, 0)),
                   pl.BlockSpec((tm, LANES), lambda i: (i, 0)),
                   pl.BlockSpec((tm, LANES), lambda i: (i, 0))],
        out_shape=[jax.ShapeDtypeStruct((t, A_KV_RANK), BF16),
                   jax.ShapeDtypeStruct((t, LANES), BF16),
                   jax.ShapeDtypeStruct((t, LANES), BF16)],
        compiler_params=_cparams(("parallel",)),
    )(aux, gc.reshape(1, A_KV_RANK), gk_pad)


def _t5_bucket(rel):
    n = jnp.maximum(rel, 0)
    max_exact = REL_BUCKETS // 2
    n_large = jnp.maximum(n, max_exact).astype(F32)
    large = max_exact + (jnp.log(n_large / max_exact) / math.log(REL_MAX_DIST / max_exact)
                         * (REL_BUCKETS - max_exact)).astype(I32)
    large = jnp.minimum(large, REL_BUCKETS - 1)
    return jnp.where(n < max_exact, n, large)


def _bias_tables(rel_bias, tq):
    assert tq + 1 >= REL_MAX_DIST
    nh = rel_bias.shape[1]
    dist = jnp.maximum(jnp.arange(3 * tq + 1, dtype=I32) - tq, 0)
    v = rel_bias.astype(F32)[_t5_bucket(dist)].T * LOG2E
    n = v.shape[1]
    x = jnp.broadcast_to(v[:, None, :], (nh, tq, n)).reshape(nh, tq * n)[:, :tq * (n - 1)].reshape(nh, tq, n - 1)
    near = x[:, :, tq:2 * tq]
    prev = x[:, :, 2 * tq:3 * tq]
    far = jnp.broadcast_to(v[:, n - 1][:, None, None], near.shape)
    return jnp.stack([near, prev, far])


def _dsa_kernel(qa_ref, iq_ref, aux_ref, iklo_ref, ikhi_ref, ckv_ref, ckvt_ref, wuk_ref, wuvt_ref, bias_ref,
                g_ref, o_ref, iqt_ref, iwt_ref, key_ref, qlt_ref, m_ref, l_ref, acc_ref, tie_ref, madd_ref,
                *, tq, topk):
    i = pl.program_id(1)
    nh = A_HEADS
    npair = IDX_HEADS // 2

    r_i = lax.broadcasted_iota(I32, (LANES, LANES), 0)
    c_i = lax.broadcasted_iota(I32, (LANES, LANES), 1)
    eye = jnp.where(r_i == c_i, 1.0, 0.0).astype(BF16)
    for p in range(npair):
        iqt_ref[:, p * tq:(p + 1) * tq] = lax.dot_general(
            eye, iq_ref[:, p * LANES:(p + 1) * LANES], NT_DIMS, preferred_element_type=F32).astype(BF16)
    iwt_ref[...] = (jnp.transpose(aux_ref[...])[IDX_DIM:IDX_DIM + IDX_HEADS, :]
                    * (IDX_HEADS ** -0.5 * IDX_DIM ** -0.5))
    for h in range(nh):
        ql = lax.dot_general(wuk_ref[h], qa_ref[:, h * A_HEAD_DIM:(h + 1) * A_HEAD_DIM], NT_DIMS,
                             preferred_element_type=F32)
        qlt_ref[:, h * tq:(h + 1) * tq] = (ql * (A_HEAD_DIM ** -0.5 * LOG2E)).astype(BF16)

    kpos = lax.broadcasted_iota(I32, (tq, tq), 0)
    qpos = lax.broadcasted_iota(I32, (tq, tq), 1) + i * tq

    def score_body(kc, carry):
        off = pl.multiple_of(kc * tq, tq)
        klo = iklo_ref[0, pl.ds(off, tq), :]
        khi = ikhi_ref[0, pl.ds(off, tq), :]
        acc = jnp.zeros((tq, tq), F32)
        for p in range(npair):
            rhs = iqt_ref[:, p * tq:(p + 1) * tq]
            se = jnp.dot(klo, rhs, preferred_element_type=F32)
            so = jnp.dot(khi, rhs, preferred_element_type=F32)
            acc = acc + jnp.maximum(se, 0.0) * iwt_ref[2 * p:2 * p + 1, :]
            acc = acc + jnp.maximum(so, 0.0) * iwt_ref[2 * p + 1:2 * p + 2, :]
        bits = lax.bitcast_convert_type(acc, I32)
        key = jnp.where(bits >= 0, bits, bits ^ jnp.int32(0x7FFFFFFF))
        key_ref[kc] = jnp.where(kpos + off <= qpos, key, jnp.int32(KEY_NEG_INF))
        return carry

    lax.fori_loop(0, i + 1, score_body, 0)

    def count_ge(cand):
        def body(kc, c):
            hit = jnp.where(key_ref[kc] >= cand, 1.0, 0.0)
            return c + jnp.sum(hit.reshape(tq // 8, 8, tq), axis=0)
        c = lax.fori_loop(0, i + 1, body, jnp.zeros((8, tq), F32))
        return jnp.sum(c, axis=0, keepdims=True)

    kf = float(topk)
    thr = jnp.where(count_ge(jnp.zeros((1, tq), I32)) >= kf, jnp.int32(0), jnp.int32(INT_MIN))

    def bit_body(j, thr):
        cand = thr | lax.shift_left(jnp.int32(1), 30 - j)
        return jnp.where(count_ge(cand) >= kf, cand, thr)

    thr = lax.fori_loop(0, 31, bit_body, thr)

    def count_gt_eq():
        def body(kc, c):
            key = key_ref[kc]
            gt = jnp.where(key > thr, 1.0, 0.0)
            eq = jnp.where(key == thr, 1.0, 0.0)
            return (c[0] + jnp.sum(gt.reshape(tq // 8, 8, tq), axis=0),
                    c[1] + jnp.sum(eq.reshape(tq // 8, 8, tq), axis=0))
        z = jnp.zeros((8, tq), F32)
        c = lax.fori_loop(0, i + 1, body, (z, z))
        return jnp.sum(c[0], axis=0, keepdims=True), jnp.sum(c[1], axis=0, keepdims=True)

    n_gt, n_eq = count_gt_eq()
    need = kf - n_gt
    tied = (n_gt + n_eq > kf) & (thr > jnp.int32(KEY_NEG_INF))
    has_tie = jnp.max(jnp.where(tied, 1.0, 0.0)) > 0.0
    tie_ref[...] = jnp.zeros(tie_ref.shape, F32)

    m_ref[...] = jnp.full(m_ref.shape, -jnp.inf, F32)
    l_ref[...] = jnp.zeros(l_ref.shape, F32)
    acc_ref[...] = jnp.zeros(acc_ref.shape, F32)

    def att_body(kc, carry):
        off = pl.multiple_of(kc * tq, tq)
        ckv = ckv_ref[0, pl.ds(off, tq), :]
        ckvt = ckvt_ref[0, kc]
        key = key_ref[kc]
        causal = key > jnp.int32(KEY_NEG_INF)

        @pl.when(jnp.logical_not(has_tie))
        def _():
            madd_ref[...] = jnp.where((key >= thr) & causal, 0.0, -jnp.inf)

        @pl.when(has_tie)
        def _():
            eq = key == thr
            eqf = jnp.where(eq, 1.0, 0.0)
            before = (lax.broadcasted_iota(I32, (tq, tq), 1) < lax.broadcasted_iota(I32, (tq, tq), 0))
            rank = jnp.dot(jnp.where(before, 1.0, 0.0).astype(BF16), eqf.astype(BF16),
                           preferred_element_type=F32) + tie_ref[...]
            keep = (key > thr) | (eq & (rank < need))
            madd_ref[...] = jnp.where(keep & causal, 0.0, -jnp.inf)
            tie_ref[...] = tie_ref[...] + jnp.sum(eqf, axis=0, keepdims=True)

        madd = madd_ref[...]
        d = jnp.minimum(i - kc, 2)
        for h in range(nh):
            s = jnp.dot(ckv, qlt_ref[:, h * tq:(h + 1) * tq], preferred_element_type=F32)
            s = s + (bias_ref[d, h] + madd)
            m_old = m_ref[h:h + 1, :]
            m_new = jnp.maximum(m_old, jnp.max(s, axis=0, keepdims=True))
            m_safe = jnp.where(m_new == -jnp.inf, 0.0, m_new)
            alpha = jnp.exp2(m_old - m_safe)
            p = jnp.exp2(s - m_safe)
            l_ref[h:h + 1, :] = alpha * l_ref[h:h + 1, :] + jnp.sum(p, axis=0, keepdims=True)
            acc_ref[h] = alpha * acc_ref[h] + jnp.dot(ckvt, p.astype(BF16), preferred_element_type=F32)
            m_ref[h:h + 1, :] = m_new
        return carry

    lax.fori_loop(0, i + 1, att_body, 0)

    outs = []
    for h in range(nh):
        o_lat = (acc_ref[h] / l_ref[h:h + 1, :]).astype(BF16)
        outs.append(jnp.transpose(jnp.dot(wuvt_ref[h], o_lat, preferred_element_type=F32)))
    o = jnp.concatenate(outs, axis=1)
    o = o * lax.rsqrt(jnp.mean(o * o, axis=-1, keepdims=True) + EPS) * g_ref[...]
    o_ref[...] = o.astype(o_ref.dtype)


def _dsa(g1, aux, ik_lo, ik_hi, ckv_n, w_uk, w_uv, bias_tab, g, bsz, L, tq):
    t = bsz * L
    nq = L // tq
    topk = min(IDX_TOPK_MAX, L // 4)
    aux_blk = A_KV_RANK // LANES
    kern = functools.partial(_dsa_kernel, tq=tq, topk=topk)
    width = A_HEADS * A_HEAD_DIM
    ckv3 = ckv_n.reshape(bsz, L, A_KV_RANK)
    ckvt = ckv_n.reshape(bsz, nq, tq, A_KV_RANK).transpose(0, 1, 3, 2)
    return pl.pallas_call(
        kern,
        grid=(bsz, nq),
        in_specs=[pl.BlockSpec((tq, width), lambda b, i: (b * nq + i, 0)),
                  pl.BlockSpec((tq, IDX_HEADS * IDX_DIM), lambda b, i: (b * nq + i, 1)),
                  pl.BlockSpec((tq, LANES), lambda b, i: (b * nq + i, aux_blk)),
                  pl.BlockSpec((1, L, LANES), lambda b, i: (b, 0, 0)),
                  pl.BlockSpec((1, L, LANES), lambda b, i: (b, 0, 0)),
                  pl.BlockSpec((1, L, A_KV_RANK), lambda b, i: (b, 0, 0)),
                  pl.BlockSpec((1, nq, A_KV_RANK, tq), lambda b, i: (b, 0, 0, 0)),
                  pl.BlockSpec((A_HEADS, A_KV_RANK, A_HEAD_DIM), lambda b, i: (0, 0, 0)),
                  pl.BlockSpec((A_HEADS, A_HEAD_DIM, A_KV_RANK), lambda b, i: (0, 0, 0)),
                  pl.BlockSpec((3, A_HEADS, tq, tq), lambda b, i: (0, 0, 0, 0)),
                  pl.BlockSpec((1, width), lambda b, i: (0, 0))],
        out_specs=pl.BlockSpec((tq, width), lambda b, i: (b * nq + i, 0)),
        out_shape=jax.ShapeDtypeStruct((t, width), BF16),
        scratch_shapes=[pltpu.VMEM((LANES, IDX_HEADS // 2 * tq), BF16),
                        pltpu.VMEM((IDX_HEADS, tq), F32),
                        pltpu.VMEM((nq, tq, tq), I32),
                        pltpu.VMEM((A_KV_RANK, A_HEADS * tq), BF16),
                        pltpu.VMEM((A_HEADS, tq), F32),
                        pltpu.VMEM((A_HEADS, tq), F32),
                        pltpu.VMEM((A_HEADS, A_KV_RANK, tq), F32),
                        pltpu.VMEM((1, tq), F32),
                        pltpu.VMEM((tq, tq), F32)],
        compiler_params=_cparams(("parallel", "arbitrary")),
    )(g1, g1, aux, ik_lo.reshape(bsz, L, LANES), ik_hi.reshape(bsz, L, LANES),
      ckv3, ckvt, w_uk, jnp.transpose(w_uv, (0, 2, 1)), bias_tab, g.reshape(1, width))


def _hgrn_kernel(q_ref, i_ref, gate_ref, f_ref, lbl_ref, ng_ref, o_ref, st_ref, bc_ref, kk_ref, sc_ref,
                 *, chunk, rblk):
    @pl.when(pl.program_id(1) == 0)
    def _():
        st_ref[...] = jnp.zeros(st_ref.shape, F32)

    ll = lbl_ref[...]
    ex = jnp.exp(ll - jnp.max(ll, axis=0, keepdims=True))
    lb_all = ex[0:1] / jnp.sum(ex, axis=0, keepdims=True)

    r_i = lax.broadcasted_iota(I32, (chunk, chunk), 0)
    c_i = lax.broadcasted_iota(I32, (chunk, chunk), 1)
    tri = jnp.where(r_i >= c_i, 1.0, 0.0).astype(BF16)
    row_k = lax.broadcasted_iota(I32, (chunk, B_DIM), 0)
    nblk = chunk // rblk
    zero_row = jnp.zeros((1, B_DIM), F32)

    growth = zero_row
    for h in range(B_HEADS):
        sl = slice(h * B_DIM, (h + 1) * B_DIM)
        lb = lb_all[:, sl]
        f = lb + (1.0 - lb) * jax.nn.sigmoid(f_ref[:, sl])
        lf = jnp.log(f)
        kk_ref[h] = 1.0 - f
        l1 = lf.astype(BF16)
        r1 = lf - l1.astype(F32)
        l2 = r1.astype(BF16)
        l3 = (r1 - l2.astype(F32)).astype(BF16)
        cs = jnp.dot(tri, jnp.concatenate([l1, l2, l3], axis=1), preferred_element_type=F32)
        bc = cs[:, :B_DIM] + cs[:, B_DIM:2 * B_DIM] + cs[:, 2 * B_DIM:]
        bc_ref[h] = bc
        for r in range(nblk):
            top = bc[r * rblk - 1:r * rblk] if r > 0 else zero_row
            growth = jnp.maximum(growth, top - bc[(r + 1) * rblk - 1:(r + 1) * rblk])
    overflow_risk = jnp.max(growth) > HGRN_MAX_BLOCK_DECAY

    def block_scores(h, before_only):
        sl = slice(h * B_DIM, (h + 1) * B_DIM)
        bc = bc_ref[h]
        kk = kk_ref[h]
        q = q_ref[:, sl].astype(F32)
        parts = []
        for r in range(nblk):
            lo, hi = r * rblk, (r + 1) * rblk
            base = bc[lo - 1:lo] if r > 0 else zero_row
            qt = (q[lo:hi] * jnp.exp(bc[lo:hi] - base)).astype(BF16)
            if before_only:
                kt = jnp.where(row_k < lo, kk * jnp.exp(jnp.where(row_k < lo, base - bc, 0.0)), 0.0)
            else:
                kt = kk * jnp.exp(jnp.where(row_k < hi, base - bc, 0.0))
            parts.append(lax.dot_general(qt, kt.astype(BF16), NT_DIMS, preferred_element_type=F32))
        return jnp.concatenate(parts, axis=0)

    @pl.when(jnp.logical_not(overflow_risk))
    def _():
        for h in range(B_HEADS):
            sc_ref[h] = jnp.where(c_i <= r_i, block_scores(h, False), 0.0)

    @pl.when(overflow_risk)
    def _():
        for h in range(B_HEADS):
            sl = slice(h * B_DIM, (h + 1) * B_DIM)
            bc = bc_ref[h]
            kk = kk_ref[h]
            q = q_ref[:, sl].astype(F32)
            sc = block_scores(h, True)
            for dlt in range(rblk):
                bc_s = pltpu.roll(bc, dlt, 0) if dlt else bc
                kk_s = pltpu.roll(kk, dlt, 0) if dlt else kk
                ok = (row_k & (rblk - 1)) >= dlt
                band = q * kk_s * jnp.exp(jnp.where(ok, bc - bc_s, -jnp.inf))
                sc = sc + jnp.where(c_i == r_i - dlt, jnp.sum(band, axis=1, keepdims=True), 0.0)
            sc_ref[h] = sc

    for h in range(B_HEADS):
        sl = slice(h * B_DIM, (h + 1) * B_DIM)
        bc = bc_ref[h]
        kk = kk_ref[h]
        q = q_ref[:, sl].astype(F32)
        v = i_ref[:, sl]
        st = st_ref[h]
        o = lax.dot_general((q * jnp.exp(bc)).astype(BF16), st.astype(BF16), NT_DIMS,
                            preferred_element_type=F32)
        o = o + jnp.dot(sc_ref[h].astype(BF16), v, preferred_element_type=F32)

        last = bc[chunk - 1:chunk]
        kd = (kk * jnp.exp(last - bc)).astype(BF16)
        st_ref[h] = st * jnp.exp(last) + lax.dot_general(v, kd, TN_DIMS, preferred_element_type=F32)

        y = o * lax.rsqrt(jnp.mean(o * o, axis=-1, keepdims=True) + EPS) * ng_ref[:, sl]
        o_ref[:, sl] = (y * _silu(gate_ref[:, sl].astype(F32))).astype(o_ref.dtype)


def _hgrn(g1, fb, lb_logits, ng, bsz, L, chunk=128, rblk=32):
    assert rblk & (rblk - 1) == 0 and chunk % rblk == 0
    t = bsz * L
    nc = L // chunk
    width = B_HEADS * B_DIM
    kern = functools.partial(_hgrn_kernel, chunk=chunk, rblk=rblk)
    return pl.pallas_call(
        kern,
        grid=(bsz, nc),
        in_specs=[pl.BlockSpec((chunk, width), lambda b, c: (b * nc + c, 2)),
                  pl.BlockSpec((chunk, width), lambda b, c: (b * nc + c, 3)),
                  pl.BlockSpec((chunk, width), lambda b, c: (b * nc + c, 4)),
                  pl.BlockSpec((chunk, width), lambda b, c: (b * nc + c, 0)),
                  pl.BlockSpec(lb_logits.shape, lambda b, c: (0, 0)),
                  pl.BlockSpec((1, width), lambda b, c: (0, 0))],
        out_specs=pl.BlockSpec((chunk, width), lambda b, c: (b * nc + c, 0)),
        out_shape=jax.ShapeDtypeStruct((t, width), BF16),
        scratch_shapes=[pltpu.VMEM((B_HEADS, B_DIM, B_DIM), F32),
                        pltpu.VMEM((B_HEADS, chunk, B_DIM), F32),
                        pltpu.VMEM((B_HEADS, chunk, B_DIM), F32),
                        pltpu.VMEM((B_HEADS, chunk, chunk), F32)],
        compiler_params=_cparams(("parallel", "arbitrary")),
    )(g1, g1, g1, fb, lb_logits, ng.reshape(1, width))


def _out_kernel(oa_ref, ob_ref, x_ref, wa_ref, wb_ref, mod_ref, g_ref, wr_ref, x1_ref, h2_ref, lg_ref):
    mix = jnp.dot(oa_ref[...], wa_ref[...], preferred_element_type=F32)
    mix = mix + jnp.dot(ob_ref[...], wb_ref[...], preferred_element_type=F32)
    x1 = x_ref[...] + mod_ref[0, 2:3, :] * mix
    x1_ref[...] = x1
    y = x1 * lax.rsqrt(jnp.mean(x1 * x1, axis=-1, keepdims=True) + EPS) * g_ref[...]
    h2 = y * (1.0 + mod_ref[0, 4:5, :]) + mod_ref[0, 3:4, :]
    h2_ref[...] = _pack_halves(h2)
    lg_ref[...] = lax.dot_general(wr_ref[...], h2.astype(BF16), NT_DIMS, preferred_element_type=F32)


def _out(oa, ob, x2d, w_out, mod3, g, w_router_t, L, tm=256):
    t, d = x2d.shape
    half = oa.shape[1]
    ne = w_router_t.shape[0]
    return pl.pallas_call(
        _out_kernel,
        grid=(t // tm,),
        in_specs=[pl.BlockSpec((tm, half), lambda i: (i, 0)),
                  pl.BlockSpec((tm, half), lambda i: (i, 0)),
                  pl.BlockSpec((tm, d), lambda i: (i, 0)),
                  pl.BlockSpec((half, d), lambda i: (0, 0)),
                  pl.BlockSpec((half, d), lambda i: (1, 0)),
                  pl.BlockSpec((1, 6, d), lambda i: (i * tm // L, 0, 0)),
                  pl.BlockSpec((1, d), lambda i: (0, 0)),
                  pl.BlockSpec((ne, d), lambda i: (0, 0))],
        out_specs=[pl.BlockSpec((tm, d), lambda i: (i, 0)),
                   pl.BlockSpec((tm, d // 2), lambda i: (i, 0)),
                   pl.BlockSpec((ne, tm), lambda i: (0, i))],
        out_shape=[jax.ShapeDtypeStruct((t, d), F32),
                   jax.ShapeDtypeStruct((t, d // 2), U32),
                   jax.ShapeDtypeStruct((ne, t), F32)],
        compiler_params=_cparams(("parallel",)),
    )(oa, ob, x2d, w_out, w_out, mod3, g.reshape(1, d), w_router_t)


def _rows_to_tile(rows, nrow):
    n = rows[0].shape[1]
    ridx = lax.broadcasted_iota(I32, (nrow, n), 0)
    out = jnp.zeros((nrow, n), rows[0].dtype)
    for r, v in enumerate(rows):
        out = jnp.where(ridx == r, jnp.broadcast_to(v, (nrow, n)), out)
    return out


def _route_kernel(lg_ref, rb_ref, eidx_ref, ew_ref, rank_ref, cnt_ref, run_ref):
    @pl.when(pl.program_id(0) == 0)
    def _():
        run_ref[...] = jnp.zeros(run_ref.shape, F32)

    ne, tt = lg_ref.shape
    per = ne // N_GROUPS
    sc = jax.nn.sigmoid(lg_ref[...])
    ch = sc + rb_ref[...]
    neg = -jnp.inf

    sub = lax.broadcasted_iota(I32, (per, tt), 0).astype(F32)
    gsc = []
    for g in range(N_GROUPS):
        cg = ch[g * per:(g + 1) * per]
        m1 = jnp.max(cg, axis=0, keepdims=True)
        first = jnp.min(jnp.where(cg == m1, sub, float(per)), axis=0, keepdims=True)
        m2 = jnp.max(jnp.where(sub == first, neg, cg), axis=0, keepdims=True)
        gsc.append(m1 + m2)
    grp = _rows_to_tile(gsc, N_GROUPS)

    gid = lax.broadcasted_iota(I32, (N_GROUPS, tt), 0).astype(F32)
    gsel = jnp.zeros((N_GROUPS, tt), F32)
    for _ in range(TOPK_GROUPS):
        mx = jnp.max(grp, axis=0, keepdims=True)
        gi = jnp.min(jnp.where(grp == mx, gid, float(N_GROUPS)), axis=0, keepdims=True)
        pick = gid == gi
        gsel = jnp.where(pick, 1.0, gsel)
        grp = jnp.where(pick, neg, grp)

    eid = lax.broadcasted_iota(I32, (ne, tt), 0).astype(F32)
    cm = jnp.full((ne, tt), neg, F32)
    for g in range(N_GROUPS):
        in_g = (eid >= float(g * per)) & (eid < float((g + 1) * per))
        cm = jnp.where(in_g & (jnp.broadcast_to(gsel[g:g + 1], (ne, tt)) > 0.5), ch, cm)

    idx_rows, w_rows = [], []
    onehot = jnp.zeros((ne, tt), F32)
    for _ in range(TOP_K):
        mx = jnp.max(cm, axis=0, keepdims=True)
        ei = jnp.min(jnp.where(cm == mx, eid, float(ne)), axis=0, keepdims=True)
        pick = eid == ei
        idx_rows.append(ei)
        w_rows.append(jnp.sum(jnp.where(pick, sc, 0.0), axis=0, keepdims=True))
        onehot = jnp.where(pick, 1.0, onehot)
        cm = jnp.where(pick, neg, cm)
    wsum = w_rows[0]
    for w in w_rows[1:]:
        wsum = wsum + w
    w_rows = [w / wsum * ROUTED_SCALE for w in w_rows]

    a_i = lax.broadcasted_iota(I32, (tt, tt), 0)
    b_i = lax.broadcasted_iota(I32, (tt, tt), 1)
    upper = jnp.where(a_i < b_i, 1.0, 0.0).astype(BF16)
    rank_full = jnp.dot(onehot.astype(BF16), upper, preferred_element_type=F32) + run_ref[...]
    r_rows = [jnp.sum(jnp.where(eid == ei, rank_full, 0.0), axis=0, keepdims=True) for ei in idx_rows]
    run = run_ref[...] + jnp.sum(onehot, axis=1, keepdims=True)
    run_ref[...] = run

    eidx_ref[...] = _rows_to_tile(idx_rows, TOP_K).astype(I32)
    ew_ref[...] = _rows_to_tile(w_rows, TOP_K)
    rank_ref[...] = _rows_to_tile(r_rows, TOP_K).astype(I32)
    cnt_ref[...] = jnp.broadcast_to(run, cnt_ref.shape)


def _route(logits_t, router_bias, tt=512):
    ne, t = logits_t.shape
    return pl.pallas_call(
        _route_kernel,
        grid=(t // tt,),
        in_specs=[pl.BlockSpec((ne, tt), lambda i: (0, i)),
                  pl.BlockSpec((ne, 1), lambda i: (0, 0))],
        out_specs=[pl.BlockSpec((TOP_K, tt), lambda i: (0, i)),
                   pl.BlockSpec((TOP_K, tt), lambda i: (0, i)),
                   pl.BlockSpec((TOP_K, tt), lambda i: (0, i)),
                   pl.BlockSpec((ne, LANES), lambda i: (0, 0))],
        out_shape=[jax.ShapeDtypeStruct((TOP_K, t), I32),
                   jax.ShapeDtypeStruct((TOP_K, t), F32),
                   jax.ShapeDtypeStruct((TOP_K, t), I32),
                   jax.ShapeDtypeStruct((ne, LANES), F32)],
        scratch_shapes=[pltpu.VMEM((ne, 1), F32)],
        compiler_params=_cparams(("arbitrary",)),
    )(logits_t, router_bias.reshape(ne, 1))


def _dest_kernel(eidx_ref, rank_ref, ps_ref, o_ref):
    ne = ps_ref.shape[0]
    tt = eidx_ref.shape[1]
    eid = lax.broadcasted_iota(I32, (ne, tt), 0)
    ps = jnp.broadcast_to(ps_ref[...], (ne, tt))
    rows = []
    for k in range(TOP_K):
        start = jnp.sum(jnp.where(eid == eidx_ref[k:k + 1, :], ps, 0.0), axis=0, keepdims=True)
        rows.append(start + rank_ref[k:k + 1, :].astype(F32))
    o_ref[...] = _rows_to_tile(rows, TOP_K).astype(I32)


def _dest(eidx, rank, pad_start, tt=2048):
    t = eidx.shape[1]
    tt = min(tt, t)
    ne = pad_start.shape[0]
    return pl.pallas_call(
        _dest_kernel,
        grid=(t // tt,),
        in_specs=[pl.BlockSpec((TOP_K, tt), lambda i: (0, i)),
                  pl.BlockSpec((TOP_K, tt), lambda i: (0, i)),
                  pl.BlockSpec((ne, 1), lambda i: (0, 0))],
        out_specs=pl.BlockSpec((TOP_K, tt), lambda i: (0, i)),
        out_shape=jax.ShapeDtypeStruct((TOP_K, t), I32),
        compiler_params=_cparams(("parallel",)),
    )(eidx, rank, pad_start.astype(F32).reshape(ne, 1))


def _dispatch_kernel(pend_ref, padded_ref, dest_ref, h_ref, xs_ref, zbuf_ref, zsem, sem, *, td, bm):
    i = pl.program_id(0)

    def tail_copy(e):
        start = pl.multiple_of(pend_ref[e] - bm, bm)
        return pltpu.make_async_copy(zbuf_ref, xs_ref.at[pl.ds(start, bm)], zsem)

    @pl.when(i == 0)
    def _():
        zbuf_ref[...] = jnp.zeros(zbuf_ref.shape, zbuf_ref.dtype)

        def start_body(e, c):
            @pl.when(padded_ref[e] > 0)
            def _():
                tail_copy(e).start()
            return c

        def wait_body(e, c):
            @pl.when(padded_ref[e] > 0)
            def _():
                tail_copy(e).wait()
            return c

        lax.fori_loop(0, N_EXPERTS, start_body, 0)
        lax.fori_loop(0, N_EXPERTS, wait_body, 0)

        def unused_copy(b):
            return pltpu.make_async_copy(zbuf_ref, xs_ref.at[pl.ds(pl.multiple_of(b * bm, bm), bm)], zsem)

        def ustart_body(b, c):
            unused_copy(b).start()
            return c

        def uwait_body(b, c):
            unused_copy(b).wait()
            return c

        first_unused = pend_ref[N_EXPERTS - 1] // bm
        lax.fori_loop(first_unused, xs_ref.shape[0] // bm, ustart_body, 0)
        lax.fori_loop(first_unused, xs_ref.shape[0] // bm, uwait_body, 0)

    for j in range(td):
        for k in range(TOP_K):
            dst = xs_ref.at[dest_ref[j * TOP_K + k]]
            pltpu.make_async_copy(h_ref.at[j], dst, sem).start(priority=k % 2)
    for _ in range(TOP_K):
        pltpu.make_async_copy(h_ref, xs_ref.at[pl.ds(0, td)], sem).wait()


def _dispatch(pad_end, padded, dest_flat, h2p, n_rows, bm, td=256):
    t, w = h2p.shape
    kern = functools.partial(_dispatch_kernel, td=td, bm=bm)
    return pl.pallas_call(
        kern,
        grid_spec=pltpu.PrefetchScalarGridSpec(
            num_scalar_prefetch=2,
            grid=(t // td,),
            in_specs=[pl.BlockSpec((td * TOP_K,), lambda i, *_: (i,), memory_space=pltpu.SMEM),
                      pl.BlockSpec((td, w), lambda i, *_: (i, 0))],
            out_specs=pl.BlockSpec(memory_space=pl.ANY),
            scratch_shapes=[pltpu.VMEM((bm, w), U32),
                            pltpu.SemaphoreType.DMA(()),
                            pltpu.SemaphoreType.DMA(())]),
        out_shape=jax.ShapeDtypeStruct((n_rows, w), U32),
        compiler_params=_cparams(("arbitrary",)),
    )(pad_end, padded, dest_flat, h2p)


def _ffn(xw, wg_ref, wu_ref, wd_ref):
    half = xw.shape[1]
    left, right = _unpack_halves(xw)
    left = left.astype(BF16)
    right = right.astype(BF16)

    def proj(w_ref):
        return (jnp.dot(left, w_ref[:half, :], preferred_element_type=F32)
                + jnp.dot(right, w_ref[half:, :], preferred_element_type=F32))

    act = (_silu(proj(wg_ref)) * proj(wu_ref)).astype(BF16)
    return jnp.dot(act, wd_ref[...], preferred_element_type=F32)


def _expert_kernel(blk_ref, eid_ref, first_ref, slot_ref, nxt_ref, more_ref, nvb_ref,
                   x_ref, wg_hbm, wu_hbm, wd_hbm, o_ref,
                   wg_f, wu_f, wd_f, wg_s, wu_s, wd_s, sems):
    i = pl.program_id(0)

    def weight_copies(e, slot):
        return (pltpu.make_async_copy(wg_hbm.at[e], wg_f.at[slot], sems.at[slot]),
                pltpu.make_async_copy(wu_hbm.at[e], wu_f.at[slot], sems.at[slot]),
                pltpu.make_async_copy(wd_hbm.at[e], wd_f.at[slot], sems.at[slot]))

    @pl.when(i == 0)
    def _():
        for cp in weight_copies(eid_ref[0], 0):
            cp.start()

    @pl.when(first_ref[i] == 1)
    def _():
        slot = slot_ref[i]
        for cp in weight_copies(eid_ref[i], slot):
            cp.wait()

        @pl.when(more_ref[i] == 1)
        def _():
            for cp in weight_copies(nxt_ref[i], 1 - slot):
                cp.start()

        wg_s[...] = wg_f[slot].astype(BF16)
        wu_s[...] = wu_f[slot].astype(BF16)
        wd_s[...] = wd_f[slot].astype(BF16)

    @pl.when(i < nvb_ref[0])
    def _():
        o_ref[...] = _pack_halves(_ffn(x_ref[...], wg_s, wu_s, wd_s))

    @pl.when(i >= nvb_ref[0])
    def _():
        o_ref[...] = jnp.zeros(o_ref.shape, o_ref.dtype)


def _experts(blk, eid, first, slot, nxt, more, nvb, xs, wg, wu, wd, bm):
    n_rows, w = xs.shape
    ne, d, f = wg.shape
    return pl.pallas_call(
        _expert_kernel,
        grid_spec=pltpu.PrefetchScalarGridSpec(
            num_scalar_prefetch=7,
            grid=(n_rows // bm,),
            in_specs=[pl.BlockSpec((bm, w), lambda i, blk, *_: (blk[i], 0)),
                      pl.BlockSpec(memory_space=pl.ANY),
                      pl.BlockSpec(memory_space=pl.ANY),
                      pl.BlockSpec(memory_space=pl.ANY)],
            out_specs=pl.BlockSpec((bm, w), lambda i, *_: (i, 0)),
            scratch_shapes=[pltpu.VMEM((2, d, f), F32), pltpu.VMEM((2, d, f), F32), pltpu.VMEM((2, f, d), F32),
                            pltpu.VMEM((d, f), BF16), pltpu.VMEM((d, f), BF16), pltpu.VMEM((f, d), BF16),
                            pltpu.SemaphoreType.DMA((2,))]),
        out_shape=jax.ShapeDtypeStruct((n_rows, w), U32),
        compiler_params=_cparams(("arbitrary",)),
    )(blk, eid, first, slot, nxt, more, nvb, xs, wg, wu, wd)


def _combine_kernel(dest_ref, dnext_ref, y_ref, h_ref, x1_ref, ew_ref, wg_ref, wu_ref, wd_ref, mod_ref, g_ref,
                    o_ref, gbuf_a, gbuf_b, sems, *, tc):
    i = pl.program_id(0)
    last = pl.num_programs(0) - 1

    def issue(idx_ref, gbuf, sem):
        for j in range(tc):
            for k in range(TOP_K):
                src = y_ref.at[idx_ref[j * TOP_K + k]]
                pltpu.make_async_copy(src, gbuf.at[k, j], sem).start(priority=k % 2)

    def wait_all(gbuf, sem):
        for k in range(TOP_K):
            pltpu.make_async_copy(y_ref.at[pl.ds(0, tc)], gbuf.at[k], sem).wait()

    def compute(gbuf):
        shared = _ffn(h_ref[...], wg_ref, wu_ref, wd_ref)
        half = h_ref.shape[1]
        ew = ew_ref[...]
        left = shared[:, :half]
        right = shared[:, half:]
        for k in range(TOP_K):
            yl, yr = _unpack_halves(gbuf[k])
            wk = ew[:, k:k + 1]
            left = left + wk * yl
            right = right + wk * yr
        x2 = x1_ref[...] + mod_ref[0, 5:6, :] * jnp.concatenate([left, right], axis=1)
        o_ref[...] = x2 * lax.rsqrt(jnp.mean(x2 * x2, axis=-1, keepdims=True) + EPS) * g_ref[...]

    @pl.when(i == 0)
    def _():
        def group_body(g, c):
            base = pl.multiple_of(g * SUBLANES, SUBLANES)
            for jj in range(SUBLANES):
                for k in range(TOP_K):
                    src = y_ref.at[dest_ref[(base + jj) * TOP_K + k]]
                    pltpu.make_async_copy(src, gbuf_a.at[k, base + jj], sems.at[0]).start(priority=k % 2)
            return c

        lax.fori_loop(0, tc // SUBLANES, group_body, 0)

    @pl.when(i % 2 == 0)
    def _():
        wait_all(gbuf_a, sems.at[0])
        issue(dnext_ref, gbuf_b, sems.at[1])
        compute(gbuf_a)

    @pl.when(i % 2 == 1)
    def _():
        wait_all(gbuf_b, sems.at[1])
        issue(dnext_ref, gbuf_a, sems.at[0])
        compute(gbuf_b)

    @pl.when((i == last) & (i % 2 == 0))
    def _():
        wait_all(gbuf_b, sems.at[1])

    @pl.when((i == last) & (i % 2 == 1))
    def _():
        wait_all(gbuf_a, sems.at[0])


def _combine(dest_flat, y, h2p, x1, ew_t, wsg, wsu, wsd, mod3, g, L, tc=256):
    t, d = x1.shape
    w = h2p.shape[1]
    nt = t // tc
    kern = functools.partial(_combine_kernel, tc=tc)
    return pl.pallas_call(
        kern,
        grid=(nt,),
        in_specs=[pl.BlockSpec((tc * TOP_K,), lambda i: (i,), memory_space=pltpu.SMEM),
                  pl.BlockSpec((tc * TOP_K,), lambda i: (jnp.minimum(i + 1, nt - 1),), memory_space=pltpu.SMEM),
                  pl.BlockSpec(memory_space=pl.ANY),
                  pl.BlockSpec((tc, w), lambda i: (i, 0)),
                  pl.BlockSpec((tc, d), lambda i: (i, 0)),
                  pl.BlockSpec((tc, TOP_K), lambda i: (i, 0)),
                  pl.BlockSpec(wsg.shape, lambda i: (0, 0)),
                  pl.BlockSpec(wsu.shape, lambda i: (0, 0)),
                  pl.BlockSpec(wsd.shape, lambda i: (0, 0)),
                  pl.BlockSpec((1, 6, d), lambda i: (i * tc // L, 0, 0)),
                  pl.BlockSpec((1, d), lambda i: (0, 0))],
        out_specs=pl.BlockSpec((tc, d), lambda i: (i, 0)),
        out_shape=jax.ShapeDtypeStruct((t, d), F32),
        scratch_shapes=[pltpu.VMEM((TOP_K, tc, w), U32),
                        pltpu.VMEM((TOP_K, tc, w), U32),
                        pltpu.SemaphoreType.DMA((2,))],
        compiler_params=_cparams(("arbitrary",)),
    )(dest_flat, dest_flat, y, h2p, x1, ew_t, wsg, wsu, wsd, mod3, g.reshape(1, d))


def _split_cols(w, sizes):
    out, off = [], 0
    for s in sizes:
        out.append(w[:, off:off + s])
        off += s
    return out


def kernel(x, c, w_ada, b_ada, norm1_g, w_in, ckv_norm_g, idx_k_norm_g, w_uk, w_uv, rel_bias, lb_logits,
           attn_out_norm_g, hgrn_out_norm_g, w_out, norm2_g, w_router, router_bias, w_e_gate, w_e_up,
           w_e_down, w_s_gate, w_s_up, w_s_down, final_norm_g):
    bsz, L, d = x.shape
    t = bsz * L
    assert w_ada.shape[0] == 1, "single-layer block"
    a_width = A_HEADS * A_HEAD_DIM
    b_width = B_HEADS * B_DIM
    sizes = (a_width, A_KV_RANK, IDX_HEADS * IDX_DIM, IDX_DIM, IDX_HEADS, b_width, b_width, b_width, b_width)
    assert w_in.shape[2] == sum(sizes)

    wq_a, wckv, wiq, wik, wiw, wq_b, wf_b, wi_b, wg_b = _split_cols(w_in[0], sizes)
    w_main = jnp.concatenate([wq_a, wiq, wq_b, wi_b, wg_b], axis=1).astype(BF16)
    w_f = wf_b.astype(BF16)
    aux_pad = LANES - IDX_DIM - IDX_HEADS
    w_aux = jnp.concatenate([wckv, wik, wiw, jnp.zeros((d, aux_pad), F32)], axis=1).astype(BF16)

    mod3 = _ada(c, w_ada[0], b_ada[0]).reshape(bsz, 6, d)
    h1 = _norm1(x, mod3, norm1_g[0]).reshape(t, d)
    g1 = _matmul(h1, w_main, BF16, tm=1024, tn=512)
    fb = _matmul(h1, w_f, F32, tm=1024, tn=512)
    aux = _matmul(h1, w_aux, F32, tm=1024, tn=w_aux.shape[1])
    ckv_n, ik_lo, ik_hi = _kvnorm(aux, ckv_norm_g[0], idx_k_norm_g[0])

    tq = min(256, L)
    o_a = _dsa(g1, aux, ik_lo, ik_hi, ckv_n, w_uk[0].astype(BF16), w_uv[0].astype(BF16),
               _bias_tables(rel_bias, tq), attn_out_norm_g[0], bsz, L, tq)
    o_b = _hgrn(g1, fb, lb_logits, hgrn_out_norm_g[0], bsz, L)

    x1, h2p, logits_t = _out(o_a, o_b, x.reshape(t, d), w_out[0].astype(BF16), mod3, norm2_g[0],
                             w_router[0].T.astype(BF16), L)

    eidx, ew, rank, cnt = _route(logits_t, router_bias[0])

    bm = 256
    counts = cnt[:, 0].astype(I32)
    padded = (counts + bm - 1) // bm * bm
    pad_end = jnp.cumsum(padded)
    pad_start = pad_end - padded
    n_rows = (t * TOP_K + N_EXPERTS * (bm - 1) + bm - 1) // bm * bm
    nb = n_rows // bm
    nvb = pad_end[-1] // bm
    blk = jnp.minimum(jnp.arange(nb, dtype=I32), nvb - 1)
    eid = jnp.minimum(jnp.sum((pad_end[None, :] <= (blk * bm)[:, None]).astype(I32), axis=1), N_EXPERTS - 1)
    ar = jnp.arange(nb, dtype=I32)
    first = ((ar < nvb) & ((ar == 0) | (eid != jnp.roll(eid, 1)))).astype(I32)
    slot = (jnp.cumsum(first) - 1) % 2
    nxt_blk = pad_end[eid] // bm
    more = (nxt_blk < nvb).astype(I32)
    nxt = eid[jnp.minimum(nxt_blk, nb - 1)]

    dest = _dest(eidx, rank, pad_start)
    dest_flat = dest.T.reshape(t * TOP_K)
    xs = _dispatch(pad_end.astype(I32), padded.astype(I32), dest_flat, h2p, n_rows, bm)

    y = _experts(blk, eid, first, slot.astype(I32), nxt.astype(I32), more, nvb.reshape(1).astype(I32), xs,
                 w_e_gate[0], w_e_up[0], w_e_down[0], bm)

    out = _combine(dest_flat, y, h2p, x1, ew.T, w_s_gate[0].astype(BF16), w_s_up[0].astype(BF16),
                   w_s_down[0].astype(BF16), mod3, final_norm_g, L)
    return out.reshape(bsz, L, d)
```

```python
import functools
import math
from typing import NamedTuple

import jax
import jax.numpy as jnp
from jax import lax
from jax.experimental import pallas as pl
from jax.experimental.pallas import tpu as pltpu

F32 = jnp.float32
BF16 = jnp.bfloat16
I32 = jnp.int32
U32 = jnp.uint32

EPS = 1e-6
A_HEADS = 8
A_HEAD_DIM = 128
A_KV_RANK = 256
IDX_HEADS = 16
IDX_DIM = 64
IDX_TOPK_MAX = 256
B_HEADS = 8
B_DIM = 128
REL_BUCKETS = 32
REL_MAX_DIST = 128
N_EXPERTS = 64
TOP_K = 8
N_GROUPS = 8
TOPK_GROUPS = 4
ROUTED_SCALE = 2.5

VMEM_LIMIT_BYTES = 56 * 1024 * 1024
LANES = 128
SUBLANES = 8


class _Tiles(NamedTuple):
    ada_cols: int = 1024
    norm_rows: int = 512
    proj_rows: int = 1024
    proj_cols: int = 512
    attn: int = 256
    hgrn_chunk: int = 128
    hgrn_rows: int = 32
    out_rows: int = 256
    route_tokens: int = 512
    dest_tokens: int = 2048
    moe_rows: int = 256
    move_tokens: int = 256


TILES = _Tiles()

NT_DIMS = (((1,), (1,)), ((), ()))
TN_DIMS = (((0,), (0,)), ((), ()))

HGRN_MAX_BLOCK_DECAY = 80.0
LOG2E = math.log2(math.e)
INT_MIN = -2 ** 31
KEY_NEG_INF = -2139095041


def _cparams(sem):
    return pltpu.CompilerParams(dimension_semantics=sem, vmem_limit_bytes=VMEM_LIMIT_BYTES)


def _silu(v):
    return v * jax.nn.sigmoid(v)


def _pack_halves(v):
    n = v.shape[1] // 2
    lo = lax.bitcast_convert_type(v[:, :n].astype(BF16).astype(F32), U32)
    hi = lax.bitcast_convert_type(v[:, n:].astype(BF16).astype(F32), U32)
    return lax.shift_right_logical(lo, jnp.uint32(16)) | (hi & jnp.uint32(0xFFFF0000))


def _unpack_halves(w):
    left = lax.bitcast_convert_type(lax.shift_left(w, jnp.uint32(16)), F32)
    right = lax.bitcast_convert_type(w & jnp.uint32(0xFFFF0000), F32)
    return left, right


def _ada_kernel(c_ref, w_ref, b_ref, o_ref):
    a = _silu(c_ref[...]).astype(BF16)
    o_ref[...] = jnp.dot(a, w_ref[...].astype(BF16), preferred_element_type=F32) + b_ref[...]


def _ada(c, w, b, tn=TILES.ada_cols):
    bsz, d = c.shape
    n = w.shape[1]
    return pl.pallas_call(
        _ada_kernel,
        grid=(n // tn,),
        in_specs=[pl.BlockSpec((bsz, d), lambda j: (0, 0)),
                  pl.BlockSpec((d, tn), lambda j: (0, j)),
                  pl.BlockSpec((1, tn), lambda j: (0, j))],
        out_specs=pl.BlockSpec((bsz, tn), lambda j: (0, j)),
        out_shape=jax.ShapeDtypeStruct((bsz, n), F32),
        compiler_params=_cparams(("arbitrary",)),
    )(c, w, b.reshape(1, n))


def _norm1_kernel(x_ref, mod_ref, g_ref, o_ref):
    x = x_ref[0]
    y = x * lax.rsqrt(jnp.mean(x * x, axis=-1, keepdims=True) + EPS) * g_ref[...]
    sh = mod_ref[0, 0:1, :]
    sc = mod_ref[0, 1:2, :]
    o_ref[0] = (y * (1.0 + sc) + sh).astype(o_ref.dtype)


def _norm1(x, mod3, g, tm=TILES.norm_rows):
    bsz, L, d = x.shape
    return pl.pallas_call(
        _norm1_kernel,
        grid=(bsz, L // tm),
        in_specs=[pl.BlockSpec((1, tm, d), lambda b, i: (b, i, 0)),
                  pl.BlockSpec((1, 6, d), lambda b, i: (b, 0, 0)),
                  pl.BlockSpec((1, d), lambda b, i: (0, 0))],
        out_specs=pl.BlockSpec((1, tm, d), lambda b, i: (b, i, 0)),
        out_shape=jax.ShapeDtypeStruct((bsz, L, d), BF16),
        compiler_params=_cparams(("parallel", "parallel")),
    )(x, mod3, g.reshape(1, d))


def _mm_kernel(a_ref, w_ref, o_ref):
    o_ref[...] = jnp.dot(a_ref[...], w_ref[...], preferred_element_type=F32).astype(o_ref.dtype)


def _matmul(a, w, out_dtype, tm, tn):
    m, k = a.shape
    n = w.shape[1]
    return pl.pallas_call(
        _mm_kernel,
        grid=(m // tm, n // tn),
        in_specs=[pl.BlockSpec((tm, k), lambda i, j: (i, 0)),
                  pl.BlockSpec((k, tn), lambda i, j: (0, j))],
        out_specs=pl.BlockSpec((tm, tn), lambda i, j: (i, j)),
        out_shape=jax.ShapeDtypeStruct((m, n), out_dtype),
        compiler_params=_cparams(("parallel", "arbitrary")),
    )(a, w)


def _kvnorm_kernel(aux_ref, gc_ref, gk_ref, ckv_ref, iklo_ref, ikhi_ref):
    ckv = aux_ref[:, :A_KV_RANK]
    ckv_ref[...] = (ckv * lax.rsqrt(jnp.mean(ckv * ckv, axis=-1, keepdims=True) + EPS)
                    * gc_ref[...]).astype(BF16)
    v = aux_ref[:, A_KV_RANK:A_KV_RANK + LANES]
    lane = lax.broadcasted_iota(I32, v.shape, 1)
    ik = jnp.where(lane < IDX_DIM, v, 0.0)
    ms = jnp.sum(ik * ik, axis=-1, keepdims=True) * (1.0 / IDX_DIM)
    ikn = ik * lax.rsqrt(ms + EPS) * gk_ref[...]
    iklo_ref[...] = ikn.astype(BF16)
    ikhi_ref[...] = pltpu.roll(ikn, IDX_DIM, 1).astype(BF16)


def _kvnorm(aux, gc, gk, tm=TILES.proj_rows):
    t = aux.shape[0]
    gk_pad = jnp.concatenate([gk, jnp.zeros((LANES - IDX_DIM,), F32)]).reshape(1, LANES)
    return pl.pallas_call(
        _kvnorm_kernel,
        grid=(t // tm,),
        in_specs=[pl.BlockSpec((tm, aux.shape[1]), lambda i: (i, 0)),
                  pl.BlockSpec((1, A_KV_RANK), lambda i: (0, 0)),
                  pl.BlockSpec((1, LANES), lambda i: (0, 0))],
        out_specs=[pl.BlockSpec((tm, A_KV_RANK), lambda i: (i, 0)),
                   pl.BlockSpec((tm, LANES), lambda i: (i, 0)),
                   pl.BlockSpec((tm, LANES), lambda i: (i, 0))],
        out_shape=[jax.ShapeDtypeStruct((t, A_KV_RANK), BF16),
                   jax.ShapeDtypeStruct((t, LANES), BF16),
                   jax.ShapeDtypeStruct((t, LANES), BF16)],
        compiler_params=_cparams(("parallel",)),
    )(aux, gc.reshape(1, A_KV_RANK), gk_pad)


def _t5_bucket(rel):
    n = jnp.maximum(rel, 0)
    max_exact = REL_BUCKETS // 2
    n_large = jnp.maximum(n, max_exact).astype(F32)
    large = max_exact + (jnp.log(n_large / max_exact) / math.log(REL_MAX_DIST / max_exact)
                         * (REL_BUCKETS - max_exact)).astype(I32)
    large = jnp.minimum(large, REL_BUCKETS - 1)
    return jnp.where(n < max_exact, n, large)


def _bias_tables(rel_bias, tq):
    assert tq + 1 >= REL_MAX_DIST
    nh = rel_bias.shape[1]
    dist = jnp.maximum(jnp.arange(3 * tq + 1, dtype=I32) - tq, 0)
    v = rel_bias.astype(F32)[_t5_bucket(dist)].T * LOG2E
    n = v.shape[1]
    x = jnp.broadcast_to(v[:, None, :], (nh, tq, n)).reshape(nh, tq * n)[:, :tq * (n - 1)].reshape(nh, tq, n - 1)
    near = x[:, :, tq:2 * tq]
    prev = x[:, :, 2 * tq:3 * tq]
    far = jnp.broadcast_to(v[:, n - 1][:, None, None], near.shape)
    return jnp.stack([near, prev, far])


def _dsa_kernel(qa_ref, iq_ref, aux_ref, iklo_ref, ikhi_ref, ckv_ref, ckvt_ref, wuk_ref, wuvt_ref, bias_ref,
                g_ref, o_ref, iqt_ref, iwt_ref, key_ref, qlt_ref, m_ref, l_ref, acc_ref, tie_ref, madd_ref,
                *, tq, topk):
    i = pl.program_id(1)
    nh = A_HEADS
    npair = IDX_HEADS // 2

    r_i = lax.broadcasted_iota(I32, (LANES, LANES), 0)
    c_i = lax.broadcasted_iota(I32, (LANES, LANES), 1)
    eye = jnp.where(r_i == c_i, 1.0, 0.0).astype(BF16)
    for p in range(npair):
        iqt_ref[:, p * tq:(p + 1) * tq] = lax.dot_general(
            eye, iq_ref[:, p * LANES:(p + 1) * LANES], NT_DIMS, preferred_element_type=F32).astype(BF16)
    iwt_ref[...] = (jnp.transpose(aux_ref[...])[IDX_DIM:IDX_DIM + IDX_HEADS, :]
                    * (IDX_HEADS ** -0.5 * IDX_DIM ** -0.5))
    for h in range(nh):
        ql = lax.dot_general(wuk_ref[h], qa_ref[:, h * A_HEAD_DIM:(h + 1) * A_HEAD_DIM], NT_DIMS,
                             preferred_element_type=F32)
        qlt_ref[:, h * tq:(h + 1) * tq] = (ql * (A_HEAD_DIM ** -0.5 * LOG2E)).astype(BF16)

    kpos = lax.broadcasted_iota(I32, (tq, tq), 0)
    qpos = lax.broadcasted_iota(I32, (tq, tq), 1) + i * tq

    def score_body(kc, carry):
        off = pl.multiple_of(kc * tq, tq)
        klo = iklo_ref[0, pl.ds(off, tq), :]
        khi = ikhi_ref[0, pl.ds(off, tq), :]
        acc = jnp.zeros((tq, tq), F32)
        for p in range(npair):
            rhs = iqt_ref[:, p * tq:(p + 1) * tq]
            se = jnp.dot(klo, rhs, preferred_element_type=F32)
            so = jnp.dot(khi, rhs, preferred_element_type=F32)
            acc = acc + jnp.maximum(se, 0.0) * iwt_ref[2 * p:2 * p + 1, :]
            acc = acc + jnp.maximum(so, 0.0) * iwt_ref[2 * p + 1:2 * p + 2, :]
        bits = lax.bitcast_convert_type(acc, I32)
        key = jnp.where(bits >= 0, bits, bits ^ jnp.int32(0x7FFFFFFF))
        key_ref[kc] = jnp.where(kpos + off <= qpos, key, jnp.int32(KEY_NEG_INF))
        return carry

    lax.fori_loop(0, i + 1, score_body, 0)

    def count_ge(cand):
        def body(kc, c):
            hit = jnp.where(key_ref[kc] >= cand, 1.0, 0.0)
            return c + jnp.sum(hit.reshape(tq // 8, 8, tq), axis=0)
        c = lax.fori_loop(0, i + 1, body, jnp.zeros((8, tq), F32))
        return jnp.sum(c, axis=0, keepdims=True)

    kf = float(topk)
    thr = jnp.where(count_ge(jnp.zeros((1, tq), I32)) >= kf, jnp.int32(0), jnp.int32(INT_MIN))

    def bit_body(j, thr):
        cand = thr | lax.shift_left(jnp.int32(1), 30 - j)
        return jnp.where(count_ge(cand) >= kf, cand, thr)

    thr = lax.fori_loop(0, 31, bit_body, thr)

    def count_gt_eq():
        def body(kc, c):
            key = key_ref[kc]
            gt = jnp.where(key > thr, 1.0, 0.0)
            eq = jnp.where(key == thr, 1.0, 0.0)
            return (c[0] + jnp.sum(gt.reshape(tq // 8, 8, tq), axis=0),
                    c[1] + jnp.sum(eq.reshape(tq // 8, 8, tq), axis=0))
        z = jnp.zeros((8, tq), F32)
        c = lax.fori_loop(0, i + 1, body, (z, z))
        return jnp.sum(c[0], axis=0, keepdims=True), jnp.sum(c[1], axis=0, keepdims=True)

    n_gt, n_eq = count_gt_eq()
    need = kf - n_gt
    tied = (n_gt + n_eq > kf) & (thr > jnp.int32(KEY_NEG_INF))
    has_tie = jnp.max(jnp.where(tied, 1.0, 0.0)) > 0.0
    tie_ref[...] = jnp.zeros(tie_ref.shape, F32)

    m_ref[...] = jnp.full(m_ref.shape, -jnp.inf, F32)
    l_ref[...] = jnp.zeros(l_ref.shape, F32)
    acc_ref[...] = jnp.zeros(acc_ref.shape, F32)

    def att_body(kc, carry):
        off = pl.multiple_of(kc * tq, tq)
        ckv = ckv_ref[0, pl.ds(off, tq), :]
        ckvt = ckvt_ref[0, kc]
        key = key_ref[kc]
        causal = key > jnp.int32(KEY_NEG_INF)

        @pl.when(jnp.logical_not(has_tie))
        def _():
            madd_ref[...] = jnp.where((key >= thr) & causal, 0.0, -jnp.inf)

        @pl.when(has_tie)
        def _():
            eq = key == thr
            eqf = jnp.where(eq, 1.0, 0.0)
            before = (lax.broadcasted_iota(I32, (tq, tq), 1) < lax.broadcasted_iota(I32, (tq, tq), 0))
            rank = jnp.dot(jnp.where(before, 1.0, 0.0).astype(BF16), eqf.astype(BF16),
                           preferred_element_type=F32) + tie_ref[...]
            keep = (key > thr) | (eq & (rank < need))
            madd_ref[...] = jnp.where(keep & causal, 0.0, -jnp.inf)
            tie_ref[...] = tie_ref[...] + jnp.sum(eqf, axis=0, keepdims=True)

        madd = madd_ref[...]
        d = jnp.minimum(i - kc, 2)
        for h in range(nh):
            s = jnp.dot(ckv, qlt_ref[:, h * tq:(h + 1) * tq], preferred_element_type=F32)
            s = s + (bias_ref[d, h] + madd)
            m_old = m_ref[h:h + 1, :]
            m_new = jnp.maximum(m_old, jnp.max(s, axis=0, keepdims=True))
            m_safe = jnp.where(m_new == -jnp.inf, 0.0, m_new)
            alpha = jnp.exp2(m_old - m_safe)
            p = jnp.exp2(s - m_safe)
            l_ref[h:h + 1, :] = alpha * l_ref[h:h + 1, :] + jnp.sum(p, axis=0, keepdims=True)
            acc_ref[h] = alpha * acc_ref[h] + jnp.dot(ckvt, p.astype(BF16), preferred_element_type=F32)
            m_ref[h:h + 1, :] = m_new
        return carry

    lax.fori_loop(0, i + 1, att_body, 0)

    outs = []
    for h in range(nh):
        o_lat = (acc_ref[h] / l_ref[h:h + 1, :]).astype(BF16)
        outs.append(jnp.transpose(jnp.dot(wuvt_ref[h], o_lat, preferred_element_type=F32)))
    o = jnp.concatenate(outs, axis=1)
    o = o * lax.rsqrt(jnp.mean(o * o, axis=-1, keepdims=True) + EPS) * g_ref[...]
    o_ref[...] = o.astype(o_ref.dtype)


def _dsa(g1, aux, ik_lo, ik_hi, ckv_n, w_uk, w_uv, bias_tab, g, bsz, L, tq):
    t = bsz * L
    nq = L // tq
    topk = min(IDX_TOPK_MAX, L // 4)
    aux_blk = A_KV_RANK // LANES
    kern = functools.partial(_dsa_kernel, tq=tq, topk=topk)
    width = A_HEADS * A_HEAD_DIM
    ckv3 = ckv_n.reshape(bsz, L, A_KV_RANK)
    ckvt = ckv_n.reshape(bsz, nq, tq, A_KV_RANK).transpose(0, 1, 3, 2)
    return pl.pallas_call(
        kern,
        grid=(bsz, nq),
        in_specs=[pl.BlockSpec((tq, width), lambda b, i: (b * nq + i, 0)),
                  pl.BlockSpec((tq, IDX_HEADS * IDX_DIM), lambda b, i: (b * nq + i, 1)),
                  pl.BlockSpec((tq, LANES), lambda b, i: (b * nq + i, aux_blk)),
                  pl.BlockSpec((1, L, LANES), lambda b, i: (b, 0, 0)),
                  pl.BlockSpec((1, L, LANES), lambda b, i: (b, 0, 0)),
                  pl.BlockSpec((1, L, A_KV_RANK), lambda b, i: (b, 0, 0)),
                  pl.BlockSpec((1, nq, A_KV_RANK, tq), lambda b, i: (b, 0, 0, 0)),
                  pl.BlockSpec((A_HEADS, A_KV_RANK, A_HEAD_DIM), lambda b, i: (0, 0, 0)),
                  pl.BlockSpec((A_HEADS, A_HEAD_DIM, A_KV_RANK), lambda b, i: (0, 0, 0)),
                  pl.BlockSpec((3, A_HEADS, tq, tq), lambda b, i: (0, 0, 0, 0)),
                  pl.BlockSpec((1, width), lambda b, i: (0, 0))],
        out_specs=pl.BlockSpec((tq, width), lambda b, i: (b * nq + i, 0)),
        out_shape=jax.ShapeDtypeStruct((t, width), BF16),
        scratch_shapes=[pltpu.VMEM((LANES, IDX_HEADS // 2 * tq), BF16),
                        pltpu.VMEM((IDX_HEADS, tq), F32),
                        pltpu.VMEM((nq, tq, tq), I32),
                        pltpu.VMEM((A_KV_RANK, A_HEADS * tq), BF16),
                        pltpu.VMEM((A_HEADS, tq), F32),
                        pltpu.VMEM((A_HEADS, tq), F32),
                        pltpu.VMEM((A_HEADS, A_KV_RANK, tq), F32),
                        pltpu.VMEM((1, tq), F32),
                        pltpu.VMEM((tq, tq), F32)],
        compiler_params=_cparams(("parallel", "arbitrary")),
    )(g1, g1, aux, ik_lo.reshape(bsz, L, LANES), ik_hi.reshape(bsz, L, LANES),
      ckv3, ckvt, w_uk, jnp.transpose(w_uv, (0, 2, 1)), bias_tab, g.reshape(1, width))


def _hgrn_kernel(q_ref, i_ref, gate_ref, f_ref, lbl_ref, ng_ref, o_ref, st_ref, bc_ref, kk_ref, sc_ref,
                 *, chunk, rblk):
    @pl.when(pl.program_id(1) == 0)
    def _():
        st_ref[...] = jnp.zeros(st_ref.shape, F32)

    ll = lbl_ref[...]
    ex = jnp.exp(ll - jnp.max(ll, axis=0, keepdims=True))
    lb_all = ex[0:1] / jnp.sum(ex, axis=0, keepdims=True)

    r_i = lax.broadcasted_iota(I32, (chunk, chunk), 0)
    c_i = lax.broadcasted_iota(I32, (chunk, chunk), 1)
    tri = jnp.where(r_i >= c_i, 1.0, 0.0).astype(BF16)
    row_k = lax.broadcasted_iota(I32, (chunk, B_DIM), 0)
    nblk = chunk // rblk
    zero_row = jnp.zeros((1, B_DIM), F32)

    growth = zero_row
    for h in range(B_HEADS):
        sl = slice(h * B_DIM, (h + 1) * B_DIM)
        lb = lb_all[:, sl]
        f = lb + (1.0 - lb) * jax.nn.sigmoid(f_ref[:, sl])
        lf = jnp.log(f)
        kk_ref[h] = 1.0 - f
        l1 = lf.astype(BF16)
        r1 = lf - l1.astype(F32)
        l2 = r1.astype(BF16)
        l3 = (r1 - l2.astype(F32)).astype(BF16)
        cs = jnp.dot(tri, jnp.concatenate([l1, l2, l3], axis=1), preferred_element_type=F32)
        bc = cs[:, :B_DIM] + cs[:, B_DIM:2 * B_DIM] + cs[:, 2 * B_DIM:]
        bc_ref[h] = bc
        for r in range(nblk):
            top = bc[r * rblk - 1:r * rblk] if r > 0 else zero_row
            growth = jnp.maximum(growth, top - bc[(r + 1) * rblk - 1:(r + 1) * rblk])
    overflow_risk = jnp.max(growth) > HGRN_MAX_BLOCK_DECAY

    def block_scores(h, before_only):
        sl = slice(h * B_DIM, (h + 1) * B_DIM)
        bc = bc_ref[h]
        kk = kk_ref[h]
        q = q_ref[:, sl].astype(F32)
        parts = []
        for r in range(nblk):
            lo, hi = r * rblk, (r + 1) * rblk
            base = bc[lo - 1:lo] if r > 0 else zero_row
            qt = (q[lo:hi] * jnp.exp(bc[lo:hi] - base)).astype(BF16)
            if before_only:
                kt = jnp.where(row_k < lo, kk * jnp.exp(jnp.where(row_k < lo, base - bc, 0.0)), 0.0)
            else:
                kt = kk * jnp.exp(jnp.where(row_k < hi, base - bc, 0.0))
            parts.append(lax.dot_general(qt, kt.astype(BF16), NT_DIMS, preferred_element_type=F32))
        return jnp.concatenate(parts, axis=0)

    @pl.when(jnp.logical_not(overflow_risk))
    def _():
        for h in range(B_HEADS):
            sc_ref[h] = jnp.where(c_i <= r_i, block_scores(h, False), 0.0)

    @pl.when(overflow_risk)
    def _():
        for h in range(B_HEADS):
            sl = slice(h * B_DIM, (h + 1) * B_DIM)
            bc = bc_ref[h]
            kk = kk_ref[h]
            q = q_ref[:, sl].astype(F32)
            sc = block_scores(h, True)
            for dlt in range(rblk):
                bc_s = pltpu.roll(bc, dlt, 0) if dlt else bc
                kk_s = pltpu.roll(kk, dlt, 0) if dlt else kk
                ok = (row_k & (rblk - 1)) >= dlt
                band = q * kk_s * jnp.exp(jnp.where(ok, bc - bc_s, -jnp.inf))
                sc = sc + jnp.where(c_i == r_i - dlt, jnp.sum(band, axis=1, keepdims=True), 0.0)
            sc_ref[h] = sc

    for h in range(B_HEADS):
        sl = slice(h * B_DIM, (h + 1) * B_DIM)
        bc = bc_ref[h]
        kk = kk_ref[h]
        q = q_ref[:, sl].astype(F32)
        v = i_ref[:, sl]
        st = st_ref[h]
        o = lax.dot_general((q * jnp.exp(bc)).astype(BF16), st.astype(BF16), NT_DIMS,
                            preferred_element_type=F32)
        o = o + jnp.dot(sc_ref[h].astype(BF16), v, preferred_element_type=F32)

        last = bc[chunk - 1:chunk]
        kd = (kk * jnp.exp(last - bc)).astype(BF16)
        st_ref[h] = st * jnp.exp(last) + lax.dot_general(v, kd, TN_DIMS, preferred_element_type=F32)

        y = o * lax.rsqrt(jnp.mean(o * o, axis=-1, keepdims=True) + EPS) * ng_ref[:, sl]
        o_ref[:, sl] = (y * _silu(gate_ref[:, sl].astype(F32))).astype(o_ref.dtype)


def _hgrn(g1, fb, lb_logits, ng, bsz, L, chunk=TILES.hgrn_chunk, rblk=TILES.hgrn_rows):
    assert rblk & (rblk - 1) == 0 and chunk % rblk == 0
    t = bsz * L
    nc = L // chunk
    width = B_HEADS * B_DIM
    kern = functools.partial(_hgrn_kernel, chunk=chunk, rblk=rblk)
    return pl.pallas_call(
        kern,
        grid=(bsz, nc),
        in_specs=[pl.BlockSpec((chunk, width), lambda b, c: (b * nc + c, 2)),
                  pl.BlockSpec((chunk, width), lambda b, c: (b * nc + c, 3)),
                  pl.BlockSpec((chunk, width), lambda b, c: (b * nc + c, 4)),
                  pl.BlockSpec((chunk, width), lambda b, c: (b * nc + c, 0)),
                  pl.BlockSpec(lb_logits.shape, lambda b, c: (0, 0)),
                  pl.BlockSpec((1, width), lambda b, c: (0, 0))],
        out_specs=pl.BlockSpec((chunk, width), lambda b, c: (b * nc + c, 0)),
        out_shape=jax.ShapeDtypeStruct((t, width), BF16),
        scratch_shapes=[pltpu.VMEM((B_HEADS, B_DIM, B_DIM), F32),
                        pltpu.VMEM((B_HEADS, chunk, B_DIM), F32),
                        pltpu.VMEM((B_HEADS, chunk, B_DIM), F32),
                        pltpu.VMEM((B_HEADS, chunk, chunk), F32)],
        compiler_params=_cparams(("parallel", "arbitrary")),
    )(g1, g1, g1, fb, lb_logits, ng.reshape(1, width))


def _out_kernel(oa_ref, ob_ref, x_ref, wa_ref, wb_ref, mod_ref, g_ref, wr_ref, x1_ref, h2_ref, lg_ref):
    mix = jnp.dot(oa_ref[...], wa_ref[...], preferred_element_type=F32)
    mix = mix + jnp.dot(ob_ref[...], wb_ref[...], preferred_element_type=F32)
    x1 = x_ref[...] + mod_ref[0, 2:3, :] * mix
    x1_ref[...] = x1
    y = x1 * lax.rsqrt(jnp.mean(x1 * x1, axis=-1, keepdims=True) + EPS) * g_ref[...]
    h2 = y * (1.0 + mod_ref[0, 4:5, :]) + mod_ref[0, 3:4, :]
    h2_ref[...] = _pack_halves(h2)
    lg_ref[...] = lax.dot_general(wr_ref[...], h2.astype(BF16), NT_DIMS, preferred_element_type=F32)


def _out(oa, ob, x2d, w_out, mod3, g, w_router_t, L, tm=TILES.out_rows):
    t, d = x2d.shape
    half = oa.shape[1]
    ne = w_router_t.shape[0]
    return pl.pallas_call(
        _out_kernel,
        grid=(t // tm,),
        in_specs=[pl.BlockSpec((tm, half), lambda i: (i, 0)),
                  pl.BlockSpec((tm, half), lambda i: (i, 0)),
                  pl.BlockSpec((tm, d), lambda i: (i, 0)),
                  pl.BlockSpec((half, d), lambda i: (0, 0)),
                  pl.BlockSpec((half, d), lambda i: (1, 0)),
                  pl.BlockSpec((1, 6, d), lambda i: (i * tm // L, 0, 0)),
                  pl.BlockSpec((1, d), lambda i: (0, 0)),
                  pl.BlockSpec((ne, d), lambda i: (0, 0))],
        out_specs=[pl.BlockSpec((tm, d), lambda i: (i, 0)),
                   pl.BlockSpec((tm, d // 2), lambda i: (i, 0)),
                   pl.BlockSpec((ne, tm), lambda i: (0, i))],
        out_shape=[jax.ShapeDtypeStruct((t, d), F32),
                   jax.ShapeDtypeStruct((t, d // 2), U32),
                   jax.ShapeDtypeStruct((ne, t), F32)],
        compiler_params=_cparams(("parallel",)),
    )(oa, ob, x2d, w_out, w_out, mod3, g.reshape(1, d), w_router_t)


def _rows_to_tile(rows, nrow):
    n = rows[0].shape[1]
    ridx = lax.broadcasted_iota(I32, (nrow, n), 0)
    out = jnp.zeros((nrow, n), rows[0].dtype)
    for r, v in enumerate(rows):
        out = jnp.where(ridx == r, jnp.broadcast_to(v, (nrow, n)), out)
    return out


def _route_kernel(lg_ref, rb_ref, eidx_ref, ew_ref, rank_ref, cnt_ref, run_ref):
    @pl.when(pl.program_id(0) == 0)
    def _():
        run_ref[...] = jnp.zeros(run_ref.shape, F32)

    ne, tt = lg_ref.shape
    per = ne // N_GROUPS
    sc = jax.nn.sigmoid(lg_ref[...])
    ch = sc + rb_ref[...]
    neg = -jnp.inf

    sub = lax.broadcasted_iota(I32, (per, tt), 0).astype(F32)
    gsc = []
    for g in range(N_GROUPS):
        cg = ch[g * per:(g + 1) * per]
        m1 = jnp.max(cg, axis=0, keepdims=True)
        first = jnp.min(jnp.where(cg == m1, sub, float(per)), axis=0, keepdims=True)
        m2 = jnp.max(jnp.where(sub == first, neg, cg), axis=0, keepdims=True)
        gsc.append(m1 + m2)
    grp = _rows_to_tile(gsc, N_GROUPS)

    gid = lax.broadcasted_iota(I32, (N_GROUPS, tt), 0).astype(F32)
    gsel = jnp.zeros((N_GROUPS, tt), F32)
    for _ in range(TOPK_GROUPS):
        mx = jnp.max(grp, axis=0, keepdims=True)
        gi = jnp.min(jnp.where(grp == mx, gid, float(N_GROUPS)), axis=0, keepdims=True)
        pick = gid == gi
        gsel = jnp.where(pick, 1.0, gsel)
        grp = jnp.where(pick, neg, grp)

    eid = lax.broadcasted_iota(I32, (ne, tt), 0).astype(F32)
    cm = jnp.full((ne, tt), neg, F32)
    for g in range(N_GROUPS):
        in_g = (eid >= float(g * per)) & (eid < float((g + 1) * per))
        cm = jnp.where(in_g & (jnp.broadcast_to(gsel[g:g + 1], (ne, tt)) > 0.5), ch, cm)

    idx_rows, w_rows = [], []
    onehot = jnp.zeros((ne, tt), F32)
    for _ in range(TOP_K):
        mx = jnp.max(cm, axis=0, keepdims=True)
        ei = jnp.min(jnp.where(cm == mx, eid, float(ne)), axis=0, keepdims=True)
        pick = eid == ei
        idx_rows.append(ei)
        w_rows.append(jnp.sum(jnp.where(pick, sc, 0.0), axis=0, keepdims=True))
        onehot = jnp.where(pick, 1.0, onehot)
        cm = jnp.where(pick, neg, cm)
    wsum = w_rows[0]
    for w in w_rows[1:]:
        wsum = wsum + w
    w_rows = [w / wsum * ROUTED_SCALE for w in w_rows]

    a_i = lax.broadcasted_iota(I32, (tt, tt), 0)
    b_i = lax.broadcasted_iota(I32, (tt, tt), 1)
    upper = jnp.where(a_i < b_i, 1.0, 0.0).astype(BF16)
    rank_full = jnp.dot(onehot.astype(BF16), upper, preferred_element_type=F32) + run_ref[...]
    r_rows = [jnp.sum(jnp.where(eid == ei, rank_full, 0.0), axis=0, keepdims=True) for ei in idx_rows]
    run = run_ref[...] + jnp.sum(onehot, axis=1, keepdims=True)
    run_ref[...] = run

    eidx_ref[...] = _rows_to_tile(idx_rows, TOP_K).astype(I32)
    ew_ref[...] = _rows_to_tile(w_rows, TOP_K)
    rank_ref[...] = _rows_to_tile(r_rows, TOP_K).astype(I32)
    cnt_ref[...] = jnp.broadcast_to(run, cnt_ref.shape)


def _route(logits_t, router_bias, tt=TILES.route_tokens):
    ne, t = logits_t.shape
    return pl.pallas_call(
        _route_kernel,
        grid=(t // tt,),
        in_specs=[pl.BlockSpec((ne, tt), lambda i: (0, i)),
                  pl.BlockSpec((ne, 1), lambda i: (0, 0))],
        out_specs=[pl.BlockSpec((TOP_K, tt), lambda i: (0, i)),
                   pl.BlockSpec((TOP_K, tt), lambda i: (0, i)),
                   pl.BlockSpec((TOP_K, tt), lambda i: (0, i)),
                   pl.BlockSpec((ne, LANES), lambda i: (0, 0))],
        out_shape=[jax.ShapeDtypeStruct((TOP_K, t), I32),
                   jax.ShapeDtypeStruct((TOP_K, t), F32),
                   jax.ShapeDtypeStruct((TOP_K, t), I32),
                   jax.ShapeDtypeStruct((ne, LANES), F32)],
        scratch_shapes=[pltpu.VMEM((ne, 1), F32)],
        compiler_params=_cparams(("arbitrary",)),
    )(logits_t, router_bias.reshape(ne, 1))


def _dest_kernel(eidx_ref, rank_ref, ps_ref, o_ref):
    ne = ps_ref.shape[0]
    tt = eidx_ref.shape[1]
    eid = lax.broadcasted_iota(I32, (ne, tt), 0)
    ps = jnp.broadcast_to(ps_ref[...], (ne, tt))
    rows = []
    for k in range(TOP_K):
        start = jnp.sum(jnp.where(eid == eidx_ref[k:k + 1, :], ps, 0.0), axis=0, keepdims=True)
        rows.append(start + rank_ref[k:k + 1, :].astype(F32))
    o_ref[...] = _rows_to_tile(rows, TOP_K).astype(I32)


def _dest(eidx, rank, pad_start, tt=TILES.dest_tokens):
    t = eidx.shape[1]
    tt = min(tt, t)
    ne = pad_start.shape[0]
    return pl.pallas_call(
        _dest_kernel,
        grid=(t // tt,),
        in_specs=[pl.BlockSpec((TOP_K, tt), lambda i: (0, i)),
                  pl.BlockSpec((TOP_K, tt), lambda i: (0, i)),
                  pl.BlockSpec((ne, 1), lambda i: (0, 0))],
        out_specs=pl.BlockSpec((TOP_K, tt), lambda i: (0, i)),
        out_shape=jax.ShapeDtypeStruct((TOP_K, t), I32),
        compiler_params=_cparams(("parallel",)),
    )(eidx, rank, pad_start.astype(F32).reshape(ne, 1))


def _dispatch_kernel(pend_ref, padded_ref, dest_ref, h_ref, xs_ref, zbuf_ref, zsem, sem, *, td, bm):
    i = pl.program_id(0)

    def tail_copy(e):
        start = pl.multiple_of(pend_ref[e] - bm, bm)
        return pltpu.make_async_copy(zbuf_ref, xs_ref.at[pl.ds(start, bm)], zsem)

    @pl.when(i == 0)
    def _():
        zbuf_ref[...] = jnp.zeros(zbuf_ref.shape, zbuf_ref.dtype)

        def start_body(e, c):
            @pl.when(padded_ref[e] > 0)
            def _():
                tail_copy(e).start()
            return c

        def wait_body(e, c):
            @pl.when(padded_ref[e] > 0)
            def _():
                tail_copy(e).wait()
            return c

        lax.fori_loop(0, N_EXPERTS, start_body, 0)
        lax.fori_loop(0, N_EXPERTS, wait_body, 0)

        def unused_copy(b):
            return pltpu.make_async_copy(zbuf_ref, xs_ref.at[pl.ds(pl.multiple_of(b * bm, bm), bm)], zsem)

        def ustart_body(b, c):
            unused_copy(b).start()
            return c

        def uwait_body(b, c):
            unused_copy(b).wait()
            return c

        first_unused = pend_ref[N_EXPERTS - 1] // bm
        lax.fori_loop(first_unused, xs_ref.shape[0] // bm, ustart_body, 0)
        lax.fori_loop(first_unused, xs_ref.shape[0] // bm, uwait_body, 0)

    for j in range(td):
        for k in range(TOP_K):
            dst = xs_ref.at[dest_ref[j * TOP_K + k]]
            pltpu.make_async_copy(h_ref.at[j], dst, sem).start(priority=k % 2)
    for _ in range(TOP_K):
        pltpu.make_async_copy(h_ref, xs_ref.at[pl.ds(0, td)], sem).wait()


def _dispatch(pad_end, padded, dest_flat, h2p, n_rows, bm, td=TILES.move_tokens):
    t, w = h2p.shape
    kern = functools.partial(_dispatch_kernel, td=td, bm=bm)
    return pl.pallas_call(
        kern,
        grid_spec=pltpu.PrefetchScalarGridSpec(
            num_scalar_prefetch=2,
            grid=(t // td,),
            in_specs=[pl.BlockSpec((td * TOP_K,), lambda i, *_: (i,), memory_space=pltpu.SMEM),
                      pl.BlockSpec((td, w), lambda i, *_: (i, 0))],
            out_specs=pl.BlockSpec(memory_space=pl.ANY),
            scratch_shapes=[pltpu.VMEM((bm, w), U32),
                            pltpu.SemaphoreType.DMA(()),
                            pltpu.SemaphoreType.DMA(())]),
        out_shape=jax.ShapeDtypeStruct((n_rows, w), U32),
        compiler_params=_cparams(("arbitrary",)),
    )(pad_end, padded, dest_flat, h2p)


def _ffn(xw, wg_ref, wu_ref, wd_ref):
    half = xw.shape[1]
    left, right = _unpack_halves(xw)
    left = left.astype(BF16)
    right = right.astype(BF16)

    def proj(w_ref):
        return (jnp.dot(left, w_ref[:half, :], preferred_element_type=F32)
                + jnp.dot(right, w_ref[half:, :], preferred_element_type=F32))

    act = (_silu(proj(wg_ref)) * proj(wu_ref)).astype(BF16)
    return jnp.dot(act, wd_ref[...], preferred_element_type=F32)


def _expert_kernel(blk_ref, eid_ref, first_ref, slot_ref, nxt_ref, more_ref, nvb_ref,
                   x_ref, wg_hbm, wu_hbm, wd_hbm, o_ref,
                   wg_f, wu_f, wd_f, wg_s, wu_s, wd_s, sems):
    i = pl.program_id(0)

    def weight_copies(e, slot):
        return (pltpu.make_async_copy(wg_hbm.at[e], wg_f.at[slot], sems.at[slot]),
                pltpu.make_async_copy(wu_hbm.at[e], wu_f.at[slot], sems.at[slot]),
                pltpu.make_async_copy(wd_hbm.at[e], wd_f.at[slot], sems.at[slot]))

    @pl.when(i == 0)
    def _():
        for cp in weight_copies(eid_ref[0], 0):
            cp.start()

    @pl.when(first_ref[i] == 1)
    def _():
        slot = slot_ref[i]
        for cp in weight_copies(eid_ref[i], slot):
            cp.wait()

        @pl.when(more_ref[i] == 1)
        def _():
            for cp in weight_copies(nxt_ref[i], 1 - slot):
                cp.start()

    @pl.when(first_ref[i] == 1)
    def _():
        slot = slot_ref[i]
        wg_s[...] = wg_f[slot].astype(BF16)
        wu_s[...] = wu_f[slot].astype(BF16)
        wd_s[...] = wd_f[slot].astype(BF16)
        o_ref[...] = _pack_halves(_ffn(x_ref[...], wg_s, wu_s, wd_s))

    @pl.when((first_ref[i] == 0) & (i < nvb_ref[0]))
    def _():
        o_ref[...] = _pack_halves(_ffn(x_ref[...], wg_s, wu_s, wd_s))

    @pl.when(i >= nvb_ref[0])
    def _():
        o_ref[...] = jnp.zeros(o_ref.shape, o_ref.dtype)


def _experts(blk, eid, first, slot, nxt, more, nvb, xs, wg, wu, wd, bm):
    n_rows, w = xs.shape
    ne, d, f = wg.shape
    return pl.pallas_call(
        _expert_kernel,
        grid_spec=pltpu.PrefetchScalarGridSpec(
            num_scalar_prefetch=7,
            grid=(n_rows // bm,),
            in_specs=[pl.BlockSpec((bm, w), lambda i, blk, *_: (blk[i], 0)),
                      pl.BlockSpec(memory_space=pl.ANY),
                      pl.BlockSpec(memory_space=pl.ANY),
                      pl.BlockSpec(memory_space=pl.ANY)],
            out_specs=pl.BlockSpec((bm, w), lambda i, *_: (i, 0)),
            scratch_shapes=[pltpu.VMEM((2, d, f), F32), pltpu.VMEM((2, d, f), F32), pltpu.VMEM((2, f, d), F32),
                            pltpu.VMEM((d, f), BF16), pltpu.VMEM((d, f), BF16), pltpu.VMEM((f, d), BF16),
                            pltpu.SemaphoreType.DMA((2,))]),
        out_shape=jax.ShapeDtypeStruct((n_rows, w), U32),
        compiler_params=_cparams(("arbitrary",)),
    )(blk, eid, first, slot, nxt, more, nvb, xs, wg, wu, wd)


def _combine_kernel(dest_ref, dnext_ref, y_ref, h_ref, x1_ref, ew_ref, wg_ref, wu_ref, wd_ref, mod_ref, g_ref,
                    o_ref, gbuf_a, gbuf_b, sems, *, tc):
    i = pl.program_id(0)
    last = pl.num_programs(0) - 1

    def issue(idx_ref, gbuf, sem):
        for j in range(tc):
            for k in range(TOP_K):
                src = y_ref.at[idx_ref[j * TOP_K + k]]
                pltpu.make_async_copy(src, gbuf.at[k, j], sem).start(priority=k % 2)

    def wait_all(gbuf, sem):
        for k in range(TOP_K):
            pltpu.make_async_copy(y_ref.at[pl.ds(0, tc)], gbuf.at[k], sem).wait()

    def compute(gbuf):
        shared = _ffn(h_ref[...], wg_ref, wu_ref, wd_ref)
        half = h_ref.shape[1]
        ew = ew_ref[...]
        left = shared[:, :half]
        right = shared[:, half:]
        for k in range(TOP_K):
            yl, yr = _unpack_halves(gbuf[k])
            wk = ew[:, k:k + 1]
            left = left + wk * yl
            right = right + wk * yr
        x2 = x1_ref[...] + mod_ref[0, 5:6, :] * jnp.concatenate([left, right], axis=1)
        o_ref[...] = x2 * lax.rsqrt(jnp.mean(x2 * x2, axis=-1, keepdims=True) + EPS) * g_ref[...]

    @pl.when(i == 0)
    def _():
        def group_body(g, c):
            base = pl.multiple_of(g * SUBLANES, SUBLANES)
            for jj in range(SUBLANES):
                for k in range(TOP_K):
                    src = y_ref.at[dest_ref[(base + jj) * TOP_K + k]]
                    pltpu.make_async_copy(src, gbuf_a.at[k, base + jj], sems.at[0]).start(priority=k % 2)
            return c

        lax.fori_loop(0, tc // SUBLANES, group_body, 0)

    @pl.when(i % 2 == 0)
    def _():
        wait_all(gbuf_a, sems.at[0])
        issue(dnext_ref, gbuf_b, sems.at[1])
        compute(gbuf_a)

    @pl.when(i % 2 == 1)
    def _():
        wait_all(gbuf_b, sems.at[1])
        issue(dnext_ref, gbuf_a, sems.at[0])
        compute(gbuf_b)

    @pl.when((i == last) & (i % 2 == 0))
    def _():
        wait_all(gbuf_b, sems.at[1])

    @pl.when((i == last) & (i % 2 == 1))
    def _():
        wait_all(gbuf_a, sems.at[0])


def _combine(dest_flat, y, h2p, x1, ew_t, wsg, wsu, wsd, mod3, g, L, tc=TILES.move_tokens):
    t, d = x1.shape
    w = h2p.shape[1]
    nt = t // tc
    kern = functools.partial(_combine_kernel, tc=tc)
    return pl.pallas_call(
        kern,
        grid=(nt,),
        in_specs=[pl.BlockSpec((tc * TOP_K,), lambda i: (i,), memory_space=pltpu.SMEM),
                  pl.BlockSpec((tc * TOP_K,), lambda i: (jnp.minimum(i + 1, nt - 1),), memory_space=pltpu.SMEM),
                  pl.BlockSpec(memory_space=pl.ANY),
                  pl.BlockSpec((tc, w), lambda i: (i, 0)),
                  pl.BlockSpec((tc, d), lambda i: (i, 0)),
                  pl.BlockSpec((tc, TOP_K), lambda i: (i, 0)),
                  pl.BlockSpec(wsg.shape, lambda i: (0, 0)),
                  pl.BlockSpec(wsu.shape, lambda i: (0, 0)),
                  pl.BlockSpec(wsd.shape, lambda i: (0, 0)),
                  pl.BlockSpec((1, 6, d), lambda i: (i * tc // L, 0, 0)),
                  pl.BlockSpec((1, d), lambda i: (0, 0))],
        out_specs=pl.BlockSpec((tc, d), lambda i: (i, 0)),
        out_shape=jax.ShapeDtypeStruct((t, d), F32),
        scratch_shapes=[pltpu.VMEM((TOP_K, tc, w), U32),
                        pltpu.VMEM((TOP_K, tc, w), U32),
                        pltpu.SemaphoreType.DMA((2,))],
        compiler_params=_cparams(("arbitrary",)),
    )(dest_flat, dest_flat, y, h2p, x1, ew_t, wsg, wsu, wsd, mod3, g.reshape(1, d))


def _split_cols(w, sizes):
    out, off = [], 0
    for s in sizes:
        out.append(w[:, off:off + s])
        off += s
    return out


def kernel(x, c, w_ada, b_ada, norm1_g, w_in, ckv_norm_g, idx_k_norm_g, w_uk, w_uv, rel_bias, lb_logits,
           attn_out_norm_g, hgrn_out_norm_g, w_out, norm2_g, w_router, router_bias, w_e_gate, w_e_up,
           w_e_down, w_s_gate, w_s_up, w_s_down, final_norm_g):
    bsz, L, d = x.shape
    t = bsz * L
    assert w_ada.shape[0] == 1, "single-layer block"
    a_width = A_HEADS * A_HEAD_DIM
    b_width = B_HEADS * B_DIM
    sizes = (a_width, A_KV_RANK, IDX_HEADS * IDX_DIM, IDX_DIM, IDX_HEADS, b_width, b_width, b_width, b_width)
    assert w_in.shape[2] == sum(sizes)

    wq_a, wckv, wiq, wik, wiw, wq_b, wf_b, wi_b, wg_b = _split_cols(w_in[0], sizes)
    w_main = jnp.concatenate([wq_a, wiq, wq_b, wi_b, wg_b], axis=1).astype(BF16)
    w_f = wf_b.astype(BF16)
    aux_pad = LANES - IDX_DIM - IDX_HEADS
    w_aux = jnp.concatenate([wckv, wik, wiw, jnp.zeros((d, aux_pad), F32)], axis=1).astype(BF16)

    mod3 = _ada(c, w_ada[0], b_ada[0]).reshape(bsz, 6, d)
    h1 = _norm1(x, mod3, norm1_g[0]).reshape(t, d)
    g1 = _matmul(h1, w_main, BF16, tm=TILES.proj_rows, tn=TILES.proj_cols)
    fb = _matmul(h1, w_f, F32, tm=TILES.proj_rows, tn=TILES.proj_cols)
    aux = _matmul(h1, w_aux, F32, tm=TILES.proj_rows, tn=w_aux.shape[1])
    ckv_n, ik_lo, ik_hi = _kvnorm(aux, ckv_norm_g[0], idx_k_norm_g[0])

    tq = min(TILES.attn, L)
    o_a = _dsa(g1, aux, ik_lo, ik_hi, ckv_n, w_uk[0].astype(BF16), w_uv[0].astype(BF16),
               _bias_tables(rel_bias, tq), attn_out_norm_g[0], bsz, L, tq)
    o_b = _hgrn(g1, fb, lb_logits, hgrn_out_norm_g[0], bsz, L)

    x1, h2p, logits_t = _out(o_a, o_b, x.reshape(t, d), w_out[0].astype(BF16), mod3, norm2_g[0],
                             w_router[0].T.astype(BF16), L)

    eidx, ew, rank, cnt = _route(logits_t, router_bias[0])

    bm = TILES.moe_rows
    counts = cnt[:, 0].astype(I32)
    padded = (counts + bm - 1) // bm * bm
    pad_end = jnp.cumsum(padded)
    pad_start = pad_end - padded
    n_rows = (t * TOP_K + N_EXPERTS * (bm - 1) + bm - 1) // bm * bm
    nb = n_rows // bm
    nvb = pad_end[-1] // bm
    blk = jnp.minimum(jnp.arange(nb, dtype=I32), nvb - 1)
    eid = jnp.minimum(jnp.sum((pad_end[None, :] <= (blk * bm)[:, None]).astype(I32), axis=1), N_EXPERTS - 1)
    ar = jnp.arange(nb, dtype=I32)
    first = ((ar < nvb) & ((ar == 0) | (eid != jnp.roll(eid, 1)))).astype(I32)
    slot = (jnp.cumsum(first) - 1) % 2
    nxt_blk = pad_end[eid] // bm
    more = (nxt_blk < nvb).astype(I32)
    nxt = eid[jnp.minimum(nxt_blk, nb - 1)]

    dest = _dest(eidx, rank, pad_start)
    dest_flat = dest.T.reshape(t * TOP_K)
    xs = _dispatch(pad_end.astype(I32), padded.astype(I32), dest_flat, h2p, n_rows, bm)

    y = _experts(blk, eid, first, slot.astype(I32), nxt.astype(I32), more, nvb.reshape(1).astype(I32), xs,
                 w_e_gate[0], w_e_up[0], w_e_down[0], bm)

    out = _combine(dest_flat, y, h2p, x1, ew.T, w_s_gate[0].astype(BF16), w_s_up[0].astype(BF16),
                   w_s_down[0].astype(BF16), mod3, final_norm_g, L)
    return out.reshape(bsz, L, d)
```

```python
import functools
import math
from typing import NamedTuple

import jax
import jax.numpy as jnp
from jax import lax
from jax.experimental import pallas as pl
from jax.experimental.pallas import tpu as pltpu

F32 = jnp.float32
BF16 = jnp.bfloat16
I32 = jnp.int32
U32 = jnp.uint32

EPS = 1e-6
A_HEADS = 8
A_HEAD_DIM = 128
A_KV_RANK = 256
IDX_HEADS = 16
IDX_DIM = 64
IDX_TOPK_MAX = 256
B_HEADS = 8
B_DIM = 128
REL_BUCKETS = 32
REL_MAX_DIST = 128
N_EXPERTS = 64
TOP_K = 8
N_GROUPS = 8
TOPK_GROUPS = 4
ROUTED_SCALE = 2.5

VMEM_LIMIT_BYTES = 56 * 1024 * 1024
LANES = 128
SUBLANES = 8


class _Tiles(NamedTuple):
    ada_cols: int = 1024
    norm_rows: int = 512
    proj_rows: int = 1024
    proj_cols: int = 512
    attn: int = 256
    hgrn_chunk: int = 128
    hgrn_rows: int = 32
    out_rows: int = 512
    route_tokens: int = 512
    dest_tokens: int = 2048
    moe_rows: int = 256
    move_tokens: int = 256


TILES = _Tiles()

NT_DIMS = (((1,), (1,)), ((), ()))
TN_DIMS = (((0,), (0,)), ((), ()))

HGRN_MAX_BLOCK_DECAY = 80.0
LOG2E = math.log2(math.e)
INT_MIN = -2 ** 31
KEY_NEG_INF = -2139095041


def _cparams(sem):
    return pltpu.CompilerParams(dimension_semantics=sem, vmem_limit_bytes=VMEM_LIMIT_BYTES)


def _silu(v):
    return v * jax.nn.sigmoid(v)


def _pack_halves(v):
    n = v.shape[1] // 2
    lo = lax.bitcast_convert_type(v[:, :n].astype(BF16).astype(F32), U32)
    hi = lax.bitcast_convert_type(v[:, n:].astype(BF16).astype(F32), U32)
    return lax.shift_right_logical(lo, jnp.uint32(16)) | (hi & jnp.uint32(0xFFFF0000))


def _unpack_halves(w):
    left = lax.bitcast_convert_type(lax.shift_left(w, jnp.uint32(16)), F32)
    right = lax.bitcast_convert_type(w & jnp.uint32(0xFFFF0000), F32)
    return left, right


def _ada_kernel(c_ref, w_ref, b_ref, o_ref):
    a = _silu(c_ref[...]).astype(BF16)
    o_ref[...] = jnp.dot(a, w_ref[...].astype(BF16), preferred_element_type=F32) + b_ref[...]


def _ada(c, w, b, tn=TILES.ada_cols):
    bsz, d = c.shape
    n = w.shape[1]
    return pl.pallas_call(
        _ada_kernel,
        grid=(n // tn,),
        in_specs=[pl.BlockSpec((bsz, d), lambda j: (0, 0)),
                  pl.BlockSpec((d, tn), lambda j: (0, j)),
                  pl.BlockSpec((1, tn), lambda j: (0, j))],
        out_specs=pl.BlockSpec((bsz, tn), lambda j: (0, j)),
        out_shape=jax.ShapeDtypeStruct((bsz, n), F32),
        compiler_params=_cparams(("arbitrary",)),
    )(c, w, b.reshape(1, n))


def _norm1_kernel(x_ref, mod_ref, g_ref, o_ref):
    x = x_ref[0]
    y = x * lax.rsqrt(jnp.mean(x * x, axis=-1, keepdims=True) + EPS) * g_ref[...]
    sh = mod_ref[0, 0:1, :]
    sc = mod_ref[0, 1:2, :]
    o_ref[0] = (y * (1.0 + sc) + sh).astype(o_ref.dtype)


def _norm1(x, mod3, g, tm=TILES.norm_rows):
    bsz, L, d = x.shape
    return pl.pallas_call(
        _norm1_kernel,
        grid=(bsz, L // tm),
        in_specs=[pl.BlockSpec((1, tm, d), lambda b, i: (b, i, 0)),
                  pl.BlockSpec((1, 6, d), lambda b, i: (b, 0, 0)),
                  pl.BlockSpec((1, d), lambda b, i: (0, 0))],
        out_specs=pl.BlockSpec((1, tm, d), lambda b, i: (b, i, 0)),
        out_shape=jax.ShapeDtypeStruct((bsz, L, d), BF16),
        compiler_params=_cparams(("parallel", "parallel")),
    )(x, mod3, g.reshape(1, d))


def _mm_kernel(a_ref, w_ref, o_ref):
    o_ref[...] = jnp.dot(a_ref[...], w_ref[...], preferred_element_type=F32).astype(o_ref.dtype)


def _matmul(a, w, out_dtype, tm, tn):
    m, k = a.shape
    n = w.shape[1]
    return pl.pallas_call(
        _mm_kernel,
        grid=(m // tm, n // tn),
        in_specs=[pl.BlockSpec((tm, k), lambda i, j: (i, 0)),
                  pl.BlockSpec((k, tn), lambda i, j: (0, j))],
        out_specs=pl.BlockSpec((tm, tn), lambda i, j: (i, j)),
        out_shape=jax.ShapeDtypeStruct((m, n), out_dtype),
        compiler_params=_cparams(("parallel", "arbitrary")),
    )(a, w)


def _kvnorm_kernel(aux_ref, gc_ref, gk_ref, ckv_ref, iklo_ref, ikhi_ref):
    ckv = aux_ref[:, :A_KV_RANK]
    ckv_ref[...] = (ckv * lax.rsqrt(jnp.mean(ckv * ckv, axis=-1, keepdims=True) + EPS)
                    * gc_ref[...]).astype(BF16)
    v = aux_ref[:, A_KV_RANK:A_KV_RANK + LANES]
    lane = lax.broadcasted_iota(I32, v.shape, 1)
    ik = jnp.where(lane < IDX_DIM, v, 0.0)
    ms = jnp.sum(ik * ik, axis=-1, keepdims=True) * (1.0 / IDX_DIM)
    ikn = ik * lax.rsqrt(ms + EPS) * gk_ref[...]
    iklo_ref[...] = ikn.astype(BF16)
    ikhi_ref[...] = pltpu.roll(ikn, IDX_DIM, 1).astype(BF16)


def _kvnorm(aux, gc, gk, tm=TILES.proj_rows):
    t = aux.shape[0]
    gk_pad = jnp.concatenate([gk, jnp.zeros((LANES - IDX_DIM,), F32)]).reshape(1, LANES)
    return pl.pallas_call(
        _kvnorm_kernel,
        grid=(t // tm,),
        in_specs=[pl.BlockSpec((tm, aux.shape[1]), lambda i: (i, 0)),
                  pl.BlockSpec((1, A_KV_RANK), lambda i: (0, 0)),
                  pl.BlockSpec((1, LANES), lambda i: (0, 0))],
        out_specs=[pl.BlockSpec((tm, A_KV_RANK), lambda i: (i, 0)),
                   pl.BlockSpec((tm, LANES), lambda i: (i, 0)),
                   pl.BlockSpec((tm, LANES), lambda i: (i, 0))],
        out_shape=[jax.ShapeDtypeStruct((t, A_KV_RANK), BF16),
                   jax.ShapeDtypeStruct((t, LANES), BF16),
                   jax.ShapeDtypeStruct((t, LANES), BF16)],
        compiler_params=_cparams(("parallel",)),
    )(aux, gc.reshape(1, A_KV_RANK), gk_pad)


def _t5_bucket(rel):
    n = jnp.maximum(rel, 0)
    max_exact = REL_BUCKETS // 2
    n_large = jnp.maximum(n, max_exact).astype(F32)
    large = max_exact + (jnp.log(n_large / max_exact) / math.log(REL_MAX_DIST / max_exact)
                         * (REL_BUCKETS - max_exact)).astype(I32)
    large = jnp.minimum(large, REL_BUCKETS - 1)
    return jnp.where(n < max_exact, n, large)


def _bias_tables(rel_bias, tq):
    assert tq + 1 >= REL_MAX_DIST
    nh = rel_bias.shape[1]
    dist = jnp.maximum(jnp.arange(3 * tq + 1, dtype=I32) - tq, 0)
    v = rel_bias.astype(F32)[_t5_bucket(dist)].T * LOG2E
    n = v.shape[1]
    x = jnp.broadcast_to(v[:, None, :], (nh, tq, n)).reshape(nh, tq * n)[:, :tq * (n - 1)].reshape(nh, tq, n - 1)
    near = x[:, :, tq:2 * tq]
    prev = x[:, :, 2 * tq:3 * tq]
    far = jnp.broadcast_to(v[:, n - 1][:, None, None], near.shape)
    return jnp.stack([near, prev, far])


def _dsa_kernel(qa_ref, iq_ref, aux_ref, iklo_ref, ikhi_ref, ckv_ref, ckvt_ref, wuk_ref, wuvt_ref, bias_ref,
                g_ref, o_ref, iqt_ref, iwt_ref, key_ref, qlt_ref, m_ref, l_ref, acc_ref, tie_ref, madd_ref,
                *, tq, topk):
    i = pl.program_id(1)
    nh = A_HEADS
    npair = IDX_HEADS // 2

    r_i = lax.broadcasted_iota(I32, (LANES, LANES), 0)
    c_i = lax.broadcasted_iota(I32, (LANES, LANES), 1)
    eye = jnp.where(r_i == c_i, 1.0, 0.0).astype(BF16)
    for p in range(npair):
        iqt_ref[:, p * tq:(p + 1) * tq] = lax.dot_general(
            eye, iq_ref[:, p * LANES:(p + 1) * LANES], NT_DIMS, preferred_element_type=F32).astype(BF16)
    iwt_ref[...] = (jnp.transpose(aux_ref[...])[IDX_DIM:IDX_DIM + IDX_HEADS, :]
                    * (IDX_HEADS ** -0.5 * IDX_DIM ** -0.5))
    for h in range(nh):
        ql = lax.dot_general(wuk_ref[h], qa_ref[:, h * A_HEAD_DIM:(h + 1) * A_HEAD_DIM], NT_DIMS,
                             preferred_element_type=F32)
        qlt_ref[:, h * tq:(h + 1) * tq] = (ql * (A_HEAD_DIM ** -0.5 * LOG2E)).astype(BF16)

    kpos = lax.broadcasted_iota(I32, (tq, tq), 0)
    qpos = lax.broadcasted_iota(I32, (tq, tq), 1) + i * tq

    def score_body(kc, carry):
        off = pl.multiple_of(kc * tq, tq)
        klo = iklo_ref[0, pl.ds(off, tq), :]
        khi = ikhi_ref[0, pl.ds(off, tq), :]
        acc = jnp.zeros((tq, tq), F32)
        for p in range(npair):
            rhs = iqt_ref[:, p * tq:(p + 1) * tq]
            se = jnp.dot(klo, rhs, preferred_element_type=F32)
            so = jnp.dot(khi, rhs, preferred_element_type=F32)
            acc = acc + jnp.maximum(se, 0.0) * iwt_ref[2 * p:2 * p + 1, :]
            acc = acc + jnp.maximum(so, 0.0) * iwt_ref[2 * p + 1:2 * p + 2, :]
        bits = lax.bitcast_convert_type(acc, I32)
        key = jnp.where(bits >= 0, bits, bits ^ jnp.int32(0x7FFFFFFF))
        key_ref[kc] = jnp.where(kpos + off <= qpos, key, jnp.int32(KEY_NEG_INF))
        return carry

    lax.fori_loop(0, i + 1, score_body, 0)

    def count_ge(cand):
        def body(kc, c):
            hit = jnp.where(key_ref[kc] >= cand, 1.0, 0.0)
            return c + jnp.sum(hit.reshape(tq // 8, 8, tq), axis=0)
        c = lax.fori_loop(0, i + 1, body, jnp.zeros((8, tq), F32))
        return jnp.sum(c, axis=0, keepdims=True)

    kf = float(topk)
    thr = jnp.where(count_ge(jnp.zeros((1, tq), I32)) >= kf, jnp.int32(0), jnp.int32(INT_MIN))

    def bit_body(j, thr):
        cand = thr | lax.shift_left(jnp.int32(1), 30 - j)
        return jnp.where(count_ge(cand) >= kf, cand, thr)

    thr = lax.fori_loop(0, 31, bit_body, thr)

    def count_gt_eq():
        def body(kc, c):
            key = key_ref[kc]
            gt = jnp.where(key > thr, 1.0, 0.0)
            eq = jnp.where(key == thr, 1.0, 0.0)
            return (c[0] + jnp.sum(gt.reshape(tq // 8, 8, tq), axis=0),
                    c[1] + jnp.sum(eq.reshape(tq // 8, 8, tq), axis=0))
        z = jnp.zeros((8, tq), F32)
        c = lax.fori_loop(0, i + 1, body, (z, z))
        return jnp.sum(c[0], axis=0, keepdims=True), jnp.sum(c[1], axis=0, keepdims=True)

    n_gt, n_eq = count_gt_eq()
    need = kf - n_gt
    tied = (n_gt + n_eq > kf) & (thr > jnp.int32(KEY_NEG_INF))
    has_tie = jnp.max(jnp.where(tied, 1.0, 0.0)) > 0.0
    tie_ref[...] = jnp.zeros(tie_ref.shape, F32)

    m_ref[...] = jnp.full(m_ref.shape, -jnp.inf, F32)
    l_ref[...] = jnp.zeros(l_ref.shape, F32)
    acc_ref[...] = jnp.zeros(acc_ref.shape, F32)

    def att_body(kc, carry):
        off = pl.multiple_of(kc * tq, tq)
        ckv = ckv_ref[0, pl.ds(off, tq), :]
        ckvt = ckvt_ref[0, kc]
        key = key_ref[kc]
        causal = key > jnp.int32(KEY_NEG_INF)

        @pl.when(jnp.logical_not(has_tie))
        def _():
            madd_ref[...] = jnp.where((key >= thr) & causal, 0.0, -jnp.inf)

        @pl.when(has_tie)
        def _():
            eq = key == thr
            eqf = jnp.where(eq, 1.0, 0.0)
            before = (lax.broadcasted_iota(I32, (tq, tq), 1) < lax.broadcasted_iota(I32, (tq, tq), 0))
            rank = jnp.dot(jnp.where(before, 1.0, 0.0).astype(BF16), eqf.astype(BF16),
                           preferred_element_type=F32) + tie_ref[...]
            keep = (key > thr) | (eq & (rank < need))
            madd_ref[...] = jnp.where(keep & causal, 0.0, -jnp.inf)
            tie_ref[...] = tie_ref[...] + jnp.sum(eqf, axis=0, keepdims=True)

        madd = madd_ref[...]
        d = jnp.minimum(i - kc, 2)
        for h in range(nh):
            s = jnp.dot(ckv, qlt_ref[:, h * tq:(h + 1) * tq], preferred_element_type=F32)
            s = s + (bias_ref[d, h] + madd)
            m_old = m_ref[h:h + 1, :]
            m_new = jnp.maximum(m_old, jnp.max(s, axis=0, keepdims=True))
            m_safe = jnp.where(m_new == -jnp.inf, 0.0, m_new)
            alpha = jnp.exp2(m_old - m_safe)
            p = jnp.exp2(s - m_safe)
            l_ref[h:h + 1, :] = alpha * l_ref[h:h + 1, :] + jnp.sum(p, axis=0, keepdims=True)
            acc_ref[h] = alpha * acc_ref[h] + jnp.dot(ckvt, p.astype(BF16), preferred_element_type=F32)
            m_ref[h:h + 1, :] = m_new
        return carry

    lax.fori_loop(0, i + 1, att_body, 0)

    outs = []
    for h in range(nh):
        o_lat = (acc_ref[h] / l_ref[h:h + 1, :]).astype(BF16)
        outs.append(jnp.transpose(jnp.dot(wuvt_ref[h], o_lat, preferred_element_type=F32)))
    o = jnp.concatenate(outs, axis=1)
    o = o * lax.rsqrt(jnp.mean(o * o, axis=-1, keepdims=True) + EPS) * g_ref[...]
    o_ref[...] = o.astype(o_ref.dtype)


def _dsa(g1, aux, ik_lo, ik_hi, ckv_n, w_uk, w_uv, bias_tab, g, bsz, L, tq):
    t = bsz * L
    nq = L // tq
    topk = min(IDX_TOPK_MAX, L // 4)
    aux_blk = A_KV_RANK // LANES
    kern = functools.partial(_dsa_kernel, tq=tq, topk=topk)
    width = A_HEADS * A_HEAD_DIM
    ckv3 = ckv_n.reshape(bsz, L, A_KV_RANK)
    ckvt = ckv_n.reshape(bsz, nq, tq, A_KV_RANK).transpose(0, 1, 3, 2)
    return pl.pallas_call(
        kern,
        grid=(bsz, nq),
        in_specs=[pl.BlockSpec((tq, width), lambda b, i: (b * nq + i, 0)),
                  pl.BlockSpec((tq, IDX_HEADS * IDX_DIM), lambda b, i: (b * nq + i, 1)),
                  pl.BlockSpec((tq, LANES), lambda b, i: (b * nq + i, aux_blk)),
                  pl.BlockSpec((1, L, LANES), lambda b, i: (b, 0, 0)),
                  pl.BlockSpec((1, L, LANES), lambda b, i: (b, 0, 0)),
                  pl.BlockSpec((1, L, A_KV_RANK), lambda b, i: (b, 0, 0)),
                  pl.BlockSpec((1, nq, A_KV_RANK, tq), lambda b, i: (b, 0, 0, 0)),
                  pl.BlockSpec((A_HEADS, A_KV_RANK, A_HEAD_DIM), lambda b, i: (0, 0, 0)),
                  pl.BlockSpec((A_HEADS, A_HEAD_DIM, A_KV_RANK), lambda b, i: (0, 0, 0)),
                  pl.BlockSpec((3, A_HEADS, tq, tq), lambda b, i: (0, 0, 0, 0)),
                  pl.BlockSpec((1, width), lambda b, i: (0, 0))],
        out_specs=pl.BlockSpec((tq, width), lambda b, i: (b * nq + i, 0)),
        out_shape=jax.ShapeDtypeStruct((t, width), BF16),
        scratch_shapes=[pltpu.VMEM((LANES, IDX_HEADS // 2 * tq), BF16),
                        pltpu.VMEM((IDX_HEADS, tq), F32),
                        pltpu.VMEM((nq, tq, tq), I32),
                        pltpu.VMEM((A_KV_RANK, A_HEADS * tq), BF16),
                        pltpu.VMEM((A_HEADS, tq), F32),
                        pltpu.VMEM((A_HEADS, tq), F32),
                        pltpu.VMEM((A_HEADS, A_KV_RANK, tq), F32),
                        pltpu.VMEM((1, tq), F32),
                        pltpu.VMEM((tq, tq), F32)],
        compiler_params=_cparams(("parallel", "arbitrary")),
    )(g1, g1, aux, ik_lo.reshape(bsz, L, LANES), ik_hi.reshape(bsz, L, LANES),
      ckv3, ckvt, w_uk, jnp.transpose(w_uv, (0, 2, 1)), bias_tab, g.reshape(1, width))


def _hgrn_kernel(q_ref, i_ref, gate_ref, f_ref, lbl_ref, ng_ref, o_ref, st_ref, bc_ref, kk_ref, sc_ref,
                 *, chunk, rblk):
    @pl.when(pl.program_id(1) == 0)
    def _():
        st_ref[...] = jnp.zeros(st_ref.shape, F32)

    ll = lbl_ref[...]
    ex = jnp.exp(ll - jnp.max(ll, axis=0, keepdims=True))
    lb_all = ex[0:1] / jnp.sum(ex, axis=0, keepdims=True)

    r_i = lax.broadcasted_iota(I32, (chunk, chunk), 0)
    c_i = lax.broadcasted_iota(I32, (chunk, chunk), 1)
    tri = jnp.where(r_i >= c_i, 1.0, 0.0).astype(BF16)
    row_k = lax.broadcasted_iota(I32, (chunk, B_DIM), 0)
    nblk = chunk // rblk
    zero_row = jnp.zeros((1, B_DIM), F32)

    growth = zero_row
    for h in range(B_HEADS):
        sl = slice(h * B_DIM, (h + 1) * B_DIM)
        lb = lb_all[:, sl]
        f = lb + (1.0 - lb) * jax.nn.sigmoid(f_ref[:, sl])
        lf = jnp.log(f)
        kk_ref[h] = 1.0 - f
        l1 = lf.astype(BF16)
        r1 = lf - l1.astype(F32)
        l2 = r1.astype(BF16)
        l3 = (r1 - l2.astype(F32)).astype(BF16)
        cs = jnp.dot(tri, jnp.concatenate([l1, l2, l3], axis=1), preferred_element_type=F32)
        bc = cs[:, :B_DIM] + cs[:, B_DIM:2 * B_DIM] + cs[:, 2 * B_DIM:]
        bc_ref[h] = bc
        for r in range(nblk):
            top = bc[r * rblk - 1:r * rblk] if r > 0 else zero_row
            growth = jnp.maximum(growth, top - bc[(r + 1) * rblk - 1:(r + 1) * rblk])
    overflow_risk = jnp.max(growth) > HGRN_MAX_BLOCK_DECAY

    def block_scores(h, before_only):
        sl = slice(h * B_DIM, (h + 1) * B_DIM)
        bc = bc_ref[h]
        kk = kk_ref[h]
        q = q_ref[:, sl].astype(F32)
        parts = []
        for r in range(nblk):
            lo, hi = r * rblk, (r + 1) * rblk
            base = bc[lo - 1:lo] if r > 0 else zero_row
            qt = (q[lo:hi] * jnp.exp(bc[lo:hi] - base)).astype(BF16)
            if before_only:
                kt = jnp.where(row_k < lo, kk * jnp.exp(jnp.where(row_k < lo, base - bc, 0.0)), 0.0)
            else:
                kt = kk * jnp.exp(jnp.where(row_k < hi, base - bc, 0.0))
            parts.append(lax.dot_general(qt, kt.astype(BF16), NT_DIMS, preferred_element_type=F32))
        return jnp.concatenate(parts, axis=0)

    @pl.when(jnp.logical_not(overflow_risk))
    def _():
        for h in range(B_HEADS):
            sc_ref[h] = jnp.where(c_i <= r_i, block_scores(h, False), 0.0)

    @pl.when(overflow_risk)
    def _():
        for h in range(B_HEADS):
            sl = slice(h * B_DIM, (h + 1) * B_DIM)
            bc = bc_ref[h]
            kk = kk_ref[h]
            q = q_ref[:, sl].astype(F32)
            sc = block_scores(h, True)
            for dlt in range(rblk):
                bc_s = pltpu.roll(bc, dlt, 0) if dlt else bc
                kk_s = pltpu.roll(kk, dlt, 0) if dlt else kk
                ok = (row_k & (rblk - 1)) >= dlt
                band = q * kk_s * jnp.exp(jnp.where(ok, bc - bc_s, -jnp.inf))
                sc = sc + jnp.where(c_i == r_i - dlt, jnp.sum(band, axis=1, keepdims=True), 0.0)
            sc_ref[h] = sc

    for h in range(B_HEADS):
        sl = slice(h * B_DIM, (h + 1) * B_DIM)
        bc = bc_ref[h]
        kk = kk_ref[h]
        q = q_ref[:, sl].astype(F32)
        v = i_ref[:, sl]
        st = st_ref[h]
        o = lax.dot_general((q * jnp.exp(bc)).astype(BF16), st.astype(BF16), NT_DIMS,
                            preferred_element_type=F32)
        o = o + jnp.dot(sc_ref[h].astype(BF16), v, preferred_element_type=F32)

        last = bc[chunk - 1:chunk]
        kd = (kk * jnp.exp(last - bc)).astype(BF16)
        st_ref[h] = st * jnp.exp(last) + lax.dot_general(v, kd, TN_DIMS, preferred_element_type=F32)

        y = o * lax.rsqrt(jnp.mean(o * o, axis=-1, keepdims=True) + EPS) * ng_ref[:, sl]
        o_ref[:, sl] = (y * _silu(gate_ref[:, sl].astype(F32))).astype(o_ref.dtype)


def _hgrn(g1, fb, lb_logits, ng, bsz, L, chunk=TILES.hgrn_chunk, rblk=TILES.hgrn_rows):
    assert rblk & (rblk - 1) == 0 and chunk % rblk == 0
    t = bsz * L
    nc = L // chunk
    width = B_HEADS * B_DIM
    kern = functools.partial(_hgrn_kernel, chunk=chunk, rblk=rblk)
    return pl.pallas_call(
        kern,
        grid=(bsz, nc),
        in_specs=[pl.BlockSpec((chunk, width), lambda b, c: (b * nc + c, 2)),
                  pl.BlockSpec((chunk, width), lambda b, c: (b * nc + c, 3)),
                  pl.BlockSpec((chunk, width), lambda b, c: (b * nc + c, 4)),
                  pl.BlockSpec((chunk, width), lambda b, c: (b * nc + c, 0)),
                  pl.BlockSpec(lb_logits.shape, lambda b, c: (0, 0)),
                  pl.BlockSpec((1, width), lambda b, c: (0, 0))],
        out_specs=pl.BlockSpec((chunk, width), lambda b, c: (b * nc + c, 0)),
        out_shape=jax.ShapeDtypeStruct((t, width), BF16),
        scratch_shapes=[pltpu.VMEM((B_HEADS, B_DIM, B_DIM), F32),
                        pltpu.VMEM((B_HEADS, chunk, B_DIM), F32),
                        pltpu.VMEM((B_HEADS, chunk, B_DIM), F32),
                        pltpu.VMEM((B_HEADS, chunk, chunk), F32)],
        compiler_params=_cparams(("parallel", "arbitrary")),
    )(g1, g1, g1, fb, lb_logits, ng.reshape(1, width))


def _out_kernel(oa_ref, ob_ref, x_ref, wa_ref, wb_ref, mod_ref, g_ref, wr_ref, x1_ref, h2_ref, lg_ref):
    mix = jnp.dot(oa_ref[...], wa_ref[...], preferred_element_type=F32)
    mix = mix + jnp.dot(ob_ref[...], wb_ref[...], preferred_element_type=F32)
    x1 = x_ref[...] + mod_ref[0, 2:3, :] * mix
    x1_ref[...] = x1
    y = x1 * lax.rsqrt(jnp.mean(x1 * x1, axis=-1, keepdims=True) + EPS) * g_ref[...]
    h2 = y * (1.0 + mod_ref[0, 4:5, :]) + mod_ref[0, 3:4, :]
    h2_ref[...] = _pack_halves(h2)
    lg_ref[...] = lax.dot_general(wr_ref[...], h2.astype(BF16), NT_DIMS, preferred_element_type=F32)


def _out(oa, ob, x2d, w_out, mod3, g, w_router_t, L, tm=TILES.out_rows):
    t, d = x2d.shape
    half = oa.shape[1]
    ne = w_router_t.shape[0]
    return pl.pallas_call(
        _out_kernel,
        grid=(t // tm,),
        in_specs=[pl.BlockSpec((tm, half), lambda i: (i, 0)),
                  pl.BlockSpec((tm, half), lambda i: (i, 0)),
                  pl.BlockSpec((tm, d), lambda i: (i, 0)),
                  pl.BlockSpec((half, d), lambda i: (0, 0)),
                  pl.BlockSpec((half, d), lambda i: (1, 0)),
                  pl.BlockSpec((1, 6, d), lambda i: (i * tm // L, 0, 0)),
                  pl.BlockSpec((1, d), lambda i: (0, 0)),
                  pl.BlockSpec((ne, d), lambda i: (0, 0))],
        out_specs=[pl.BlockSpec((tm, d), lambda i: (i, 0)),
                   pl.BlockSpec((tm, d // 2), lambda i: (i, 0)),
                   pl.BlockSpec((ne, tm), lambda i: (0, i))],
        out_shape=[jax.ShapeDtypeStruct((t, d), F32),
                   jax.ShapeDtypeStruct((t, d // 2), U32),
                   jax.ShapeDtypeStruct((ne, t), F32)],
        compiler_params=_cparams(("parallel",)),
    )(oa, ob, x2d, w_out, w_out, mod3, g.reshape(1, d), w_router_t)


def _rows_to_tile(rows, nrow):
    n = rows[0].shape[1]
    ridx = lax.broadcasted_iota(I32, (nrow, n), 0)
    out = jnp.zeros((nrow, n), rows[0].dtype)
    for r, v in enumerate(rows):
        out = jnp.where(ridx == r, jnp.broadcast_to(v, (nrow, n)), out)
    return out


def _route_kernel(lg_ref, rb_ref, eidx_ref, ew_ref, rank_ref, cnt_ref, run_ref):
    @pl.when(pl.program_id(0) == 0)
    def _():
        run_ref[...] = jnp.zeros(run_ref.shape, F32)

    ne, tt = lg_ref.shape
    per = ne // N_GROUPS
    sc = jax.nn.sigmoid(lg_ref[...])
    ch = sc + rb_ref[...]
    neg = -jnp.inf

    sub = lax.broadcasted_iota(I32, (per, tt), 0).astype(F32)
    gsc = []
    for g in range(N_GROUPS):
        cg = ch[g * per:(g + 1) * per]
        m1 = jnp.max(cg, axis=0, keepdims=True)
        first = jnp.min(jnp.where(cg == m1, sub, float(per)), axis=0, keepdims=True)
        m2 = jnp.max(jnp.where(sub == first, neg, cg), axis=0, keepdims=True)
        gsc.append(m1 + m2)
    grp = _rows_to_tile(gsc, N_GROUPS)

    gid = lax.broadcasted_iota(I32, (N_GROUPS, tt), 0).astype(F32)
    gsel = jnp.zeros((N_GROUPS, tt), F32)
    for _ in range(TOPK_GROUPS):
        mx = jnp.max(grp, axis=0, keepdims=True)
        gi = jnp.min(jnp.where(grp == mx, gid, float(N_GROUPS)), axis=0, keepdims=True)
        pick = gid == gi
        gsel = jnp.where(pick, 1.0, gsel)
        grp = jnp.where(pick, neg, grp)

    eid = lax.broadcasted_iota(I32, (ne, tt), 0).astype(F32)
    cm = jnp.full((ne, tt), neg, F32)
    for g in range(N_GROUPS):
        in_g = (eid >= float(g * per)) & (eid < float((g + 1) * per))
        cm = jnp.where(in_g & (jnp.broadcast_to(gsel[g:g + 1], (ne, tt)) > 0.5), ch, cm)

    idx_rows, w_rows = [], []
    onehot = jnp.zeros((ne, tt), F32)
    for _ in range(TOP_K):
        mx = jnp.max(cm, axis=0, keepdims=True)
        ei = jnp.min(jnp.where(cm == mx, eid, float(ne)), axis=0, keepdims=True)
        pick = eid == ei
        idx_rows.append(ei)
        w_rows.append(jnp.sum(jnp.where(pick, sc, 0.0), axis=0, keepdims=True))
        onehot = jnp.where(pick, 1.0, onehot)
        cm = jnp.where(pick, neg, cm)
    wsum = w_rows[0]
    for w in w_rows[1:]:
        wsum = wsum + w
    w_rows = [w / wsum * ROUTED_SCALE for w in w_rows]

    a_i = lax.broadcasted_iota(I32, (tt, tt), 0)
    b_i = lax.broadcasted_iota(I32, (tt, tt), 1)
    upper = jnp.where(a_i < b_i, 1.0, 0.0).astype(BF16)
    rank_full = jnp.dot(onehot.astype(BF16), upper, preferred_element_type=F32) + run_ref[...]
    r_rows = [jnp.sum(jnp.where(eid == ei, rank_full, 0.0), axis=0, keepdims=True) for ei in idx_rows]
    run = run_ref[...] + jnp.sum(onehot, axis=1, keepdims=True)
    run_ref[...] = run

    eidx_ref[...] = _rows_to_tile(idx_rows, TOP_K).astype(I32)
    ew_ref[...] = _rows_to_tile(w_rows, TOP_K)
    rank_ref[...] = _rows_to_tile(r_rows, TOP_K).astype(I32)
    cnt_ref[...] = jnp.broadcast_to(run, cnt_ref.shape)


def _route(logits_t, router_bias, tt=TILES.route_tokens):
    ne, t = logits_t.shape
    return pl.pallas_call(
        _route_kernel,
        grid=(t // tt,),
        in_specs=[pl.BlockSpec((ne, tt), lambda i: (0, i)),
                  pl.BlockSpec((ne, 1), lambda i: (0, 0))],
        out_specs=[pl.BlockSpec((TOP_K, tt), lambda i: (0, i)),
                   pl.BlockSpec((TOP_K, tt), lambda i: (0, i)),
                   pl.BlockSpec((TOP_K, tt), lambda i: (0, i)),
                   pl.BlockSpec((ne, LANES), lambda i: (0, 0))],
        out_shape=[jax.ShapeDtypeStruct((TOP_K, t), I32),
                   jax.ShapeDtypeStruct((TOP_K, t), F32),
                   jax.ShapeDtypeStruct((TOP_K, t), I32),
                   jax.ShapeDtypeStruct((ne, LANES), F32)],
        scratch_shapes=[pltpu.VMEM((ne, 1), F32)],
        compiler_params=_cparams(("arbitrary",)),
    )(logits_t, router_bias.reshape(ne, 1))


def _dest_kernel(eidx_ref, rank_ref, ps_ref, o_ref):
    ne = ps_ref.shape[0]
    tt = eidx_ref.shape[1]
    eid = lax.broadcasted_iota(I32, (ne, tt), 0)
    ps = jnp.broadcast_to(ps_ref[...], (ne, tt))
    rows = []
    for k in range(TOP_K):
        start = jnp.sum(jnp.where(eid == eidx_ref[k:k + 1, :], ps, 0.0), axis=0, keepdims=True)
        rows.append(start + rank_ref[k:k + 1, :].astype(F32))
    o_ref[...] = _rows_to_tile(rows, TOP_K).astype(I32)


def _dest(eidx, rank, pad_start, tt=TILES.dest_tokens):
    t = eidx.shape[1]
    tt = min(tt, t)
    ne = pad_start.shape[0]
    return pl.pallas_call(
        _dest_kernel,
        grid=(t // tt,),
        in_specs=[pl.BlockSpec((TOP_K, tt), lambda i: (0, i)),
                  pl.BlockSpec((TOP_K, tt), lambda i: (0, i)),
                  pl.BlockSpec((ne, 1), lambda i: (0, 0))],
        out_specs=pl.BlockSpec((TOP_K, tt), lambda i: (0, i)),
        out_shape=jax.ShapeDtypeStruct((TOP_K, t), I32),
        compiler_params=_cparams(("parallel",)),
    )(eidx, rank, pad_start.astype(F32).reshape(ne, 1))


def _dispatch_kernel(pend_ref, padded_ref, dest_ref, h_ref, xs_ref, zbuf_ref, zsem, sem, *, td, bm):
    i = pl.program_id(0)

    def tail_copy(e):
        start = pl.multiple_of(pend_ref[e] - bm, bm)
        return pltpu.make_async_copy(zbuf_ref, xs_ref.at[pl.ds(start, bm)], zsem)

    @pl.when(i == 0)
    def _():
        zbuf_ref[...] = jnp.zeros(zbuf_ref.shape, zbuf_ref.dtype)

        def start_body(e, c):
            @pl.when(padded_ref[e] > 0)
            def _():
                tail_copy(e).start()
            return c

        def wait_body(e, c):
            @pl.when(padded_ref[e] > 0)
            def _():
                tail_copy(e).wait()
            return c

        lax.fori_loop(0, N_EXPERTS, start_body, 0)
        lax.fori_loop(0, N_EXPERTS, wait_body, 0)

        def unused_copy(b):
            return pltpu.make_async_copy(zbuf_ref, xs_ref.at[pl.ds(pl.multiple_of(b * bm, bm), bm)], zsem)

        def ustart_body(b, c):
            unused_copy(b).start()
            return c

        def uwait_body(b, c):
            unused_copy(b).wait()
            return c

        first_unused = pend_ref[N_EXPERTS - 1] // bm
        lax.fori_loop(first_unused, xs_ref.shape[0] // bm, ustart_body, 0)
        lax.fori_loop(first_unused, xs_ref.shape[0] // bm, uwait_body, 0)

    for j in range(td):
        for k in range(TOP_K):
            dst = xs_ref.at[dest_ref[j * TOP_K + k]]
            pltpu.make_async_copy(h_ref.at[j], dst, sem).start(priority=k % 2)
    for _ in range(TOP_K):
        pltpu.make_async_copy(h_ref, xs_ref.at[pl.ds(0, td)], sem).wait()


def _dispatch(pad_end, padded, dest_flat, h2p, n_rows, bm, td=TILES.move_tokens):
    t, w = h2p.shape
    kern = functools.partial(_dispatch_kernel, td=td, bm=bm)
    return pl.pallas_call(
        kern,
        grid_spec=pltpu.PrefetchScalarGridSpec(
            num_scalar_prefetch=2,
            grid=(t // td,),
            in_specs=[pl.BlockSpec((td * TOP_K,), lambda i, *_: (i,), memory_space=pltpu.SMEM),
                      pl.BlockSpec((td, w), lambda i, *_: (i, 0))],
            out_specs=pl.BlockSpec(memory_space=pl.ANY),
            scratch_shapes=[pltpu.VMEM((bm, w), U32),
                            pltpu.SemaphoreType.DMA(()),
                            pltpu.SemaphoreType.DMA(())]),
        out_shape=jax.ShapeDtypeStruct((n_rows, w), U32),
        compiler_params=_cparams(("arbitrary",)),
    )(pad_end, padded, dest_flat, h2p)


def _ffn(xw, wg_ref, wu_ref, wd_ref):
    half = xw.shape[1]
    left, right = _unpack_halves(xw)
    left = left.astype(BF16)
    right = right.astype(BF16)

    def proj(w_ref):
        return (jnp.dot(left, w_ref[:half, :], preferred_element_type=F32)
                + jnp.dot(right, w_ref[half:, :], preferred_element_type=F32))

    act = (_silu(proj(wg_ref)) * proj(wu_ref)).astype(BF16)
    return jnp.dot(act, wd_ref[...], preferred_element_type=F32)


def _expert_kernel(blk_ref, eid_ref, first_ref, slot_ref, nxt_ref, more_ref, nvb_ref,
                   x_ref, wg_hbm, wu_hbm, wd_hbm, o_ref,
                   wg_f, wu_f, wd_f, wg_s, wu_s, wd_s, sems):
    i = pl.program_id(0)

    def weight_copies(e, slot):
        return (pltpu.make_async_copy(wg_hbm.at[e], wg_f.at[slot], sems.at[slot]),
                pltpu.make_async_copy(wu_hbm.at[e], wu_f.at[slot], sems.at[slot]),
                pltpu.make_async_copy(wd_hbm.at[e], wd_f.at[slot], sems.at[slot]))

    @pl.when(i == 0)
    def _():
        for cp in weight_copies(eid_ref[0], 0):
            cp.start(priority=1)

    @pl.when(first_ref[i] == 1)
    def _():
        slot = slot_ref[i]
        for cp in weight_copies(eid_ref[i], slot):
            cp.wait()

        @pl.when(more_ref[i] == 1)
        def _():
            for cp in weight_copies(nxt_ref[i], 1 - slot):
                cp.start(priority=1)

    @pl.when(first_ref[i] == 1)
    def _():
        slot = slot_ref[i]
        wg_s[...] = wg_f[slot].astype(BF16)
        wu_s[...] = wu_f[slot].astype(BF16)
        wd_s[...] = wd_f[slot].astype(BF16)
        o_ref[...] = _pack_halves(_ffn(x_ref[...], wg_s, wu_s, wd_s))

    @pl.when((first_ref[i] == 0) & (i < nvb_ref[0]))
    def _():
        o_ref[...] = _pack_halves(_ffn(x_ref[...], wg_s, wu_s, wd_s))

    @pl.when(i >= nvb_ref[0])
    def _():
        o_ref[...] = jnp.zeros(o_ref.shape, o_ref.dtype)


def _experts(blk, eid, first, slot, nxt, more, nvb, xs, wg, wu, wd, bm):
    n_rows, w = xs.shape
    ne, d, f = wg.shape
    return pl.pallas_call(
        _expert_kernel,
        grid_spec=pltpu.PrefetchScalarGridSpec(
            num_scalar_prefetch=7,
            grid=(n_rows // bm,),
            in_specs=[pl.BlockSpec((bm, w), lambda i, blk, *_: (blk[i], 0)),
                      pl.BlockSpec(memory_space=pl.ANY),
                      pl.BlockSpec(memory_space=pl.ANY),
                      pl.BlockSpec(memory_space=pl.ANY)],
            out_specs=pl.BlockSpec((bm, w), lambda i, *_: (i, 0)),
            scratch_shapes=[pltpu.VMEM((2, d, f), F32), pltpu.VMEM((2, d, f), F32), pltpu.VMEM((2, f, d), F32),
                            pltpu.VMEM((d, f), BF16), pltpu.VMEM((d, f), BF16), pltpu.VMEM((f, d), BF16),
                            pltpu.SemaphoreType.DMA((2,))]),
        out_shape=jax.ShapeDtypeStruct((n_rows, w), U32),
        compiler_params=_cparams(("arbitrary",)),
    )(blk, eid, first, slot, nxt, more, nvb, xs, wg, wu, wd)


def _combine_kernel(dest_ref, dnext_ref, y_ref, h_ref, x1_ref, ew_ref, wg_ref, wu_ref, wd_ref, mod_ref, g_ref,
                    o_ref, gbuf_a, gbuf_b, sems, *, tc):
    i = pl.program_id(0)
    last = pl.num_programs(0) - 1

    def issue(idx_ref, gbuf, sem):
        for j in range(tc):
            for k in range(TOP_K):
                src = y_ref.at[idx_ref[j * TOP_K + k]]
                pltpu.make_async_copy(src, gbuf.at[k, j], sem).start(priority=k % 2)

    def wait_all(gbuf, sem):
        for k in range(TOP_K):
            pltpu.make_async_copy(y_ref.at[pl.ds(0, tc)], gbuf.at[k], sem).wait()

    def compute(gbuf):
        shared = _ffn(h_ref[...], wg_ref, wu_ref, wd_ref)
        half = h_ref.shape[1]
        ew = ew_ref[...]
        left = shared[:, :half]
        right = shared[:, half:]
        for k in range(TOP_K):
            yl, yr = _unpack_halves(gbuf[k])
            wk = ew[:, k:k + 1]
            left = left + wk * yl
            right = right + wk * yr
        x2 = x1_ref[...] + mod_ref[0, 5:6, :] * jnp.concatenate([left, right], axis=1)
        o_ref[...] = x2 * lax.rsqrt(jnp.mean(x2 * x2, axis=-1, keepdims=True) + EPS) * g_ref[...]

    @pl.when(i == 0)
    def _():
        def group_body(g, c):
            base = pl.multiple_of(g * SUBLANES, SUBLANES)
            for jj in range(SUBLANES):
                for k in range(TOP_K):
                    src = y_ref.at[dest_ref[(base + jj) * TOP_K + k]]
                    pltpu.make_async_copy(src, gbuf_a.at[k, base + jj], sems.at[0]).start(priority=k % 2)
            return c

        lax.fori_loop(0, tc // SUBLANES, group_body, 0)

    @pl.when(i % 2 == 0)
    def _():
        wait_all(gbuf_a, sems.at[0])
        issue(dnext_ref, gbuf_b, sems.at[1])
        compute(gbuf_a)

    @pl.when(i % 2 == 1)
    def _():
        wait_all(gbuf_b, sems.at[1])
        issue(dnext_ref, gbuf_a, sems.at[0])
        compute(gbuf_b)

    @pl.when((i == last) & (i % 2 == 0))
    def _():
        wait_all(gbuf_b, sems.at[1])

    @pl.when((i == last) & (i % 2 == 1))
    def _():
        wait_all(gbuf_a, sems.at[0])


def _combine(dest_flat, y, h2p, x1, ew_t, wsg, wsu, wsd, mod3, g, L, tc=TILES.move_tokens):
    t, d = x1.shape
    w = h2p.shape[1]
    nt = t // tc
    kern = functools.partial(_combine_kernel, tc=tc)
    return pl.pallas_call(
        kern,
        grid=(nt,),
        in_specs=[pl.BlockSpec((tc * TOP_K,), lambda i: (i,), memory_space=pltpu.SMEM),
                  pl.BlockSpec((tc * TOP_K,), lambda i: (jnp.minimum(i + 1, nt - 1),), memory_space=pltpu.SMEM),
                  pl.BlockSpec(memory_space=pl.ANY),
                  pl.BlockSpec((tc, w), lambda i: (i, 0)),
                  pl.BlockSpec((tc, d), lambda i: (i, 0)),
                  pl.BlockSpec((tc, TOP_K), lambda i: (i, 0)),
                  pl.BlockSpec(wsg.shape, lambda i: (0, 0)),
                  pl.BlockSpec(wsu.shape, lambda i: (0, 0)),
                  pl.BlockSpec(wsd.shape, lambda i: (0, 0)),
                  pl.BlockSpec((1, 6, d), lambda i: (i * tc // L, 0, 0)),
                  pl.BlockSpec((1, d), lambda i: (0, 0))],
        out_specs=pl.BlockSpec((tc, d), lambda i: (i, 0)),
        out_shape=jax.ShapeDtypeStruct((t, d), F32),
        scratch_shapes=[pltpu.VMEM((TOP_K, tc, w), U32),
                        pltpu.VMEM((TOP_K, tc, w), U32),
                        pltpu.SemaphoreType.DMA((2,))],
        compiler_params=_cparams(("arbitrary",)),
    )(dest_flat, dest_flat, y, h2p, x1, ew_t, wsg, wsu, wsd, mod3, g.reshape(1, d))


def _split_cols(w, sizes):
    out, off = [], 0
    for s in sizes:
        out.append(w[:, off:off + s])
        off += s
    return out


def kernel(x, c, w_ada, b_ada, norm1_g, w_in, ckv_norm_g, idx_k_norm_g, w_uk, w_uv, rel_bias, lb_logits,
           attn_out_norm_g, hgrn_out_norm_g, w_out, norm2_g, w_router, router_bias, w_e_gate, w_e_up,
           w_e_down, w_s_gate, w_s_up, w_s_down, final_norm_g):
    bsz, L, d = x.shape
    t = bsz * L
    assert w_ada.shape[0] == 1, "single-layer block"
    a_width = A_HEADS * A_HEAD_DIM
    b_width = B_HEADS * B_DIM
    sizes = (a_width, A_KV_RANK, IDX_HEADS * IDX_DIM, IDX_DIM, IDX_HEADS, b_width, b_width, b_width, b_width)
    assert w_in.shape[2] == sum(sizes)

    wq_a, wckv, wiq, wik, wiw, wq_b, wf_b, wi_b, wg_b = _split_cols(w_in[0], sizes)
    w_main = jnp.concatenate([wq_a, wiq, wq_b, wi_b, wg_b], axis=1).astype(BF16)
    w_f = wf_b.astype(BF16)
    aux_pad = LANES - IDX_DIM - IDX_HEADS
    w_aux = jnp.concatenate([wckv, wik, wiw, jnp.zeros((d, aux_pad), F32)], axis=1).astype(BF16)

    mod3 = _ada(c, w_ada[0], b_ada[0]).reshape(bsz, 6, d)
    h1 = _norm1(x, mod3, norm1_g[0]).reshape(t, d)
    g1 = _matmul(h1, w_main, BF16, tm=TILES.proj_rows, tn=TILES.proj_cols)
    fb = _matmul(h1, w_f, F32, tm=TILES.proj_rows, tn=TILES.proj_cols)
    aux = _matmul(h1, w_aux, F32, tm=TILES.proj_rows, tn=w_aux.shape[1])
    ckv_n, ik_lo, ik_hi = _kvnorm(aux, ckv_norm_g[0], idx_k_norm_g[0])

    tq = min(TILES.attn, L)
    o_a = _dsa(g1, aux, ik_lo, ik_hi, ckv_n, w_uk[0].astype(BF16), w_uv[0].astype(BF16),
               _bias_tables(rel_bias, tq), attn_out_norm_g[0], bsz, L, tq)
    o_b = _hgrn(g1, fb, lb_logits, hgrn_out_norm_g[0], bsz, L)

    x1, h2p, logits_t = _out(o_a, o_b, x.reshape(t, d), w_out[0].astype(BF16), mod3, norm2_g[0],
                             w_router[0].T.astype(BF16), L)

    eidx, ew, rank, cnt = _route(logits_t, router_bias[0])

    bm = TILES.moe_rows
    counts = cnt[:, 0].astype(I32)
    padded = (counts + bm - 1) // bm * bm
    pad_end = jnp.cumsum(padded)
    pad_start = pad_end - padded
    n_rows = (t * TOP_K + N_EXPERTS * (bm - 1) + bm - 1) // bm * bm
    nb = n_rows // bm
    nvb = pad_end[-1] // bm
    blk = jnp.minimum(jnp.arange(nb, dtype=I32), nvb - 1)
    eid = jnp.minimum(jnp.sum((pad_end[None, :] <= (blk * bm)[:, None]).astype(I32), axis=1), N_EXPERTS - 1)
    ar = jnp.arange(nb, dtype=I32)
    first = ((ar < nvb) & ((ar == 0) | (eid != jnp.roll(eid, 1)))).astype(I32)
    slot = (jnp.cumsum(first) - 1) % 2
    nxt_blk = pad_end[eid] // bm
    more = (nxt_blk < nvb).astype(I32)
    nxt = eid[jnp.minimum(nxt_blk, nb - 1)]

    dest = _dest(eidx, rank, pad_start)
    dest_flat = dest.T.reshape(t * TOP_K)
    xs = _dispatch(pad_end.astype(I32), padded.astype(I32), dest_flat, h2p, n_rows, bm)

    y = _experts(blk, eid, first, slot.astype(I32), nxt.astype(I32), more, nvb.reshape(1).astype(I32), xs,
                 w_e_gate[0], w_e_up[0], w_e_down[0], bm)

    out = _combine(dest_flat, y, h2p, x1, ew.T, w_s_gate[0].astype(BF16), w_s_up[0].astype(BF16),
                   w_s_down[0].astype(BF16), mod3, final_norm_g, L)
    return out.reshape(bsz, L, d)
```

```python
import functools
import math
from typing import NamedTuple

import jax
import jax.numpy as jnp
from jax import lax
from jax.experimental import pallas as pl
from jax.experimental.pallas import tpu as pltpu

F32 = jnp.float32
BF16 = jnp.bfloat16
I32 = jnp.int32
U32 = jnp.uint32

EPS = 1e-6
A_HEADS = 8
A_HEAD_DIM = 128
A_KV_RANK = 256
IDX_HEADS = 16
IDX_DIM = 64
IDX_TOPK_MAX = 256
B_HEADS = 8
B_DIM = 128
REL_BUCKETS = 32
REL_MAX_DIST = 128
N_EXPERTS = 64
TOP_K = 8
N_GROUPS = 8
TOPK_GROUPS = 4
ROUTED_SCALE = 2.5

VMEM_LIMIT_BYTES = 56 * 1024 * 1024
LANES = 128
SUBLANES = 8


class _Tiles(NamedTuple):
    ada_cols: int = 1024
    norm_rows: int = 512
    proj_rows: int = 1024
    proj_cols: int = 512
    attn: int = 256
    hgrn_chunk: int = 128
    hgrn_rows: int = 32
    out_rows: int = 512
    route_tokens: int = 512
    dest_tokens: int = 2048
    moe_rows: int = 256
    move_tokens: int = 256


TILES = _Tiles()

NT_DIMS = (((1,), (1,)), ((), ()))
TN_DIMS = (((0,), (0,)), ((), ()))

HGRN_MAX_BLOCK_DECAY = 80.0
LOG2E = math.log2(math.e)
INT_MIN = -2 ** 31
KEY_NEG_INF = -2139095041


def _cparams(sem):
    return pltpu.CompilerParams(dimension_semantics=sem, vmem_limit_bytes=VMEM_LIMIT_BYTES)


def _silu(v):
    return v * jax.nn.sigmoid(v)


def _pack_halves(v):
    n = v.shape[1] // 2
    lo = lax.bitcast_convert_type(v[:, :n].astype(BF16).astype(F32), U32)
    hi = lax.bitcast_convert_type(v[:, n:].astype(BF16).astype(F32), U32)
    return lax.shift_right_logical(lo, jnp.uint32(16)) | (hi & jnp.uint32(0xFFFF0000))


def _unpack_halves(w):
    left = lax.bitcast_convert_type(lax.shift_left(w, jnp.uint32(16)), F32)
    right = lax.bitcast_convert_type(w & jnp.uint32(0xFFFF0000), F32)
    return left, right


def _ada_kernel(c_ref, w_ref, b_ref, o_ref):
    a = _silu(c_ref[...]).astype(BF16)
    o_ref[...] = jnp.dot(a, w_ref[...].astype(BF16), preferred_element_type=F32) + b_ref[...]


def _ada(c, w, b, tn=TILES.ada_cols):
    bsz, d = c.shape
    n = w.shape[1]
    return pl.pallas_call(
        _ada_kernel,
        grid=(n // tn,),
        in_specs=[pl.BlockSpec((bsz, d), lambda j: (0, 0)),
                  pl.BlockSpec((d, tn), lambda j: (0, j)),
                  pl.BlockSpec((1, tn), lambda j: (0, j))],
        out_specs=pl.BlockSpec((bsz, tn), lambda j: (0, j)),
        out_shape=jax.ShapeDtypeStruct((bsz, n), F32),
        compiler_params=_cparams(("arbitrary",)),
    )(c, w, b.reshape(1, n))


def _norm1_kernel(x_ref, mod_ref, g_ref, o_ref):
    x = x_ref[0]
    y = x * lax.rsqrt(jnp.mean(x * x, axis=-1, keepdims=True) + EPS) * g_ref[...]
    sh = mod_ref[0, 0:1, :]
    sc = mod_ref[0, 1:2, :]
    o_ref[0] = (y * (1.0 + sc) + sh).astype(o_ref.dtype)


def _norm1(x, mod3, g, tm=TILES.norm_rows):
    bsz, L, d = x.shape
    return pl.pallas_call(
        _norm1_kernel,
        grid=(bsz, L // tm),
        in_specs=[pl.BlockSpec((1, tm, d), lambda b, i: (b, i, 0)),
                  pl.BlockSpec((1, 6, d), lambda b, i: (b, 0, 0)),
                  pl.BlockSpec((1, d), lambda b, i: (0, 0))],
        out_specs=pl.BlockSpec((1, tm, d), lambda b, i: (b, i, 0)),
        out_shape=jax.ShapeDtypeStruct((bsz, L, d), BF16),
        compiler_params=_cparams(("parallel", "parallel")),
    )(x, mod3, g.reshape(1, d))


def _mm_kernel(a_ref, w_ref, o_ref):
    o_ref[...] = jnp.dot(a_ref[...], w_ref[...], preferred_element_type=F32).astype(o_ref.dtype)


def _matmul(a, w, out_dtype, tm, tn):
    m, k = a.shape
    n = w.shape[1]
    return pl.pallas_call(
        _mm_kernel,
        grid=(m // tm, n // tn),
        in_specs=[pl.BlockSpec((tm, k), lambda i, j: (i, 0)),
                  pl.BlockSpec((k, tn), lambda i, j: (0, j))],
        out_specs=pl.BlockSpec((tm, tn), lambda i, j: (i, j)),
        out_shape=jax.ShapeDtypeStruct((m, n), out_dtype),
        compiler_params=_cparams(("parallel", "arbitrary")),
    )(a, w)


def _kvnorm_kernel(aux_ref, gc_ref, gk_ref, ckv_ref, iklo_ref, ikhi_ref):
    ckv = aux_ref[:, :A_KV_RANK]
    ckv_ref[...] = (ckv * lax.rsqrt(jnp.mean(ckv * ckv, axis=-1, keepdims=True) + EPS)
                    * gc_ref[...]).astype(BF16)
    v = aux_ref[:, A_KV_RANK:A_KV_RANK + LANES]
    lane = lax.broadcasted_iota(I32, v.shape, 1)
    ik = jnp.where(lane < IDX_DIM, v, 0.0)
    ms = jnp.sum(ik * ik, axis=-1, keepdims=True) * (1.0 / IDX_DIM)
    ikn = ik * lax.rsqrt(ms + EPS) * gk_ref[...]
    iklo_ref[...] = ikn.astype(BF16)
    ikhi_ref[...] = pltpu.roll(ikn, IDX_DIM, 1).astype(BF16)


def _kvnorm(aux, gc, gk, tm=TILES.proj_rows):
    t = aux.shape[0]
    gk_pad = jnp.concatenate([gk, jnp.zeros((LANES - IDX_DIM,), F32)]).reshape(1, LANES)
    return pl.pallas_call(
        _kvnorm_kernel,
        grid=(t // tm,),
        in_specs=[pl.BlockSpec((tm, aux.shape[1]), lambda i: (i, 0)),
                  pl.BlockSpec((1, A_KV_RANK), lambda i: (0, 0)),
                  pl.BlockSpec((1, LANES), lambda i: (0, 0))],
        out_specs=[pl.BlockSpec((tm, A_KV_RANK), lambda i: (i, 0)),
                   pl.BlockSpec((tm, LANES), lambda i: (i, 0)),
                   pl.BlockSpec((tm, LANES), lambda i: (i, 0))],
        out_shape=[jax.ShapeDtypeStruct((t, A_KV_RANK), BF16),
                   jax.ShapeDtypeStruct((t, LANES), BF16),
                   jax.ShapeDtypeStruct((t, LANES), BF16)],
        compiler_params=_cparams(("parallel",)),
    )(aux, gc.reshape(1, A_KV_RANK), gk_pad)


def _t5_bucket(rel):
    n = jnp.maximum(rel, 0)
    max_exact = REL_BUCKETS // 2
    n_large = jnp.maximum(n, max_exact).astype(F32)
    large = max_exact + (jnp.log(n_large / max_exact) / math.log(REL_MAX_DIST / max_exact)
                         * (REL_BUCKETS - max_exact)).astype(I32)
    large = jnp.minimum(large, REL_BUCKETS - 1)
    return jnp.where(n < max_exact, n, large)


def _bias_tables(rel_bias, tq):
    assert tq + 1 >= REL_MAX_DIST
    nh = rel_bias.shape[1]
    dist = jnp.maximum(jnp.arange(3 * tq + 1, dtype=I32) - tq, 0)
    v = rel_bias.astype(F32)[_t5_bucket(dist)].T * LOG2E
    n = v.shape[1]
    x = jnp.broadcast_to(v[:, None, :], (nh, tq, n)).reshape(nh, tq * n)[:, :tq * (n - 1)].reshape(nh, tq, n - 1)
    near = x[:, :, tq:2 * tq]
    prev = x[:, :, 2 * tq:3 * tq]
    far = jnp.broadcast_to(v[:, n - 1][:, None, None], near.shape)
    return jnp.stack([near, prev, far])


def _dsa_kernel(qa_ref, iq_ref, aux_ref, iklo_ref, ikhi_ref, ckv_ref, ckvt_ref, wuk_ref, wuvt_ref, bias_ref,
                g_ref, o_ref, iqt_ref, iwt_ref, key_ref, qlt_ref, m_ref, l_ref, acc_ref, tie_ref, madd_ref,
                *, tq, topk):
    i = pl.program_id(1)
    nh = A_HEADS
    npair = IDX_HEADS // 2

    r_i = lax.broadcasted_iota(I32, (LANES, LANES), 0)
    c_i = lax.broadcasted_iota(I32, (LANES, LANES), 1)
    eye = jnp.where(r_i == c_i, 1.0, 0.0).astype(BF16)
    for p in range(npair):
        iqt_ref[:, p * tq:(p + 1) * tq] = lax.dot_general(
            eye, iq_ref[:, p * LANES:(p + 1) * LANES], NT_DIMS, preferred_element_type=F32).astype(BF16)
    iwt_ref[...] = (jnp.transpose(aux_ref[...])[IDX_DIM:IDX_DIM + IDX_HEADS, :]
                    * (IDX_HEADS ** -0.5 * IDX_DIM ** -0.5))
    for h in range(nh):
        ql = lax.dot_general(wuk_ref[h], qa_ref[:, h * A_HEAD_DIM:(h + 1) * A_HEAD_DIM], NT_DIMS,
                             preferred_element_type=F32)
        qlt_ref[:, h * tq:(h + 1) * tq] = (ql * (A_HEAD_DIM ** -0.5 * LOG2E)).astype(BF16)

    kpos = lax.broadcasted_iota(I32, (tq, tq), 0)
    qpos = lax.broadcasted_iota(I32, (tq, tq), 1) + i * tq

    def score_body(kc, carry):
        off = pl.multiple_of(kc * tq, tq)
        klo = iklo_ref[0, pl.ds(off, tq), :]
        khi = ikhi_ref[0, pl.ds(off, tq), :]
        acc = jnp.zeros((tq, tq), F32)
        for p in range(npair):
            rhs = iqt_ref[:, p * tq:(p + 1) * tq]
            se = jnp.dot(klo, rhs, preferred_element_type=F32)
            so = jnp.dot(khi, rhs, preferred_element_type=F32)
            acc = acc + jnp.maximum(se, 0.0) * iwt_ref[2 * p:2 * p + 1, :]
            acc = acc + jnp.maximum(so, 0.0) * iwt_ref[2 * p + 1:2 * p + 2, :]
        bits = lax.bitcast_convert_type(acc, I32)
        key = jnp.where(bits >= 0, bits, bits ^ jnp.int32(0x7FFFFFFF))
        key_ref[kc] = jnp.where(kpos + off <= qpos, key, jnp.int32(KEY_NEG_INF))
        return carry

    lax.fori_loop(0, i + 1, score_body, 0)

    def count_ge(cand):
        def body(kc, c):
            hit = jnp.where(key_ref[kc] >= cand, 1.0, 0.0)
            return c + jnp.sum(hit.reshape(tq // 8, 8, tq), axis=0)
        c = lax.fori_loop(0, i + 1, body, jnp.zeros((8, tq), F32))
        return jnp.sum(c, axis=0, keepdims=True)

    kf = float(topk)
    thr = jnp.where(count_ge(jnp.zeros((1, tq), I32)) >= kf, jnp.int32(0), jnp.int32(INT_MIN))

    def bit_body(j, thr):
        cand = thr | lax.shift_left(jnp.int32(1), 30 - j)
        return jnp.where(count_ge(cand) >= kf, cand, thr)

    thr = lax.fori_loop(0, 31, bit_body, thr)

    def count_gt_eq():
        def body(kc, c):
            key = key_ref[kc]
            gt = jnp.where(key > thr, 1.0, 0.0)
            eq = jnp.where(key == thr, 1.0, 0.0)
            return (c[0] + jnp.sum(gt.reshape(tq // 8, 8, tq), axis=0),
                    c[1] + jnp.sum(eq.reshape(tq // 8, 8, tq), axis=0))
        z = jnp.zeros((8, tq), F32)
        c = lax.fori_loop(0, i + 1, body, (z, z))
        return jnp.sum(c[0], axis=0, keepdims=True), jnp.sum(c[1], axis=0, keepdims=True)

    n_gt, n_eq = count_gt_eq()
    need = kf - n_gt
    tied = (n_gt + n_eq > kf) & (thr > jnp.int32(KEY_NEG_INF))
    has_tie = jnp.max(jnp.where(tied, 1.0, 0.0)) > 0.0
    tie_ref[...] = jnp.zeros(tie_ref.shape, F32)

    m_ref[...] = jnp.full(m_ref.shape, -jnp.inf, F32)
    l_ref[...] = jnp.zeros(l_ref.shape, F32)
    acc_ref[...] = jnp.zeros(acc_ref.shape, F32)

    def att_body(kc, carry):
        off = pl.multiple_of(kc * tq, tq)
        ckv = ckv_ref[0, pl.ds(off, tq), :]
        ckvt = ckvt_ref[0, kc]
        key = key_ref[kc]
        causal = key > jnp.int32(KEY_NEG_INF)

        @pl.when(jnp.logical_not(has_tie))
        def _():
            madd_ref[...] = jnp.where((key >= thr) & causal, 0.0, -jnp.inf)

        @pl.when(has_tie)
        def _():
            eq = key == thr
            eqf = jnp.where(eq, 1.0, 0.0)
            before = (lax.broadcasted_iota(I32, (tq, tq), 1) < lax.broadcasted_iota(I32, (tq, tq), 0))
            rank = jnp.dot(jnp.where(before, 1.0, 0.0).astype(BF16), eqf.astype(BF16),
                           preferred_element_type=F32) + tie_ref[...]
            keep = (key > thr) | (eq & (rank < need))
            madd_ref[...] = jnp.where(keep & causal, 0.0, -jnp.inf)
            tie_ref[...] = tie_ref[...] + jnp.sum(eqf, axis=0, keepdims=True)

        madd = madd_ref[...]
        d = jnp.minimum(i - kc, 2)
        for h in range(nh):
            s = jnp.dot(ckv, qlt_ref[:, h * tq:(h + 1) * tq], preferred_element_type=F32)
            s = s + (bias_ref[d, h] + madd)
            m_old = m_ref[h:h + 1, :]
            m_new = jnp.maximum(m_old, jnp.max(s, axis=0, keepdims=True))
            m_safe = jnp.where(m_new == -jnp.inf, 0.0, m_new)
            alpha = jnp.exp2(m_old - m_safe)
            p = jnp.exp2(s - m_safe)
            l_ref[h:h + 1, :] = alpha * l_ref[h:h + 1, :] + jnp.sum(p, axis=0, keepdims=True)
            acc_ref[h] = alpha * acc_ref[h] + jnp.dot(ckvt, p.astype(BF16), preferred_element_type=F32)
            m_ref[h:h + 1, :] = m_new
        return carry

    lax.fori_loop(0, i + 1, att_body, 0)

    outs = []
    for h in range(nh):
        o_lat = (acc_ref[h] / l_ref[h:h + 1, :]).astype(BF16)
        outs.append(jnp.transpose(jnp.dot(wuvt_ref[h], o_lat, preferred_element_type=F32)))
    o = jnp.concatenate(outs, axis=1)
    o = o * lax.rsqrt(jnp.mean(o * o, axis=-1, keepdims=True) + EPS) * g_ref[...]
    o_ref[...] = o.astype(o_ref.dtype)


def _dsa(g1, aux, ik_lo, ik_hi, ckv_n, w_uk, w_uv, bias_tab, g, bsz, L, tq):
    t = bsz * L
    nq = L // tq
    topk = min(IDX_TOPK_MAX, L // 4)
    aux_blk = A_KV_RANK // LANES
    kern = functools.partial(_dsa_kernel, tq=tq, topk=topk)
    width = A_HEADS * A_HEAD_DIM
    ckv3 = ckv_n.reshape(bsz, L, A_KV_RANK)
    ckvt = ckv_n.reshape(bsz, nq, tq, A_KV_RANK).transpose(0, 1, 3, 2)
    return pl.pallas_call(
        kern,
        grid=(bsz, nq),
        in_specs=[pl.BlockSpec((tq, width), lambda b, i: (b * nq + i, 0)),
                  pl.BlockSpec((tq, IDX_HEADS * IDX_DIM), lambda b, i: (b * nq + i, 1)),
                  pl.BlockSpec((tq, LANES), lambda b, i: (b * nq + i, aux_blk)),
                  pl.BlockSpec((1, L, LANES), lambda b, i: (b, 0, 0)),
                  pl.BlockSpec((1, L, LANES), lambda b, i: (b, 0, 0)),
                  pl.BlockSpec((1, L, A_KV_RANK), lambda b, i: (b, 0, 0)),
                  pl.BlockSpec((1, nq, A_KV_RANK, tq), lambda b, i: (b, 0, 0, 0)),
                  pl.BlockSpec((A_HEADS, A_KV_RANK, A_HEAD_DIM), lambda b, i: (0, 0, 0)),
                  pl.BlockSpec((A_HEADS, A_HEAD_DIM, A_KV_RANK), lambda b, i: (0, 0, 0)),
                  pl.BlockSpec((3, A_HEADS, tq, tq), lambda b, i: (0, 0, 0, 0)),
                  pl.BlockSpec((1, width), lambda b, i: (0, 0))],
        out_specs=pl.BlockSpec((tq, width), lambda b, i: (b * nq + i, 0)),
        out_shape=jax.ShapeDtypeStruct((t, width), BF16),
        scratch_shapes=[pltpu.VMEM((LANES, IDX_HEADS // 2 * tq), BF16),
                        pltpu.VMEM((IDX_HEADS, tq), F32),
                        pltpu.VMEM((nq, tq, tq), I32),
                        pltpu.VMEM((A_KV_RANK, A_HEADS * tq), BF16),
                        pltpu.VMEM((A_HEADS, tq), F32),
                        pltpu.VMEM((A_HEADS, tq), F32),
                        pltpu.VMEM((A_HEADS, A_KV_RANK, tq), F32),
                        pltpu.VMEM((1, tq), F32),
                        pltpu.VMEM((tq, tq), F32)],
        compiler_params=_cparams(("parallel", "arbitrary")),
    )(g1, g1, aux, ik_lo.reshape(bsz, L, LANES), ik_hi.reshape(bsz, L, LANES),
      ckv3, ckvt, w_uk, jnp.transpose(w_uv, (0, 2, 1)), bias_tab, g.reshape(1, width))


def _hgrn_kernel(q_ref, i_ref, gate_ref, f_ref, lbl_ref, ng_ref, o_ref, st_ref, bc_ref, kk_ref, sc_ref,
                 *, chunk, rblk):
    @pl.when(pl.program_id(1) == 0)
    def _():
        st_ref[...] = jnp.zeros(st_ref.shape, F32)

    ll = lbl_ref[...]
    ex = jnp.exp(ll - jnp.max(ll, axis=0, keepdims=True))
    lb_all = ex[0:1] / jnp.sum(ex, axis=0, keepdims=True)

    r_i = lax.broadcasted_iota(I32, (chunk, chunk), 0)
    c_i = lax.broadcasted_iota(I32, (chunk, chunk), 1)
    tri = jnp.where(r_i >= c_i, 1.0, 0.0).astype(BF16)
    row_k = lax.broadcasted_iota(I32, (chunk, B_DIM), 0)
    nblk = chunk // rblk
    zero_row = jnp.zeros((1, B_DIM), F32)

    growth = zero_row
    for h in range(B_HEADS):
        sl = slice(h * B_DIM, (h + 1) * B_DIM)
        lb = lb_all[:, sl]
        f = lb + (1.0 - lb) * jax.nn.sigmoid(f_ref[:, sl])
        lf = jnp.log(f)
        kk_ref[h] = 1.0 - f
        l1 = lf.astype(BF16)
        r1 = lf - l1.astype(F32)
        l2 = r1.astype(BF16)
        l3 = (r1 - l2.astype(F32)).astype(BF16)
        cs = jnp.dot(tri, jnp.concatenate([l1, l2, l3], axis=1), preferred_element_type=F32)
        bc = cs[:, :B_DIM] + cs[:, B_DIM:2 * B_DIM] + cs[:, 2 * B_DIM:]
        bc_ref[h] = bc
        for r in range(nblk):
            top = bc[r * rblk - 1:r * rblk] if r > 0 else zero_row
            growth = jnp.maximum(growth, top - bc[(r + 1) * rblk - 1:(r + 1) * rblk])
    overflow_risk = jnp.max(growth) > HGRN_MAX_BLOCK_DECAY

    def block_scores(h, before_only):
        sl = slice(h * B_DIM, (h + 1) * B_DIM)
        bc = bc_ref[h]
        kk = kk_ref[h]
        q = q_ref[:, sl].astype(F32)
        parts = []
        for r in range(nblk):
            lo, hi = r * rblk, (r + 1) * rblk
            base = bc[lo - 1:lo] if r > 0 else zero_row
            qt = (q[lo:hi] * jnp.exp(bc[lo:hi] - base)).astype(BF16)
            if before_only:
                kt = jnp.where(row_k < lo, kk * jnp.exp(jnp.where(row_k < lo, base - bc, 0.0)), 0.0)
            else:
                kt = kk * jnp.exp(jnp.where(row_k < hi, base - bc, 0.0))
            parts.append(lax.dot_general(qt, kt.astype(BF16), NT_DIMS, preferred_element_type=F32))
        return jnp.concatenate(parts, axis=0)

    @pl.when(jnp.logical_not(overflow_risk))
    def _():
        for h in range(B_HEADS):
            sc_ref[h] = jnp.where(c_i <= r_i, block_scores(h, False), 0.0)

    @pl.when(overflow_risk)
    def _():
        for h in range(B_HEADS):
            sl = slice(h * B_DIM, (h + 1) * B_DIM)
            bc = bc_ref[h]
            kk = kk_ref[h]
            q = q_ref[:, sl].astype(F32)
            sc = block_scores(h, True)
            for dlt in range(rblk):
                bc_s = pltpu.roll(bc, dlt, 0) if dlt else bc
                kk_s = pltpu.roll(kk, dlt, 0) if dlt else kk
                ok = (row_k & (rblk - 1)) >= dlt
                band = q * kk_s * jnp.exp(jnp.where(ok, bc - bc_s, -jnp.inf))
                sc = sc + jnp.where(c_i == r_i - dlt, jnp.sum(band, axis=1, keepdims=True), 0.0)
            sc_ref[h] = sc

    for h in range(B_HEADS):
        sl = slice(h * B_DIM, (h + 1) * B_DIM)
        bc = bc_ref[h]
        kk = kk_ref[h]
        q = q_ref[:, sl].astype(F32)
        v = i_ref[:, sl]
        st = st_ref[h]
        o = lax.dot_general((q * jnp.exp(bc)).astype(BF16), st.astype(BF16), NT_DIMS,
                            preferred_element_type=F32)
        o = o + jnp.dot(sc_ref[h].astype(BF16), v, preferred_element_type=F32)

        last = bc[chunk - 1:chunk]
        kd = (kk * jnp.exp(last - bc)).astype(BF16)
        st_ref[h] = st * jnp.exp(last) + lax.dot_general(v, kd, TN_DIMS, preferred_element_type=F32)

        y = o * lax.rsqrt(jnp.mean(o * o, axis=-1, keepdims=True) + EPS) * ng_ref[:, sl]
        o_ref[:, sl] = (y * _silu(gate_ref[:, sl].astype(F32))).astype(o_ref.dtype)


def _hgrn(g1, fb, lb_logits, ng, bsz, L, chunk=TILES.hgrn_chunk, rblk=TILES.hgrn_rows):
    assert rblk & (rblk - 1) == 0 and chunk % rblk == 0
    t = bsz * L
    nc = L // chunk
    width = B_HEADS * B_DIM
    kern = functools.partial(_hgrn_kernel, chunk=chunk, rblk=rblk)
    return pl.pallas_call(
        kern,
        grid=(bsz, nc),
        in_specs=[pl.BlockSpec((chunk, width), lambda b, c: (b * nc + c, 2)),
                  pl.BlockSpec((chunk, width), lambda b, c: (b * nc + c, 3)),
                  pl.BlockSpec((chunk, width), lambda b, c: (b * nc + c, 4)),
                  pl.BlockSpec((chunk, width), lambda b, c: (b * nc + c, 0)),
                  pl.BlockSpec(lb_logits.shape, lambda b, c: (0, 0)),
                  pl.BlockSpec((1, width), lambda b, c: (0, 0))],
        out_specs=pl.BlockSpec((chunk, width), lambda b, c: (b * nc + c, 0)),
        out_shape=jax.ShapeDtypeStruct((t, width), BF16),
        scratch_shapes=[pltpu.VMEM((B_HEADS, B_DIM, B_DIM), F32),
                        pltpu.VMEM((B_HEADS, chunk, B_DIM), F32),
                        pltpu.VMEM((B_HEADS, chunk, B_DIM), F32),
                        pltpu.VMEM((B_HEADS, chunk, chunk), F32)],
        compiler_params=_cparams(("parallel", "arbitrary")),
    )(g1, g1, g1, fb, lb_logits, ng.reshape(1, width))


def _out_kernel(oa_ref, ob_ref, x_ref, wa_ref, wb_ref, mod_ref, g_ref, wr_ref, x1_ref, h2_ref, lg_ref):
    mix = jnp.dot(oa_ref[...], wa_ref[...], preferred_element_type=F32)
    mix = mix + jnp.dot(ob_ref[...], wb_ref[...], preferred_element_type=F32)
    x1 = x_ref[...] + mod_ref[0, 2:3, :] * mix
    x1_ref[...] = x1
    y = x1 * lax.rsqrt(jnp.mean(x1 * x1, axis=-1, keepdims=True) + EPS) * g_ref[...]
    h2 = y * (1.0 + mod_ref[0, 4:5, :]) + mod_ref[0, 3:4, :]
    h2_ref[...] = _pack_halves(h2)
    lg_ref[...] = lax.dot_general(wr_ref[...], h2.astype(BF16), NT_DIMS, preferred_element_type=F32)


def _out(oa, ob, x2d, w_out, mod3, g, w_router_t, L, tm=TILES.out_rows):
    t, d = x2d.shape
    half = oa.shape[1]
    ne = w_router_t.shape[0]
    return pl.pallas_call(
        _out_kernel,
        grid=(t // tm,),
        in_specs=[pl.BlockSpec((tm, half), lambda i: (i, 0)),
                  pl.BlockSpec((tm, half), lambda i: (i, 0)),
                  pl.BlockSpec((tm, d), lambda i: (i, 0)),
                  pl.BlockSpec((half, d), lambda i: (0, 0)),
                  pl.BlockSpec((half, d), lambda i: (1, 0)),
                  pl.BlockSpec((1, 6, d), lambda i: (i * tm // L, 0, 0)),
                  pl.BlockSpec((1, d), lambda i: (0, 0)),
                  pl.BlockSpec((ne, d), lambda i: (0, 0))],
        out_specs=[pl.BlockSpec((tm, d), lambda i: (i, 0)),
                   pl.BlockSpec((tm, d // 2), lambda i: (i, 0)),
                   pl.BlockSpec((ne, tm), lambda i: (0, i))],
        out_shape=[jax.ShapeDtypeStruct((t, d), F32),
                   jax.ShapeDtypeStruct((t, d // 2), U32),
                   jax.ShapeDtypeStruct((ne, t), F32)],
        compiler_params=_cparams(("parallel",)),
    )(oa, ob, x2d, w_out, w_out, mod3, g.reshape(1, d), w_router_t)


def _rows_to_tile(rows, nrow):
    n = rows[0].shape[1]
    ridx = lax.broadcasted_iota(I32, (nrow, n), 0)
    out = jnp.zeros((nrow, n), rows[0].dtype)
    for r, v in enumerate(rows):
        out = jnp.where(ridx == r, jnp.broadcast_to(v, (nrow, n)), out)
    return out


def _route_kernel(lg_ref, rb_ref, eidx_ref, ew_ref, rank_ref, cnt_ref, run_ref):
    @pl.when(pl.program_id(0) == 0)
    def _():
        run_ref[...] = jnp.zeros(run_ref.shape, F32)

    ne, tt = lg_ref.shape
    per = ne // N_GROUPS
    sc = jax.nn.sigmoid(lg_ref[...])
    ch = sc + rb_ref[...]
    neg = -jnp.inf

    sub = lax.broadcasted_iota(I32, (per, tt), 0).astype(F32)
    gsc = []
    for g in range(N_GROUPS):
        cg = ch[g * per:(g + 1) * per]
        m1 = jnp.max(cg, axis=0, keepdims=True)
        first = jnp.min(jnp.where(cg == m1, sub, float(per)), axis=0, keepdims=True)
        m2 = jnp.max(jnp.where(sub == first, neg, cg), axis=0, keepdims=True)
        gsc.append(m1 + m2)
    grp = _rows_to_tile(gsc, N_GROUPS)

    gid = lax.broadcasted_iota(I32, (N_GROUPS, tt), 0).astype(F32)
    gsel = jnp.zeros((N_GROUPS, tt), F32)
    for _ in range(TOPK_GROUPS):
        mx = jnp.max(grp, axis=0, keepdims=True)
        gi = jnp.min(jnp.where(grp == mx, gid, float(N_GROUPS)), axis=0, keepdims=True)
        pick = gid == gi
        gsel = jnp.where(pick, 1.0, gsel)
        grp = jnp.where(pick, neg, grp)

    eid = lax.broadcasted_iota(I32, (ne, tt), 0).astype(F32)
    cm = jnp.full((ne, tt), neg, F32)
    for g in range(N_GROUPS):
        in_g = (eid >= float(g * per)) & (eid < float((g + 1) * per))
        cm = jnp.where(in_g & (jnp.broadcast_to(gsel[g:g + 1], (ne, tt)) > 0.5), ch, cm)

    idx_rows, w_rows = [], []
    onehot = jnp.zeros((ne, tt), F32)
    for _ in range(TOP_K):
        mx = jnp.max(cm, axis=0, keepdims=True)
        ei = jnp.min(jnp.where(cm == mx, eid, float(ne)), axis=0, keepdims=True)
        pick = eid == ei
        idx_rows.append(ei)
        w_rows.append(jnp.sum(jnp.where(pick, sc, 0.0), axis=0, keepdims=True))
        onehot = jnp.where(pick, 1.0, onehot)
        cm = jnp.where(pick, neg, cm)
    wsum = w_rows[0]
    for w in w_rows[1:]:
        wsum = wsum + w
    w_rows = [w / wsum * ROUTED_SCALE for w in w_rows]

    a_i = lax.broadcasted_iota(I32, (tt, tt), 0)
    b_i = lax.broadcasted_iota(I32, (tt, tt), 1)
    upper = jnp.where(a_i < b_i, 1.0, 0.0).astype(BF16)
    rank_full = jnp.dot(onehot.astype(BF16), upper, preferred_element_type=F32) + run_ref[...]
    r_rows = [jnp.sum(jnp.where(eid == ei, rank_full, 0.0), axis=0, keepdims=True) for ei in idx_rows]
    run = run_ref[...] + jnp.sum(onehot, axis=1, keepdims=True)
    run_ref[...] = run

    eidx_ref[...] = _rows_to_tile(idx_rows, TOP_K).astype(I32)
    ew_ref[...] = _rows_to_tile(w_rows, TOP_K)
    rank_ref[...] = _rows_to_tile(r_rows, TOP_K).astype(I32)
    cnt_ref[...] = jnp.broadcast_to(run, cnt_ref.shape)


def _route(logits_t, router_bias, tt=TILES.route_tokens):
    ne, t = logits_t.shape
    return pl.pallas_call(
        _route_kernel,
        grid=(t // tt,),
        in_specs=[pl.BlockSpec((ne, tt), lambda i: (0, i)),
                  pl.BlockSpec((ne, 1), lambda i: (0, 0))],
        out_specs=[pl.BlockSpec((TOP_K, tt), lambda i: (0, i)),
                   pl.BlockSpec((TOP_K, tt), lambda i: (0, i)),
                   pl.BlockSpec((TOP_K, tt), lambda i: (0, i)),
                   pl.BlockSpec((ne, LANES), lambda i: (0, 0))],
        out_shape=[jax.ShapeDtypeStruct((TOP_K, t), I32),
                   jax.ShapeDtypeStruct((TOP_K, t), F32),
                   jax.ShapeDtypeStruct((TOP_K, t), I32),
                   jax.ShapeDtypeStruct((ne, LANES), F32)],
        scratch_shapes=[pltpu.VMEM((ne, 1), F32)],
        compiler_params=_cparams(("arbitrary",)),
    )(logits_t, router_bias.reshape(ne, 1))


def _dest_kernel(eidx_ref, rank_ref, ps_ref, o_ref):
    ne = ps_ref.shape[0]
    tt = eidx_ref.shape[1]
    eid = lax.broadcasted_iota(I32, (ne, tt), 0)
    ps = jnp.broadcast_to(ps_ref[...], (ne, tt))
    rows = []
    for k in range(TOP_K):
        start = jnp.sum(jnp.where(eid == eidx_ref[k:k + 1, :], ps, 0.0), axis=0, keepdims=True)
        rows.append(start + rank_ref[k:k + 1, :].astype(F32))
    o_ref[...] = _rows_to_tile(rows, TOP_K).astype(I32)


def _dest(eidx, rank, pad_start, tt=TILES.dest_tokens):
    t = eidx.shape[1]
    tt = min(tt, t)
    ne = pad_start.shape[0]
    return pl.pallas_call(
        _dest_kernel,
        grid=(t // tt,),
        in_specs=[pl.BlockSpec((TOP_K, tt), lambda i: (0, i)),
                  pl.BlockSpec((TOP_K, tt), lambda i: (0, i)),
                  pl.BlockSpec((ne, 1), lambda i: (0, 0))],
        out_specs=pl.BlockSpec((TOP_K, tt), lambda i: (0, i)),
        out_shape=jax.ShapeDtypeStruct((TOP_K, t), I32),
        compiler_params=_cparams(("parallel",)),
    )(eidx, rank, pad_start.astype(F32).reshape(ne, 1))


def _dispatch_kernel(pend_ref, padded_ref, dest_ref, h_ref, xs_ref, hbuf_ref, zbuf_ref, zsem, sem, *, td, bm):
    i = pl.program_id(0)

    def tail_copy(e):
        start = pl.multiple_of(pend_ref[e] - bm, bm)
        return pltpu.make_async_copy(zbuf_ref, xs_ref.at[pl.ds(start, bm)], zsem)

    @pl.when(i == 0)
    def _():
        zbuf_ref[...] = jnp.zeros(zbuf_ref.shape, zbuf_ref.dtype)

        def start_body(e, c):
            @pl.when(padded_ref[e] > 0)
            def _():
                tail_copy(e).start()
            return c

        def wait_body(e, c):
            @pl.when(padded_ref[e] > 0)
            def _():
                tail_copy(e).wait()
            return c

        lax.fori_loop(0, N_EXPERTS, start_body, 0)
        lax.fori_loop(0, N_EXPERTS, wait_body, 0)

        def unused_copy(b):
            return pltpu.make_async_copy(zbuf_ref, xs_ref.at[pl.ds(pl.multiple_of(b * bm, bm), bm)], zsem)

        def ustart_body(b, c):
            unused_copy(b).start()
            return c

        def uwait_body(b, c):
            unused_copy(b).wait()
            return c

        first_unused = pend_ref[N_EXPERTS - 1] // bm
        lax.fori_loop(first_unused, xs_ref.shape[0] // bm, ustart_body, 0)
        lax.fori_loop(first_unused, xs_ref.shape[0] // bm, uwait_body, 0)

    v = h_ref[...]
    for s in range(SUBLANES):
        hbuf_ref[:, s, :] = v[:, s * LANES:(s + 1) * LANES]
    for j in range(td):
        for k in range(TOP_K):
            dst = xs_ref.at[dest_ref[j * TOP_K + k]]
            pltpu.make_async_copy(hbuf_ref.at[j], dst, sem).start(priority=k % 2)
    for _ in range(TOP_K):
        pltpu.make_async_copy(hbuf_ref, xs_ref.at[pl.ds(0, td)], sem).wait()


def _dispatch(pad_end, padded, dest_flat, h2p, n_rows, bm, td=TILES.move_tokens):
    t, w = h2p.shape
    assert w == SUBLANES * LANES
    kern = functools.partial(_dispatch_kernel, td=td, bm=bm)
    return pl.pallas_call(
        kern,
        grid_spec=pltpu.PrefetchScalarGridSpec(
            num_scalar_prefetch=2,
            grid=(t // td,),
            in_specs=[pl.BlockSpec((td * TOP_K,), lambda i, *_: (i,), memory_space=pltpu.SMEM),
                      pl.BlockSpec((td, w), lambda i, *_: (i, 0))],
            out_specs=pl.BlockSpec(memory_space=pl.ANY),
            scratch_shapes=[pltpu.VMEM((td, SUBLANES, LANES), U32),
                            pltpu.VMEM((bm, SUBLANES, LANES), U32),
                            pltpu.SemaphoreType.DMA(()),
                            pltpu.SemaphoreType.DMA(())]),
        out_shape=jax.ShapeDtypeStruct((n_rows, SUBLANES, LANES), U32),
        compiler_params=_cparams(("arbitrary",)),
    )(pad_end, padded, dest_flat, h2p)


def _ffn(xw, wg_ref, wu_ref, wd_ref):
    half = xw.shape[1]
    left, right = _unpack_halves(xw)
    left = left.astype(BF16)
    right = right.astype(BF16)

    def proj(w_ref):
        return (jnp.dot(left, w_ref[:half, :], preferred_element_type=F32)
                + jnp.dot(right, w_ref[half:, :], preferred_element_type=F32))

    act = (_silu(proj(wg_ref)) * proj(wu_ref)).astype(BF16)
    return jnp.dot(act, wd_ref[...], preferred_element_type=F32)


def _expert_kernel(blk_ref, eid_ref, first_ref, slot_ref, nxt_ref, more_ref, nvb_ref, *refs):
    x_ref, wg_hbm, wu_hbm, wd_hbm, o_ref, wg_f, wu_f, wd_f, wg_s, wu_s, wd_s, sems = refs
    i = pl.program_id(0)

    def rows():
        return jnp.concatenate([x_ref[:, s, :] for s in range(SUBLANES)], axis=1)

    def weight_copies(e, slot):
        return (pltpu.make_async_copy(wg_hbm.at[e], wg_f.at[slot], sems.at[slot]),
                pltpu.make_async_copy(wu_hbm.at[e], wu_f.at[slot], sems.at[slot]),
                pltpu.make_async_copy(wd_hbm.at[e], wd_f.at[slot], sems.at[slot]))

    @pl.when(i == 0)
    def _():
        for cp in weight_copies(eid_ref[0], 0):
            cp.start(priority=1)

    @pl.when(first_ref[i] == 1)
    def _():
        slot = slot_ref[i]
        for cp in weight_copies(eid_ref[i], slot):
            cp.wait()

        @pl.when(more_ref[i] == 1)
        def _():
            for cp in weight_copies(nxt_ref[i], 1 - slot):
                cp.start(priority=1)

    @pl.when(first_ref[i] == 1)
    def _():
        slot = slot_ref[i]
        wg_s[...] = wg_f[slot].astype(BF16)
        wu_s[...] = wu_f[slot].astype(BF16)
        wd_s[...] = wd_f[slot].astype(BF16)
        o_ref[...] = _pack_halves(_ffn(rows(), wg_s, wu_s, wd_s))

    @pl.when((first_ref[i] == 0) & (i < nvb_ref[0]))
    def _():
        o_ref[...] = _pack_halves(_ffn(rows(), wg_s, wu_s, wd_s))

    @pl.when(i >= nvb_ref[0])
    def _():
        o_ref[...] = jnp.zeros(o_ref.shape, o_ref.dtype)


def _experts(blk, eid, first, slot, nxt, more, nvb, xs, wg, wu, wd, bm):
    n_rows, nchunk, lanes = xs.shape
    w = nchunk * lanes
    ne, d, f = wg.shape
    x_specs = [pl.BlockSpec((bm, nchunk, lanes), lambda i, blk, *_: (blk[i], 0, 0))]
    return pl.pallas_call(
        _expert_kernel,
        grid_spec=pltpu.PrefetchScalarGridSpec(
            num_scalar_prefetch=7,
            grid=(n_rows // bm,),
            in_specs=x_specs + [pl.BlockSpec(memory_space=pl.ANY),
                                pl.BlockSpec(memory_space=pl.ANY),
                                pl.BlockSpec(memory_space=pl.ANY)],
            out_specs=pl.BlockSpec((bm, w), lambda i, *_: (i, 0)),
            scratch_shapes=[pltpu.VMEM((2, d, f), F32), pltpu.VMEM((2, d, f), F32), pltpu.VMEM((2, f, d), F32),
                            pltpu.VMEM((d, f), BF16), pltpu.VMEM((d, f), BF16), pltpu.VMEM((f, d), BF16),
                            pltpu.SemaphoreType.DMA((2,))]),
        out_shape=jax.ShapeDtypeStruct((n_rows, w), U32),
        compiler_params=_cparams(("arbitrary",)),
    )(blk, eid, first, slot, nxt, more, nvb, xs, wg, wu, wd)


def _combine_kernel(dest_ref, dnext_ref, y_ref, h_ref, x1_ref, ew_ref, wg_ref, wu_ref, wd_ref, mod_ref, g_ref,
                    o_ref, gbuf_a, gbuf_b, sems, *, tc):
    i = pl.program_id(0)
    last = pl.num_programs(0) - 1

    def issue(idx_ref, gbuf, sem):
        for j in range(tc):
            for k in range(TOP_K):
                src = y_ref.at[idx_ref[j * TOP_K + k]]
                pltpu.make_async_copy(src, gbuf.at[k, j], sem).start(priority=k % 2)

    def wait_all(gbuf, sem):
        for k in range(TOP_K):
            pltpu.make_async_copy(y_ref.at[pl.ds(0, tc)], gbuf.at[k], sem).wait()

    def compute(gbuf):
        shared = _ffn(h_ref[...], wg_ref, wu_ref, wd_ref)
        half = h_ref.shape[1]
        ew = ew_ref[...]
        left = shared[:, :half]
        right = shared[:, half:]
        for k in range(TOP_K):
            yl, yr = _unpack_halves(gbuf[k])
            wk = ew[:, k:k + 1]
            left = left + wk * yl
            right = right + wk * yr
        x2 = x1_ref[...] + mod_ref[0, 5:6, :] * jnp.concatenate([left, right], axis=1)
        o_ref[...] = x2 * lax.rsqrt(jnp.mean(x2 * x2, axis=-1, keepdims=True) + EPS) * g_ref[...]

    @pl.when(i == 0)
    def _():
        def group_body(g, c):
            base = pl.multiple_of(g * SUBLANES, SUBLANES)
            for jj in range(SUBLANES):
                for k in range(TOP_K):
                    src = y_ref.at[dest_ref[(base + jj) * TOP_K + k]]
                    pltpu.make_async_copy(src, gbuf_a.at[k, base + jj], sems.at[0]).start(priority=k % 2)
            return c

        lax.fori_loop(0, tc // SUBLANES, group_body, 0)

    @pl.when(i % 2 == 0)
    def _():
        wait_all(gbuf_a, sems.at[0])
        issue(dnext_ref, gbuf_b, sems.at[1])
        compute(gbuf_a)

    @pl.when(i % 2 == 1)
    def _():
        wait_all(gbuf_b, sems.at[1])
        issue(dnext_ref, gbuf_a, sems.at[0])
        compute(gbuf_b)

    @pl.when((i == last) & (i % 2 == 0))
    def _():
        wait_all(gbuf_b, sems.at[1])

    @pl.when((i == last) & (i % 2 == 1))
    def _():
        wait_all(gbuf_a, sems.at[0])


def _combine(dest_flat, y, h2p, x1, ew_t, wsg, wsu, wsd, mod3, g, L, tc=TILES.move_tokens):
    t, d = x1.shape
    w = h2p.shape[1]
    nt = t // tc
    kern = functools.partial(_combine_kernel, tc=tc)
    return pl.pallas_call(
        kern,
        grid=(nt,),
        in_specs=[pl.BlockSpec((tc * TOP_K,), lambda i: (i,), memory_space=pltpu.SMEM),
                  pl.BlockSpec((tc * TOP_K,), lambda i: (jnp.minimum(i + 1, nt - 1),), memory_space=pltpu.SMEM),
                  pl.BlockSpec(memory_space=pl.ANY),
                  pl.BlockSpec((tc, w), lambda i: (i, 0)),
                  pl.BlockSpec((tc, d), lambda i: (i, 0)),
                  pl.BlockSpec((tc, TOP_K), lambda i: (i, 0)),
                  pl.BlockSpec(wsg.shape, lambda i: (0, 0)),
                  pl.BlockSpec(wsu.shape, lambda i: (0, 0)),
                  pl.BlockSpec(wsd.shape, lambda i: (0, 0)),
                  pl.BlockSpec((1, 6, d), lambda i: (i * tc // L, 0, 0)),
                  pl.BlockSpec((1, d), lambda i: (0, 0))],
        out_specs=pl.BlockSpec((tc, d), lambda i: (i, 0)),
        out_shape=jax.ShapeDtypeStruct((t, d), F32),
        scratch_shapes=[pltpu.VMEM((TOP_K, tc, w), U32),
                        pltpu.VMEM((TOP_K, tc, w), U32),
                        pltpu.SemaphoreType.DMA((2,))],
        compiler_params=_cparams(("arbitrary",)),
    )(dest_flat, dest_flat, y, h2p, x1, ew_t, wsg, wsu, wsd, mod3, g.reshape(1, d))


def _split_cols(w, sizes):
    out, off = [], 0
    for s in sizes:
        out.append(w[:, off:off + s])
        off += s
    return out


def kernel(x, c, w_ada, b_ada, norm1_g, w_in, ckv_norm_g, idx_k_norm_g, w_uk, w_uv, rel_bias, lb_logits,
           attn_out_norm_g, hgrn_out_norm_g, w_out, norm2_g, w_router, router_bias, w_e_gate, w_e_up,
           w_e_down, w_s_gate, w_s_up, w_s_down, final_norm_g):
    bsz, L, d = x.shape
    t = bsz * L
    assert w_ada.shape[0] == 1, "single-layer block"
    a_width = A_HEADS * A_HEAD_DIM
    b_width = B_HEADS * B_DIM
    sizes = (a_width, A_KV_RANK, IDX_HEADS * IDX_DIM, IDX_DIM, IDX_HEADS, b_width, b_width, b_width, b_width)
    assert w_in.shape[2] == sum(sizes)

    wq_a, wckv, wiq, wik, wiw, wq_b, wf_b, wi_b, wg_b = _split_cols(w_in[0], sizes)
    w_main = jnp.concatenate([wq_a, wiq, wq_b, wi_b, wg_b], axis=1).astype(BF16)
    w_f = wf_b.astype(BF16)
    aux_pad = LANES - IDX_DIM - IDX_HEADS
    w_aux = jnp.concatenate([wckv, wik, wiw, jnp.zeros((d, aux_pad), F32)], axis=1).astype(BF16)

    mod3 = _ada(c, w_ada[0], b_ada[0]).reshape(bsz, 6, d)
    h1 = _norm1(x, mod3, norm1_g[0]).reshape(t, d)
    g1 = _matmul(h1, w_main, BF16, tm=TILES.proj_rows, tn=TILES.proj_cols)
    fb = _matmul(h1, w_f, F32, tm=TILES.proj_rows, tn=TILES.proj_cols)
    aux = _matmul(h1, w_aux, F32, tm=TILES.proj_rows, tn=w_aux.shape[1])
    ckv_n, ik_lo, ik_hi = _kvnorm(aux, ckv_norm_g[0], idx_k_norm_g[0])

    tq = min(TILES.attn, L)
    o_a = _dsa(g1, aux, ik_lo, ik_hi, ckv_n, w_uk[0].astype(BF16), w_uv[0].astype(BF16),
               _bias_tables(rel_bias, tq), attn_out_norm_g[0], bsz, L, tq)
    o_b = _hgrn(g1, fb, lb_logits, hgrn_out_norm_g[0], bsz, L)

    x1, h2p, logits_t = _out(o_a, o_b, x.reshape(t, d), w_out[0].astype(BF16), mod3, norm2_g[0],
                             w_router[0].T.astype(BF16), L)

    eidx, ew, rank, cnt = _route(logits_t, router_bias[0])

    bm = TILES.moe_rows
    counts = cnt[:, 0].astype(I32)
    padded = (counts + bm - 1) // bm * bm
    pad_end = jnp.cumsum(padded)
    pad_start = pad_end - padded
    n_rows = (t * TOP_K + N_EXPERTS * (bm - 1) + bm - 1) // bm * bm
    nb = n_rows // bm
    nvb = pad_end[-1] // bm
    blk = jnp.minimum(jnp.arange(nb, dtype=I32), nvb - 1)
    eid = jnp.minimum(jnp.sum((pad_end[None, :] <= (blk * bm)[:, None]).astype(I32), axis=1), N_EXPERTS - 1)
    ar = jnp.arange(nb, dtype=I32)
    first = ((ar < nvb) & ((ar == 0) | (eid != jnp.roll(eid, 1)))).astype(I32)
    slot = (jnp.cumsum(first) - 1) % 2
    nxt_blk = pad_end[eid] // bm
    more = (nxt_blk < nvb).astype(I32)
    nxt = eid[jnp.minimum(nxt_blk, nb - 1)]

    dest = _dest(eidx, rank, pad_start)
    dest_flat = dest.T.reshape(t * TOP_K)
    xs = _dispatch(pad_end.astype(I32), padded.astype(I32), dest_flat, h2p, n_rows, bm)

    y = _experts(blk, eid, first, slot.astype(I32), nxt.astype(I32), more, nvb.reshape(1).astype(I32), xs,
                 w_e_gate[0], w_e_up[0], w_e_down[0], bm)

    out = _combine(dest_flat, y, h2p, x1, ew.T, w_s_gate[0].astype(BF16), w_s_up[0].astype(BF16),
                   w_s_down[0].astype(BF16), mod3, final_norm_g, L)
    return out.reshape(bsz, L, d)
```

```python
import functools
import math
from typing import NamedTuple

import jax
import jax.numpy as jnp
from jax import lax
from jax.experimental import pallas as pl
from jax.experimental.pallas import tpu as pltpu

F32 = jnp.float32
BF16 = jnp.bfloat16
I32 = jnp.int32
U32 = jnp.uint32

EPS = 1e-6
A_HEADS = 8
A_HEAD_DIM = 128
A_KV_RANK = 256
IDX_HEADS = 16
IDX_DIM = 64
IDX_TOPK_MAX = 256
B_HEADS = 8
B_DIM = 128
REL_BUCKETS = 32
REL_MAX_DIST = 128
N_EXPERTS = 64
TOP_K = 8
N_GROUPS = 8
TOPK_GROUPS = 4
ROUTED_SCALE = 2.5

VMEM_LIMIT_BYTES = 56 * 1024 * 1024
LANES = 128
SUBLANES = 8


class _Tiles(NamedTuple):
    ada_cols: int = 1024
    norm_rows: int = 512
    proj_rows: int = 1024
    proj_cols: int = 512
    attn: int = 256
    hgrn_chunk: int = 128
    hgrn_rows: int = 32
    out_rows: int = 512
    route_tokens: int = 512
    dest_tokens: int = 2048
    moe_rows: int = 512
    move_tokens: int = 256


TILES = _Tiles()

NT_DIMS = (((1,), (1,)), ((), ()))
TN_DIMS = (((0,), (0,)), ((), ()))

HGRN_MAX_BLOCK_DECAY = 80.0
LOG2E = math.log2(math.e)
INT_MIN = -2 ** 31
KEY_NEG_INF = -2139095041


def _cparams(sem):
    return pltpu.CompilerParams(dimension_semantics=sem, vmem_limit_bytes=VMEM_LIMIT_BYTES)


def _silu(v):
    return v * jax.nn.sigmoid(v)


def _pack_halves(v):
    n = v.shape[1] // 2
    lo = lax.bitcast_convert_type(v[:, :n].astype(BF16).astype(F32), U32)
    hi = lax.bitcast_convert_type(v[:, n:].astype(BF16).astype(F32), U32)
    return lax.shift_right_logical(lo, jnp.uint32(16)) | (hi & jnp.uint32(0xFFFF0000))


def _unpack_halves(w):
    left = lax.bitcast_convert_type(lax.shift_left(w, jnp.uint32(16)), F32)
    right = lax.bitcast_convert_type(w & jnp.uint32(0xFFFF0000), F32)
    return left, right


def _ada_kernel(c_ref, w_ref, b_ref, o_ref):
    a = _silu(c_ref[...]).astype(BF16)
    o_ref[...] = jnp.dot(a, w_ref[...].astype(BF16), preferred_element_type=F32) + b_ref[...]


def _ada(c, w, b, tn=TILES.ada_cols):
    bsz, d = c.shape
    n = w.shape[1]
    return pl.pallas_call(
        _ada_kernel,
        grid=(n // tn,),
        in_specs=[pl.BlockSpec((bsz, d), lambda j: (0, 0)),
                  pl.BlockSpec((d, tn), lambda j: (0, j)),
                  pl.BlockSpec((1, tn), lambda j: (0, j))],
        out_specs=pl.BlockSpec((bsz, tn), lambda j: (0, j)),
        out_shape=jax.ShapeDtypeStruct((bsz, n), F32),
        compiler_params=_cparams(("arbitrary",)),
    )(c, w, b.reshape(1, n))


def _norm1_kernel(x_ref, mod_ref, g_ref, o_ref):
    x = x_ref[0]
    y = x * lax.rsqrt(jnp.mean(x * x, axis=-1, keepdims=True) + EPS) * g_ref[...]
    sh = mod_ref[0, 0:1, :]
    sc = mod_ref[0, 1:2, :]
    o_ref[0] = (y * (1.0 + sc) + sh).astype(o_ref.dtype)


def _norm1(x, mod3, g, tm=TILES.norm_rows):
    bsz, L, d = x.shape
    return pl.pallas_call(
        _norm1_kernel,
        grid=(bsz, L // tm),
        in_specs=[pl.BlockSpec((1, tm, d), lambda b, i: (b, i, 0)),
                  pl.BlockSpec((1, 6, d), lambda b, i: (b, 0, 0)),
                  pl.BlockSpec((1, d), lambda b, i: (0, 0))],
        out_specs=pl.BlockSpec((1, tm, d), lambda b, i: (b, i, 0)),
        out_shape=jax.ShapeDtypeStruct((bsz, L, d), BF16),
        compiler_params=_cparams(("parallel", "parallel")),
    )(x, mod3, g.reshape(1, d))


def _mm_kernel(a_ref, w_ref, o_ref):
    o_ref[...] = jnp.dot(a_ref[...], w_ref[...], preferred_element_type=F32).astype(o_ref.dtype)


def _matmul(a, w, out_dtype, tm, tn):
    m, k = a.shape
    n = w.shape[1]
    return pl.pallas_call(
        _mm_kernel,
        grid=(m // tm, n // tn),
        in_specs=[pl.BlockSpec((tm, k), lambda i, j: (i, 0)),
                  pl.BlockSpec((k, tn), lambda i, j: (0, j))],
        out_specs=pl.BlockSpec((tm, tn), lambda i, j: (i, j)),
        out_shape=jax.ShapeDtypeStruct((m, n), out_dtype),
        compiler_params=_cparams(("parallel", "arbitrary")),
    )(a, w)


def _kvnorm_kernel(aux_ref, gc_ref, gk_ref, ckv_ref, iklo_ref, ikhi_ref):
    ckv = aux_ref[:, :A_KV_RANK]
    ckv_ref[...] = (ckv * lax.rsqrt(jnp.mean(ckv * ckv, axis=-1, keepdims=True) + EPS)
                    * gc_ref[...]).astype(BF16)
    v = aux_ref[:, A_KV_RANK:A_KV_RANK + LANES]
    lane = lax.broadcasted_iota(I32, v.shape, 1)
    ik = jnp.where(lane < IDX_DIM, v, 0.0)
    ms = jnp.sum(ik * ik, axis=-1, keepdims=True) * (1.0 / IDX_DIM)
    ikn = ik * lax.rsqrt(ms + EPS) * gk_ref[...]
    iklo_ref[...] = ikn.astype(BF16)
    ikhi_ref[...] = pltpu.roll(ikn, IDX_DIM, 1).astype(BF16)


def _kvnorm(aux, gc, gk, tm=TILES.proj_rows):
    t = aux.shape[0]
    gk_pad = jnp.concatenate([gk, jnp.zeros((LANES - IDX_DIM,), F32)]).reshape(1, LANES)
    return pl.pallas_call(
        _kvnorm_kernel,
        grid=(t // tm,),
        in_specs=[pl.BlockSpec((tm, aux.shape[1]), lambda i: (i, 0)),
                  pl.BlockSpec((1, A_KV_RANK), lambda i: (0, 0)),
                  pl.BlockSpec((1, LANES), lambda i: (0, 0))],
        out_specs=[pl.BlockSpec((tm, A_KV_RANK), lambda i: (i, 0)),
                   pl.BlockSpec((tm, LANES), lambda i: (i, 0)),
                   pl.BlockSpec((tm, LANES), lambda i: (i, 0))],
        out_shape=[jax.ShapeDtypeStruct((t, A_KV_RANK), BF16),
                   jax.ShapeDtypeStruct((t, LANES), BF16),
                   jax.ShapeDtypeStruct((t, LANES), BF16)],
        compiler_params=_cparams(("parallel",)),
    )(aux, gc.reshape(1, A_KV_RANK), gk_pad)


def _t5_bucket(rel):
    n = jnp.maximum(rel, 0)
    max_exact = REL_BUCKETS // 2
    n_large = jnp.maximum(n, max_exact).astype(F32)
    large = max_exact + (jnp.log(n_large / max_exact) / math.log(REL_MAX_DIST / max_exact)
                         * (REL_BUCKETS - max_exact)).astype(I32)
    large = jnp.minimum(large, REL_BUCKETS - 1)
    return jnp.where(n < max_exact, n, large)


def _bias_tables(rel_bias, tq):
    assert tq + 1 >= REL_MAX_DIST
    nh = rel_bias.shape[1]
    dist = jnp.maximum(jnp.arange(3 * tq + 1, dtype=I32) - tq, 0)
    v = rel_bias.astype(F32)[_t5_bucket(dist)].T * LOG2E
    n = v.shape[1]
    x = jnp.broadcast_to(v[:, None, :], (nh, tq, n)).reshape(nh, tq * n)[:, :tq * (n - 1)].reshape(nh, tq, n - 1)
    near = x[:, :, tq:2 * tq]
    prev = x[:, :, 2 * tq:3 * tq]
    far = jnp.broadcast_to(v[:, n - 1][:, None, None], near.shape)
    return jnp.stack([near, prev, far])


def _dsa_kernel(qa_ref, iq_ref, aux_ref, iklo_ref, ikhi_ref, ckv_ref, ckvt_ref, wuk_ref, wuvt_ref, bias_ref,
                g_ref, o_ref, iqt_ref, iwt_ref, key_ref, qlt_ref, m_ref, l_ref, acc_ref, tie_ref, madd_ref,
                *, tq, topk):
    i = pl.program_id(1)
    nh = A_HEADS
    npair = IDX_HEADS // 2

    r_i = lax.broadcasted_iota(I32, (LANES, LANES), 0)
    c_i = lax.broadcasted_iota(I32, (LANES, LANES), 1)
    eye = jnp.where(r_i == c_i, 1.0, 0.0).astype(BF16)
    for p in range(npair):
        iqt_ref[:, p * tq:(p + 1) * tq] = lax.dot_general(
            eye, iq_ref[:, p * LANES:(p + 1) * LANES], NT_DIMS, preferred_element_type=F32).astype(BF16)
    iwt_ref[...] = (jnp.transpose(aux_ref[...])[IDX_DIM:IDX_DIM + IDX_HEADS, :]
                    * (IDX_HEADS ** -0.5 * IDX_DIM ** -0.5))
    for h in range(nh):
        ql = lax.dot_general(wuk_ref[h], qa_ref[:, h * A_HEAD_DIM:(h + 1) * A_HEAD_DIM], NT_DIMS,
                             preferred_element_type=F32)
        qlt_ref[:, h * tq:(h + 1) * tq] = (ql * (A_HEAD_DIM ** -0.5 * LOG2E)).astype(BF16)

    kpos = lax.broadcasted_iota(I32, (tq, tq), 0)
    qpos = lax.broadcasted_iota(I32, (tq, tq), 1) + i * tq

    def score_body(kc, carry):
        off = pl.multiple_of(kc * tq, tq)
        klo = iklo_ref[0, pl.ds(off, tq), :]
        khi = ikhi_ref[0, pl.ds(off, tq), :]
        acc = jnp.zeros((tq, tq), F32)
        for p in range(npair):
            rhs = iqt_ref[:, p * tq:(p + 1) * tq]
            se = jnp.dot(klo, rhs, preferred_element_type=F32)
            so = jnp.dot(khi, rhs, preferred_element_type=F32)
            acc = acc + jnp.maximum(se, 0.0) * iwt_ref[2 * p:2 * p + 1, :]
            acc = acc + jnp.maximum(so, 0.0) * iwt_ref[2 * p + 1:2 * p + 2, :]
        bits = lax.bitcast_convert_type(acc, I32)
        key = jnp.where(bits >= 0, bits, bits ^ jnp.int32(0x7FFFFFFF))
        key_ref[kc] = jnp.where(kpos + off <= qpos, key, jnp.int32(KEY_NEG_INF))
        return carry

    lax.fori_loop(0, i + 1, score_body, 0)

    def count_ge(cand):
        def body(kc, c):
            hit = jnp.where(key_ref[kc] >= cand, 1.0, 0.0)
            return c + jnp.sum(hit.reshape(tq // 8, 8, tq), axis=0)
        c = lax.fori_loop(0, i + 1, body, jnp.zeros((8, tq), F32))
        return jnp.sum(c, axis=0, keepdims=True)

    kf = float(topk)
    thr = jnp.where(count_ge(jnp.zeros((1, tq), I32)) >= kf, jnp.int32(0), jnp.int32(INT_MIN))

    def bit_body(j, thr):
        cand = thr | lax.shift_left(jnp.int32(1), 30 - j)
        return jnp.where(count_ge(cand) >= kf, cand, thr)

    thr = lax.fori_loop(0, 31, bit_body, thr)

    def count_gt_eq():
        def body(kc, c):
            key = key_ref[kc]
            gt = jnp.where(key > thr, 1.0, 0.0)
            eq = jnp.where(key == thr, 1.0, 0.0)
            return (c[0] + jnp.sum(gt.reshape(tq // 8, 8, tq), axis=0),
                    c[1] + jnp.sum(eq.reshape(tq // 8, 8, tq), axis=0))
        z = jnp.zeros((8, tq), F32)
        c = lax.fori_loop(0, i + 1, body, (z, z))
        return jnp.sum(c[0], axis=0, keepdims=True), jnp.sum(c[1], axis=0, keepdims=True)

    n_gt, n_eq = count_gt_eq()
    need = kf - n_gt
    tied = (n_gt + n_eq > kf) & (thr > jnp.int32(KEY_NEG_INF))
    has_tie = jnp.max(jnp.where(tied, 1.0, 0.0)) > 0.0
    tie_ref[...] = jnp.zeros(tie_ref.shape, F32)

    m_ref[...] = jnp.full(m_ref.shape, -jnp.inf, F32)
    l_ref[...] = jnp.zeros(l_ref.shape, F32)
    acc_ref[...] = jnp.zeros(acc_ref.shape, F32)

    def att_body(kc, carry):
        off = pl.multiple_of(kc * tq, tq)
        ckv = ckv_ref[0, pl.ds(off, tq), :]
        ckvt = ckvt_ref[0, kc]
        key = key_ref[kc]
        causal = key > jnp.int32(KEY_NEG_INF)

        @pl.when(jnp.logical_not(has_tie))
        def _():
            madd_ref[...] = jnp.where((key >= thr) & causal, 0.0, -jnp.inf)

        @pl.when(has_tie)
        def _():
            eq = key == thr
            eqf = jnp.where(eq, 1.0, 0.0)
            before = (lax.broadcasted_iota(I32, (tq, tq), 1) < lax.broadcasted_iota(I32, (tq, tq), 0))
            rank = jnp.dot(jnp.where(before, 1.0, 0.0).astype(BF16), eqf.astype(BF16),
                           preferred_element_type=F32) + tie_ref[...]
            keep = (key > thr) | (eq & (rank < need))
            madd_ref[...] = jnp.where(keep & causal, 0.0, -jnp.inf)
            tie_ref[...] = tie_ref[...] + jnp.sum(eqf, axis=0, keepdims=True)

        madd = madd_ref[...]
        d = jnp.minimum(i - kc, 2)
        for h in range(nh):
            s = jnp.dot(ckv, qlt_ref[:, h * tq:(h + 1) * tq], preferred_element_type=F32)
            s = s + (bias_ref[d, h] + madd)
            m_old = m_ref[h:h + 1, :]
            m_new = jnp.maximum(m_old, jnp.max(s, axis=0, keepdims=True))
            m_safe = jnp.where(m_new == -jnp.inf, 0.0, m_new)
            alpha = jnp.exp2(m_old - m_safe)
            p = jnp.exp2(s - m_safe)
            l_ref[h:h + 1, :] = alpha * l_ref[h:h + 1, :] + jnp.sum(p, axis=0, keepdims=True)
            acc_ref[h] = alpha * acc_ref[h] + jnp.dot(ckvt, p.astype(BF16), preferred_element_type=F32)
            m_ref[h:h + 1, :] = m_new
        return carry

    lax.fori_loop(0, i + 1, att_body, 0)

    outs = []
    for h in range(nh):
        o_lat = (acc_ref[h] / l_ref[h:h + 1, :]).astype(BF16)
        outs.append(jnp.transpose(jnp.dot(wuvt_ref[h], o_lat, preferred_element_type=F32)))
    o = jnp.concatenate(outs, axis=1)
    o = o * lax.rsqrt(jnp.mean(o * o, axis=-1, keepdims=True) + EPS) * g_ref[...]
    o_ref[...] = o.astype(o_ref.dtype)


def _dsa(g1, aux, ik_lo, ik_hi, ckv_n, w_uk, w_uv, bias_tab, g, bsz, L, tq):
    t = bsz * L
    nq = L // tq
    topk = min(IDX_TOPK_MAX, L // 4)
    aux_blk = A_KV_RANK // LANES
    kern = functools.partial(_dsa_kernel, tq=tq, topk=topk)
    width = A_HEADS * A_HEAD_DIM
    ckv3 = ckv_n.reshape(bsz, L, A_KV_RANK)
    ckvt = ckv_n.reshape(bsz, nq, tq, A_KV_RANK).transpose(0, 1, 3, 2)
    return pl.pallas_call(
        kern,
        grid=(bsz, nq),
        in_specs=[pl.BlockSpec((tq, width), lambda b, i: (b * nq + i, 0)),
                  pl.BlockSpec((tq, IDX_HEADS * IDX_DIM), lambda b, i: (b * nq + i, 1)),
                  pl.BlockSpec((tq, LANES), lambda b, i: (b * nq + i, aux_blk)),
                  pl.BlockSpec((1, L, LANES), lambda b, i: (b, 0, 0)),
                  pl.BlockSpec((1, L, LANES), lambda b, i: (b, 0, 0)),
                  pl.BlockSpec((1, L, A_KV_RANK), lambda b, i: (b, 0, 0)),
                  pl.BlockSpec((1, nq, A_KV_RANK, tq), lambda b, i: (b, 0, 0, 0)),
                  pl.BlockSpec((A_HEADS, A_KV_RANK, A_HEAD_DIM), lambda b, i: (0, 0, 0)),
                  pl.BlockSpec((A_HEADS, A_HEAD_DIM, A_KV_RANK), lambda b, i: (0, 0, 0)),
                  pl.BlockSpec((3, A_HEADS, tq, tq), lambda b, i: (0, 0, 0, 0)),
                  pl.BlockSpec((1, width), lambda b, i: (0, 0))],
        out_specs=pl.BlockSpec((tq, width), lambda b, i: (b * nq + i, 0)),
        out_shape=jax.ShapeDtypeStruct((t, width), BF16),
        scratch_shapes=[pltpu.VMEM((LANES, IDX_HEADS // 2 * tq), BF16),
                        pltpu.VMEM((IDX_HEADS, tq), F32),
                        pltpu.VMEM((nq, tq, tq), I32),
                        pltpu.VMEM((A_KV_RANK, A_HEADS * tq), BF16),
                        pltpu.VMEM((A_HEADS, tq), F32),
                        pltpu.VMEM((A_HEADS, tq), F32),
                        pltpu.VMEM((A_HEADS, A_KV_RANK, tq), F32),
                        pltpu.VMEM((1, tq), F32),
                        pltpu.VMEM((tq, tq), F32)],
        compiler_params=_cparams(("parallel", "arbitrary")),
    )(g1, g1, aux, ik_lo.reshape(bsz, L, LANES), ik_hi.reshape(bsz, L, LANES),
      ckv3, ckvt, w_uk, jnp.transpose(w_uv, (0, 2, 1)), bias_tab, g.reshape(1, width))


def _hgrn_kernel(q_ref, i_ref, gate_ref, f_ref, lbl_ref, ng_ref, o_ref, st_ref, bc_ref, kk_ref, sc_ref,
                 *, chunk, rblk):
    @pl.when(pl.program_id(1) == 0)
    def _():
        st_ref[...] = jnp.zeros(st_ref.shape, F32)

    ll = lbl_ref[...]
    ex = jnp.exp(ll - jnp.max(ll, axis=0, keepdims=True))
    lb_all = ex[0:1] / jnp.sum(ex, axis=0, keepdims=True)

    r_i = lax.broadcasted_iota(I32, (chunk, chunk), 0)
    c_i = lax.broadcasted_iota(I32, (chunk, chunk), 1)
    tri = jnp.where(r_i >= c_i, 1.0, 0.0).astype(BF16)
    row_k = lax.broadcasted_iota(I32, (chunk, B_DIM), 0)
    nblk = chunk // rblk
    zero_row = jnp.zeros((1, B_DIM), F32)

    growth = zero_row
    for h in range(B_HEADS):
        sl = slice(h * B_DIM, (h + 1) * B_DIM)
        lb = lb_all[:, sl]
        f = lb + (1.0 - lb) * jax.nn.sigmoid(f_ref[:, sl])
        lf = jnp.log(f)
        kk_ref[h] = 1.0 - f
        l1 = lf.astype(BF16)
        r1 = lf - l1.astype(F32)
        l2 = r1.astype(BF16)
        l3 = (r1 - l2.astype(F32)).astype(BF16)
        cs = jnp.dot(tri, jnp.concatenate([l1, l2, l3], axis=1), preferred_element_type=F32)
        bc = cs[:, :B_DIM] + cs[:, B_DIM:2 * B_DIM] + cs[:, 2 * B_DIM:]
        bc_ref[h] = bc
        for r in range(nblk):
            top = bc[r * rblk - 1:r * rblk] if r > 0 else zero_row
            growth = jnp.maximum(growth, top - bc[(r + 1) * rblk - 1:(r + 1) * rblk])
    overflow_risk = jnp.max(growth) > HGRN_MAX_BLOCK_DECAY

    def block_scores(h, before_only):
        sl = slice(h * B_DIM, (h + 1) * B_DIM)
        bc = bc_ref[h]
        kk = kk_ref[h]
        q = q_ref[:, sl].astype(F32)
        parts = []
        for r in range(nblk):
            lo, hi = r * rblk, (r + 1) * rblk
            base = bc[lo - 1:lo] if r > 0 else zero_row
            qt = (q[lo:hi] * jnp.exp(bc[lo:hi] - base)).astype(BF16)
            if before_only:
                kt = jnp.where(row_k < lo, kk * jnp.exp(jnp.where(row_k < lo, base - bc, 0.0)), 0.0)
            else:
                kt = kk * jnp.exp(jnp.where(row_k < hi, base - bc, 0.0))
            parts.append(lax.dot_general(qt, kt.astype(BF16), NT_DIMS, preferred_element_type=F32))
        return jnp.concatenate(parts, axis=0)

    @pl.when(jnp.logical_not(overflow_risk))
    def _():
        for h in range(B_HEADS):
            sc_ref[h] = jnp.where(c_i <= r_i, block_scores(h, False), 0.0)

    @pl.when(overflow_risk)
    def _():
        for h in range(B_HEADS):
            sl = slice(h * B_DIM, (h + 1) * B_DIM)
            bc = bc_ref[h]
            kk = kk_ref[h]
            q = q_ref[:, sl].astype(F32)
            sc = block_scores(h, True)
            for dlt in range(rblk):
                bc_s = pltpu.roll(bc, dlt, 0) if dlt else bc
                kk_s = pltpu.roll(kk, dlt, 0) if dlt else kk
                ok = (row_k & (rblk - 1)) >= dlt
                band = q * kk_s * jnp.exp(jnp.where(ok, bc - bc_s, -jnp.inf))
                sc = sc + jnp.where(c_i == r_i - dlt, jnp.sum(band, axis=1, keepdims=True), 0.0)
            sc_ref[h] = sc

    for h in range(B_HEADS):
        sl = slice(h * B_DIM, (h + 1) * B_DIM)
        bc = bc_ref[h]
        kk = kk_ref[h]
        q = q_ref[:, sl].astype(F32)
        v = i_ref[:, sl]
        st = st_ref[h]
        o = lax.dot_general((q * jnp.exp(bc)).astype(BF16), st.astype(BF16), NT_DIMS,
                            preferred_element_type=F32)
        o = o + jnp.dot(sc_ref[h].astype(BF16), v, preferred_element_type=F32)

        last = bc[chunk - 1:chunk]
        kd = (kk * jnp.exp(last - bc)).astype(BF16)
        st_ref[h] = st * jnp.exp(last) + lax.dot_general(v, kd, TN_DIMS, preferred_element_type=F32)

        y = o * lax.rsqrt(jnp.mean(o * o, axis=-1, keepdims=True) + EPS) * ng_ref[:, sl]
        o_ref[:, sl] = (y * _silu(gate_ref[:, sl].astype(F32))).astype(o_ref.dtype)


def _hgrn(g1, fb, lb_logits, ng, bsz, L, chunk=TILES.hgrn_chunk, rblk=TILES.hgrn_rows):
    assert rblk & (rblk - 1) == 0 and chunk % rblk == 0
    t = bsz * L
    nc = L // chunk
    width = B_HEADS * B_DIM
    kern = functools.partial(_hgrn_kernel, chunk=chunk, rblk=rblk)
    return pl.pallas_call(
        kern,
        grid=(bsz, nc),
        in_specs=[pl.BlockSpec((chunk, width), lambda b, c: (b * nc + c, 2)),
                  pl.BlockSpec((chunk, width), lambda b, c: (b * nc + c, 3)),
                  pl.BlockSpec((chunk, width), lambda b, c: (b * nc + c, 4)),
                  pl.BlockSpec((chunk, width), lambda b, c: (b * nc + c, 0)),
                  pl.BlockSpec(lb_logits.shape, lambda b, c: (0, 0)),
                  pl.BlockSpec((1, width), lambda b, c: (0, 0))],
        out_specs=pl.BlockSpec((chunk, width), lambda b, c: (b * nc + c, 0)),
        out_shape=jax.ShapeDtypeStruct((t, width), BF16),
        scratch_shapes=[pltpu.VMEM((B_HEADS, B_DIM, B_DIM), F32),
                        pltpu.VMEM((B_HEADS, chunk, B_DIM), F32),
                        pltpu.VMEM((B_HEADS, chunk, B_DIM), F32),
                        pltpu.VMEM((B_HEADS, chunk, chunk), F32)],
        compiler_params=_cparams(("parallel", "arbitrary")),
    )(g1, g1, g1, fb, lb_logits, ng.reshape(1, width))


def _out_kernel(oa_ref, ob_ref, x_ref, wa_ref, wb_ref, mod_ref, g_ref, wr_ref, x1_ref, h2_ref, lg_ref):
    mix = jnp.dot(oa_ref[...], wa_ref[...], preferred_element_type=F32)
    mix = mix + jnp.dot(ob_ref[...], wb_ref[...], preferred_element_type=F32)
    x1 = x_ref[...] + mod_ref[0, 2:3, :] * mix
    x1_ref[...] = x1
    y = x1 * lax.rsqrt(jnp.mean(x1 * x1, axis=-1, keepdims=True) + EPS) * g_ref[...]
    h2 = y * (1.0 + mod_ref[0, 4:5, :]) + mod_ref[0, 3:4, :]
    h2_ref[...] = _pack_halves(h2)
    lg_ref[...] = lax.dot_general(wr_ref[...], h2.astype(BF16), NT_DIMS, preferred_element_type=F32)


def _out(oa, ob, x2d, w_out, mod3, g, w_router_t, L, tm=TILES.out_rows):
    t, d = x2d.shape
    half = oa.shape[1]
    ne = w_router_t.shape[0]
    return pl.pallas_call(
        _out_kernel,
        grid=(t // tm,),
        in_specs=[pl.BlockSpec((tm, half), lambda i: (i, 0)),
                  pl.BlockSpec((tm, half), lambda i: (i, 0)),
                  pl.BlockSpec((tm, d), lambda i: (i, 0)),
                  pl.BlockSpec((half, d), lambda i: (0, 0)),
                  pl.BlockSpec((half, d), lambda i: (1, 0)),
                  pl.BlockSpec((1, 6, d), lambda i: (i * tm // L, 0, 0)),
                  pl.BlockSpec((1, d), lambda i: (0, 0)),
                  pl.BlockSpec((ne, d), lambda i: (0, 0))],
        out_specs=[pl.BlockSpec((tm, d), lambda i: (i, 0)),
                   pl.BlockSpec((tm, d // 2), lambda i: (i, 0)),
                   pl.BlockSpec((ne, tm), lambda i: (0, i))],
        out_shape=[jax.ShapeDtypeStruct((t, d), F32),
                   jax.ShapeDtypeStruct((t, d // 2), U32),
                   jax.ShapeDtypeStruct((ne, t), F32)],
        compiler_params=_cparams(("parallel",)),
    )(oa, ob, x2d, w_out, w_out, mod3, g.reshape(1, d), w_router_t)


def _rows_to_tile(rows, nrow):
    n = rows[0].shape[1]
    ridx = lax.broadcasted_iota(I32, (nrow, n), 0)
    out = jnp.zeros((nrow, n), rows[0].dtype)
    for r, v in enumerate(rows):
        out = jnp.where(ridx == r, jnp.broadcast_to(v, (nrow, n)), out)
    return out


def _route_kernel(lg_ref, rb_ref, eidx_ref, ew_ref, rank_ref, cnt_ref, run_ref):
    @pl.when(pl.program_id(0) == 0)
    def _():
        run_ref[...] = jnp.zeros(run_ref.shape, F32)

    ne, tt = lg_ref.shape
    per = ne // N_GROUPS
    sc = jax.nn.sigmoid(lg_ref[...])
    ch = sc + rb_ref[...]
    neg = -jnp.inf

    sub = lax.broadcasted_iota(I32, (per, tt), 0).astype(F32)
    gsc = []
    for g in range(N_GROUPS):
        cg = ch[g * per:(g + 1) * per]
        m1 = jnp.max(cg, axis=0, keepdims=True)
        first = jnp.min(jnp.where(cg == m1, sub, float(per)), axis=0, keepdims=True)
        m2 = jnp.max(jnp.where(sub == first, neg, cg), axis=0, keepdims=True)
        gsc.append(m1 + m2)
    grp = _rows_to_tile(gsc, N_GROUPS)

    gid = lax.broadcasted_iota(I32, (N_GROUPS, tt), 0).astype(F32)
    gsel = jnp.zeros((N_GROUPS, tt), F32)
    for _ in range(TOPK_GROUPS):
        mx = jnp.max(grp, axis=0, keepdims=True)
        gi = jnp.min(jnp.where(grp == mx, gid, float(N_GROUPS)), axis=0, keepdims=True)
        pick = gid == gi
        gsel = jnp.where(pick, 1.0, gsel)
        grp = jnp.where(pick, neg, grp)

    eid = lax.broadcasted_iota(I32, (ne, tt), 0).astype(F32)
    cm = jnp.full((ne, tt), neg, F32)
    for g in range(N_GROUPS):
        in_g = (eid >= float(g * per)) & (eid < float((g + 1) * per))
        cm = jnp.where(in_g & (jnp.broadcast_to(gsel[g:g + 1], (ne, tt)) > 0.5), ch, cm)

    idx_rows, w_rows = [], []
    onehot = jnp.zeros((ne, tt), F32)
    for _ in range(TOP_K):
        mx = jnp.max(cm, axis=0, keepdims=True)
        ei = jnp.min(jnp.where(cm == mx, eid, float(ne)), axis=0, keepdims=True)
        pick = eid == ei
        idx_rows.append(ei)
        w_rows.append(jnp.sum(jnp.where(pick, sc, 0.0), axis=0, keepdims=True))
        onehot = jnp.where(pick, 1.0, onehot)
        cm = jnp.where(pick, neg, cm)
    wsum = w_rows[0]
    for w in w_rows[1:]:
        wsum = wsum + w
    w_rows = [w / wsum * ROUTED_SCALE for w in w_rows]

    a_i = lax.broadcasted_iota(I32, (tt, tt), 0)
    b_i = lax.broadcasted_iota(I32, (tt, tt), 1)
    upper = jnp.where(a_i < b_i, 1.0, 0.0).astype(BF16)
    rank_full = jnp.dot(onehot.astype(BF16), upper, preferred_element_type=F32) + run_ref[...]
    r_rows = [jnp.sum(jnp.where(eid == ei, rank_full, 0.0), axis=0, keepdims=True) for ei in idx_rows]
    run = run_ref[...] + jnp.sum(onehot, axis=1, keepdims=True)
    run_ref[...] = run

    eidx_ref[...] = _rows_to_tile(idx_rows, TOP_K).astype(I32)
    ew_ref[...] = _rows_to_tile(w_rows, TOP_K)
    rank_ref[...] = _rows_to_tile(r_rows, TOP_K).astype(I32)
    cnt_ref[...] = jnp.broadcast_to(run, cnt_ref.shape)


def _route(logits_t, router_bias, tt=TILES.route_tokens):
    ne, t = logits_t.shape
    return pl.pallas_call(
        _route_kernel,
        grid=(t // tt,),
        in_specs=[pl.BlockSpec((ne, tt), lambda i: (0, i)),
                  pl.BlockSpec((ne, 1), lambda i: (0, 0))],
        out_specs=[pl.BlockSpec((TOP_K, tt), lambda i: (0, i)),
                   pl.BlockSpec((TOP_K, tt), lambda i: (0, i)),
                   pl.BlockSpec((TOP_K, tt), lambda i: (0, i)),
                   pl.BlockSpec((ne, LANES), lambda i: (0, 0))],
        out_shape=[jax.ShapeDtypeStruct((TOP_K, t), I32),
                   jax.ShapeDtypeStruct((TOP_K, t), F32),
                   jax.ShapeDtypeStruct((TOP_K, t), I32),
                   jax.ShapeDtypeStruct((ne, LANES), F32)],
        scratch_shapes=[pltpu.VMEM((ne, 1), F32)],
        compiler_params=_cparams(("arbitrary",)),
    )(logits_t, router_bias.reshape(ne, 1))


def _dest_kernel(eidx_ref, rank_ref, ps_ref, o_ref):
    ne = ps_ref.shape[0]
    tt = eidx_ref.shape[1]
    eid = lax.broadcasted_iota(I32, (ne, tt), 0)
    ps = jnp.broadcast_to(ps_ref[...], (ne, tt))
    rows = []
    for k in range(TOP_K):
        start = jnp.sum(jnp.where(eid == eidx_ref[k:k + 1, :], ps, 0.0), axis=0, keepdims=True)
        rows.append(start + rank_ref[k:k + 1, :].astype(F32))
    o_ref[...] = _rows_to_tile(rows, TOP_K).astype(I32)


def _dest(eidx, rank, pad_start, tt=TILES.dest_tokens):
    t = eidx.shape[1]
    tt = min(tt, t)
    ne = pad_start.shape[0]
    return pl.pallas_call(
        _dest_kernel,
        grid=(t // tt,),
        in_specs=[pl.BlockSpec((TOP_K, tt), lambda i: (0, i)),
                  pl.BlockSpec((TOP_K, tt), lambda i: (0, i)),
                  pl.BlockSpec((ne, 1), lambda i: (0, 0))],
        out_specs=pl.BlockSpec((TOP_K, tt), lambda i: (0, i)),
        out_shape=jax.ShapeDtypeStruct((TOP_K, t), I32),
        compiler_params=_cparams(("parallel",)),
    )(eidx, rank, pad_start.astype(F32).reshape(ne, 1))


def _dispatch_kernel(pend_ref, padded_ref, dest_ref, h_ref, xs_ref, zbuf_ref, zsem, sem, *, td, bm):
    i = pl.program_id(0)

    def tail_copy(e):
        start = pl.multiple_of(pend_ref[e] - bm, bm)
        return pltpu.make_async_copy(zbuf_ref, xs_ref.at[pl.ds(start, bm)], zsem)

    @pl.when(i == 0)
    def _():
        zbuf_ref[...] = jnp.zeros(zbuf_ref.shape, zbuf_ref.dtype)

        def start_body(e, c):
            @pl.when(padded_ref[e] > 0)
            def _():
                tail_copy(e).start()
            return c

        def wait_body(e, c):
            @pl.when(padded_ref[e] > 0)
            def _():
                tail_copy(e).wait()
            return c

        lax.fori_loop(0, N_EXPERTS, start_body, 0)
        lax.fori_loop(0, N_EXPERTS, wait_body, 0)

        def unused_copy(b):
            return pltpu.make_async_copy(zbuf_ref, xs_ref.at[pl.ds(pl.multiple_of(b * bm, bm), bm)], zsem)

        def ustart_body(b, c):
            unused_copy(b).start()
            return c

        def uwait_body(b, c):
            unused_copy(b).wait()
            return c

        first_unused = pend_ref[N_EXPERTS - 1] // bm
        lax.fori_loop(first_unused, xs_ref.shape[0] // bm, ustart_body, 0)
        lax.fori_loop(first_unused, xs_ref.shape[0] // bm, uwait_body, 0)

    for j in range(td):
        for k in range(TOP_K):
            dst = xs_ref.at[dest_ref[j * TOP_K + k]]
            pltpu.make_async_copy(h_ref.at[j], dst, sem).start(priority=k % 2)
    for _ in range(TOP_K):
        pltpu.make_async_copy(h_ref, xs_ref.at[pl.ds(0, td)], sem).wait()


def _dispatch(pad_end, padded, dest_flat, h2p, n_rows, bm, td=TILES.move_tokens):
    t, w = h2p.shape
    kern = functools.partial(_dispatch_kernel, td=td, bm=bm)
    return pl.pallas_call(
        kern,
        grid_spec=pltpu.PrefetchScalarGridSpec(
            num_scalar_prefetch=2,
            grid=(t // td,),
            in_specs=[pl.BlockSpec((td * TOP_K,), lambda i, *_: (i,), memory_space=pltpu.SMEM),
                      pl.BlockSpec((td, w), lambda i, *_: (i, 0))],
            out_specs=pl.BlockSpec(memory_space=pl.ANY),
            scratch_shapes=[pltpu.VMEM((bm, w), U32),
                            pltpu.SemaphoreType.DMA(()),
                            pltpu.SemaphoreType.DMA(())]),
        out_shape=jax.ShapeDtypeStruct((n_rows, w), U32),
        compiler_params=_cparams(("arbitrary",)),
    )(pad_end, padded, dest_flat, h2p)


def _ffn(xw, wg_ref, wu_ref, wd_ref):
    half = xw.shape[1]
    left, right = _unpack_halves(xw)
    left = left.astype(BF16)
    right = right.astype(BF16)

    def proj(w_ref):
        return (jnp.dot(left, w_ref[:half, :], preferred_element_type=F32)
                + jnp.dot(right, w_ref[half:, :], preferred_element_type=F32))

    act = (_silu(proj(wg_ref)) * proj(wu_ref)).astype(BF16)
    return jnp.dot(act, wd_ref[...], preferred_element_type=F32)


def _expert_kernel(blk_ref, eid_ref, first_ref, slot_ref, nxt_ref, more_ref, nvb_ref,
                   x_ref, wg_hbm, wu_hbm, wd_hbm, o_ref,
                   wg_f, wu_f, wd_f, wg_s, wu_s, wd_s, sems):
    i = pl.program_id(0)

    def weight_copies(e, slot):
        return (pltpu.make_async_copy(wg_hbm.at[e], wg_f.at[slot], sems.at[slot]),
                pltpu.make_async_copy(wu_hbm.at[e], wu_f.at[slot], sems.at[slot]),
                pltpu.make_async_copy(wd_hbm.at[e], wd_f.at[slot], sems.at[slot]))

    @pl.when(i == 0)
    def _():
        for cp in weight_copies(eid_ref[0], 0):
            cp.start(priority=1)

    @pl.when(first_ref[i] == 1)
    def _():
        slot = slot_ref[i]
        for cp in weight_copies(eid_ref[i], slot):
            cp.wait()

        @pl.when(more_ref[i] == 1)
        def _():
            for cp in weight_copies(nxt_ref[i], 1 - slot):
                cp.start(priority=1)

    @pl.when(first_ref[i] == 1)
    def _():
        slot = slot_ref[i]
        wg_s[...] = wg_f[slot].astype(BF16)
        wu_s[...] = wu_f[slot].astype(BF16)
        wd_s[...] = wd_f[slot].astype(BF16)
        o_ref[...] = _pack_halves(_ffn(x_ref[...], wg_s, wu_s, wd_s))

    @pl.when((first_ref[i] == 0) & (i < nvb_ref[0]))
    def _():
        o_ref[...] = _pack_halves(_ffn(x_ref[...], wg_s, wu_s, wd_s))

    @pl.when(i >= nvb_ref[0])
    def _():
        o_ref[...] = jnp.zeros(o_ref.shape, o_ref.dtype)


def _experts(blk, eid, first, slot, nxt, more, nvb, xs, wg, wu, wd, bm):
    n_rows, w = xs.shape
    ne, d, f = wg.shape
    return pl.pallas_call(
        _expert_kernel,
        grid_spec=pltpu.PrefetchScalarGridSpec(
            num_scalar_prefetch=7,
            grid=(n_rows // bm,),
            in_specs=[pl.BlockSpec((bm, w), lambda i, blk, *_: (blk[i], 0)),
                      pl.BlockSpec(memory_space=pl.ANY),
                      pl.BlockSpec(memory_space=pl.ANY),
                      pl.BlockSpec(memory_space=pl.ANY)],
            out_specs=pl.BlockSpec((bm, w), lambda i, *_: (i, 0)),
            scratch_shapes=[pltpu.VMEM((2, d, f), F32), pltpu.VMEM((2, d, f), F32), pltpu.VMEM((2, f, d), F32),
                            pltpu.VMEM((d, f), BF16), pltpu.VMEM((d, f), BF16), pltpu.VMEM((f, d), BF16),
                            pltpu.SemaphoreType.DMA((2,))]),
        out_shape=jax.ShapeDtypeStruct((n_rows, w), U32),
        compiler_params=_cparams(("arbitrary",)),
    )(blk, eid, first, slot, nxt, more, nvb, xs, wg, wu, wd)


def _combine_kernel(dest_ref, dnext_ref, y_ref, h_ref, x1_ref, ew_ref, wg_ref, wu_ref, wd_ref, mod_ref, g_ref,
                    o_ref, gbuf_a, gbuf_b, sems, *, tc):
    i = pl.program_id(0)
    last = pl.num_programs(0) - 1

    def issue(idx_ref, gbuf, sem):
        for j in range(tc):
            for k in range(TOP_K):
                src = y_ref.at[idx_ref[j * TOP_K + k]]
                pltpu.make_async_copy(src, gbuf.at[k, j], sem).start(priority=k % 2)

    def wait_all(gbuf, sem):
        for k in range(TOP_K):
            pltpu.make_async_copy(y_ref.at[pl.ds(0, tc)], gbuf.at[k], sem).wait()

    def compute(gbuf):
        shared = _ffn(h_ref[...], wg_ref, wu_ref, wd_ref)
        half = h_ref.shape[1]
        ew = ew_ref[...]
        left = shared[:, :half]
        right = shared[:, half:]
        for k in range(TOP_K):
            yl, yr = _unpack_halves(gbuf[k])
            wk = ew[:, k:k + 1]
            left = left + wk * yl
            right = right + wk * yr
        x2 = x1_ref[...] + mod_ref[0, 5:6, :] * jnp.concatenate([left, right], axis=1)
        o_ref[...] = x2 * lax.rsqrt(jnp.mean(x2 * x2, axis=-1, keepdims=True) + EPS) * g_ref[...]

    @pl.when(i == 0)
    def _():
        def group_body(g, c):
            base = pl.multiple_of(g * SUBLANES, SUBLANES)
            for jj in range(SUBLANES):
                for k in range(TOP_K):
                    src = y_ref.at[dest_ref[(base + jj) * TOP_K + k]]
                    pltpu.make_async_copy(src, gbuf_a.at[k, base + jj], sems.at[0]).start(priority=k % 2)
            return c

        lax.fori_loop(0, tc // SUBLANES, group_body, 0)

    @pl.when(i % 2 == 0)
    def _():
        wait_all(gbuf_a, sems.at[0])
        issue(dnext_ref, gbuf_b, sems.at[1])
        compute(gbuf_a)

    @pl.when(i % 2 == 1)
    def _():
        wait_all(gbuf_b, sems.at[1])
        issue(dnext_ref, gbuf_a, sems.at[0])
        compute(gbuf_b)

    @pl.when((i == last) & (i % 2 == 0))
    def _():
        wait_all(gbuf_b, sems.at[1])

    @pl.when((i == last) & (i % 2 == 1))
    def _():
        wait_all(gbuf_a, sems.at[0])


def _combine(dest_flat, y, h2p, x1, ew_t, wsg, wsu, wsd, mod3, g, L, tc=TILES.move_tokens):
    t, d = x1.shape
    w = h2p.shape[1]
    nt = t // tc
    kern = functools.partial(_combine_kernel, tc=tc)
    return pl.pallas_call(
        kern,
        grid=(nt,),
        in_specs=[pl.BlockSpec((tc * TOP_K,), lambda i: (i,), memory_space=pltpu.SMEM),
                  pl.BlockSpec((tc * TOP_K,), lambda i: (jnp.minimum(i + 1, nt - 1),), memory_space=pltpu.SMEM),
                  pl.BlockSpec(memory_space=pl.ANY),
                  pl.BlockSpec((tc, w), lambda i: (i, 0)),
                  pl.BlockSpec((tc, d), lambda i: (i, 0)),
                  pl.BlockSpec((tc, TOP_K), lambda i: (i, 0)),
                  pl.BlockSpec(wsg.shape, lambda i: (0, 0)),
                  pl.BlockSpec(wsu.shape, lambda i: (0, 0)),
                  pl.BlockSpec(wsd.shape, lambda i: (0, 0)),
                  pl.BlockSpec((1, 6, d), lambda i: (i * tc // L, 0, 0)),
                  pl.BlockSpec((1, d), lambda i: (0, 0))],
        out_specs=pl.BlockSpec((tc, d), lambda i: (i, 0)),
        out_shape=jax.ShapeDtypeStruct((t, d), F32),
        scratch_shapes=[pltpu.VMEM((TOP_K, tc, w), U32),
                        pltpu.VMEM((TOP_K, tc, w), U32),
                        pltpu.SemaphoreType.DMA((2,))],
        compiler_params=_cparams(("arbitrary",)),
    )(dest_flat, dest_flat, y, h2p, x1, ew_t, wsg, wsu, wsd, mod3, g.reshape(1, d))


def _split_cols(w, sizes):
    out, off = [], 0
    for s in sizes:
        out.append(w[:, off:off + s])
        off += s
    return out


def kernel(x, c, w_ada, b_ada, norm1_g, w_in, ckv_norm_g, idx_k_norm_g, w_uk, w_uv, rel_bias, lb_logits,
           attn_out_norm_g, hgrn_out_norm_g, w_out, norm2_g, w_router, router_bias, w_e_gate, w_e_up,
           w_e_down, w_s_gate, w_s_up, w_s_down, final_norm_g):
    bsz, L, d = x.shape
    t = bsz * L
    assert w_ada.shape[0] == 1, "single-layer block"
    a_width = A_HEADS * A_HEAD_DIM
    b_width = B_HEADS * B_DIM
    sizes = (a_width, A_KV_RANK, IDX_HEADS * IDX_DIM, IDX_DIM, IDX_HEADS, b_width, b_width, b_width, b_width)
    assert w_in.shape[2] == sum(sizes)

    wq_a, wckv, wiq, wik, wiw, wq_b, wf_b, wi_b, wg_b = _split_cols(w_in[0], sizes)
    w_main = jnp.concatenate([wq_a, wiq, wq_b, wi_b, wg_b], axis=1).astype(BF16)
    w_f = wf_b.astype(BF16)
    aux_pad = LANES - IDX_DIM - IDX_HEADS
    w_aux = jnp.concatenate([wckv, wik, wiw, jnp.zeros((d, aux_pad), F32)], axis=1).astype(BF16)

    mod3 = _ada(c, w_ada[0], b_ada[0]).reshape(bsz, 6, d)
    h1 = _norm1(x, mod3, norm1_g[0]).reshape(t, d)
    g1 = _matmul(h1, w_main, BF16, tm=TILES.proj_rows, tn=TILES.proj_cols)
    fb = _matmul(h1, w_f, F32, tm=TILES.proj_rows, tn=TILES.proj_cols)
    aux = _matmul(h1, w_aux, F32, tm=TILES.proj_rows, tn=w_aux.shape[1])
    ckv_n, ik_lo, ik_hi = _kvnorm(aux, ckv_norm_g[0], idx_k_norm_g[0])

    tq = min(TILES.attn, L)
    o_a = _dsa(g1, aux, ik_lo, ik_hi, ckv_n, w_uk[0].astype(BF16), w_uv[0].astype(BF16),
               _bias_tables(rel_bias, tq), attn_out_norm_g[0], bsz, L, tq)
    o_b = _hgrn(g1, fb, lb_logits, hgrn_out_norm_g[0], bsz, L)

    x1, h2p, logits_t = _out(o_a, o_b, x.reshape(t, d), w_out[0].astype(BF16), mod3, norm2_g[0],
                             w_router[0].T.astype(BF16), L)

    eidx, ew, rank, cnt = _route(logits_t, router_bias[0])

    bm = TILES.moe_rows
    counts = cnt[:, 0].astype(I32)
    padded = (counts + bm - 1) // bm * bm
    pad_end = jnp.cumsum(padded)
    pad_start = pad_end - padded
    n_rows = (t * TOP_K + N_EXPERTS * (bm - 1) + bm - 1) // bm * bm
    nb = n_rows // bm
    nvb = pad_end[-1] // bm
    blk = jnp.minimum(jnp.arange(nb, dtype=I32), nvb - 1)
    eid = jnp.minimum(jnp.sum((pad_end[None, :] <= (blk * bm)[:, None]).astype(I32), axis=1), N_EXPERTS - 1)
    ar = jnp.arange(nb, dtype=I32)
    first = ((ar < nvb) & ((ar == 0) | (eid != jnp.roll(eid, 1)))).astype(I32)
    slot = (jnp.cumsum(first) - 1) % 2
    nxt_blk = pad_end[eid] // bm
    more = (nxt_blk < nvb).astype(I32)
    nxt = eid[jnp.minimum(nxt_blk, nb - 1)]

    dest = _dest(eidx, rank, pad_start)
    dest_flat = dest.T.reshape(t * TOP_K)
    xs = _dispatch(pad_end.astype(I32), padded.astype(I32), dest_flat, h2p, n_rows, bm)

    y = _experts(blk, eid, first, slot.astype(I32), nxt.astype(I32), more, nvb.reshape(1).astype(I32), xs,
                 w_e_gate[0], w_e_up[0], w_e_down[0], bm)

    out = _combine(dest_flat, y, h2p, x1, ew.T, w_s_gate[0].astype(BF16), w_s_up[0].astype(BF16),
                   w_s_down[0].astype(BF16), mod3, final_norm_g, L)
    return out.reshape(bsz, L, d)
```

```python
import functools
import math
from typing import NamedTuple

import jax
import jax.numpy as jnp
from jax import lax
from jax.experimental import pallas as pl
from jax.experimental.pallas import tpu as pltpu

F32 = jnp.float32
BF16 = jnp.bfloat16
I32 = jnp.int32
U32 = jnp.uint32

EPS = 1e-6
A_HEADS = 8
A_HEAD_DIM = 128
A_KV_RANK = 256
IDX_HEADS = 16
IDX_DIM = 64
IDX_TOPK_MAX = 256
B_HEADS = 8
B_DIM = 128
REL_BUCKETS = 32
REL_MAX_DIST = 128
N_EXPERTS = 64
TOP_K = 8
N_GROUPS = 8
TOPK_GROUPS = 4
ROUTED_SCALE = 2.5

VMEM_LIMIT_BYTES = 56 * 1024 * 1024
LANES = 128
SUBLANES = 8


class _Tiles(NamedTuple):
    ada_cols: int = 1024
    norm_rows: int = 1024
    proj_rows: int = 1024
    proj_cols: int = 1024
    attn: int = 256
    hgrn_chunk: int = 128
    hgrn_rows: int = 32
    out_rows: int = 512
    route_tokens: int = 512
    dest_tokens: int = 2048
    moe_rows: int = 512
    move_tokens: int = 256


TILES = _Tiles()

NT_DIMS = (((1,), (1,)), ((), ()))
TN_DIMS = (((0,), (0,)), ((), ()))

HGRN_MAX_BLOCK_DECAY = 80.0
LOG2E = math.log2(math.e)
INT_MIN = -2 ** 31
KEY_NEG_INF = -2139095041


def _cparams(sem):
    return pltpu.CompilerParams(dimension_semantics=sem, vmem_limit_bytes=VMEM_LIMIT_BYTES)


def _silu(v):
    return v * jax.nn.sigmoid(v)


def _pack_halves(v):
    n = v.shape[1] // 2
    lo = lax.bitcast_convert_type(v[:, :n].astype(BF16).astype(F32), U32)
    hi = lax.bitcast_convert_type(v[:, n:].astype(BF16).astype(F32), U32)
    return lax.shift_right_logical(lo, jnp.uint32(16)) | (hi & jnp.uint32(0xFFFF0000))


def _unpack_halves(w):
    left = lax.bitcast_convert_type(lax.shift_left(w, jnp.uint32(16)), F32)
    right = lax.bitcast_convert_type(w & jnp.uint32(0xFFFF0000), F32)
    return left, right


def _ada_kernel(c_ref, w_ref, b_ref, o_ref):
    a = _silu(c_ref[...]).astype(BF16)
    o_ref[...] = jnp.dot(a, w_ref[...].astype(BF16), preferred_element_type=F32) + b_ref[...]


def _ada(c, w, b, tn=TILES.ada_cols):
    bsz, d = c.shape
    n = w.shape[1]
    return pl.pallas_call(
        _ada_kernel,
        grid=(n // tn,),
        in_specs=[pl.BlockSpec((bsz, d), lambda j: (0, 0)),
                  pl.BlockSpec((d, tn), lambda j: (0, j)),
                  pl.BlockSpec((1, tn), lambda j: (0, j))],
        out_specs=pl.BlockSpec((bsz, tn), lambda j: (0, j)),
        out_shape=jax.ShapeDtypeStruct((bsz, n), F32),
        compiler_params=_cparams(("arbitrary",)),
    )(c, w, b.reshape(1, n))


def _norm1_kernel(x_ref, mod_ref, g_ref, o_ref):
    x = x_ref[0]
    y = x * lax.rsqrt(jnp.mean(x * x, axis=-1, keepdims=True) + EPS) * g_ref[...]
    sh = mod_ref[0, 0:1, :]
    sc = mod_ref[0, 1:2, :]
    o_ref[0] = (y * (1.0 + sc) + sh).astype(o_ref.dtype)


def _norm1(x, mod3, g, tm=TILES.norm_rows):
    bsz, L, d = x.shape
    tm = min(tm, L)
    return pl.pallas_call(
        _norm1_kernel,
        grid=(bsz, L // tm),
        in_specs=[pl.BlockSpec((1, tm, d), lambda b, i: (b, i, 0)),
                  pl.BlockSpec((1, 6, d), lambda b, i: (b, 0, 0)),
                  pl.BlockSpec((1, d), lambda b, i: (0, 0))],
        out_specs=pl.BlockSpec((1, tm, d), lambda b, i: (b, i, 0)),
        out_shape=jax.ShapeDtypeStruct((bsz, L, d), BF16),
        compiler_params=_cparams(("parallel", "parallel")),
    )(x, mod3, g.reshape(1, d))


def _mm_kernel(a_ref, w_ref, o_ref):
    o_ref[...] = jnp.dot(a_ref[...], w_ref[...], preferred_element_type=F32).astype(o_ref.dtype)


def _matmul(a, w, out_dtype, tm, tn):
    m, k = a.shape
    n = w.shape[1]
    return pl.pallas_call(
        _mm_kernel,
        grid=(m // tm, n // tn),
        in_specs=[pl.BlockSpec((tm, k), lambda i, j: (i, 0)),
                  pl.BlockSpec((k, tn), lambda i, j: (0, j))],
        out_specs=pl.BlockSpec((tm, tn), lambda i, j: (i, j)),
        out_shape=jax.ShapeDtypeStruct((m, n), out_dtype),
        compiler_params=_cparams(("parallel", "arbitrary")),
    )(a, w)


def _kvnorm_kernel(aux_ref, gc_ref, gk_ref, ckv_ref, iklo_ref, ikhi_ref):
    ckv = aux_ref[:, :A_KV_RANK]
    ckv_ref[...] = (ckv * lax.rsqrt(jnp.mean(ckv * ckv, axis=-1, keepdims=True) + EPS)
                    * gc_ref[...]).astype(BF16)
    v = aux_ref[:, A_KV_RANK:A_KV_RANK + LANES]
    lane = lax.broadcasted_iota(I32, v.shape, 1)
    ik = jnp.where(lane < IDX_DIM, v, 0.0)
    ms = jnp.sum(ik * ik, axis=-1, keepdims=True) * (1.0 / IDX_DIM)
    ikn = ik * lax.rsqrt(ms + EPS) * gk_ref[...]
    iklo_ref[...] = ikn.astype(BF16)
    ikhi_ref[...] = pltpu.roll(ikn, IDX_DIM, 1).astype(BF16)


def _kvnorm(aux, gc, gk, tm=TILES.proj_rows):
    t = aux.shape[0]
    gk_pad = jnp.concatenate([gk, jnp.zeros((LANES - IDX_DIM,), F32)]).reshape(1, LANES)
    return pl.pallas_call(
        _kvnorm_kernel,
        grid=(t // tm,),
        in_specs=[pl.BlockSpec((tm, aux.shape[1]), lambda i: (i, 0)),
                  pl.BlockSpec((1, A_KV_RANK), lambda i: (0, 0)),
                  pl.BlockSpec((1, LANES), lambda i: (0, 0))],
        out_specs=[pl.BlockSpec((tm, A_KV_RANK), lambda i: (i, 0)),
                   pl.BlockSpec((tm, LANES), lambda i: (i, 0)),
                   pl.BlockSpec((tm, LANES), lambda i: (i, 0))],
        out_shape=[jax.ShapeDtypeStruct((t, A_KV_RANK), BF16),
                   jax.ShapeDtypeStruct((t, LANES), BF16),
                   jax.ShapeDtypeStruct((t, LANES), BF16)],
        compiler_params=_cparams(("parallel",)),
    )(aux, gc.reshape(1, A_KV_RANK), gk_pad)


def _t5_bucket(rel):
    n = jnp.maximum(rel, 0)
    max_exact = REL_BUCKETS // 2
    n_large = jnp.maximum(n, max_exact).astype(F32)
    large = max_exact + (jnp.log(n_large / max_exact) / math.log(REL_MAX_DIST / max_exact)
                         * (REL_BUCKETS - max_exact)).astype(I32)
    large = jnp.minimum(large, REL_BUCKETS - 1)
    return jnp.where(n < max_exact, n, large)


def _bias_tables(rel_bias, tq):
    assert tq + 1 >= REL_MAX_DIST
    nh = rel_bias.shape[1]
    dist = jnp.maximum(jnp.arange(3 * tq + 1, dtype=I32) - tq, 0)
    v = rel_bias.astype(F32)[_t5_bucket(dist)].T * LOG2E
    n = v.shape[1]
    x = jnp.broadcast_to(v[:, None, :], (nh, tq, n)).reshape(nh, tq * n)[:, :tq * (n - 1)].reshape(nh, tq, n - 1)
    near = x[:, :, tq:2 * tq]
    prev = x[:, :, 2 * tq:3 * tq]
    far = jnp.broadcast_to(v[:, n - 1][:, None, None], near.shape)
    return jnp.stack([near, prev, far])


def _dsa_kernel(qa_ref, iq_ref, aux_ref, iklo_ref, ikhi_ref, ckv_ref, ckvt_ref, wuk_ref, wuvt_ref, bias_ref,
                g_ref, o_ref, iqt_ref, iwt_ref, key_ref, qlt_ref, m_ref, l_ref, acc_ref, tie_ref, madd_ref,
                *, tq, topk):
    i = pl.program_id(1)
    nh = A_HEADS
    npair = IDX_HEADS // 2

    r_i = lax.broadcasted_iota(I32, (LANES, LANES), 0)
    c_i = lax.broadcasted_iota(I32, (LANES, LANES), 1)
    eye = jnp.where(r_i == c_i, 1.0, 0.0).astype(BF16)
    for p in range(npair):
        iqt_ref[:, p * tq:(p + 1) * tq] = lax.dot_general(
            eye, iq_ref[:, p * LANES:(p + 1) * LANES], NT_DIMS, preferred_element_type=F32).astype(BF16)
    iwt_ref[...] = (jnp.transpose(aux_ref[...])[IDX_DIM:IDX_DIM + IDX_HEADS, :]
                    * (IDX_HEADS ** -0.5 * IDX_DIM ** -0.5))
    for h in range(nh):
        ql = lax.dot_general(wuk_ref[h], qa_ref[:, h * A_HEAD_DIM:(h + 1) * A_HEAD_DIM], NT_DIMS,
                             preferred_element_type=F32)
        qlt_ref[:, h * tq:(h + 1) * tq] = (ql * (A_HEAD_DIM ** -0.5 * LOG2E)).astype(BF16)

    kpos = lax.broadcasted_iota(I32, (tq, tq), 0)
    qpos = lax.broadcasted_iota(I32, (tq, tq), 1) + i * tq

    def score_body(kc, carry):
        off = pl.multiple_of(kc * tq, tq)
        klo = iklo_ref[0, pl.ds(off, tq), :]
        khi = ikhi_ref[0, pl.ds(off, tq), :]
        acc = jnp.zeros((tq, tq), F32)
        for p in range(npair):
            rhs = iqt_ref[:, p * tq:(p + 1) * tq]
            se = jnp.dot(klo, rhs, preferred_element_type=F32)
            so = jnp.dot(khi, rhs, preferred_element_type=F32)
            acc = acc + jnp.maximum(se, 0.0) * iwt_ref[2 * p:2 * p + 1, :]
            acc = acc + jnp.maximum(so, 0.0) * iwt_ref[2 * p + 1:2 * p + 2, :]
        bits = lax.bitcast_convert_type(acc, I32)
        key = jnp.where(bits >= 0, bits, bits ^ jnp.int32(0x7FFFFFFF))
        key_ref[kc] = jnp.where(kpos + off <= qpos, key, jnp.int32(KEY_NEG_INF))
        return carry

    lax.fori_loop(0, i + 1, score_body, 0)

    def count_ge(cand):
        def body(kc, c):
            hit = jnp.where(key_ref[kc] >= cand, 1.0, 0.0)
            return c + jnp.sum(hit.reshape(tq // 8, 8, tq), axis=0)
        c = lax.fori_loop(0, i + 1, body, jnp.zeros((8, tq), F32))
        return jnp.sum(c, axis=0, keepdims=True)

    kf = float(topk)
    thr = jnp.where(count_ge(jnp.zeros((1, tq), I32)) >= kf, jnp.int32(0), jnp.int32(INT_MIN))

    def bit_body(j, thr):
        cand = thr | lax.shift_left(jnp.int32(1), 30 - j)
        return jnp.where(count_ge(cand) >= kf, cand, thr)

    thr = lax.fori_loop(0, 31, bit_body, thr)

    def count_gt_eq():
        def body(kc, c):
            key = key_ref[kc]
            gt = jnp.where(key > thr, 1.0, 0.0)
            eq = jnp.where(key == thr, 1.0, 0.0)
            return (c[0] + jnp.sum(gt.reshape(tq // 8, 8, tq), axis=0),
                    c[1] + jnp.sum(eq.reshape(tq // 8, 8, tq), axis=0))
        z = jnp.zeros((8, tq), F32)
        c = lax.fori_loop(0, i + 1, body, (z, z))
        return jnp.sum(c[0], axis=0, keepdims=True), jnp.sum(c[1], axis=0, keepdims=True)

    n_gt, n_eq = count_gt_eq()
    need = kf - n_gt
    tied = (n_gt + n_eq > kf) & (thr > jnp.int32(KEY_NEG_INF))
    has_tie = jnp.max(jnp.where(tied, 1.0, 0.0)) > 0.0
    tie_ref[...] = jnp.zeros(tie_ref.shape, F32)

    m_ref[...] = jnp.full(m_ref.shape, -jnp.inf, F32)
    l_ref[...] = jnp.zeros(l_ref.shape, F32)
    acc_ref[...] = jnp.zeros(acc_ref.shape, F32)

    def att_body(kc, carry):
        off = pl.multiple_of(kc * tq, tq)
        ckv = ckv_ref[0, pl.ds(off, tq), :]
        ckvt = ckvt_ref[0, kc]
        key = key_ref[kc]
        causal = key > jnp.int32(KEY_NEG_INF)

        @pl.when(jnp.logical_not(has_tie))
        def _():
            madd_ref[...] = jnp.where((key >= thr) & causal, 0.0, -jnp.inf)

        @pl.when(has_tie)
        def _():
            eq = key == thr
            eqf = jnp.where(eq, 1.0, 0.0)
            before = (lax.broadcasted_iota(I32, (tq, tq), 1) < lax.broadcasted_iota(I32, (tq, tq), 0))
            rank = jnp.dot(jnp.where(before, 1.0, 0.0).astype(BF16), eqf.astype(BF16),
                           preferred_element_type=F32) + tie_ref[...]
            keep = (key > thr) | (eq & (rank < need))
            madd_ref[...] = jnp.where(keep & causal, 0.0, -jnp.inf)
            tie_ref[...] = tie_ref[...] + jnp.sum(eqf, axis=0, keepdims=True)

        madd = madd_ref[...]
        d = jnp.minimum(i - kc, 2)
        for h in range(nh):
            s = jnp.dot(ckv, qlt_ref[:, h * tq:(h + 1) * tq], preferred_element_type=F32)
            s = s + (bias_ref[d, h] + madd)
            m_old = m_ref[h:h + 1, :]
            m_new = jnp.maximum(m_old, jnp.max(s, axis=0, keepdims=True))
            m_safe = jnp.where(m_new == -jnp.inf, 0.0, m_new)
            alpha = jnp.exp2(m_old - m_safe)
            p = jnp.exp2(s - m_safe)
            l_ref[h:h + 1, :] = alpha * l_ref[h:h + 1, :] + jnp.sum(p, axis=0, keepdims=True)
            acc_ref[h] = alpha * acc_ref[h] + jnp.dot(ckvt, p.astype(BF16), preferred_element_type=F32)
            m_ref[h:h + 1, :] = m_new
        return carry

    lax.fori_loop(0, i + 1, att_body, 0)

    outs = []
    for h in range(nh):
        o_lat = (acc_ref[h] / l_ref[h:h + 1, :]).astype(BF16)
        outs.append(jnp.transpose(jnp.dot(wuvt_ref[h], o_lat, preferred_element_type=F32)))
    o = jnp.concatenate(outs, axis=1)
    o = o * lax.rsqrt(jnp.mean(o * o, axis=-1, keepdims=True) + EPS) * g_ref[...]
    o_ref[...] = o.astype(o_ref.dtype)


def _dsa(g1, aux, ik_lo, ik_hi, ckv_n, w_uk, w_uv, bias_tab, g, bsz, L, tq):
    t = bsz * L
    nq = L // tq
    topk = min(IDX_TOPK_MAX, L // 4)
    aux_blk = A_KV_RANK // LANES
    kern = functools.partial(_dsa_kernel, tq=tq, topk=topk)
    width = A_HEADS * A_HEAD_DIM
    ckv3 = ckv_n.reshape(bsz, L, A_KV_RANK)
    ckvt = ckv_n.reshape(bsz, nq, tq, A_KV_RANK).transpose(0, 1, 3, 2)
    return pl.pallas_call(
        kern,
        grid=(bsz, nq),
        in_specs=[pl.BlockSpec((tq, width), lambda b, i: (b * nq + i, 0)),
                  pl.BlockSpec((tq, IDX_HEADS * IDX_DIM), lambda b, i: (b * nq + i, 1)),
                  pl.BlockSpec((tq, LANES), lambda b, i: (b * nq + i, aux_blk)),
                  pl.BlockSpec((1, L, LANES), lambda b, i: (b, 0, 0)),
                  pl.BlockSpec((1, L, LANES), lambda b, i: (b, 0, 0)),
                  pl.BlockSpec((1, L, A_KV_RANK), lambda b, i: (b, 0, 0)),
                  pl.BlockSpec((1, nq, A_KV_RANK, tq), lambda b, i: (b, 0, 0, 0)),
                  pl.BlockSpec((A_HEADS, A_KV_RANK, A_HEAD_DIM), lambda b, i: (0, 0, 0)),
                  pl.BlockSpec((A_HEADS, A_HEAD_DIM, A_KV_RANK), lambda b, i: (0, 0, 0)),
                  pl.BlockSpec((3, A_HEADS, tq, tq), lambda b, i: (0, 0, 0, 0)),
                  pl.BlockSpec((1, width), lambda b, i: (0, 0))],
        out_specs=pl.BlockSpec((tq, width), lambda b, i: (b * nq + i, 0)),
        out_shape=jax.ShapeDtypeStruct((t, width), BF16),
        scratch_shapes=[pltpu.VMEM((LANES, IDX_HEADS // 2 * tq), BF16),
                        pltpu.VMEM((IDX_HEADS, tq), F32),
                        pltpu.VMEM((nq, tq, tq), I32),
                        pltpu.VMEM((A_KV_RANK, A_HEADS * tq), BF16),
                        pltpu.VMEM((A_HEADS, tq), F32),
                        pltpu.VMEM((A_HEADS, tq), F32),
                        pltpu.VMEM((A_HEADS, A_KV_RANK, tq), F32),
                        pltpu.VMEM((1, tq), F32),
                        pltpu.VMEM((tq, tq), F32)],
        compiler_params=_cparams(("parallel", "arbitrary")),
    )(g1, g1, aux, ik_lo.reshape(bsz, L, LANES), ik_hi.reshape(bsz, L, LANES),
      ckv3, ckvt, w_uk, jnp.transpose(w_uv, (0, 2, 1)), bias_tab, g.reshape(1, width))


def _hgrn_kernel(q_ref, i_ref, gate_ref, f_ref, lbl_ref, ng_ref, o_ref, st_ref, bc_ref, kk_ref, sc_ref,
                 *, chunk, rblk):
    @pl.when(pl.program_id(1) == 0)
    def _():
        st_ref[...] = jnp.zeros(st_ref.shape, F32)

    ll = lbl_ref[...]
    ex = jnp.exp(ll - jnp.max(ll, axis=0, keepdims=True))
    lb_all = ex[0:1] / jnp.sum(ex, axis=0, keepdims=True)

    r_i = lax.broadcasted_iota(I32, (chunk, chunk), 0)
    c_i = lax.broadcasted_iota(I32, (chunk, chunk), 1)
    tri = jnp.where(r_i >= c_i, 1.0, 0.0).astype(BF16)
    row_k = lax.broadcasted_iota(I32, (chunk, B_DIM), 0)
    nblk = chunk // rblk
    zero_row = jnp.zeros((1, B_DIM), F32)

    growth = zero_row
    for h in range(B_HEADS):
        sl = slice(h * B_DIM, (h + 1) * B_DIM)
        lb = lb_all[:, sl]
        f = lb + (1.0 - lb) * jax.nn.sigmoid(f_ref[:, sl])
        lf = jnp.log(f)
        kk_ref[h] = 1.0 - f
        l1 = lf.astype(BF16)
        r1 = lf - l1.astype(F32)
        l2 = r1.astype(BF16)
        l3 = (r1 - l2.astype(F32)).astype(BF16)
        cs = jnp.dot(tri, jnp.concatenate([l1, l2, l3], axis=1), preferred_element_type=F32)
        bc = cs[:, :B_DIM] + cs[:, B_DIM:2 * B_DIM] + cs[:, 2 * B_DIM:]
        bc_ref[h] = bc
        for r in range(nblk):
            top = bc[r * rblk - 1:r * rblk] if r > 0 else zero_row
            growth = jnp.maximum(growth, top - bc[(r + 1) * rblk - 1:(r + 1) * rblk])
    overflow_risk = jnp.max(growth) > HGRN_MAX_BLOCK_DECAY

    def block_scores(h, before_only):
        sl = slice(h * B_DIM, (h + 1) * B_DIM)
        bc = bc_ref[h]
        kk = kk_ref[h]
        q = q_ref[:, sl].astype(F32)
        parts = []
        for r in range(nblk):
            lo, hi = r * rblk, (r + 1) * rblk
            base = bc[lo - 1:lo] if r > 0 else zero_row
            qt = (q[lo:hi] * jnp.exp(bc[lo:hi] - base)).astype(BF16)
            if before_only:
                kt = jnp.where(row_k < lo, kk * jnp.exp(jnp.where(row_k < lo, base - bc, 0.0)), 0.0)
            else:
                kt = kk * jnp.exp(jnp.where(row_k < hi, base - bc, 0.0))
            parts.append(lax.dot_general(qt, kt.astype(BF16), NT_DIMS, preferred_element_type=F32))
        return jnp.concatenate(parts, axis=0)

    @pl.when(jnp.logical_not(overflow_risk))
    def _():
        for h in range(B_HEADS):
            sc_ref[h] = jnp.where(c_i <= r_i, block_scores(h, False), 0.0)

    @pl.when(overflow_risk)
    def _():
        for h in range(B_HEADS):
            sl = slice(h * B_DIM, (h + 1) * B_DIM)
            bc = bc_ref[h]
            kk = kk_ref[h]
            q = q_ref[:, sl].astype(F32)
            sc = block_scores(h, True)
            for dlt in range(rblk):
                bc_s = pltpu.roll(bc, dlt, 0) if dlt else bc
                kk_s = pltpu.roll(kk, dlt, 0) if dlt else kk
                ok = (row_k & (rblk - 1)) >= dlt
                band = q * kk_s * jnp.exp(jnp.where(ok, bc - bc_s, -jnp.inf))
                sc = sc + jnp.where(c_i == r_i - dlt, jnp.sum(band, axis=1, keepdims=True), 0.0)
            sc_ref[h] = sc

    for h in range(B_HEADS):
        sl = slice(h * B_DIM, (h + 1) * B_DIM)
        bc = bc_ref[h]
        kk = kk_ref[h]
        q = q_ref[:, sl].astype(F32)
        v = i_ref[:, sl]
        st = st_ref[h]
        o = lax.dot_general((q * jnp.exp(bc)).astype(BF16), st.astype(BF16), NT_DIMS,
                            preferred_element_type=F32)
        o = o + jnp.dot(sc_ref[h].astype(BF16), v, preferred_element_type=F32)

        last = bc[chunk - 1:chunk]
        kd = (kk * jnp.exp(last - bc)).astype(BF16)
        st_ref[h] = st * jnp.exp(last) + lax.dot_general(v, kd, TN_DIMS, preferred_element_type=F32)

        y = o * lax.rsqrt(jnp.mean(o * o, axis=-1, keepdims=True) + EPS) * ng_ref[:, sl]
        o_ref[:, sl] = (y * _silu(gate_ref[:, sl].astype(F32))).astype(o_ref.dtype)


def _hgrn(g1, fb, lb_logits, ng, bsz, L, chunk=TILES.hgrn_chunk, rblk=TILES.hgrn_rows):
    assert rblk & (rblk - 1) == 0 and chunk % rblk == 0
    t = bsz * L
    nc = L // chunk
    width = B_HEADS * B_DIM
    kern = functools.partial(_hgrn_kernel, chunk=chunk, rblk=rblk)
    return pl.pallas_call(
        kern,
        grid=(bsz, nc),
        in_specs=[pl.BlockSpec((chunk, width), lambda b, c: (b * nc + c, 2)),
                  pl.BlockSpec((chunk, width), lambda b, c: (b * nc + c, 3)),
                  pl.BlockSpec((chunk, width), lambda b, c: (b * nc + c, 4)),
                  pl.BlockSpec((chunk, width), lambda b, c: (b * nc + c, 0)),
                  pl.BlockSpec(lb_logits.shape, lambda b, c: (0, 0)),
                  pl.BlockSpec((1, width), lambda b, c: (0, 0))],
        out_specs=pl.BlockSpec((chunk, width), lambda b, c: (b * nc + c, 0)),
        out_shape=jax.ShapeDtypeStruct((t, width), BF16),
        scratch_shapes=[pltpu.VMEM((B_HEADS, B_DIM, B_DIM), F32),
                        pltpu.VMEM((B_HEADS, chunk, B_DIM), F32),
                        pltpu.VMEM((B_HEADS, chunk, B_DIM), F32),
                        pltpu.VMEM((B_HEADS, chunk, chunk), F32)],
        compiler_params=_cparams(("parallel", "arbitrary")),
    )(g1, g1, g1, fb, lb_logits, ng.reshape(1, width))


def _out_kernel(oa_ref, ob_ref, x_ref, wa_ref, wb_ref, mod_ref, g_ref, wr_ref, x1_ref, h2_ref, lg_ref):
    mix = jnp.dot(oa_ref[...], wa_ref[...], preferred_element_type=F32)
    mix = mix + jnp.dot(ob_ref[...], wb_ref[...], preferred_element_type=F32)
    x1 = x_ref[...] + mod_ref[0, 2:3, :] * mix
    x1_ref[...] = x1
    y = x1 * lax.rsqrt(jnp.mean(x1 * x1, axis=-1, keepdims=True) + EPS) * g_ref[...]
    h2 = y * (1.0 + mod_ref[0, 4:5, :]) + mod_ref[0, 3:4, :]
    h2_ref[...] = _pack_halves(h2)
    lg_ref[...] = lax.dot_general(wr_ref[...], h2.astype(BF16), NT_DIMS, preferred_element_type=F32)


def _out(oa, ob, x2d, w_out, mod3, g, w_router_t, L, tm=TILES.out_rows):
    t, d = x2d.shape
    half = oa.shape[1]
    ne = w_router_t.shape[0]
    return pl.pallas_call(
        _out_kernel,
        grid=(t // tm,),
        in_specs=[pl.BlockSpec((tm, half), lambda i: (i, 0)),
                  pl.BlockSpec((tm, half), lambda i: (i, 0)),
                  pl.BlockSpec((tm, d), lambda i: (i, 0)),
                  pl.BlockSpec((half, d), lambda i: (0, 0)),
                  pl.BlockSpec((half, d), lambda i: (1, 0)),
                  pl.BlockSpec((1, 6, d), lambda i: (i * tm // L, 0, 0)),
                  pl.BlockSpec((1, d), lambda i: (0, 0)),
                  pl.BlockSpec((ne, d), lambda i: (0, 0))],
        out_specs=[pl.BlockSpec((tm, d), lambda i: (i, 0)),
                   pl.BlockSpec((tm, d // 2), lambda i: (i, 0)),
                   pl.BlockSpec((ne, tm), lambda i: (0, i))],
        out_shape=[jax.ShapeDtypeStruct((t, d), F32),
                   jax.ShapeDtypeStruct((t, d // 2), U32),
                   jax.ShapeDtypeStruct((ne, t), F32)],
        compiler_params=_cparams(("parallel",)),
    )(oa, ob, x2d, w_out, w_out, mod3, g.reshape(1, d), w_router_t)


def _rows_to_tile(rows, nrow):
    n = rows[0].shape[1]
    ridx = lax.broadcasted_iota(I32, (nrow, n), 0)
    out = jnp.zeros((nrow, n), rows[0].dtype)
    for r, v in enumerate(rows):
        out = jnp.where(ridx == r, jnp.broadcast_to(v, (nrow, n)), out)
    return out


def _route_kernel(lg_ref, rb_ref, eidx_ref, ew_ref, rank_ref, cnt_ref, run_ref):
    @pl.when(pl.program_id(0) == 0)
    def _():
        run_ref[...] = jnp.zeros(run_ref.shape, F32)

    ne, tt = lg_ref.shape
    per = ne // N_GROUPS
    sc = jax.nn.sigmoid(lg_ref[...])
    ch = sc + rb_ref[...]
    neg = -jnp.inf

    sub = lax.broadcasted_iota(I32, (per, tt), 0).astype(F32)
    gsc = []
    for g in range(N_GROUPS):
        cg = ch[g * per:(g + 1) * per]
        m1 = jnp.max(cg, axis=0, keepdims=True)
        first = jnp.min(jnp.where(cg == m1, sub, float(per)), axis=0, keepdims=True)
        m2 = jnp.max(jnp.where(sub == first, neg, cg), axis=0, keepdims=True)
        gsc.append(m1 + m2)
    grp = _rows_to_tile(gsc, N_GROUPS)

    gid = lax.broadcasted_iota(I32, (N_GROUPS, tt), 0).astype(F32)
    gsel = jnp.zeros((N_GROUPS, tt), F32)
    for _ in range(TOPK_GROUPS):
        mx = jnp.max(grp, axis=0, keepdims=True)
        gi = jnp.min(jnp.where(grp == mx, gid, float(N_GROUPS)), axis=0, keepdims=True)
        pick = gid == gi
        gsel = jnp.where(pick, 1.0, gsel)
        grp = jnp.where(pick, neg, grp)

    eid = lax.broadcasted_iota(I32, (ne, tt), 0).astype(F32)
    cm = jnp.full((ne, tt), neg, F32)
    for g in range(N_GROUPS):
        in_g = (eid >= float(g * per)) & (eid < float((g + 1) * per))
        cm = jnp.where(in_g & (jnp.broadcast_to(gsel[g:g + 1], (ne, tt)) > 0.5), ch, cm)

    idx_rows, w_rows = [], []
    onehot = jnp.zeros((ne, tt), F32)
    for _ in range(TOP_K):
        mx = jnp.max(cm, axis=0, keepdims=True)
        ei = jnp.min(jnp.where(cm == mx, eid, float(ne)), axis=0, keepdims=True)
        pick = eid == ei
        idx_rows.append(ei)
        w_rows.append(jnp.sum(jnp.where(pick, sc, 0.0), axis=0, keepdims=True))
        onehot = jnp.where(pick, 1.0, onehot)
        cm = jnp.where(pick, neg, cm)
    wsum = w_rows[0]
    for w in w_rows[1:]:
        wsum = wsum + w
    w_rows = [w / wsum * ROUTED_SCALE for w in w_rows]

    a_i = lax.broadcasted_iota(I32, (tt, tt), 0)
    b_i = lax.broadcasted_iota(I32, (tt, tt), 1)
    upper = jnp.where(a_i < b_i, 1.0, 0.0).astype(BF16)
    rank_full = jnp.dot(onehot.astype(BF16), upper, preferred_element_type=F32) + run_ref[...]
    r_rows = [jnp.sum(jnp.where(eid == ei, rank_full, 0.0), axis=0, keepdims=True) for ei in idx_rows]
    run = run_ref[...] + jnp.sum(onehot, axis=1, keepdims=True)
    run_ref[...] = run

    eidx_ref[...] = _rows_to_tile(idx_rows, TOP_K).astype(I32)
    ew_ref[...] = _rows_to_tile(w_rows, TOP_K)
    rank_ref[...] = _rows_to_tile(r_rows, TOP_K).astype(I32)
    cnt_ref[...] = jnp.broadcast_to(run, cnt_ref.shape)


def _route(logits_t, router_bias, tt=TILES.route_tokens):
    ne, t = logits_t.shape
    return pl.pallas_call(
        _route_kernel,
        grid=(t // tt,),
        in_specs=[pl.BlockSpec((ne, tt), lambda i: (0, i)),
                  pl.BlockSpec((ne, 1), lambda i: (0, 0))],
        out_specs=[pl.BlockSpec((TOP_K, tt), lambda i: (0, i)),
                   pl.BlockSpec((TOP_K, tt), lambda i: (0, i)),
                   pl.BlockSpec((TOP_K, tt), lambda i: (0, i)),
                   pl.BlockSpec((ne, LANES), lambda i: (0, 0))],
        out_shape=[jax.ShapeDtypeStruct((TOP_K, t), I32),
                   jax.ShapeDtypeStruct((TOP_K, t), F32),
                   jax.ShapeDtypeStruct((TOP_K, t), I32),
                   jax.ShapeDtypeStruct((ne, LANES), F32)],
        scratch_shapes=[pltpu.VMEM((ne, 1), F32)],
        compiler_params=_cparams(("arbitrary",)),
    )(logits_t, router_bias.reshape(ne, 1))


def _dest_kernel(eidx_ref, rank_ref, ps_ref, o_ref):
    ne = ps_ref.shape[0]
    tt = eidx_ref.shape[1]
    eid = lax.broadcasted_iota(I32, (ne, tt), 0)
    ps = jnp.broadcast_to(ps_ref[...], (ne, tt))
    rows = []
    for k in range(TOP_K):
        start = jnp.sum(jnp.where(eid == eidx_ref[k:k + 1, :], ps, 0.0), axis=0, keepdims=True)
        rows.append(start + rank_ref[k:k + 1, :].astype(F32))
    o_ref[...] = _rows_to_tile(rows, TOP_K).astype(I32)


def _dest(eidx, rank, pad_start, tt=TILES.dest_tokens):
    t = eidx.shape[1]
    tt = min(tt, t)
    ne = pad_start.shape[0]
    return pl.pallas_call(
        _dest_kernel,
        grid=(t // tt,),
        in_specs=[pl.BlockSpec((TOP_K, tt), lambda i: (0, i)),
                  pl.BlockSpec((TOP_K, tt), lambda i: (0, i)),
                  pl.BlockSpec((ne, 1), lambda i: (0, 0))],
        out_specs=pl.BlockSpec((TOP_K, tt), lambda i: (0, i)),
        out_shape=jax.ShapeDtypeStruct((TOP_K, t), I32),
        compiler_params=_cparams(("parallel",)),
    )(eidx, rank, pad_start.astype(F32).reshape(ne, 1))


def _dispatch_kernel(pend_ref, padded_ref, dest_ref, h_ref, xs_ref, zbuf_ref, zsem, sem, *, td, bm):
    i = pl.program_id(0)

    def tail_copy(e):
        start = pl.multiple_of(pend_ref[e] - bm, bm)
        return pltpu.make_async_copy(zbuf_ref, xs_ref.at[pl.ds(start, bm)], zsem)

    @pl.when(i == 0)
    def _():
        zbuf_ref[...] = jnp.zeros(zbuf_ref.shape, zbuf_ref.dtype)

        def start_body(e, c):
            @pl.when(padded_ref[e] > 0)
            def _():
                tail_copy(e).start()
            return c

        def wait_body(e, c):
            @pl.when(padded_ref[e] > 0)
            def _():
                tail_copy(e).wait()
            return c

        lax.fori_loop(0, N_EXPERTS, start_body, 0)
        lax.fori_loop(0, N_EXPERTS, wait_body, 0)

        def unused_copy(b):
            return pltpu.make_async_copy(zbuf_ref, xs_ref.at[pl.ds(pl.multiple_of(b * bm, bm), bm)], zsem)

        def ustart_body(b, c):
            unused_copy(b).start()
            return c

        def uwait_body(b, c):
            unused_copy(b).wait()
            return c

        first_unused = pend_ref[N_EXPERTS - 1] // bm
        lax.fori_loop(first_unused, xs_ref.shape[0] // bm, ustart_body, 0)
        lax.fori_loop(first_unused, xs_ref.shape[0] // bm, uwait_body, 0)

    for j in range(td):
        for k in range(TOP_K):
            dst = xs_ref.at[dest_ref[j * TOP_K + k]]
            pltpu.make_async_copy(h_ref.at[j], dst, sem).start(priority=k % 2)
    for _ in range(TOP_K):
        pltpu.make_async_copy(h_ref, xs_ref.at[pl.ds(0, td)], sem).wait()


def _dispatch(pad_end, padded, dest_flat, h2p, n_rows, bm, td=TILES.move_tokens):
    t, w = h2p.shape
    kern = functools.partial(_dispatch_kernel, td=td, bm=bm)
    return pl.pallas_call(
        kern,
        grid_spec=pltpu.PrefetchScalarGridSpec(
            num_scalar_prefetch=2,
            grid=(t // td,),
            in_specs=[pl.BlockSpec((td * TOP_K,), lambda i, *_: (i,), memory_space=pltpu.SMEM),
                      pl.BlockSpec((td, w), lambda i, *_: (i, 0))],
            out_specs=pl.BlockSpec(memory_space=pl.ANY),
            scratch_shapes=[pltpu.VMEM((bm, w), U32),
                            pltpu.SemaphoreType.DMA(()),
                            pltpu.SemaphoreType.DMA(())]),
        out_shape=jax.ShapeDtypeStruct((n_rows, w), U32),
        compiler_params=_cparams(("arbitrary",)),
    )(pad_end, padded, dest_flat, h2p)


def _ffn(xw, wg_ref, wu_ref, wd_ref):
    half = xw.shape[1]
    left, right = _unpack_halves(xw)
    left = left.astype(BF16)
    right = right.astype(BF16)

    def proj(w_ref):
        return (jnp.dot(left, w_ref[:half, :], preferred_element_type=F32)
                + jnp.dot(right, w_ref[half:, :], preferred_element_type=F32))

    act = (_silu(proj(wg_ref)) * proj(wu_ref)).astype(BF16)
    return jnp.dot(act, wd_ref[...], preferred_element_type=F32)


def _expert_kernel(blk_ref, eid_ref, first_ref, slot_ref, nxt_ref, more_ref, nvb_ref,
                   x_ref, wg_hbm, wu_hbm, wd_hbm, o_ref,
                   wg_f, wu_f, wd_f, wg_s, wu_s, wd_s, sems):
    i = pl.program_id(0)

    def weight_copies(e, slot):
        return (pltpu.make_async_copy(wg_hbm.at[e], wg_f.at[slot], sems.at[slot]),
                pltpu.make_async_copy(wu_hbm.at[e], wu_f.at[slot], sems.at[slot]),
                pltpu.make_async_copy(wd_hbm.at[e], wd_f.at[slot], sems.at[slot]))

    @pl.when(i == 0)
    def _():
        for cp in weight_copies(eid_ref[0], 0):
            cp.start(priority=1)

    @pl.when(first_ref[i] == 1)
    def _():
        slot = slot_ref[i]
        for cp in weight_copies(eid_ref[i], slot):
            cp.wait()

        @pl.when(more_ref[i] == 1)
        def _():
            for cp in weight_copies(nxt_ref[i], 1 - slot):
                cp.start(priority=1)

    @pl.when(first_ref[i] == 1)
    def _():
        slot = slot_ref[i]
        wg_s[...] = wg_f[slot].astype(BF16)
        wu_s[...] = wu_f[slot].astype(BF16)
        wd_s[...] = wd_f[slot].astype(BF16)
        o_ref[...] = _pack_halves(_ffn(x_ref[...], wg_s, wu_s, wd_s))

    @pl.when((first_ref[i] == 0) & (i < nvb_ref[0]))
    def _():
        o_ref[...] = _pack_halves(_ffn(x_ref[...], wg_s, wu_s, wd_s))

    @pl.when(i >= nvb_ref[0])
    def _():
        o_ref[...] = jnp.zeros(o_ref.shape, o_ref.dtype)


def _experts(blk, eid, first, slot, nxt, more, nvb, xs, wg, wu, wd, bm):
    n_rows, w = xs.shape
    ne, d, f = wg.shape
    return pl.pallas_call(
        _expert_kernel,
        grid_spec=pltpu.PrefetchScalarGridSpec(
            num_scalar_prefetch=7,
            grid=(n_rows // bm,),
            in_specs=[pl.BlockSpec((bm, w), lambda i, blk, *_: (blk[i], 0)),
                      pl.BlockSpec(memory_space=pl.ANY),
                      pl.BlockSpec(memory_space=pl.ANY),
                      pl.BlockSpec(memory_space=pl.ANY)],
            out_specs=pl.BlockSpec((bm, w), lambda i, *_: (i, 0)),
            scratch_shapes=[pltpu.VMEM((2, d, f), F32), pltpu.VMEM((2, d, f), F32), pltpu.VMEM((2, f, d), F32),
                            pltpu.VMEM((d, f), BF16), pltpu.VMEM((d, f), BF16), pltpu.VMEM((f, d), BF16),
                            pltpu.SemaphoreType.DMA((2,))]),
        out_shape=jax.ShapeDtypeStruct((n_rows, w), U32),
        compiler_params=_cparams(("arbitrary",)),
    )(blk, eid, first, slot, nxt, more, nvb, xs, wg, wu, wd)


def _combine_kernel(dest_ref, dnext_ref, y_ref, h_ref, x1_ref, ew_ref, wg_ref, wu_ref, wd_ref, mod_ref, g_ref,
                    o_ref, gbuf_a, gbuf_b, sems, *, tc):
    i = pl.program_id(0)
    last = pl.num_programs(0) - 1

    def issue(idx_ref, gbuf, sem):
        for j in range(tc):
            for k in range(TOP_K):
                src = y_ref.at[idx_ref[j * TOP_K + k]]
                pltpu.make_async_copy(src, gbuf.at[k, j], sem).start(priority=k % 2)

    def wait_all(gbuf, sem):
        for k in range(TOP_K):
            pltpu.make_async_copy(y_ref.at[pl.ds(0, tc)], gbuf.at[k], sem).wait()

    def compute(gbuf):
        shared = _ffn(h_ref[...], wg_ref, wu_ref, wd_ref)
        half = h_ref.shape[1]
        ew = ew_ref[...]
        left = shared[:, :half]
        right = shared[:, half:]
        for k in range(TOP_K):
            yl, yr = _unpack_halves(gbuf[k])
            wk = ew[:, k:k + 1]
            left = left + wk * yl
            right = right + wk * yr
        x2 = x1_ref[...] + mod_ref[0, 5:6, :] * jnp.concatenate([left, right], axis=1)
        o_ref[...] = x2 * lax.rsqrt(jnp.mean(x2 * x2, axis=-1, keepdims=True) + EPS) * g_ref[...]

    @pl.when(i == 0)
    def _():
        def group_body(g, c):
            base = pl.multiple_of(g * SUBLANES, SUBLANES)
            for jj in range(SUBLANES):
                for k in range(TOP_K):
                    src = y_ref.at[dest_ref[(base + jj) * TOP_K + k]]
                    pltpu.make_async_copy(src, gbuf_a.at[k, base + jj], sems.at[0]).start(priority=k % 2)
            return c

        lax.fori_loop(0, tc // SUBLANES, group_body, 0)

    @pl.when(i % 2 == 0)
    def _():
        wait_all(gbuf_a, sems.at[0])
        issue(dnext_ref, gbuf_b, sems.at[1])
        compute(gbuf_a)

    @pl.when(i % 2 == 1)
    def _():
        wait_all(gbuf_b, sems.at[1])
        issue(dnext_ref, gbuf_a, sems.at[0])
        compute(gbuf_b)

    @pl.when((i == last) & (i % 2 == 0))
    def _():
        wait_all(gbuf_b, sems.at[1])

    @pl.when((i == last) & (i % 2 == 1))
    def _():
        wait_all(gbuf_a, sems.at[0])


def _combine(dest_flat, y, h2p, x1, ew_t, wsg, wsu, wsd, mod3, g, L, tc=TILES.move_tokens):
    t, d = x1.shape
    w = h2p.shape[1]
    nt = t // tc
    kern = functools.partial(_combine_kernel, tc=tc)
    return pl.pallas_call(
        kern,
        grid=(nt,),
        in_specs=[pl.BlockSpec((tc * TOP_K,), lambda i: (i,), memory_space=pltpu.SMEM),
                  pl.BlockSpec((tc * TOP_K,), lambda i: (jnp.minimum(i + 1, nt - 1),), memory_space=pltpu.SMEM),
                  pl.BlockSpec(memory_space=pl.ANY),
                  pl.BlockSpec((tc, w), lambda i: (i, 0)),
                  pl.BlockSpec((tc, d), lambda i: (i, 0)),
                  pl.BlockSpec((tc, TOP_K), lambda i: (i, 0)),
                  pl.BlockSpec(wsg.shape, lambda i: (0, 0)),
                  pl.BlockSpec(wsu.shape, lambda i: (0, 0)),
                  pl.BlockSpec(wsd.shape, lambda i: (0, 0)),
                  pl.BlockSpec((1, 6, d), lambda i: (i * tc // L, 0, 0)),
                  pl.BlockSpec((1, d), lambda i: (0, 0))],
        out_specs=pl.BlockSpec((tc, d), lambda i: (i, 0)),
        out_shape=jax.ShapeDtypeStruct((t, d), F32),
        scratch_shapes=[pltpu.VMEM((TOP_K, tc, w), U32),
                        pltpu.VMEM((TOP_K, tc, w), U32),
                        pltpu.SemaphoreType.DMA((2,))],
        compiler_params=_cparams(("arbitrary",)),
    )(dest_flat, dest_flat, y, h2p, x1, ew_t, wsg, wsu, wsd, mod3, g.reshape(1, d))


def _split_cols(w, sizes):
    out, off = [], 0
    for s in sizes:
        out.append(w[:, off:off + s])
        off += s
    return out


def kernel(x, c, w_ada, b_ada, norm1_g, w_in, ckv_norm_g, idx_k_norm_g, w_uk, w_uv, rel_bias, lb_logits,
           attn_out_norm_g, hgrn_out_norm_g, w_out, norm2_g, w_router, router_bias, w_e_gate, w_e_up,
           w_e_down, w_s_gate, w_s_up, w_s_down, final_norm_g):
    bsz, L, d = x.shape
    t = bsz * L
    assert w_ada.shape[0] == 1, "single-layer block"
    a_width = A_HEADS * A_HEAD_DIM
    b_width = B_HEADS * B_DIM
    sizes = (a_width, A_KV_RANK, IDX_HEADS * IDX_DIM, IDX_DIM, IDX_HEADS, b_width, b_width, b_width, b_width)
    assert w_in.shape[2] == sum(sizes)

    wq_a, wckv, wiq, wik, wiw, wq_b, wf_b, wi_b, wg_b = _split_cols(w_in[0], sizes)
    w_main = jnp.concatenate([wq_a, wiq, wq_b, wi_b, wg_b], axis=1).astype(BF16)
    w_f = wf_b.astype(BF16)
    aux_pad = LANES - IDX_DIM - IDX_HEADS
    w_aux = jnp.concatenate([wckv, wik, wiw, jnp.zeros((d, aux_pad), F32)], axis=1).astype(BF16)

    mod3 = _ada(c, w_ada[0], b_ada[0]).reshape(bsz, 6, d)
    h1 = _norm1(x, mod3, norm1_g[0]).reshape(t, d)
    g1 = _matmul(h1, w_main, BF16, tm=TILES.proj_rows, tn=TILES.proj_cols)
    fb = _matmul(h1, w_f, F32, tm=TILES.proj_rows, tn=TILES.proj_cols)
    aux = _matmul(h1, w_aux, F32, tm=TILES.proj_rows, tn=w_aux.shape[1])
    ckv_n, ik_lo, ik_hi = _kvnorm(aux, ckv_norm_g[0], idx_k_norm_g[0])

    tq = min(TILES.attn, L)
    o_a = _dsa(g1, aux, ik_lo, ik_hi, ckv_n, w_uk[0].astype(BF16), w_uv[0].astype(BF16),
               _bias_tables(rel_bias, tq), attn_out_norm_g[0], bsz, L, tq)
    o_b = _hgrn(g1, fb, lb_logits, hgrn_out_norm_g[0], bsz, L)

    x1, h2p, logits_t = _out(o_a, o_b, x.reshape(t, d), w_out[0].astype(BF16), mod3, norm2_g[0],
                             w_router[0].T.astype(BF16), L)

    eidx, ew, rank, cnt = _route(logits_t, router_bias[0])

    bm = TILES.moe_rows
    counts = cnt[:, 0].astype(I32)
    padded = (counts + bm - 1) // bm * bm
    pad_end = jnp.cumsum(padded)
    pad_start = pad_end - padded
    n_rows = (t * TOP_K + N_EXPERTS * (bm - 1) + bm - 1) // bm * bm
    nb = n_rows // bm
    nvb = pad_end[-1] // bm
    blk = jnp.minimum(jnp.arange(nb, dtype=I32), nvb - 1)
    eid = jnp.minimum(jnp.sum((pad_end[None, :] <= (blk * bm)[:, None]).astype(I32), axis=1), N_EXPERTS - 1)
    ar = jnp.arange(nb, dtype=I32)
    first = ((ar < nvb) & ((ar == 0) | (eid != jnp.roll(eid, 1)))).astype(I32)
    slot = (jnp.cumsum(first) - 1) % 2
    nxt_blk = pad_end[eid] // bm
    more = (nxt_blk < nvb).astype(I32)
    nxt = eid[jnp.minimum(nxt_blk, nb - 1)]

    dest = _dest(eidx, rank, pad_start)
    dest_flat = dest.T.reshape(t * TOP_K)
    xs = _dispatch(pad_end.astype(I32), padded.astype(I32), dest_flat, h2p, n_rows, bm)

    y = _experts(blk, eid, first, slot.astype(I32), nxt.astype(I32), more, nvb.reshape(1).astype(I32), xs,
                 w_e_gate[0], w_e_up[0], w_e_down[0], bm)

    out = _combine(dest_flat, y, h2p, x1, ew.T, w_s_gate[0].astype(BF16), w_s_up[0].astype(BF16),
                   w_s_down[0].astype(BF16), mod3, final_norm_g, L)
    return out.reshape(bsz, L, d)
```

```python
import functools
import math
from typing import NamedTuple

import jax
import jax.numpy as jnp
from jax import lax
from jax.experimental import pallas as pl
from jax.experimental.pallas import tpu as pltpu

F32 = jnp.float32
BF16 = jnp.bfloat16
I32 = jnp.int32
U32 = jnp.uint32

EPS = 1e-6
A_HEADS = 8
A_HEAD_DIM = 128
A_KV_RANK = 256
IDX_HEADS = 16
IDX_DIM = 64
IDX_TOPK_MAX = 256
B_HEADS = 8
B_DIM = 128
REL_BUCKETS = 32
REL_MAX_DIST = 128
N_EXPERTS = 64
TOP_K = 8
N_GROUPS = 8
TOPK_GROUPS = 4
ROUTED_SCALE = 2.5

VMEM_LIMIT_BYTES = 56 * 1024 * 1024
LANES = 128
SUBLANES = 8


class _Tiles(NamedTuple):
    ada_cols: int = 1024
    norm_rows: int = 1024
    proj_rows: int = 1024
    proj_cols: int = 1024
    attn: int = 256
    hgrn_chunk: int = 128
    hgrn_rows: int = 32
    out_rows: int = 512
    route_tokens: int = 512
    dest_tokens: int = 2048
    moe_rows: int = 512
    move_tokens: int = 256


TILES = _Tiles()

NT_DIMS = (((1,), (1,)), ((), ()))
TN_DIMS = (((0,), (0,)), ((), ()))

HGRN_MAX_BLOCK_DECAY = 80.0
LOG2E = math.log2(math.e)
INT_MIN = -2 ** 31
KEY_NEG_INF = -2139095041


def _cparams(sem):
    return pltpu.CompilerParams(dimension_semantics=sem, vmem_limit_bytes=VMEM_LIMIT_BYTES)


def _silu(v):
    return v * jax.nn.sigmoid(v)


def _pack_halves(v):
    n = v.shape[1] // 2
    lo = lax.bitcast_convert_type(v[:, :n].astype(BF16).astype(F32), U32)
    hi = lax.bitcast_convert_type(v[:, n:].astype(BF16).astype(F32), U32)
    return lax.shift_right_logical(lo, jnp.uint32(16)) | (hi & jnp.uint32(0xFFFF0000))


def _unpack_halves(w):
    left = lax.bitcast_convert_type(lax.shift_left(w, jnp.uint32(16)), F32)
    right = lax.bitcast_convert_type(w & jnp.uint32(0xFFFF0000), F32)
    return left, right


def _ada_kernel(c_ref, w_ref, b_ref, o_ref):
    a = _silu(c_ref[...]).astype(BF16)
    o_ref[...] = jnp.dot(a, w_ref[...].astype(BF16), preferred_element_type=F32) + b_ref[...]


def _ada(c, w, b, tn=TILES.ada_cols):
    bsz, d = c.shape
    n = w.shape[1]
    return pl.pallas_call(
        _ada_kernel,
        grid=(n // tn,),
        in_specs=[pl.BlockSpec((bsz, d), lambda j: (0, 0)),
                  pl.BlockSpec((d, tn), lambda j: (0, j)),
                  pl.BlockSpec((1, tn), lambda j: (0, j))],
        out_specs=pl.BlockSpec((bsz, tn), lambda j: (0, j)),
        out_shape=jax.ShapeDtypeStruct((bsz, n), F32),
        compiler_params=_cparams(("arbitrary",)),
    )(c, w, b.reshape(1, n))


def _norm1_kernel(x_ref, mod_ref, g_ref, o_ref):
    x = x_ref[0]
    y = x * lax.rsqrt(jnp.mean(x * x, axis=-1, keepdims=True) + EPS) * g_ref[...]
    sh = mod_ref[0, 0:1, :]
    sc = mod_ref[0, 1:2, :]
    o_ref[0] = (y * (1.0 + sc) + sh).astype(o_ref.dtype)


def _norm1(x, mod3, g, tm=TILES.norm_rows):
    bsz, L, d = x.shape
    tm = min(tm, L)
    return pl.pallas_call(
        _norm1_kernel,
        grid=(bsz, L // tm),
        in_specs=[pl.BlockSpec((1, tm, d), lambda b, i: (b, i, 0)),
                  pl.BlockSpec((1, 6, d), lambda b, i: (b, 0, 0)),
                  pl.BlockSpec((1, d), lambda b, i: (0, 0))],
        out_specs=pl.BlockSpec((1, tm, d), lambda b, i: (b, i, 0)),
        out_shape=jax.ShapeDtypeStruct((bsz, L, d), BF16),
        compiler_params=_cparams(("parallel", "parallel")),
    )(x, mod3, g.reshape(1, d))


def _mm_kernel(a_ref, w_ref, o_ref):
    o_ref[...] = jnp.dot(a_ref[...], w_ref[...], preferred_element_type=F32).astype(o_ref.dtype)


def _matmul(a, w, out_dtype, tm, tn):
    m, k = a.shape
    n = w.shape[1]
    return pl.pallas_call(
        _mm_kernel,
        grid=(m // tm, n // tn),
        in_specs=[pl.BlockSpec((tm, k), lambda i, j: (i, 0)),
                  pl.BlockSpec((k, tn), lambda i, j: (0, j))],
        out_specs=pl.BlockSpec((tm, tn), lambda i, j: (i, j)),
        out_shape=jax.ShapeDtypeStruct((m, n), out_dtype),
        compiler_params=_cparams(("parallel", "arbitrary")),
    )(a, w)


def _kvnorm_kernel(aux_ref, gc_ref, gk_ref, ckv_ref, iklo_ref, ikhi_ref):
    ckv = aux_ref[:, :A_KV_RANK]
    ckv_ref[...] = (ckv * lax.rsqrt(jnp.mean(ckv * ckv, axis=-1, keepdims=True) + EPS)
                    * gc_ref[...]).astype(BF16)
    v = aux_ref[:, A_KV_RANK:A_KV_RANK + LANES]
    lane = lax.broadcasted_iota(I32, v.shape, 1)
    ik = jnp.where(lane < IDX_DIM, v, 0.0)
    ms = jnp.sum(ik * ik, axis=-1, keepdims=True) * (1.0 / IDX_DIM)
    ikn = ik * lax.rsqrt(ms + EPS) * gk_ref[...]
    iklo_ref[...] = ikn.astype(BF16)
    ikhi_ref[...] = pltpu.roll(ikn, IDX_DIM, 1).astype(BF16)


def _kvnorm(aux, gc, gk, tm=TILES.proj_rows):
    t = aux.shape[0]
    gk_pad = jnp.concatenate([gk, jnp.zeros((LANES - IDX_DIM,), F32)]).reshape(1, LANES)
    return pl.pallas_call(
        _kvnorm_kernel,
        grid=(t // tm,),
        in_specs=[pl.BlockSpec((tm, aux.shape[1]), lambda i: (i, 0)),
                  pl.BlockSpec((1, A_KV_RANK), lambda i: (0, 0)),
                  pl.BlockSpec((1, LANES), lambda i: (0, 0))],
        out_specs=[pl.BlockSpec((tm, A_KV_RANK), lambda i: (i, 0)),
                   pl.BlockSpec((tm, LANES), lambda i: (i, 0)),
                   pl.BlockSpec((tm, LANES), lambda i: (i, 0))],
        out_shape=[jax.ShapeDtypeStruct((t, A_KV_RANK), BF16),
                   jax.ShapeDtypeStruct((t, LANES), BF16),
                   jax.ShapeDtypeStruct((t, LANES), BF16)],
        compiler_params=_cparams(("parallel",)),
    )(aux, gc.reshape(1, A_KV_RANK), gk_pad)


def _t5_bucket(rel):
    n = jnp.maximum(rel, 0)
    max_exact = REL_BUCKETS // 2
    n_large = jnp.maximum(n, max_exact).astype(F32)
    large = max_exact + (jnp.log(n_large / max_exact) / math.log(REL_MAX_DIST / max_exact)
                         * (REL_BUCKETS - max_exact)).astype(I32)
    large = jnp.minimum(large, REL_BUCKETS - 1)
    return jnp.where(n < max_exact, n, large)


def _bias_tables(rel_bias, tq):
    assert tq + 1 >= REL_MAX_DIST
    nh = rel_bias.shape[1]
    dist = jnp.maximum(jnp.arange(3 * tq + 1, dtype=I32) - tq, 0)
    v = rel_bias.astype(F32)[_t5_bucket(dist)].T * LOG2E
    n = v.shape[1]
    x = jnp.broadcast_to(v[:, None, :], (nh, tq, n)).reshape(nh, tq * n)[:, :tq * (n - 1)].reshape(nh, tq, n - 1)
    near = x[:, :, tq:2 * tq]
    prev = x[:, :, 2 * tq:3 * tq]
    far = jnp.broadcast_to(v[:, n - 1][:, None, None], near.shape)
    return jnp.stack([near, prev, far])


def _dsa_kernel(qa_ref, iq_ref, aux_ref, iklo_ref, ikhi_ref, ckv_ref, ckvt_ref, wuk_ref, wuvt_ref, bias_ref,
                g_ref, o_ref, iqt_ref, iwt_ref, key_ref, qlt_ref, m_ref, l_ref, acc_ref, tie_ref, madd_ref,
                *, tq, topk):
    i = pl.program_id(1)
    nh = A_HEADS
    npair = IDX_HEADS // 2

    r_i = lax.broadcasted_iota(I32, (LANES, LANES), 0)
    c_i = lax.broadcasted_iota(I32, (LANES, LANES), 1)
    eye = jnp.where(r_i == c_i, 1.0, 0.0).astype(BF16)
    for p in range(npair):
        iqt_ref[:, p * tq:(p + 1) * tq] = lax.dot_general(
            eye, iq_ref[:, p * LANES:(p + 1) * LANES], NT_DIMS, preferred_element_type=F32).astype(BF16)
    iwt_ref[...] = (jnp.transpose(aux_ref[...])[IDX_DIM:IDX_DIM + IDX_HEADS, :]
                    * (IDX_HEADS ** -0.5 * IDX_DIM ** -0.5))
    for h in range(nh):
        ql = lax.dot_general(wuk_ref[h], qa_ref[:, h * A_HEAD_DIM:(h + 1) * A_HEAD_DIM], NT_DIMS,
                             preferred_element_type=F32)
        qlt_ref[:, h * tq:(h + 1) * tq] = (ql * (A_HEAD_DIM ** -0.5 * LOG2E)).astype(BF16)

    kpos = lax.broadcasted_iota(I32, (tq, tq), 0)
    qpos = lax.broadcasted_iota(I32, (tq, tq), 1) + i * tq

    def score_body(kc, carry):
        off = pl.multiple_of(kc * tq, tq)
        klo = iklo_ref[0, pl.ds(off, tq), :]
        khi = ikhi_ref[0, pl.ds(off, tq), :]
        acc = jnp.zeros((tq, tq), F32)
        for p in range(npair):
            rhs = iqt_ref[:, p * tq:(p + 1) * tq]
            se = jnp.dot(klo, rhs, preferred_element_type=F32)
            so = jnp.dot(khi, rhs, preferred_element_type=F32)
            acc = acc + jnp.maximum(se, 0.0) * iwt_ref[2 * p:2 * p + 1, :]
            acc = acc + jnp.maximum(so, 0.0) * iwt_ref[2 * p + 1:2 * p + 2, :]
        bits = lax.bitcast_convert_type(acc, I32)
        key = jnp.where(bits >= 0, bits, bits ^ jnp.int32(0x7FFFFFFF))
        key_ref[kc] = jnp.where(kpos + off <= qpos, key, jnp.int32(KEY_NEG_INF))
        return carry

    lax.fori_loop(0, i + 1, score_body, 0)

    def count_ge(cand):
        def body(kc, c):
            hit = jnp.where(key_ref[kc] >= cand, 1.0, 0.0)
            return c + jnp.sum(hit.reshape(tq // 8, 8, tq), axis=0)
        c = lax.fori_loop(0, i + 1, body, jnp.zeros((8, tq), F32))
        return jnp.sum(c, axis=0, keepdims=True)

    kf = float(topk)
    thr = jnp.where(count_ge(jnp.zeros((1, tq), I32)) >= kf, jnp.int32(0), jnp.int32(INT_MIN))

    def bit_body(j, thr):
        cand = thr | lax.shift_left(jnp.int32(1), 30 - j)
        return jnp.where(count_ge(cand) >= kf, cand, thr)

    thr = lax.fori_loop(0, 31, bit_body, thr)

    def count_gt_eq():
        def body(kc, c):
            key = key_ref[kc]
            gt = jnp.where(key > thr, 1.0, 0.0)
            eq = jnp.where(key == thr, 1.0, 0.0)
            return (c[0] + jnp.sum(gt.reshape(tq // 8, 8, tq), axis=0),
                    c[1] + jnp.sum(eq.reshape(tq // 8, 8, tq), axis=0))
        z = jnp.zeros((8, tq), F32)
        c = lax.fori_loop(0, i + 1, body, (z, z))
        return jnp.sum(c[0], axis=0, keepdims=True), jnp.sum(c[1], axis=0, keepdims=True)

    n_gt, n_eq = count_gt_eq()
    need = kf - n_gt
    tied = (n_gt + n_eq > kf) & (thr > jnp.int32(KEY_NEG_INF))
    has_tie = jnp.max(jnp.where(tied, 1.0, 0.0)) > 0.0
    tie_ref[...] = jnp.zeros(tie_ref.shape, F32)

    m_ref[...] = jnp.full(m_ref.shape, -jnp.inf, F32)
    l_ref[...] = jnp.zeros(l_ref.shape, F32)
    acc_ref[...] = jnp.zeros(acc_ref.shape, F32)

    def att_body(kc, carry):
        off = pl.multiple_of(kc * tq, tq)
        key = key_ref[kc]
        causal = key > jnp.int32(KEY_NEG_INF)

        @pl.when(jnp.logical_not(has_tie))
        def _():
            madd_ref[...] = jnp.where((key >= thr) & causal, 0.0, -jnp.inf)

        @pl.when(has_tie)
        def _():
            eq = key == thr
            eqf = jnp.where(eq, 1.0, 0.0)
            before = (lax.broadcasted_iota(I32, (tq, tq), 1) < lax.broadcasted_iota(I32, (tq, tq), 0))
            rank = jnp.dot(jnp.where(before, 1.0, 0.0).astype(BF16), eqf.astype(BF16),
                           preferred_element_type=F32) + tie_ref[...]
            keep = (key > thr) | (eq & (rank < need))
            madd_ref[...] = jnp.where(keep & causal, 0.0, -jnp.inf)
            tie_ref[...] = tie_ref[...] + jnp.sum(eqf, axis=0, keepdims=True)

        d = jnp.minimum(i - kc, 2)
        for h in range(nh):
            s = jnp.dot(ckv_ref[0, pl.ds(off, tq), :], qlt_ref[:, h * tq:(h + 1) * tq],
                        preferred_element_type=F32)
            s = s + bias_ref[d, h] + madd_ref[...]
            m_old = m_ref[h:h + 1, :]
            m_new = jnp.maximum(m_old, jnp.max(s, axis=0, keepdims=True))
            m_safe = jnp.where(m_new == -jnp.inf, 0.0, m_new)
            alpha = jnp.exp2(m_old - m_safe)
            p = jnp.exp2(s - m_safe)
            l_ref[h:h + 1, :] = alpha * l_ref[h:h + 1, :] + jnp.sum(p, axis=0, keepdims=True)
            acc_ref[h] = alpha * acc_ref[h] + jnp.dot(ckvt_ref[0, kc], p.astype(BF16),
                                                      preferred_element_type=F32)
            m_ref[h:h + 1, :] = m_new
        return carry

    lax.fori_loop(0, i + 1, att_body, 0)

    outs = []
    for h in range(nh):
        o_lat = (acc_ref[h] / l_ref[h:h + 1, :]).astype(BF16)
        outs.append(jnp.transpose(jnp.dot(wuvt_ref[h], o_lat, preferred_element_type=F32)))
    o = jnp.concatenate(outs, axis=1)
    o = o * lax.rsqrt(jnp.mean(o * o, axis=-1, keepdims=True) + EPS) * g_ref[...]
    o_ref[...] = o.astype(o_ref.dtype)


def _dsa(g1, aux, ik_lo, ik_hi, ckv_n, w_uk, w_uv, bias_tab, g, bsz, L, tq):
    t = bsz * L
    nq = L // tq
    topk = min(IDX_TOPK_MAX, L // 4)
    aux_blk = A_KV_RANK // LANES
    kern = functools.partial(_dsa_kernel, tq=tq, topk=topk)
    width = A_HEADS * A_HEAD_DIM
    ckv3 = ckv_n.reshape(bsz, L, A_KV_RANK)
    ckvt = ckv_n.reshape(bsz, nq, tq, A_KV_RANK).transpose(0, 1, 3, 2)
    return pl.pallas_call(
        kern,
        grid=(bsz, nq),
        in_specs=[pl.BlockSpec((tq, width), lambda b, i: (b * nq + i, 0)),
                  pl.BlockSpec((tq, IDX_HEADS * IDX_DIM), lambda b, i: (b * nq + i, 1)),
                  pl.BlockSpec((tq, LANES), lambda b, i: (b * nq + i, aux_blk)),
                  pl.BlockSpec((1, L, LANES), lambda b, i: (b, 0, 0)),
                  pl.BlockSpec((1, L, LANES), lambda b, i: (b, 0, 0)),
                  pl.BlockSpec((1, L, A_KV_RANK), lambda b, i: (b, 0, 0)),
                  pl.BlockSpec((1, nq, A_KV_RANK, tq), lambda b, i: (b, 0, 0, 0)),
                  pl.BlockSpec((A_HEADS, A_KV_RANK, A_HEAD_DIM), lambda b, i: (0, 0, 0)),
                  pl.BlockSpec((A_HEADS, A_HEAD_DIM, A_KV_RANK), lambda b, i: (0, 0, 0)),
                  pl.BlockSpec((3, A_HEADS, tq, tq), lambda b, i: (0, 0, 0, 0)),
                  pl.BlockSpec((1, width), lambda b, i: (0, 0))],
        out_specs=pl.BlockSpec((tq, width), lambda b, i: (b * nq + i, 0)),
        out_shape=jax.ShapeDtypeStruct((t, width), BF16),
        scratch_shapes=[pltpu.VMEM((LANES, IDX_HEADS // 2 * tq), BF16),
                        pltpu.VMEM((IDX_HEADS, tq), F32),
                        pltpu.VMEM((nq, tq, tq), I32),
                        pltpu.VMEM((A_KV_RANK, A_HEADS * tq), BF16),
                        pltpu.VMEM((A_HEADS, tq), F32),
                        pltpu.VMEM((A_HEADS, tq), F32),
                        pltpu.VMEM((A_HEADS, A_KV_RANK, tq), F32),
                        pltpu.VMEM((1, tq), F32),
                        pltpu.VMEM((tq, tq), F32)],
        compiler_params=_cparams(("parallel", "arbitrary")),
    )(g1, g1, aux, ik_lo.reshape(bsz, L, LANES), ik_hi.reshape(bsz, L, LANES),
      ckv3, ckvt, w_uk, jnp.transpose(w_uv, (0, 2, 1)), bias_tab, g.reshape(1, width))


def _hgrn_kernel(q_ref, i_ref, gate_ref, f_ref, lbl_ref, ng_ref, o_ref, st_ref, bc_ref, kk_ref, sc_ref,
                 *, chunk, rblk):
    @pl.when(pl.program_id(1) == 0)
    def _():
        st_ref[...] = jnp.zeros(st_ref.shape, F32)

    ll = lbl_ref[...]
    ex = jnp.exp(ll - jnp.max(ll, axis=0, keepdims=True))
    lb_all = ex[0:1] / jnp.sum(ex, axis=0, keepdims=True)

    r_i = lax.broadcasted_iota(I32, (chunk, chunk), 0)
    c_i = lax.broadcasted_iota(I32, (chunk, chunk), 1)
    tri = jnp.where(r_i >= c_i, 1.0, 0.0).astype(BF16)
    row_k = lax.broadcasted_iota(I32, (chunk, B_DIM), 0)
    nblk = chunk // rblk
    zero_row = jnp.zeros((1, B_DIM), F32)

    growth = zero_row
    for h in range(B_HEADS):
        sl = slice(h * B_DIM, (h + 1) * B_DIM)
        lb = lb_all[:, sl]
        f = lb + (1.0 - lb) * jax.nn.sigmoid(f_ref[:, sl])
        lf = jnp.log(f)
        kk_ref[h] = 1.0 - f
        l1 = lf.astype(BF16)
        r1 = lf - l1.astype(F32)
        l2 = r1.astype(BF16)
        l3 = (r1 - l2.astype(F32)).astype(BF16)
        cs = jnp.dot(tri, jnp.concatenate([l1, l2, l3], axis=1), preferred_element_type=F32)
        bc = cs[:, :B_DIM] + cs[:, B_DIM:2 * B_DIM] + cs[:, 2 * B_DIM:]
        bc_ref[h] = bc
        for r in range(nblk):
            top = bc[r * rblk - 1:r * rblk] if r > 0 else zero_row
            growth = jnp.maximum(growth, top - bc[(r + 1) * rblk - 1:(r + 1) * rblk])
    overflow_risk = jnp.max(growth) > HGRN_MAX_BLOCK_DECAY

    def block_scores(h, before_only):
        sl = slice(h * B_DIM, (h + 1) * B_DIM)
        bc = bc_ref[h]
        kk = kk_ref[h]
        q = q_ref[:, sl].astype(F32)
        parts = []
        for r in range(nblk):
            lo, hi = r * rblk, (r + 1) * rblk
            base = bc[lo - 1:lo] if r > 0 else zero_row
            qt = (q[lo:hi] * jnp.exp(bc[lo:hi] - base)).astype(BF16)
            if before_only:
                kt = jnp.where(row_k < lo, kk * jnp.exp(jnp.where(row_k < lo, base - bc, 0.0)), 0.0)
            else:
                kt = kk * jnp.exp(jnp.where(row_k < hi, base - bc, 0.0))
            parts.append(lax.dot_general(qt, kt.astype(BF16), NT_DIMS, preferred_element_type=F32))
        return jnp.concatenate(parts, axis=0)

    @pl.when(jnp.logical_not(overflow_risk))
    def _():
        for h in range(B_HEADS):
            sc_ref[h] = jnp.where(c_i <= r_i, block_scores(h, False), 0.0)

    @pl.when(overflow_risk)
    def _():
        for h in range(B_HEADS):
            sl = slice(h * B_DIM, (h + 1) * B_DIM)
            bc = bc_ref[h]
            kk = kk_ref[h]
            q = q_ref[:, sl].astype(F32)
            sc = block_scores(h, True)
            for dlt in range(rblk):
                bc_s = pltpu.roll(bc, dlt, 0) if dlt else bc
                kk_s = pltpu.roll(kk, dlt, 0) if dlt else kk
                ok = (row_k & (rblk - 1)) >= dlt
                band = q * kk_s * jnp.exp(jnp.where(ok, bc - bc_s, -jnp.inf))
                sc = sc + jnp.where(c_i == r_i - dlt, jnp.sum(band, axis=1, keepdims=True), 0.0)
            sc_ref[h] = sc

    for h in range(B_HEADS):
        sl = slice(h * B_DIM, (h + 1) * B_DIM)
        bc = bc_ref[h]
        kk = kk_ref[h]
        q = q_ref[:, sl].astype(F32)
        v = i_ref[:, sl]
        st = st_ref[h]
        o = lax.dot_general((q * jnp.exp(bc)).astype(BF16), st.astype(BF16), NT_DIMS,
                            preferred_element_type=F32)
        o = o + jnp.dot(sc_ref[h].astype(BF16), v, preferred_element_type=F32)

        last = bc[chunk - 1:chunk]
        kd = (kk * jnp.exp(last - bc)).astype(BF16)
        st_ref[h] = st * jnp.exp(last) + lax.dot_general(v, kd, TN_DIMS, preferred_element_type=F32)

        y = o * lax.rsqrt(jnp.mean(o * o, axis=-1, keepdims=True) + EPS) * ng_ref[:, sl]
        o_ref[:, sl] = (y * _silu(gate_ref[:, sl].astype(F32))).astype(o_ref.dtype)


def _hgrn(g1, fb, lb_logits, ng, bsz, L, chunk=TILES.hgrn_chunk, rblk=TILES.hgrn_rows):
    assert rblk & (rblk - 1) == 0 and chunk % rblk == 0
    t = bsz * L
    nc = L // chunk
    width = B_HEADS * B_DIM
    kern = functools.partial(_hgrn_kernel, chunk=chunk, rblk=rblk)
    return pl.pallas_call(
        kern,
        grid=(bsz, nc),
        in_specs=[pl.BlockSpec((chunk, width), lambda b, c: (b * nc + c, 2)),
                  pl.BlockSpec((chunk, width), lambda b, c: (b * nc + c, 3)),
                  pl.BlockSpec((chunk, width), lambda b, c: (b * nc + c, 4)),
                  pl.BlockSpec((chunk, width), lambda b, c: (b * nc + c, 0)),
                  pl.BlockSpec(lb_logits.shape, lambda b, c: (0, 0)),
                  pl.BlockSpec((1, width), lambda b, c: (0, 0))],
        out_specs=pl.BlockSpec((chunk, width), lambda b, c: (b * nc + c, 0)),
        out_shape=jax.ShapeDtypeStruct((t, width), BF16),
        scratch_shapes=[pltpu.VMEM((B_HEADS, B_DIM, B_DIM), F32),
                        pltpu.VMEM((B_HEADS, chunk, B_DIM), F32),
                        pltpu.VMEM((B_HEADS, chunk, B_DIM), F32),
                        pltpu.VMEM((B_HEADS, chunk, chunk), F32)],
        compiler_params=_cparams(("parallel", "arbitrary")),
    )(g1, g1, g1, fb, lb_logits, ng.reshape(1, width))


def _out_kernel(oa_ref, ob_ref, x_ref, wa_ref, wb_ref, mod_ref, g_ref, wr_ref, x1_ref, h2_ref, lg_ref):
    mix = jnp.dot(oa_ref[...], wa_ref[...], preferred_element_type=F32)
    mix = mix + jnp.dot(ob_ref[...], wb_ref[...], preferred_element_type=F32)
    x1 = x_ref[...] + mod_ref[0, 2:3, :] * mix
    x1_ref[...] = x1
    y = x1 * lax.rsqrt(jnp.mean(x1 * x1, axis=-1, keepdims=True) + EPS) * g_ref[...]
    h2 = y * (1.0 + mod_ref[0, 4:5, :]) + mod_ref[0, 3:4, :]
    h2_ref[...] = _pack_halves(h2)
    lg_ref[...] = lax.dot_general(wr_ref[...], h2.astype(BF16), NT_DIMS, preferred_element_type=F32)


def _out(oa, ob, x2d, w_out, mod3, g, w_router_t, L, tm=TILES.out_rows):
    t, d = x2d.shape
    half = oa.shape[1]
    ne = w_router_t.shape[0]
    return pl.pallas_call(
        _out_kernel,
        grid=(t // tm,),
        in_specs=[pl.BlockSpec((tm, half), lambda i: (i, 0)),
                  pl.BlockSpec((tm, half), lambda i: (i, 0)),
                  pl.BlockSpec((tm, d), lambda i: (i, 0)),
                  pl.BlockSpec((half, d), lambda i: (0, 0)),
                  pl.BlockSpec((half, d), lambda i: (1, 0)),
                  pl.BlockSpec((1, 6, d), lambda i: (i * tm // L, 0, 0)),
                  pl.BlockSpec((1, d), lambda i: (0, 0)),
                  pl.BlockSpec((ne, d), lambda i: (0, 0))],
        out_specs=[pl.BlockSpec((tm, d), lambda i: (i, 0)),
                   pl.BlockSpec((tm, d // 2), lambda i: (i, 0)),
                   pl.BlockSpec((ne, tm), lambda i: (0, i))],
        out_shape=[jax.ShapeDtypeStruct((t, d), F32),
                   jax.ShapeDtypeStruct((t, d // 2), U32),
                   jax.ShapeDtypeStruct((ne, t), F32)],
        compiler_params=_cparams(("parallel",)),
    )(oa, ob, x2d, w_out, w_out, mod3, g.reshape(1, d), w_router_t)


def _rows_to_tile(rows, nrow):
    n = rows[0].shape[1]
    ridx = lax.broadcasted_iota(I32, (nrow, n), 0)
    out = jnp.zeros((nrow, n), rows[0].dtype)
    for r, v in enumerate(rows):
        out = jnp.where(ridx == r, jnp.broadcast_to(v, (nrow, n)), out)
    return out


def _route_kernel(lg_ref, rb_ref, eidx_ref, ew_ref, rank_ref, cnt_ref, run_ref):
    @pl.when(pl.program_id(0) == 0)
    def _():
        run_ref[...] = jnp.zeros(run_ref.shape, F32)

    ne, tt = lg_ref.shape
    per = ne // N_GROUPS
    sc = jax.nn.sigmoid(lg_ref[...])
    ch = sc + rb_ref[...]
    neg = -jnp.inf

    sub = lax.broadcasted_iota(I32, (per, tt), 0).astype(F32)
    gsc = []
    for g in range(N_GROUPS):
        cg = ch[g * per:(g + 1) * per]
        m1 = jnp.max(cg, axis=0, keepdims=True)
        first = jnp.min(jnp.where(cg == m1, sub, float(per)), axis=0, keepdims=True)
        m2 = jnp.max(jnp.where(sub == first, neg, cg), axis=0, keepdims=True)
        gsc.append(m1 + m2)
    grp = _rows_to_tile(gsc, N_GROUPS)

    gid = lax.broadcasted_iota(I32, (N_GROUPS, tt), 0).astype(F32)
    gsel = jnp.zeros((N_GROUPS, tt), F32)
    for _ in range(TOPK_GROUPS):
        mx = jnp.max(grp, axis=0, keepdims=True)
        gi = jnp.min(jnp.where(grp == mx, gid, float(N_GROUPS)), axis=0, keepdims=True)
        pick = gid == gi
        gsel = jnp.where(pick, 1.0, gsel)
        grp = jnp.where(pick, neg, grp)

    eid = lax.broadcasted_iota(I32, (ne, tt), 0).astype(F32)
    cm = jnp.full((ne, tt), neg, F32)
    for g in range(N_GROUPS):
        in_g = (eid >= float(g * per)) & (eid < float((g + 1) * per))
        cm = jnp.where(in_g & (jnp.broadcast_to(gsel[g:g + 1], (ne, tt)) > 0.5), ch, cm)

    idx_rows, w_rows = [], []
    onehot = jnp.zeros((ne, tt), F32)
    for _ in range(TOP_K):
        mx = jnp.max(cm, axis=0, keepdims=True)
        ei = jnp.min(jnp.where(cm == mx, eid, float(ne)), axis=0, keepdims=True)
        pick = eid == ei
        idx_rows.append(ei)
        w_rows.append(jnp.sum(jnp.where(pick, sc, 0.0), axis=0, keepdims=True))
        onehot = jnp.where(pick, 1.0, onehot)
        cm = jnp.where(pick, neg, cm)
    wsum = w_rows[0]
    for w in w_rows[1:]:
        wsum = wsum + w
    w_rows = [w / wsum * ROUTED_SCALE for w in w_rows]

    a_i = lax.broadcasted_iota(I32, (tt, tt), 0)
    b_i = lax.broadcasted_iota(I32, (tt, tt), 1)
    upper = jnp.where(a_i < b_i, 1.0, 0.0).astype(BF16)
    rank_full = jnp.dot(onehot.astype(BF16), upper, preferred_element_type=F32) + run_ref[...]
    r_rows = [jnp.sum(jnp.where(eid == ei, rank_full, 0.0), axis=0, keepdims=True) for ei in idx_rows]
    run = run_ref[...] + jnp.sum(onehot, axis=1, keepdims=True)
    run_ref[...] = run

    eidx_ref[...] = _rows_to_tile(idx_rows, TOP_K).astype(I32)
    ew_ref[...] = _rows_to_tile(w_rows, TOP_K)
    rank_ref[...] = _rows_to_tile(r_rows, TOP_K).astype(I32)
    cnt_ref[...] = jnp.broadcast_to(run, cnt_ref.shape)


def _route(logits_t, router_bias, tt=TILES.route_tokens):
    ne, t = logits_t.shape
    return pl.pallas_call(
        _route_kernel,
        grid=(t // tt,),
        in_specs=[pl.BlockSpec((ne, tt), lambda i: (0, i)),
                  pl.BlockSpec((ne, 1), lambda i: (0, 0))],
        out_specs=[pl.BlockSpec((TOP_K, tt), lambda i: (0, i)),
                   pl.BlockSpec((TOP_K, tt), lambda i: (0, i)),
                   pl.BlockSpec((TOP_K, tt), lambda i: (0, i)),
                   pl.BlockSpec((ne, LANES), lambda i: (0, 0))],
        out_shape=[jax.ShapeDtypeStruct((TOP_K, t), I32),
                   jax.ShapeDtypeStruct((TOP_K, t), F32),
                   jax.ShapeDtypeStruct((TOP_K, t), I32),
                   jax.ShapeDtypeStruct((ne, LANES), F32)],
        scratch_shapes=[pltpu.VMEM((ne, 1), F32)],
        compiler_params=_cparams(("arbitrary",)),
    )(logits_t, router_bias.reshape(ne, 1))


def _dest_kernel(eidx_ref, rank_ref, ps_ref, o_ref):
    ne = ps_ref.shape[0]
    tt = eidx_ref.shape[1]
    eid = lax.broadcasted_iota(I32, (ne, tt), 0)
    ps = jnp.broadcast_to(ps_ref[...], (ne, tt))
    rows = []
    for k in range(TOP_K):
        start = jnp.sum(jnp.where(eid == eidx_ref[k:k + 1, :], ps, 0.0), axis=0, keepdims=True)
        rows.append(start + rank_ref[k:k + 1, :].astype(F32))
    o_ref[...] = _rows_to_tile(rows, TOP_K).astype(I32)


def _dest(eidx, rank, pad_start, tt=TILES.dest_tokens):
    t = eidx.shape[1]
    tt = min(tt, t)
    ne = pad_start.shape[0]
    return pl.pallas_call(
        _dest_kernel,
        grid=(t // tt,),
        in_specs=[pl.BlockSpec((TOP_K, tt), lambda i: (0, i)),
                  pl.BlockSpec((TOP_K, tt), lambda i: (0, i)),
                  pl.BlockSpec((ne, 1), lambda i: (0, 0))],
        out_specs=pl.BlockSpec((TOP_K, tt), lambda i: (0, i)),
        out_shape=jax.ShapeDtypeStruct((TOP_K, t), I32),
        compiler_params=_cparams(("parallel",)),
    )(eidx, rank, pad_start.astype(F32).reshape(ne, 1))


def _dispatch_kernel(pend_ref, padded_ref, dest_ref, h_ref, xs_ref, zbuf_ref, zsem, sem, *, td, bm):
    i = pl.program_id(0)

    def tail_copy(e):
        start = pl.multiple_of(pend_ref[e] - bm, bm)
        return pltpu.make_async_copy(zbuf_ref, xs_ref.at[pl.ds(start, bm)], zsem)

    @pl.when(i == 0)
    def _():
        zbuf_ref[...] = jnp.zeros(zbuf_ref.shape, zbuf_ref.dtype)

        def start_body(e, c):
            @pl.when(padded_ref[e] > 0)
            def _():
                tail_copy(e).start()
            return c

        def wait_body(e, c):
            @pl.when(padded_ref[e] > 0)
            def _():
                tail_copy(e).wait()
            return c

        lax.fori_loop(0, N_EXPERTS, start_body, 0)
        lax.fori_loop(0, N_EXPERTS, wait_body, 0)

        def unused_copy(b):
            return pltpu.make_async_copy(zbuf_ref, xs_ref.at[pl.ds(pl.multiple_of(b * bm, bm), bm)], zsem)

        def ustart_body(b, c):
            unused_copy(b).start()
            return c

        def uwait_body(b, c):
            unused_copy(b).wait()
            return c

        first_unused = pend_ref[N_EXPERTS - 1] // bm
        lax.fori_loop(first_unused, xs_ref.shape[0] // bm, ustart_body, 0)
        lax.fori_loop(first_unused, xs_ref.shape[0] // bm, uwait_body, 0)

    for j in range(td):
        for k in range(TOP_K):
            dst = xs_ref.at[dest_ref[j * TOP_K + k]]
            pltpu.make_async_copy(h_ref.at[j], dst, sem).start(priority=k % 2)
    for _ in range(TOP_K):
        pltpu.make_async_copy(h_ref, xs_ref.at[pl.ds(0, td)], sem).wait()


def _dispatch(pad_end, padded, dest_flat, h2p, n_rows, bm, td=TILES.move_tokens):
    t, w = h2p.shape
    kern = functools.partial(_dispatch_kernel, td=td, bm=bm)
    return pl.pallas_call(
        kern,
        grid_spec=pltpu.PrefetchScalarGridSpec(
            num_scalar_prefetch=2,
            grid=(t // td,),
            in_specs=[pl.BlockSpec((td * TOP_K,), lambda i, *_: (i,), memory_space=pltpu.SMEM),
                      pl.BlockSpec((td, w), lambda i, *_: (i, 0))],
            out_specs=pl.BlockSpec(memory_space=pl.ANY),
            scratch_shapes=[pltpu.VMEM((bm, w), U32),
                            pltpu.SemaphoreType.DMA(()),
                            pltpu.SemaphoreType.DMA(())]),
        out_shape=jax.ShapeDtypeStruct((n_rows, w), U32),
        compiler_params=_cparams(("arbitrary",)),
    )(pad_end, padded, dest_flat, h2p)


def _ffn(xw, wg_ref, wu_ref, wd_ref):
    half = xw.shape[1]
    left, right = _unpack_halves(xw)
    left = left.astype(BF16)
    right = right.astype(BF16)

    def proj(w_ref):
        return (jnp.dot(left, w_ref[:half, :], preferred_element_type=F32)
                + jnp.dot(right, w_ref[half:, :], preferred_element_type=F32))

    act = (_silu(proj(wg_ref)) * proj(wu_ref)).astype(BF16)
    return jnp.dot(act, wd_ref[...], preferred_element_type=F32)


def _expert_kernel(blk_ref, eid_ref, first_ref, slot_ref, nxt_ref, more_ref, nvb_ref,
                   x_ref, wg_hbm, wu_hbm, wd_hbm, o_ref,
                   wg_f, wu_f, wd_f, wg_s, wu_s, wd_s, sems):
    i = pl.program_id(0)

    def weight_copies(e, slot):
        return (pltpu.make_async_copy(wg_hbm.at[e], wg_f.at[slot], sems.at[slot]),
                pltpu.make_async_copy(wu_hbm.at[e], wu_f.at[slot], sems.at[slot]),
                pltpu.make_async_copy(wd_hbm.at[e], wd_f.at[slot], sems.at[slot]))

    @pl.when(i == 0)
    def _():
        for cp in weight_copies(eid_ref[0], 0):
            cp.start(priority=1)

    @pl.when(first_ref[i] == 1)
    def _():
        slot = slot_ref[i]
        for cp in weight_copies(eid_ref[i], slot):
            cp.wait()

        @pl.when(more_ref[i] == 1)
        def _():
            for cp in weight_copies(nxt_ref[i], 1 - slot):
                cp.start(priority=1)

    @pl.when(first_ref[i] == 1)
    def _():
        slot = slot_ref[i]
        wg_s[...] = wg_f[slot].astype(BF16)
        wu_s[...] = wu_f[slot].astype(BF16)
        wd_s[...] = wd_f[slot].astype(BF16)
        o_ref[...] = _pack_halves(_ffn(x_ref[...], wg_s, wu_s, wd_s))

    @pl.when((first_ref[i] == 0) & (i < nvb_ref[0]))
    def _():
        o_ref[...] = _pack_halves(_ffn(x_ref[...], wg_s, wu_s, wd_s))

    @pl.when(i >= nvb_ref[0])
    def _():
        o_ref[...] = jnp.zeros(o_ref.shape, o_ref.dtype)


def _experts(blk, eid, first, slot, nxt, more, nvb, xs, wg, wu, wd, bm):
    n_rows, w = xs.shape
    ne, d, f = wg.shape
    return pl.pallas_call(
        _expert_kernel,
        grid_spec=pltpu.PrefetchScalarGridSpec(
            num_scalar_prefetch=7,
            grid=(n_rows // bm,),
            in_specs=[pl.BlockSpec((bm, w), lambda i, blk, *_: (blk[i], 0)),
                      pl.BlockSpec(memory_space=pl.ANY),
                      pl.BlockSpec(memory_space=pl.ANY),
                      pl.BlockSpec(memory_space=pl.ANY)],
            out_specs=pl.BlockSpec((bm, w), lambda i, *_: (i, 0)),
            scratch_shapes=[pltpu.VMEM((2, d, f), F32), pltpu.VMEM((2, d, f), F32), pltpu.VMEM((2, f, d), F32),
                            pltpu.VMEM((d, f), BF16), pltpu.VMEM((d, f), BF16), pltpu.VMEM((f, d), BF16),
                            pltpu.SemaphoreType.DMA((2,))]),
        out_shape=jax.ShapeDtypeStruct((n_rows, w), U32),
        compiler_params=_cparams(("arbitrary",)),
    )(blk, eid, first, slot, nxt, more, nvb, xs, wg, wu, wd)


def _combine_kernel(dest_ref, dnext_ref, y_ref, h_ref, x1_ref, ew_ref, wg_ref, wu_ref, wd_ref, mod_ref, g_ref,
                    o_ref, gbuf_a, gbuf_b, sems, *, tc):
    i = pl.program_id(0)
    last = pl.num_programs(0) - 1

    def issue(idx_ref, gbuf, sem):
        for j in range(tc):
            for k in range(TOP_K):
                src = y_ref.at[idx_ref[j * TOP_K + k]]
                pltpu.make_async_copy(src, gbuf.at[k, j], sem).start(priority=k % 2)

    def wait_all(gbuf, sem):
        for k in range(TOP_K):
            pltpu.make_async_copy(y_ref.at[pl.ds(0, tc)], gbuf.at[k], sem).wait()

    def compute(gbuf):
        shared = _ffn(h_ref[...], wg_ref, wu_ref, wd_ref)
        half = h_ref.shape[1]
        ew = ew_ref[...]
        left = shared[:, :half]
        right = shared[:, half:]
        for k in range(TOP_K):
            yl, yr = _unpack_halves(gbuf[k])
            wk = ew[:, k:k + 1]
            left = left + wk * yl
            right = right + wk * yr
        x2 = x1_ref[...] + mod_ref[0, 5:6, :] * jnp.concatenate([left, right], axis=1)
        o_ref[...] = x2 * lax.rsqrt(jnp.mean(x2 * x2, axis=-1, keepdims=True) + EPS) * g_ref[...]

    @pl.when(i == 0)
    def _():
        def group_body(g, c):
            base = pl.multiple_of(g * SUBLANES, SUBLANES)
            for jj in range(SUBLANES):
                for k in range(TOP_K):
                    src = y_ref.at[dest_ref[(base + jj) * TOP_K + k]]
                    pltpu.make_async_copy(src, gbuf_a.at[k, base + jj], sems.at[0]).start(priority=k % 2)
            return c

        lax.fori_loop(0, tc // SUBLANES, group_body, 0)

    @pl.when(i % 2 == 0)
    def _():
        wait_all(gbuf_a, sems.at[0])
        issue(dnext_ref, gbuf_b, sems.at[1])
        compute(gbuf_a)

    @pl.when(i % 2 == 1)
    def _():
        wait_all(gbuf_b, sems.at[1])
        issue(dnext_ref, gbuf_a, sems.at[0])
        compute(gbuf_b)

    @pl.when((i == last) & (i % 2 == 0))
    def _():
        wait_all(gbuf_b, sems.at[1])

    @pl.when((i == last) & (i % 2 == 1))
    def _():
        wait_all(gbuf_a, sems.at[0])


def _combine(dest_flat, y, h2p, x1, ew_t, wsg, wsu, wsd, mod3, g, L, tc=TILES.move_tokens):
    t, d = x1.shape
    w = h2p.shape[1]
    nt = t // tc
    kern = functools.partial(_combine_kernel, tc=tc)
    return pl.pallas_call(
        kern,
        grid=(nt,),
        in_specs=[pl.BlockSpec((tc * TOP_K,), lambda i: (i,), memory_space=pltpu.SMEM),
                  pl.BlockSpec((tc * TOP_K,), lambda i: (jnp.minimum(i + 1, nt - 1),), memory_space=pltpu.SMEM),
                  pl.BlockSpec(memory_space=pl.ANY),
                  pl.BlockSpec((tc, w), lambda i: (i, 0)),
                  pl.BlockSpec((tc, d), lambda i: (i, 0)),
                  pl.BlockSpec((tc, TOP_K), lambda i: (i, 0)),
                  pl.BlockSpec(wsg.shape, lambda i: (0, 0)),
                  pl.BlockSpec(wsu.shape, lambda i: (0, 0)),
                  pl.BlockSpec(wsd.shape, lambda i: (0, 0)),
                  pl.BlockSpec((1, 6, d), lambda i: (i * tc // L, 0, 0)),
                  pl.BlockSpec((1, d), lambda i: (0, 0))],
        out_specs=pl.BlockSpec((tc, d), lambda i: (i, 0)),
        out_shape=jax.ShapeDtypeStruct((t, d), F32),
        scratch_shapes=[pltpu.VMEM((TOP_K, tc, w), U32),
                        pltpu.VMEM((TOP_K, tc, w), U32),
                        pltpu.SemaphoreType.DMA((2,))],
        compiler_params=_cparams(("arbitrary",)),
    )(dest_flat, dest_flat, y, h2p, x1, ew_t, wsg, wsu, wsd, mod3, g.reshape(1, d))


def _split_cols(w, sizes):
    out, off = [], 0
    for s in sizes:
        out.append(w[:, off:off + s])
        off += s
    return out


def kernel(x, c, w_ada, b_ada, norm1_g, w_in, ckv_norm_g, idx_k_norm_g, w_uk, w_uv, rel_bias, lb_logits,
           attn_out_norm_g, hgrn_out_norm_g, w_out, norm2_g, w_router, router_bias, w_e_gate, w_e_up,
           w_e_down, w_s_gate, w_s_up, w_s_down, final_norm_g):
    bsz, L, d = x.shape
    t = bsz * L
    assert w_ada.shape[0] == 1, "single-layer block"
    a_width = A_HEADS * A_HEAD_DIM
    b_width = B_HEADS * B_DIM
    sizes = (a_width, A_KV_RANK, IDX_HEADS * IDX_DIM, IDX_DIM, IDX_HEADS, b_width, b_width, b_width, b_width)
    assert w_in.shape[2] == sum(sizes)

    wq_a, wckv, wiq, wik, wiw, wq_b, wf_b, wi_b, wg_b = _split_cols(w_in[0], sizes)
    w_main = jnp.concatenate([wq_a, wiq, wq_b, wi_b, wg_b], axis=1).astype(BF16)
    w_f = wf_b.astype(BF16)
    aux_pad = LANES - IDX_DIM - IDX_HEADS
    w_aux = jnp.concatenate([wckv, wik, wiw, jnp.zeros((d, aux_pad), F32)], axis=1).astype(BF16)

    mod3 = _ada(c, w_ada[0], b_ada[0]).reshape(bsz, 6, d)
    h1 = _norm1(x, mod3, norm1_g[0]).reshape(t, d)
    g1 = _matmul(h1, w_main, BF16, tm=TILES.proj_rows, tn=TILES.proj_cols)
    fb = _matmul(h1, w_f, F32, tm=TILES.proj_rows, tn=TILES.proj_cols)
    aux = _matmul(h1, w_aux, F32, tm=TILES.proj_rows, tn=w_aux.shape[1])
    ckv_n, ik_lo, ik_hi = _kvnorm(aux, ckv_norm_g[0], idx_k_norm_g[0])

    tq = min(TILES.attn, L)
    o_a = _dsa(g1, aux, ik_lo, ik_hi, ckv_n, w_uk[0].astype(BF16), w_uv[0].astype(BF16),
               _bias_tables(rel_bias, tq), attn_out_norm_g[0], bsz, L, tq)
    o_b = _hgrn(g1, fb, lb_logits, hgrn_out_norm_g[0], bsz, L)

    x1, h2p, logits_t = _out(o_a, o_b, x.reshape(t, d), w_out[0].astype(BF16), mod3, norm2_g[0],
                             w_router[0].T.astype(BF16), L)

    eidx, ew, rank, cnt = _route(logits_t, router_bias[0])

    bm = TILES.moe_rows
    counts = cnt[:, 0].astype(I32)
    padded = (counts + bm - 1) // bm * bm
    pad_end = jnp.cumsum(padded)
    pad_start = pad_end - padded
    n_rows = (t * TOP_K + N_EXPERTS * (bm - 1) + bm - 1) // bm * bm
    nb = n_rows // bm
    nvb = pad_end[-1] // bm
    blk = jnp.minimum(jnp.arange(nb, dtype=I32), nvb - 1)
    eid = jnp.minimum(jnp.sum((pad_end[None, :] <= (blk * bm)[:, None]).astype(I32), axis=1), N_EXPERTS - 1)
    ar = jnp.arange(nb, dtype=I32)
    first = ((ar < nvb) & ((ar == 0) | (eid != jnp.roll(eid, 1)))).astype(I32)
    slot = (jnp.cumsum(first) - 1) % 2
    nxt_blk = pad_end[eid] // bm
    more = (nxt_blk < nvb).astype(I32)
    nxt = eid[jnp.minimum(nxt_blk, nb - 1)]

    dest = _dest(eidx, rank, pad_start)
    dest_flat = dest.T.reshape(t * TOP_K)
    xs = _dispatch(pad_end.astype(I32), padded.astype(I32), dest_flat, h2p, n_rows, bm)

    y = _experts(blk, eid, first, slot.astype(I32), nxt.astype(I32), more, nvb.reshape(1).astype(I32), xs,
                 w_e_gate[0], w_e_up[0], w_e_down[0], bm)

    out = _combine(dest_flat, y, h2p, x1, ew.T, w_s_gate[0].astype(BF16), w_s_up[0].astype(BF16),
                   w_s_down[0].astype(BF16), mod3, final_norm_g, L)
    return out.reshape(bsz, L, d)
```
